```python
import jax, jax.numpy as jnp
from jax import lax
import numpy as np

D_MODEL = 1024
BATCH = 16
SEQ = 256
DEPTH = 2
DEC_BATCH = 2
DEC_SEQ = 2048
PAST_LEN = 256

GRID_W = 64
N_MIXERS = 2
EPS = 1e-6
N_MOD = 9
D_FF = 2816
FFN_RES = 0.5
D_RNN = 1280
RG_BLOCKS = 10
RG_BLOCK = D_RNN // RG_BLOCKS
CONV_W = 4
CONV_PAD_L = (CONV_W - 1) // 2
RG_C = 8.0
GLA_HEADS = 4
GLA_QK = D_MODEL // 2
GLA_V = D_MODEL
GLA_DK = GLA_QK // GLA_HEADS
GLA_DV = GLA_V // GLA_HEADS
GLA_RANK = 16
GLA_TAU = 16.0
GLA_CHUNK = 32
GLA_IN = 2 * GLA_QK + 2 * GLA_V + 2 * GLA_RANK

kernel_name = "hybrid_rglru_gla_diffusion_step"


def rms_norm(x, g):
    xf = x.astype(jnp.float32)
    y = xf * lax.rsqrt(jnp.mean(xf * xf, axis=-1, keepdims=True) + EPS)
    return (y * g.astype(jnp.float32)).astype(x.dtype)


def ada_mod(cvec, w, b):
    m = jax.nn.silu(cvec) @ w + b
    return m.reshape(-1, 1, N_MOD, D_MODEL)


def modulated(x, mod, gains, j):
    return rms_norm(x, gains[j]) * (1.0 + mod[:, :, 3 * j + 1]) + mod[:, :, 3 * j]


def swiglu(h, w_in, w_out):
    g, u = jnp.split(h @ w_in, 2, axis=-1)
    return (jax.nn.silu(g) * u) @ w_out


def half_ffn(x, mod, gains, j, w_in, w_out):
    return x + FFN_RES * mod[:, :, 3 * j + 2] * swiglu(modulated(x, mod, gains, j), w_in, w_out)


def centred_dwconv(x, w, b):
    y = lax.conv_general_dilated(x, w[:, None, :], window_strides=(1,),
                                 padding=[(CONV_PAD_L, CONV_W - 1 - CONV_PAD_L)],
                                 dimension_numbers=('NWC', 'WIO', 'NWC'),
                                 feature_group_count=x.shape[-1])
    return y + b


def rglru_scan(xc, w_a, b_a, w_i, b_i, lam, h0, reverse):
    bsz, t_len, _ = xc.shape
    f32 = jnp.float32
    xf = xc.astype(f32)
    xb = xf.reshape(bsz, t_len, RG_BLOCKS, RG_BLOCK)
    r = jax.nn.sigmoid(jnp.einsum('btnk,nkj->btnj', xb, w_a.astype(f32)).reshape(bsz, t_len, D_RNN) + b_a.astype(f32))
    gi = jax.nn.sigmoid(jnp.einsum('btnk,nkj->btnj', xb, w_i.astype(f32)).reshape(bsz, t_len, D_RNN) + b_i.astype(f32))
    log_a = -RG_C * r * jax.nn.softplus(-lam.astype(f32))
    a = jnp.exp(log_a)
    u = jnp.sqrt(-jnp.expm1(2.0 * log_a)) * (gi * xf)

    def step(h, inp):
        a_t, u_t = inp
        h = a_t * h + u_t
        return h, h

    h_last, hs = lax.scan(step, h0.astype(f32), (a.swapaxes(0, 1), u.swapaxes(0, 1)), reverse=reverse)
    return hs.swapaxes(0, 1), h_last


def rglru_mixer(h, w_in, conv_w, conv_b, w_a, b_a, w_i, b_i, lam, w_out, h0_f, h0_b):
    gate_br, x_br = jnp.split(h @ w_in, 2, axis=-1)
    xc = centred_dwconv(x_br, conv_w, conv_b)
    y_f, s_f = rglru_scan(xc, w_a[0], b_a[0], w_i[0], b_i[0], lam[0], h0_f, False)
    y_b, s_b = rglru_scan(xc, w_a[1], b_a[1], w_i[1], b_i[1], lam[1], h0_b, True)
    y = (y_f + y_b).astype(h.dtype) * jax.nn.gelu(gate_br)
    return y @ w_out, jnp.stack([s_f, s_b], axis=1)


def gla_chunked(q, k, v, log_g, s0):
    bsz, t_len, n_h = q.shape[0], q.shape[1], q.shape[2]
    n_chunks = t_len // GLA_CHUNK

    def chunk(a):
        return a.reshape((bsz, n_chunks, GLA_CHUNK) + a.shape[2:])

    q, k, v, log_g = chunk(q), chunk(k), chunk(v), chunk(log_g)
    b = jnp.cumsum(log_g, axis=2)
    b_last = b[:, :, -1:]
    causal = jnp.tril(jnp.ones((GLA_CHUNK, GLA_CHUNK), dtype=bool))[None, None, :, :, None, None]
    rel = jnp.where(causal, b[:, :, :, None] - b[:, :, None, :], -jnp.inf)
    scores = jnp.einsum('bnihd,bnjhd,bnijhd->bnhij', q, k, jnp.exp(rel))
    o_intra = jnp.einsum('bnhij,bnjhv->bnihv', scores, v)
    u = jnp.einsum('bnchk,bnchv->bnhkv', k * jnp.exp(b_last - b), v)
    a_chunk = jnp.exp(b_last[:, :, 0])

    def step(s, inp):
        a_n, u_n = inp
        return a_n[..., None] * s + u_n, s

    s_final, s_enter = lax.scan(step, s0, (jnp.moveaxis(a_chunk, 1, 0), jnp.moveaxis(u, 1, 0)))
    s_enter = jnp.moveaxis(s_enter, 0, 1)
    o_inter = jnp.einsum('bnchk,bnhkv->bnchv', q * jnp.exp(b), s_enter)
    return (o_intra + o_inter).reshape(bsz, t_len, n_h, v.shape[-1]), s_final


def gla_mixer(h, w_in, w_g2, b_g, g_norm, w_out, s0_f, s0_b):
    bsz, t_len, _ = h.shape
    f32 = jnp.float32
    proj = h @ w_in
    q, k, v, r, z_f, z_b = jnp.split(proj, [GLA_QK, 2 * GLA_QK, 2 * GLA_QK + GLA_V,
                                          2 * GLA_QK + 2 * GLA_V, 2 * GLA_QK + 2 * GLA_V + GLA_RANK], axis=-1)
    q = q.astype(f32).reshape(bsz, t_len, GLA_HEADS, GLA_DK) * (GLA_DK ** -0.5)
    k = k.astype(f32).reshape(bsz, t_len, GLA_HEADS, GLA_DK)
    v = v.astype(f32).reshape(bsz, t_len, GLA_HEADS, GLA_DV)

    def log_gate(z, w2, bias):
        zz = z.astype(f32) @ w2.astype(f32) + bias.astype(f32)
        return (-jax.nn.softplus(-zz) / GLA_TAU).reshape(bsz, t_len, GLA_HEADS, GLA_DK)

    def flip(t):
        return jnp.flip(t, axis=1)

    o_f, s_f = gla_chunked(q, k, v, log_gate(z_f, w_g2[0], b_g[0]), s0_f.astype(f32))
    o_b, s_b = gla_chunked(flip(q), flip(k), flip(v), flip(log_gate(z_b, w_g2[1], b_g[1])), s0_b.astype(f32))
    o = o_f + flip(o_b)
    o = o * lax.rsqrt(jnp.mean(o * o, axis=-1, keepdims=True) + EPS) * g_norm.astype(f32).reshape(GLA_HEADS, GLA_DV)
    o = o.reshape(bsz, t_len, GLA_V).astype(h.dtype) * jax.nn.silu(r)
    return o @ w_out, jnp.stack([s_f, s_b], axis=1)


def to_col_major(x, rows):
    bsz, t_len, d = x.shape
    return x.reshape(bsz, rows, GRID_W, d).swapaxes(1, 2).reshape(bsz, t_len, d)


def from_col_major(x, rows):
    bsz, t_len, d = x.shape
    return x.reshape(bsz, GRID_W, rows, d).swapaxes(1, 2).reshape(bsz, t_len, d)


def setup_inputs(seed: int = 0) -> dict:
    key = jax.random.key(seed)
    ks = iter(jax.random.split(key, 32))
    n_a = (DEPTH + 1) // 2
    n_b = DEPTH // 2
    f32 = jnp.float32

    def nrm(shape, scale):
        return jax.random.normal(next(ks), shape, f32) * scale

    x_prompt = nrm((BATCH, SEQ, D_MODEL), 1.0)
    x_sample = nrm((DEC_BATCH, DEC_SEQ, D_MODEL), 1.0)
    c = nrm((DEC_BATCH, D_MODEL), 1.0)
    state_rglru = nrm((DEC_BATCH, n_a, 2, D_RNN), 1.0)
    state_gla = nrm((DEC_BATCH, n_b, 2, GLA_HEADS, GLA_DK, GLA_DV), 1.0)
    c_ctx = nrm((D_MODEL,), 1.0)
    ada_w = nrm((DEPTH, D_MODEL, N_MOD * D_MODEL), 0.5 * D_MODEL ** -0.5)
    ada_b = nrm((DEPTH, N_MOD * D_MODEL), 0.02)
    norm_g = 1.0 + nrm((DEPTH, 3, D_MODEL), 0.05)
    ffn_w_in = nrm((DEPTH, 2, D_MODEL, 2 * D_FF), D_MODEL ** -0.5)
    ffn_w_out = nrm((DEPTH, 2, D_FF, D_MODEL), D_FF ** -0.5)
    rg_w_in = nrm((n_a, D_MODEL, 2 * D_RNN), D_MODEL ** -0.5)
    rg_conv_w = nrm((n_a, CONV_W, D_RNN), CONV_W ** -0.5)
    rg_conv_b = nrm((n_a, D_RNN), 0.02)
    rg_w_a = nrm((n_a, 2, RG_BLOCKS, RG_BLOCK, RG_BLOCK), RG_BLOCK ** -0.5)
    rg_b_a = nrm((n_a, 2, D_RNN), 0.02)
    rg_w_i = nrm((n_a, 2, RG_BLOCKS, RG_BLOCK, RG_BLOCK), RG_BLOCK ** -0.5)
    rg_b_i = nrm((n_a, 2, D_RNN), 0.02)
    a_init = jax.random.uniform(next(ks), (n_a, 2, D_RNN), f32, minval=0.9, maxval=0.999) ** (1.0 / RG_C)
    rg_lambda = jnp.log(a_init) - jnp.log1p(-a_init)
    rg_w_out = nrm((n_a, D_RNN, D_MODEL), D_RNN ** -0.5)
    gla_w_in = nrm((n_b, D_MODEL, GLA_IN), D_MODEL ** -0.5)
    gla_w_g2 = nrm((n_b, 2, GLA_RANK, GLA_QK), GLA_RANK ** -0.5)
    gla_b_g = nrm((n_b, 2, GLA_QK), 0.02)
    gla_norm = 1.0 + nrm((n_b, GLA_V), 0.05)
    gla_w_out = nrm((n_b, GLA_V, D_MODEL), GLA_V ** -0.5)
    final_norm = 1.0 + nrm((D_MODEL,), 0.05)
    return {"x_prompt": x_prompt, "x_sample": x_sample, "c": c, "state_rglru": state_rglru,
            "state_gla": state_gla, "c_ctx": c_ctx, "ada_w": ada_w, "ada_b": ada_b, "norm_g": norm_g,
            "ffn_w_in": ffn_w_in, "ffn_w_out": ffn_w_out, "rg_w_in": rg_w_in, "rg_conv_w": rg_conv_w,
            "rg_conv_b": rg_conv_b, "rg_w_a": rg_w_a, "rg_b_a": rg_b_a, "rg_w_i": rg_w_i, "rg_b_i": rg_b_i,
            "rg_lambda": rg_lambda, "rg_w_out": rg_w_out, "gla_w_in": gla_w_in, "gla_w_g2": gla_w_g2,
            "gla_b_g": gla_b_g, "gla_norm": gla_norm, "gla_w_out": gla_w_out, "final_norm": final_norm}


def reference(x_prompt, x_sample, c, state_rglru, state_gla, c_ctx, ada_w, ada_b, norm_g, ffn_w_in, ffn_w_out,
              rg_w_in, rg_conv_w, rg_conv_b, rg_w_a, rg_b_a, rg_w_i, rg_b_i, rg_lambda, rg_w_out,
              gla_w_in, gla_w_g2, gla_b_g, gla_norm, gla_w_out, final_norm):
    rows = x_sample.shape[1] // GRID_W
    n_prompt = x_prompt.shape[0]
    rg_states, gla_states = [], []
    xp, xs = x_prompt, x_sample
    for i in range(DEPTH):
        mod_p = ada_mod(c_ctx, ada_w[i], ada_b[i])
        mod_s = ada_mod(c, ada_w[i], ada_b[i])
        xp = half_ffn(xp, mod_p, norm_g[i], 0, ffn_w_in[i, 0], ffn_w_out[i, 0])
        xs = half_ffn(xs, mod_s, norm_g[i], 0, ffn_w_in[i, 0], ffn_w_out[i, 0])
        hp = modulated(xp, mod_p, norm_g[i], 1)
        hs = modulated(xs, mod_s, norm_g[i], 1)
        j = i // N_MIXERS
        if i % N_MIXERS == 0:
            rg = (rg_w_in[j], rg_conv_w[j], rg_conv_b[j], rg_w_a[j], rg_b_a[j], rg_w_i[j], rg_b_i[j],
                  rg_lambda[j], rg_w_out[j])
            zero = jnp.zeros((n_prompt, D_RNN), jnp.float32)
            yp, st = rglru_mixer(hp, *rg, zero, zero)
            ys, _ = rglru_mixer(hs, *rg, state_rglru[:, j, 0], state_rglru[:, j, 1])
            rg_states.append(st.astype(x_prompt.dtype))
        else:
            gl = (gla_w_in[j], gla_w_g2[j], gla_b_g[j], gla_norm[j], gla_w_out[j])
            zero = jnp.zeros((n_prompt, GLA_HEADS, GLA_DK, GLA_DV), jnp.float32)
            yp, st = gla_mixer(hp, *gl, zero, zero)
            ys_cm, _ = gla_mixer(to_col_major(hs, rows), *gl, state_gla[:, j, 0], state_gla[:, j, 1])
            ys = from_col_major(ys_cm, rows)
            gla_states.append(st.astype(x_prompt.dtype))
        xp = xp + mod_p[:, :, 5] * yp
        xs = xs + mod_s[:, :, 5] * ys
        xp = half_ffn(xp, mod_p, norm_g[i], 2, ffn_w_in[i, 1], ffn_w_out[i, 1])
        xs = half_ffn(xs, mod_s, norm_g[i], 2, ffn_w_in[i, 1], ffn_w_out[i, 1])
    y_prompt = rms_norm(xp, final_norm)
    y_sample = rms_norm(xs, final_norm)
    new_state_rglru = jnp.stack(rg_states, axis=1)
    new_state_gla = jnp.stack(gla_states, axis=1)
    return (y_prompt, y_sample, new_state_rglru, new_state_gla)
```

```python
import functools

import jax
import jax.numpy as jnp
from jax import lax
from jax.experimental import pallas as pl
from jax.experimental.pallas import tpu as pltpu

F32 = jnp.float32
BF16 = jnp.bfloat16

D_MODEL = 1024
BATCH = 16
SEQ = 256
DEC_BATCH = 2
DEC_SEQ = 2048
GRID_W = 64
EPS = 1e-6
N_MOD = 9
D_FF = 2816
FFN_RES = 0.5
D_RNN = 1280
RG_BLOCKS = 10
RG_BLOCK = 128
CONV_W = 4
RG_C = 8.0
GLA_HEADS = 4
GLA_QK = 512
GLA_V = 1024
GLA_DK = 128
GLA_DV = 256
GLA_RANK = 16
GLA_TAU = 16.0

N_PROMPT_TOK = BATCH * SEQ
N_TOK = N_PROMPT_TOK + DEC_BATCH * DEC_SEQ
N_GROUPS = 1 + DEC_BATCH
GROUP_ROWS = DEC_SEQ
PROMPT_BLOCKS = N_PROMPT_TOK // GROUP_ROWS

VMEM_LIMIT_BYTES = 56 * 1024 * 1024

TM = 512
FF_CHUNK = 1408
RG_CT = 256
SCAN_L = 256
SCAN_V = GROUP_ROWS // SCAN_L
GLA_C = 128


def _cparams(*sem):
    return pltpu.CompilerParams(dimension_semantics=sem, vmem_limit_bytes=VMEM_LIMIT_BYTES)


def _group_of_tile(i, rows):
    return jnp.maximum((i * rows) // GROUP_ROWS - (PROMPT_BLOCKS - 1), 0)


def _sigmoid(x):
    return 1.0 / (1.0 + jnp.exp(-x))


def _softplus(x):
    return jnp.maximum(x, 0.0) + jnp.log1p(jnp.exp(-jnp.abs(x)))


def _expm1(x):
    u = jnp.exp(x)
    near = jnp.where(u == 1.0, x, (u - 1.0) * x / jnp.log(u))
    return jnp.where(jnp.abs(u - 1.0) < 0.5, near, u - 1.0)


def _rms(x):
    return x * lax.rsqrt(jnp.mean(x * x, axis=-1, keepdims=True) + EPS)


def _modulated(x, mod_ref, gain_ref, j):
    y = _rms(x) * gain_ref[j:j + 1, :]
    return y * (1.0 + mod_ref[3 * j + 1:3 * j + 2, :]) + mod_ref[3 * j:3 * j + 1, :]


def _ada_kernel(c_ref, w_ref, b_ref, o_ref):
    c = c_ref[...]
    s = (c * _sigmoid(c)).astype(BF16)
    o_ref[...] = jnp.dot(s, w_ref[...].astype(BF16), preferred_element_type=F32) + b_ref[...]


def _ada_mods(cvecs, ada_w, ada_b):
    depth = ada_w.shape[0]
    out = pl.pallas_call(
        _ada_kernel,
        grid=(depth, N_MOD),
        in_specs=[
            pl.BlockSpec((8, D_MODEL), lambda l, k: (0, 0)),
            pl.BlockSpec((None, D_MODEL, D_MODEL), lambda l, k: (l, 0, k)),
            pl.BlockSpec((None, None, 1, D_MODEL), lambda l, k: (l, k, 0, 0)),
        ],
        out_specs=pl.BlockSpec((None, None, 8, D_MODEL), lambda l, k: (l, k, 0, 0)),
        out_shape=jax.ShapeDtypeStruct((depth, N_MOD, 8, D_MODEL), F32),
        compiler_params=_cparams("arbitrary", "arbitrary"),
        name="ada_mod",
    )(cvecs, ada_w, ada_b.reshape(depth, N_MOD, 1, D_MODEL))
    return out[:, :, :N_GROUPS, :].transpose(0, 2, 1, 3)


def _ffn_kernel(x_ref, mod_ref, gain_ref, win_ref, wout_ref, fn_ref, o_ref, *, j, final):
    x = x_ref[...]
    hb = _modulated(x, mod_ref, gain_ref, j).astype(BF16)
    acc = jnp.zeros((TM, D_MODEL), F32)
    for c in range(D_FF // FF_CHUNK):
        g = jnp.dot(hb, win_ref[:, c * FF_CHUNK:(c + 1) * FF_CHUNK], preferred_element_type=F32)
        u = jnp.dot(hb, win_ref[:, D_FF + c * FF_CHUNK:D_FF + (c + 1) * FF_CHUNK],
                    preferred_element_type=F32)
        a = (g * _sigmoid(g) * u).astype(BF16)
        acc = acc + jnp.dot(a, wout_ref[c * FF_CHUNK:(c + 1) * FF_CHUNK, :], preferred_element_type=F32)
    out = x + FFN_RES * mod_ref[3 * j + 2:3 * j + 3, :] * acc
    if final:
        out = _rms(out) * fn_ref[...]
    o_ref[...] = out


def _half_ffn(x, mods_l, gains_l, w_in, w_out, final_norm, j, final):
    return pl.pallas_call(
        functools.partial(_ffn_kernel, j=j, final=final),
        grid=(N_TOK // TM,),
        in_specs=[
            pl.BlockSpec((TM, D_MODEL), lambda i: (i, 0)),
            pl.BlockSpec((None, N_MOD, D_MODEL), lambda i: (_group_of_tile(i, TM), 0, 0)),
            pl.BlockSpec((3, D_MODEL), lambda i: (0, 0)),
            pl.BlockSpec((D_MODEL, 2 * D_FF), lambda i: (0, 0), pipeline_mode=pl.Buffered(1)),
            pl.BlockSpec((D_FF, D_MODEL), lambda i: (0, 0), pipeline_mode=pl.Buffered(1)),
            pl.BlockSpec((1, D_MODEL), lambda i: (0, 0)),
        ],
        out_specs=pl.BlockSpec((TM, D_MODEL), lambda i: (i, 0)),
        out_shape=jax.ShapeDtypeStruct((N_TOK, D_MODEL), F32),
        compiler_params=_cparams("arbitrary"),
        name="half_ffn",
    )(x, mods_l, gains_l, w_in, w_out, final_norm)


def _out_proj_kernel(y_ref, w_ref, x_ref, mod_ref, o_ref):
    upd = jnp.dot(y_ref[...].astype(BF16), w_ref[...], preferred_element_type=F32)
    o_ref[...] = x_ref[...] + mod_ref[5:6, :] * upd


def _out_proj(y, w_out, x, mods_l):
    k = y.shape[1]
    return pl.pallas_call(
        _out_proj_kernel,
        grid=(N_TOK // TM,),
        in_specs=[
            pl.BlockSpec((TM, k), lambda i: (i, 0)),
            pl.BlockSpec((k, D_MODEL), lambda i: (0, 0), pipeline_mode=pl.Buffered(1)),
            pl.BlockSpec((TM, D_MODEL), lambda i: (i, 0)),
            pl.BlockSpec((None, N_MOD, D_MODEL), lambda i: (_group_of_tile(i, TM), 0, 0)),
        ],
        out_specs=pl.BlockSpec((TM, D_MODEL), lambda i: (i, 0)),
        out_shape=jax.ShapeDtypeStruct((N_TOK, D_MODEL), F32),
        compiler_params=_cparams("arbitrary"),
        name="mixer_out_proj",
    )(y, w_out, x, mods_l)


def _gelu_tanh(x):
    return 0.5 * x * (1.0 + jnp.tanh(0.7978845608028654 * (x + 0.044715 * (x * x * x))))


def _rg_proj_kernel(x_ref, mod_ref, gain_ref, w_ref, gate_ref, xbr_ref):
    hb = _modulated(x_ref[...], mod_ref, gain_ref, 1).astype(BF16)
    gate_ref[...] = _gelu_tanh(jnp.dot(hb, w_ref[:, :D_RNN], preferred_element_type=F32))
    xbr_ref[...] = jnp.dot(hb, w_ref[:, D_RNN:], preferred_element_type=F32)


def _rg_proj(x, mods_l, gains_l, w_in):
    return pl.pallas_call(
        _rg_proj_kernel,
        grid=(N_TOK // TM,),
        in_specs=[
            pl.BlockSpec((TM, D_MODEL), lambda i: (i, 0)),
            pl.BlockSpec((None, N_MOD, D_MODEL), lambda i: (_group_of_tile(i, TM), 0, 0)),
            pl.BlockSpec((3, D_MODEL), lambda i: (0, 0)),
            pl.BlockSpec((D_MODEL, 2 * D_RNN), lambda i: (0, 0), pipeline_mode=pl.Buffered(1)),
        ],
        out_specs=[pl.BlockSpec((TM, D_RNN), lambda i: (i, 0))] * 2,
        out_shape=[jax.ShapeDtypeStruct((N_TOK, D_RNN), F32)] * 2,
        compiler_params=_cparams("arbitrary"),
        name="rg_in_proj",
    )(x, mods_l, gains_l, w_in)


def _rg_scan_kernel(xbr_ref, gate_ref, cw_ref, cb_ref, wa_ref, ba_ref, wi_ref, bi_ref, lam_ref, h0_ref,
                    y_ref, stf_ref, stb_ref, af_ref, uf_ref, ab_ref, ub_ref):
    blk = pl.program_id(0)
    is_sample = blk >= PROMPT_BLOCKS
    seq_len = jnp.where(is_sample, DEC_SEQ, SEQ)
    chain = jnp.where(is_sample, 1.0, 0.0).astype(F32)
    rows = GROUP_ROWS

    t = lax.broadcasted_iota(jnp.int32, (rows, RG_BLOCK), 0) & (seq_len - 1)
    has_m1 = t >= 1
    has_p1 = t < seq_len - 1
    has_p2 = t < seq_len - 2

    for nb in range(RG_CT // RG_BLOCK):
        ls = slice(nb * RG_BLOCK, (nb + 1) * RG_BLOCK)
        x = xbr_ref[:, ls]
        xc = (cw_ref[0:1, ls] * jnp.where(has_m1, pltpu.roll(x, 1, 0), 0.0)
              + cw_ref[1:2, ls] * x
              + cw_ref[2:3, ls] * jnp.where(has_p1, pltpu.roll(x, rows - 1, 0), 0.0)
              + cw_ref[3:4, ls] * jnp.where(has_p2, pltpu.roll(x, rows - 2, 0), 0.0)
              + cb_ref[0:1, ls])
        xcb = xc.astype(BF16)
        for d, (a_ref, u_ref) in enumerate(((af_ref, uf_ref), (ab_ref, ub_ref))):
            r = _sigmoid(jnp.dot(xcb, wa_ref[d, nb], preferred_element_type=F32) + ba_ref[d:d + 1, ls])
            gi = _sigmoid(jnp.dot(xcb, wi_ref[d, nb], preferred_element_type=F32) + bi_ref[d:d + 1, ls])
            log_a = (-RG_C) * r * _softplus(-lam_ref[d:d + 1, ls])
            a_ref[nb] = jnp.exp(log_a)
            u_ref[nb] = jnp.sqrt(-_expm1(2.0 * log_a)) * (gi * xc)

    nbt = RG_CT // RG_BLOCK

    def step(i, carry):
        sf = pl.ds(i, SCAN_V, stride=SCAN_L)
        sb = pl.ds(SCAN_L - 1 - i, SCAN_V, stride=SCAN_L)
        out = []
        for nb in range(nbt):
            hf, pf, hb, pb = carry[4 * nb:4 * nb + 4]
            a = af_ref[nb, sf, :]
            hf = a * hf + uf_ref[nb, sf, :]
            pf = a * pf
            uf_ref[nb, sf, :] = hf
            af_ref[nb, sf, :] = pf
            a = ab_ref[nb, sb, :]
            hb = a * hb + ub_ref[nb, sb, :]
            pb = a * pb
            ub_ref[nb, sb, :] = hb
            ab_ref[nb, sb, :] = pb
            out += [hf, pf, hb, pb]
        return tuple(out)

    zero = jnp.zeros((SCAN_V, RG_BLOCK), F32)
    one = jnp.ones((SCAN_V, RG_BLOCK), F32)
    lax.fori_loop(0, SCAN_L, step, (zero, one, zero, one) * nbt)

    for nb in range(nbt):
        ls = slice(nb * RG_BLOCK, (nb + 1) * RG_BLOCK)
        e = chain * h0_ref[0:1, ls]
        for v in range(SCAN_V):
            seg = slice(v * SCAN_L, (v + 1) * SCAN_L)
            last = (v + 1) * SCAN_L - 1
            e_next = chain * (uf_ref[nb, last:last + 1, :] + af_ref[nb, last:last + 1, :] * e)
            uf_ref[nb, seg, :] = uf_ref[nb, seg, :] + af_ref[nb, seg, :] * e
            e = e_next
        e = chain * h0_ref[1:2, ls]
        for v in reversed(range(SCAN_V)):
            seg = slice(v * SCAN_L, (v + 1) * SCAN_L)
            first = v * SCAN_L
            e_next = chain * (ub_ref[nb, first:first + 1, :] + ab_ref[nb, first:first + 1, :] * e)
            ub_ref[nb, seg, :] = ub_ref[nb, seg, :] + ab_ref[nb, seg, :] * e
            e = e_next
        stf_ref[:, ls] = uf_ref[nb, pl.ds(SCAN_L - 1, SCAN_V, stride=SCAN_L), :]
        stb_ref[:, ls] = ub_ref[nb, pl.ds(0, SCAN_V, stride=SCAN_L), :]
        y_ref[:, ls] = (uf_ref[nb] + ub_ref[nb]) * gate_ref[:, ls]


def _rg_scan(xbr, gate, conv_w, conv_b, w_a, b_a, w_i, b_i, lam, h0):
    n_blk = N_TOK // GROUP_ROWS
    nbt = RG_CT // RG_BLOCK
    row_tile = pl.BlockSpec((GROUP_ROWS, RG_CT), lambda b, c: (b, c))
    vec2 = pl.BlockSpec((2, RG_CT), lambda b, c: (0, c))
    wspec = pl.BlockSpec((2, nbt, RG_BLOCK, RG_BLOCK), lambda b, c: (0, c, 0, 0))
    return pl.pallas_call(
        _rg_scan_kernel,
        grid=(n_blk, D_RNN // RG_CT),
        in_specs=[
            row_tile, row_tile,
            pl.BlockSpec((CONV_W, RG_CT), lambda b, c: (0, c)),
            pl.BlockSpec((1, RG_CT), lambda b, c: (0, c)),
            wspec, vec2, wspec, vec2, vec2,
            pl.BlockSpec((None, 2, RG_CT), lambda b, c: (jnp.maximum(b - PROMPT_BLOCKS, 0), 0, c)),
        ],
        out_specs=[row_tile] + [pl.BlockSpec((SCAN_V, RG_CT), lambda b, c: (b, c))] * 2,
        out_shape=[jax.ShapeDtypeStruct((N_TOK, D_RNN), F32)]
        + [jax.ShapeDtypeStruct((n_blk * SCAN_V, D_RNN), F32)] * 2,
        scratch_shapes=[pltpu.VMEM((nbt, GROUP_ROWS, RG_BLOCK), F32)] * 4,
        compiler_params=_cparams("arbitrary", "arbitrary"),
        name="rg_scan",
    )(xbr, gate, conv_w, conv_b, w_a, b_a, w_i, b_i, lam, h0)


def _gla_proj_kernel(x_ref, mod_ref, gain_ref, w_ref, wz_ref, w2_ref, bg_ref,
                     q_ref, k_ref, v_ref, r_ref, lg_ref):
    hb = _modulated(x_ref[...], mod_ref, gain_ref, 1).astype(BF16)
    q_ref[...] = jnp.dot(hb, w_ref[:, :GLA_QK], preferred_element_type=F32) * (GLA_DK ** -0.5)
    k_ref[...] = jnp.dot(hb, w_ref[:, GLA_QK:2 * GLA_QK], preferred_element_type=F32)
    v_ref[...] = jnp.dot(hb, w_ref[:, 2 * GLA_QK:2 * GLA_QK + GLA_V], preferred_element_type=F32)
    r = jnp.dot(hb, w_ref[:, 2 * GLA_QK + GLA_V:], preferred_element_type=F32)
    r_ref[...] = r * _sigmoid(r)
    z = jnp.dot(hb, wz_ref[...], preferred_element_type=F32).astype(BF16)
    zz = jnp.dot(z, w2_ref[...], preferred_element_type=F32) + bg_ref[...]
    lg_ref[...] = -_softplus(-zz) * (1.0 / GLA_TAU)


def _gla_proj(x, mods_l, gains_l, w_main, w_z, w2cat, bg):
    def rows(w):
        return pl.BlockSpec((TM, w), lambda i: (i, 0))

    def whole(a):
        return pl.BlockSpec(a.shape, lambda i: (0,) * a.ndim, pipeline_mode=pl.Buffered(1))

    widths = (GLA_QK, GLA_QK, GLA_V, GLA_V, 2 * GLA_QK)
    return pl.pallas_call(
        _gla_proj_kernel,
        grid=(N_TOK // TM,),
        in_specs=[
            rows(D_MODEL),
            pl.BlockSpec((None, N_MOD, D_MODEL), lambda i: (_group_of_tile(i, TM), 0, 0)),
            pl.BlockSpec((3, D_MODEL), lambda i: (0, 0)),
            whole(w_main), whole(w_z), whole(w2cat), whole(bg),
        ],
        out_specs=[rows(w) for w in widths],
        out_shape=[jax.ShapeDtypeStruct((N_TOK, w), F32) for w in widths],
        compiler_params=_cparams("arbitrary"),
        name="gla_in_proj",
    )(x, mods_l, gains_l, w_main, w_z, w2cat, bg)


def _split3(x):
    hi = x.astype(BF16)
    r1 = x - hi.astype(F32)
    mid = r1.astype(BF16)
    lo = (r1 - mid.astype(F32)).astype(BF16)
    return hi, mid, lo


def _dot_tn(a, b):
    return lax.dot_general(a, b, (((0,), (0,)), ((), ())), preferred_element_type=F32)


def _dot_nt(a, b):
    return lax.dot_general(a, b, (((1,), (1,)), ((), ())), preferred_element_type=F32)


def _level_ref_rows(cum, m, backward):
    c = cum.shape[0]
    pick = m - 1 if backward else m
    if 2 * m >= 8:
        c3 = cum.reshape(c // (2 * m), 2 * m, cum.shape[1])
        return jnp.broadcast_to(c3[:, pick:pick + 1, :], c3.shape).reshape(cum.shape)
    pos = lax.broadcasted_iota(jnp.int32, cum.shape, 0) & (2 * m - 1)
    out = cum
    for p in range(2 * m):
        if p != pick:
            out = jnp.where(pos == p, pltpu.roll(cum, (p - pick) % c, 0), out)
    return out


def _gla_chunk(q, k, v, lg, s, masks, tri, ones_c, backward):
    c = q.shape[0]
    parts = _split3(lg)
    cum = sum(jnp.dot(tri, p, preferred_element_type=F32) for p in parts)
    tot_col = sum(_dot_tn(p, ones_c) for p in parts)
    tot_row = cum[0:1, :] if backward else cum[c - 1:c, :]
    sb = s.astype(BF16)
    o = jnp.dot((q * jnp.exp(cum)).astype(BF16), sb, preferred_element_type=F32)
    kd = (k * jnp.exp(tot_row - cum)).astype(BF16)
    vb = v.astype(BF16)
    decay = jnp.exp(tot_col)
    s_new = jnp.concatenate([decay, decay], axis=1) * s + _dot_tn(kd, vb)

    scores = jnp.where(masks[0], _dot_nt(q.astype(BF16), k.astype(BF16)), 0.0)
    m = c // 2
    lvl = 1
    while m >= 1:
        e = jnp.exp(-jnp.abs(cum - _level_ref_rows(cum, m, backward)))
        scores = scores + jnp.where(masks[lvl], _dot_nt((q * e).astype(BF16), (k * e).astype(BF16)), 0.0)
        m //= 2
        lvl += 1
    o = o + jnp.dot(scores.astype(BF16), vb, preferred_element_type=F32)
    return o, s_new


def _gla_kernel(*refs, t_len, has_s0):
    if has_s0:
        q_ref, k_ref, v_ref, lgf_ref, lgb_ref, r_ref, gn_ref, s0_ref, y_ref, sf_ref, sb_ref = refs
    else:
        q_ref, k_ref, v_ref, lgf_ref, lgb_ref, r_ref, gn_ref, y_ref, sout_ref, sf_ref, sb_ref = refs
    c = GLA_C
    n_chunks = t_len // c
    ri = lax.broadcasted_iota(jnp.int32, (c, c), 0)
    ci = lax.broadcasted_iota(jnp.int32, (c, c), 1)
    x = ri ^ ci
    levels = [c >> (l + 1) for l in range(c.bit_length() - 1)]
    masks_f = [ri == ci] + [(ri > ci) & (x >= m) & (x < 2 * m) for m in levels]
    masks_b = [ri == ci] + [(ri < ci) & (x >= m) & (x < 2 * m) for m in levels]
    tri_f = (ri >= ci).astype(BF16)
    tri_b = (ri <= ci).astype(BF16)
    ones_c = jnp.ones((c, 128), BF16)

    if has_s0:
        sf_ref[...] = s0_ref[0]
        sb_ref[...] = s0_ref[1]
    else:
        sf_ref[...] = jnp.zeros_like(sf_ref)
        sb_ref[...] = jnp.zeros_like(sb_ref)
    y_ref[...] = jnp.zeros_like(y_ref)

    def body(i, carry):
        rf = pl.ds(pl.multiple_of(i * c, c), c)
        rb = pl.ds(pl.multiple_of((n_chunks - 1 - i) * c, c), c)
        o, s_new = _gla_chunk(q_ref[rf, :], k_ref[rf, :], v_ref[rf, :], lgf_ref[rf, :], sf_ref[...],
                              masks_f, tri_f, ones_c, False)
        y_ref[rf, :] += o
        sf_ref[...] = s_new
        o, s_new = _gla_chunk(q_ref[rb, :], k_ref[rb, :], v_ref[rb, :], lgb_ref[rb, :], sb_ref[...],
                              masks_b, tri_b, ones_c, True)
        y_ref[rb, :] += o
        sb_ref[...] = s_new
        return carry

    lax.fori_loop(0, n_chunks, body, 0)

    if not has_s0:
        sout_ref[0] = sf_ref[...]
        sout_ref[1] = sb_ref[...]
    o = y_ref[...]
    y_ref[...] = _rms(o) * gn_ref[...] * r_ref[...]


def _gla_mix(q, k, v, lg, r, gnorm, s0, *, t_len, row_block0, n_seq):
    has_s0 = s0 is not None

    def rows(w, off=0):
        return pl.BlockSpec((t_len, w), lambda s, h: (row_block0 + s, off + h))

    state_spec = pl.BlockSpec((None, 2, None, GLA_DK, GLA_DV), lambda s, h: (s, 0, h, 0, 0))
    in_specs = [rows(GLA_DK), rows(GLA_DK), rows(GLA_DV), rows(GLA_DK), rows(GLA_DK, GLA_HEADS),
                rows(GLA_DV), pl.BlockSpec((None, 1, GLA_DV), lambda s, h: (h, 0, 0))]
    args = [q, k, v, lg, lg, r, gnorm.reshape(GLA_HEADS, 1, GLA_DV)]
    y_spec = pl.BlockSpec((t_len, GLA_DV), lambda s, h: (s, h))
    y_shape = jax.ShapeDtypeStruct((n_seq * t_len, GLA_V), F32)
    if has_s0:
        in_specs.append(state_spec)
        args.append(s0)
        out_specs, out_shape = y_spec, y_shape
    else:
        out_specs = [y_spec, state_spec]
        out_shape = [y_shape, jax.ShapeDtypeStruct((n_seq, 2, GLA_HEADS, GLA_DK, GLA_DV), F32)]
    return pl.pallas_call(
        functools.partial(_gla_kernel, t_len=t_len, has_s0=has_s0),
        grid=(n_seq, GLA_HEADS),
        in_specs=in_specs,
        out_specs=out_specs,
        out_shape=out_shape,
        scratch_shapes=[pltpu.VMEM((GLA_DK, GLA_DV), F32)] * 2,
        compiler_params=_cparams("arbitrary", "arbitrary"),
        name="gla_mix_sample" if has_s0 else "gla_mix_prompt",
    )(*args)


def _gla_out_kernel(yp_ref, ys_ref, w_ref, x_ref, mod_ref, o_ref):
    is_prompt = pl.program_id(0) < N_PROMPT_TOK // TM
    y = jnp.where(is_prompt, yp_ref[...], ys_ref[...]).astype(BF16)
    o_ref[...] = x_ref[...] + mod_ref[5:6, :] * jnp.dot(y, w_ref[...], preferred_element_type=F32)


def _gla_out_proj(y_p, y_s, w_out, x, mods_l):
    n_p = N_PROMPT_TOK // TM
    return pl.pallas_call(
        _gla_out_kernel,
        grid=(N_TOK // TM,),
        in_specs=[
            pl.BlockSpec((TM, GLA_V), lambda i: (jnp.minimum(i, n_p - 1), 0)),
            pl.BlockSpec((TM, GLA_V), lambda i: (jnp.maximum(i - n_p, 0), 0)),
            pl.BlockSpec((GLA_V, D_MODEL), lambda i: (0, 0), pipeline_mode=pl.Buffered(1)),
            pl.BlockSpec((TM, D_MODEL), lambda i: (i, 0)),
            pl.BlockSpec((None, N_MOD, D_MODEL), lambda i: (_group_of_tile(i, TM), 0, 0)),
        ],
        out_specs=pl.BlockSpec((TM, D_MODEL), lambda i: (i, 0)),
        out_shape=jax.ShapeDtypeStruct((N_TOK, D_MODEL), F32),
        compiler_params=_cparams("arbitrary"),
        name="gla_out_proj",
    )(y_p, y_s, w_out, x, mods_l)


def _sample_to_col_major(x):
    xs = x[N_PROMPT_TOK:].reshape(DEC_BATCH, DEC_SEQ // GRID_W, GRID_W, D_MODEL)
    xs = xs.swapaxes(1, 2).reshape(DEC_BATCH * DEC_SEQ, D_MODEL)
    return jnp.concatenate([x[:N_PROMPT_TOK], xs], axis=0)


def _sample_from_col_major(xs):
    xs = xs.reshape(DEC_BATCH, GRID_W, DEC_SEQ // GRID_W, D_MODEL)
    return xs.swapaxes(1, 2).reshape(DEC_BATCH, DEC_SEQ, D_MODEL)


def kernel(x_prompt, x_sample, c, state_rglru, state_gla, c_ctx, ada_w, ada_b, norm_g, ffn_w_in, ffn_w_out,
           rg_w_in, rg_conv_w, rg_conv_b, rg_w_a, rg_b_a, rg_w_i, rg_b_i, rg_lambda, rg_w_out,
           gla_w_in, gla_w_g2, gla_b_g, gla_norm, gla_w_out, final_norm):
    depth = ada_w.shape[0]
    x = jnp.concatenate([x_prompt.reshape(N_PROMPT_TOK, D_MODEL),
                         x_sample.reshape(DEC_BATCH * DEC_SEQ, D_MODEL)], axis=0)
    cvecs = jnp.concatenate([c_ctx[None, :], c, jnp.zeros((8 - N_GROUPS, D_MODEL), F32)], axis=0)
    mods = _ada_mods(cvecs, ada_w, ada_b)
    fn = final_norm.reshape(1, D_MODEL)
    rg_states, gla_states = [], []

    for i in range(depth):
        mods_l, gains_l = mods[i], norm_g[i]
        j = i // 2
        if i % 2 == 1:
            x = _sample_to_col_major(x)
        x = _half_ffn(x, mods_l, gains_l, ffn_w_in[i, 0].astype(BF16), ffn_w_out[i, 0].astype(BF16), fn, 0, False)
        if i % 2 == 0:
            gate, xbr = _rg_proj(x, mods_l, gains_l, rg_w_in[j].astype(BF16))
            y, st_f, st_b = _rg_scan(xbr, gate, rg_conv_w[j], rg_conv_b[j].reshape(1, D_RNN),
                             rg_w_a[j].astype(BF16), rg_b_a[j], rg_w_i[j].astype(BF16), rg_b_i[j],
                             rg_lambda[j], state_rglru[:, j])
            rg_states.append(jnp.stack([st_f[:BATCH], st_b[:BATCH]], axis=1))
            x = _out_proj(y, rg_w_out[j].astype(BF16), x, mods_l)
        else:
            w_in = gla_w_in[j]
            n_main = 2 * GLA_QK + 2 * GLA_V
            w_z = jnp.pad(w_in[:, n_main:], ((0, 0), (0, 128 - 2 * GLA_RANK))).astype(BF16)
            w2cat = jnp.zeros((128, 2 * GLA_QK), F32)
            w2cat = w2cat.at[:GLA_RANK, :GLA_QK].set(gla_w_g2[j, 0])
            w2cat = w2cat.at[GLA_RANK:2 * GLA_RANK, GLA_QK:].set(gla_w_g2[j, 1]).astype(BF16)
            q, k, v, r, lg = _gla_proj(x, mods_l, gains_l, w_in[:, :n_main].astype(BF16), w_z, w2cat,
                                       gla_b_g[j].reshape(1, 2 * GLA_QK))
            y_p, st = _gla_mix(q, k, v, lg, r, gla_norm[j], None, t_len=SEQ, row_block0=0, n_seq=BATCH)
            y_s = _gla_mix(q, k, v, lg, r, gla_norm[j], state_gla[:, j], t_len=DEC_SEQ,
                           row_block0=N_PROMPT_TOK // DEC_SEQ, n_seq=DEC_BATCH)
            gla_states.append(st)
            x = _gla_out_proj(y_p, y_s, gla_w_out[j].astype(BF16), x, mods_l)
        x = _half_ffn(x, mods_l, gains_l, ffn_w_in[i, 1].astype(BF16), ffn_w_out[i, 1].astype(BF16), fn, 2,
                      i == depth - 1)
        if i % 2 == 1:
            x = jnp.concatenate(
                [x[:N_PROMPT_TOK], _sample_from_col_major(x[N_PROMPT_TOK:]).reshape(-1, D_MODEL)], axis=0)

    y_prompt = x[:N_PROMPT_TOK].reshape(BATCH, SEQ, D_MODEL)
    y_sample = x[N_PROMPT_TOK:].reshape(DEC_BATCH, DEC_SEQ, D_MODEL)
    return (y_prompt, y_sample, jnp.stack(rg_states, axis=1), jnp.stack(gla_states, axis=1))
```

```python
import functools

import jax
import jax.numpy as jnp
from jax import lax
from jax.experimental import pallas as pl
from jax.experimental.pallas import tpu as pltpu

F32 = jnp.float32
BF16 = jnp.bfloat16

D_MODEL = 1024
BATCH = 16
SEQ = 256
DEC_BATCH = 2
DEC_SEQ = 2048
GRID_W = 64
EPS = 1e-6
N_MOD = 9
D_FF = 2816
FFN_RES = 0.5
D_RNN = 1280
RG_BLOCKS = 10
RG_BLOCK = 128
CONV_W = 4
RG_C = 8.0
GLA_HEADS = 4
GLA_QK = 512
GLA_V = 1024
GLA_DK = 128
GLA_DV = 256
GLA_RANK = 16
GLA_TAU = 16.0

N_PROMPT_TOK = BATCH * SEQ
N_TOK = N_PROMPT_TOK + DEC_BATCH * DEC_SEQ
N_GROUPS = 1 + DEC_BATCH
GROUP_ROWS = DEC_SEQ
PROMPT_BLOCKS = N_PROMPT_TOK // GROUP_ROWS

VMEM_LIMIT_BYTES = 56 * 1024 * 1024

TM = 512
FF_CHUNK = 1408
RG_CT = 256
SCAN_L = 256
SCAN_V = GROUP_ROWS // SCAN_L
GLA_C = 128
GLA_LEVELS = tuple(GLA_C >> (l + 1) for l in range(GLA_C.bit_length() - 1))


def _cparams(*sem):
    return pltpu.CompilerParams(dimension_semantics=sem, vmem_limit_bytes=VMEM_LIMIT_BYTES)


def _group_of_tile(i, rows):
    return jnp.maximum((i * rows) // GROUP_ROWS - (PROMPT_BLOCKS - 1), 0)


def _sigmoid(x):
    return 1.0 / (1.0 + jnp.exp(-x))


def _softplus(x):
    return jnp.maximum(x, 0.0) + jnp.log1p(jnp.exp(-jnp.abs(x)))


def _expm1(x):
    u = jnp.exp(x)
    near = jnp.where(u == 1.0, x, (u - 1.0) * x / jnp.log(u))
    return jnp.where(jnp.abs(u - 1.0) < 0.5, near, u - 1.0)


def _rms(x):
    return x * lax.rsqrt(jnp.mean(x * x, axis=-1, keepdims=True) + EPS)


def _modulated(x, mod_ref, gain_ref, j):
    y = _rms(x) * gain_ref[j:j + 1, :]
    return y * (1.0 + mod_ref[3 * j + 1:3 * j + 2, :]) + mod_ref[3 * j:3 * j + 1, :]


def _ada_kernel(c_ref, w_ref, b_ref, o_ref):
    c = c_ref[...]
    s = (c * _sigmoid(c)).astype(BF16)
    o_ref[...] = jnp.dot(s, w_ref[...].astype(BF16), preferred_element_type=F32) + b_ref[...]


def _ada_mods(cvecs, ada_w, ada_b):
    depth = ada_w.shape[0]
    out = pl.pallas_call(
        _ada_kernel,
        grid=(depth, N_MOD),
        in_specs=[
            pl.BlockSpec((8, D_MODEL), lambda l, k: (0, 0)),
            pl.BlockSpec((None, D_MODEL, D_MODEL), lambda l, k: (l, 0, k)),
            pl.BlockSpec((None, None, 1, D_MODEL), lambda l, k: (l, k, 0, 0)),
        ],
        out_specs=pl.BlockSpec((None, None, 8, D_MODEL), lambda l, k: (l, k, 0, 0)),
        out_shape=jax.ShapeDtypeStruct((depth, N_MOD, 8, D_MODEL), F32),
        compiler_params=_cparams("arbitrary", "arbitrary"),
        name="ada_mod",
    )(cvecs, ada_w, ada_b.reshape(depth, N_MOD, 1, D_MODEL))
    return out[:, :, :N_GROUPS, :].transpose(0, 2, 1, 3)


def _ffn_kernel(x_ref, mod_ref, gain_ref, win_ref, wout_ref, fn_ref, o_ref, *, j, final):
    x = x_ref[...]
    hb = _modulated(x, mod_ref, gain_ref, j).astype(BF16)
    acc = jnp.zeros((TM, D_MODEL), F32)
    for c in range(D_FF // FF_CHUNK):
        g = jnp.dot(hb, win_ref[:, c * FF_CHUNK:(c + 1) * FF_CHUNK], preferred_element_type=F32)
        u = jnp.dot(hb, win_ref[:, D_FF + c * FF_CHUNK:D_FF + (c + 1) * FF_CHUNK],
                    preferred_element_type=F32)
        a = (g * _sigmoid(g) * u).astype(BF16)
        acc = acc + jnp.dot(a, wout_ref[c * FF_CHUNK:(c + 1) * FF_CHUNK, :], preferred_element_type=F32)
    out = x + FFN_RES * mod_ref[3 * j + 2:3 * j + 3, :] * acc
    if final:
        out = _rms(out) * fn_ref[...]
    o_ref[...] = out


def _half_ffn(x, mods_l, gains_l, w_in, w_out, layer, half, final_norm, final):
    j = 2 * half
    return pl.pallas_call(
        functools.partial(_ffn_kernel, j=j, final=final),
        grid=(N_TOK // TM,),
        in_specs=[
            pl.BlockSpec((TM, D_MODEL), lambda i: (i, 0)),
            pl.BlockSpec((None, N_MOD, D_MODEL), lambda i: (_group_of_tile(i, TM), 0, 0)),
            pl.BlockSpec((3, D_MODEL), lambda i: (0, 0)),
            pl.BlockSpec((None, None, D_MODEL, 2 * D_FF), lambda i: (layer, half, 0, 0),
                         pipeline_mode=pl.Buffered(1)),
            pl.BlockSpec((None, None, D_FF, D_MODEL), lambda i: (layer, half, 0, 0),
                         pipeline_mode=pl.Buffered(1)),
            pl.BlockSpec((1, D_MODEL), lambda i: (0, 0)),
        ],
        out_specs=pl.BlockSpec((TM, D_MODEL), lambda i: (i, 0)),
        out_shape=jax.ShapeDtypeStruct((N_TOK, D_MODEL), F32),
        compiler_params=_cparams("arbitrary"),
        name="half_ffn",
    )(x, mods_l, gains_l, w_in, w_out, final_norm)


def _gelu_tanh(x):
    return 0.5 * x * (1.0 + jnp.tanh(0.7978845608028654 * (x + 0.044715 * (x * x * x))))


def _rg_proj_kernel(x_ref, mod_ref, gain_ref, w_ref, gate_ref, xbr_ref):
    hb = _modulated(x_ref[...], mod_ref, gain_ref, 1).astype(BF16)
    gate = _gelu_tanh(jnp.dot(hb, w_ref[:, :D_RNN], preferred_element_type=F32))
    xbr = jnp.dot(hb, w_ref[:, D_RNN:], preferred_element_type=F32)
    for s in range(TM // SCAN_L):
        rows = slice(s * SCAN_L, (s + 1) * SCAN_L)
        lanes = slice(s * D_RNN, (s + 1) * D_RNN)
        gate_ref[:, lanes] = gate[rows]
        xbr_ref[:, lanes] = xbr[rows]


def _rg_proj(x, mods_l, gains_l, w_in):
    seg_per_tile = TM // SCAN_L
    tiles_per_blk = SCAN_V // seg_per_tile
    out_spec = pl.BlockSpec((SCAN_L, seg_per_tile * D_RNN), lambda i: (i // tiles_per_blk, i % tiles_per_blk))
    out_shape = jax.ShapeDtypeStruct((N_TOK // SCAN_V, SCAN_V * D_RNN), F32)
    return pl.pallas_call(
        _rg_proj_kernel,
        grid=(N_TOK // TM,),
        in_specs=[
            pl.BlockSpec((TM, D_MODEL), lambda i: (i, 0)),
            pl.BlockSpec((None, N_MOD, D_MODEL), lambda i: (_group_of_tile(i, TM), 0, 0)),
            pl.BlockSpec((3, D_MODEL), lambda i: (0, 0)),
            pl.BlockSpec((D_MODEL, 2 * D_RNN), lambda i: (0, 0), pipeline_mode=pl.Buffered(1)),
        ],
        out_specs=[out_spec] * 2,
        out_shape=[out_shape] * 2,
        compiler_params=_cparams("arbitrary"),
        name="rg_in_proj",
    )(x, mods_l, gains_l, w_in)


def _seg_shift(row, down):
    v = lax.broadcasted_iota(jnp.int32, row.shape, 0)
    if down:
        return jnp.where(v >= 1, pltpu.roll(row, 1, 0), 0.0)
    return jnp.where(v < SCAN_V - 1, pltpu.roll(row, SCAN_V - 1, 0), 0.0)


def _rg_scan_kernel(xbr_ref, gate_ref, cw_ref, cb_ref, wa_ref, ba_ref, wi_ref, bi_ref, lam_ref, h0_ref,
                    y_ref, stf_ref, stb_ref, af_ref, uf_ref, ab_ref, ub_ref):
    chain = jnp.where(pl.program_id(0) >= PROMPT_BLOCKS, 1.0, 0.0).astype(F32)
    nbt = RG_CT // RG_BLOCK
    rows = SCAN_L * SCAN_V

    for nb in range(nbt):
        ls = slice(nb * RG_BLOCK, (nb + 1) * RG_BLOCK)
        x = xbr_ref[:, :, ls]
        first, second, last = x[0], x[1], x[SCAN_L - 1]
        xm1 = jnp.concatenate([(chain * _seg_shift(last, True))[None], x[:-1]], axis=0)
        xp1 = jnp.concatenate([x[1:], (chain * _seg_shift(first, False))[None]], axis=0)
        xp2 = jnp.concatenate([x[2:], (chain * _seg_shift(first, False))[None],
                               (chain * _seg_shift(second, False))[None]], axis=0)
        xc = (cw_ref[0:1, ls] * xm1 + cw_ref[1:2, ls] * x + cw_ref[2:3, ls] * xp1 + cw_ref[3:4, ls] * xp2
              + cb_ref[0:1, ls]).reshape(rows, RG_BLOCK)
        xcb = xc.astype(BF16)
        for d, (a_ref, u_ref) in enumerate(((af_ref, uf_ref), (ab_ref, ub_ref))):
            r = _sigmoid(jnp.dot(xcb, wa_ref[d, nb], preferred_element_type=F32) + ba_ref[d:d + 1, ls])
            gi = _sigmoid(jnp.dot(xcb, wi_ref[d, nb], preferred_element_type=F32) + bi_ref[d:d + 1, ls])
            log_a = (-RG_C) * r * _softplus(-lam_ref[d:d + 1, ls])
            a_ref[nb] = jnp.exp(log_a)
            u_ref[nb] = jnp.sqrt(-_expm1(2.0 * log_a)) * (gi * xc)

    def step(i, carry):
        sf = pl.ds(pl.multiple_of(i * SCAN_V, SCAN_V), SCAN_V)
        sb = pl.ds(pl.multiple_of((SCAN_L - 1 - i) * SCAN_V, SCAN_V), SCAN_V)
        out = []
        for nb in range(nbt):
            hf, pf, hb, pb = carry[4 * nb:4 * nb + 4]
            a = af_ref[nb, sf, :]
            hf = a * hf + uf_ref[nb, sf, :]
            pf = a * pf
            uf_ref[nb, sf, :] = hf
            af_ref[nb, sf, :] = pf
            a = ab_ref[nb, sb, :]
            hb = a * hb + ub_ref[nb, sb, :]
            pb = a * pb
            ub_ref[nb, sb, :] = hb
            ab_ref[nb, sb, :] = pb
            out += [hf, pf, hb, pb]
        return tuple(out)

    zero = jnp.zeros((SCAN_V, RG_BLOCK), F32)
    one = jnp.ones((SCAN_V, RG_BLOCK), F32)
    lax.fori_loop(0, SCAN_L, step, (zero, one, zero, one) * nbt, unroll=4)

    end = slice(rows - SCAN_V, rows)
    start = slice(0, SCAN_V)
    for nb in range(nbt):
        ls = slice(nb * RG_BLOCK, (nb + 1) * RG_BLOCK)
        h_end, p_end = uf_ref[nb, end, :], af_ref[nb, end, :]
        entry = [chain * h0_ref[0:1, ls]]
        for v in range(SCAN_V - 1):
            entry.append(chain * (h_end[v:v + 1] + p_end[v:v + 1] * entry[v]))
        ef = jnp.concatenate(entry, axis=0)
        h_start, p_start = ub_ref[nb, start, :], ab_ref[nb, start, :]
        entry = [chain * h0_ref[1:2, ls]]
        for v in range(SCAN_V - 1, 0, -1):
            entry.append(chain * (h_start[v:v + 1] + p_start[v:v + 1] * entry[-1]))
        eb = jnp.concatenate(entry[::-1], axis=0)
        hf = uf_ref[nb].reshape(SCAN_L, SCAN_V, RG_BLOCK) + af_ref[nb].reshape(SCAN_L, SCAN_V, RG_BLOCK) * ef[None]
        hb = ub_ref[nb].reshape(SCAN_L, SCAN_V, RG_BLOCK) + ab_ref[nb].reshape(SCAN_L, SCAN_V, RG_BLOCK) * eb[None]
        stf_ref[:, ls] = hf[SCAN_L - 1]
        stb_ref[:, ls] = hb[0]
        y_ref[:, :, ls] = (hf + hb) * gate_ref[:, :, ls]


def _rg_scan(xbr, gate, conv_w, conv_b, w_a, b_a, w_i, b_i, lam, h0):
    n_blk = N_TOK // GROUP_ROWS
    nbt = RG_CT // RG_BLOCK
    scan_shape = (n_blk * SCAN_L, SCAN_V, D_RNN)
    tile = pl.BlockSpec((SCAN_L, SCAN_V, RG_CT), lambda b, c: (b, 0, c))
    vec2 = pl.BlockSpec((2, RG_CT), lambda b, c: (0, c))
    wspec = pl.BlockSpec((2, nbt, RG_BLOCK, RG_BLOCK), lambda b, c: (0, c, 0, 0))
    y, st_f, st_b = pl.pallas_call(
        _rg_scan_kernel,
        grid=(n_blk, D_RNN // RG_CT),
        in_specs=[
            tile, tile,
            pl.BlockSpec((CONV_W, RG_CT), lambda b, c: (0, c)),
            pl.BlockSpec((1, RG_CT), lambda b, c: (0, c)),
            wspec, vec2, wspec, vec2, vec2,
            pl.BlockSpec((None, 2, RG_CT), lambda b, c: (jnp.maximum(b - PROMPT_BLOCKS, 0), 0, c)),
        ],
        out_specs=[tile] + [pl.BlockSpec((SCAN_V, RG_CT), lambda b, c: (b, c))] * 2,
        out_shape=[jax.ShapeDtypeStruct(scan_shape, F32)]
        + [jax.ShapeDtypeStruct((n_blk * SCAN_V, D_RNN), F32)] * 2,
        scratch_shapes=[pltpu.VMEM((nbt, GROUP_ROWS, RG_BLOCK), F32)] * 4,
        compiler_params=_cparams("arbitrary", "arbitrary"),
        name="rg_scan",
    )(xbr.reshape(scan_shape), gate.reshape(scan_shape), conv_w, conv_b, w_a, b_a, w_i, b_i, lam, h0)
    return y.reshape(n_blk * SCAN_L, SCAN_V * D_RNN), st_f, st_b


def _rg_out_kernel(y_ref, w_ref, x_ref, mod_ref, o_ref):
    y = jnp.concatenate([y_ref[:, s * D_RNN:(s + 1) * D_RNN] for s in range(TM // SCAN_L)], axis=0)
    upd = jnp.dot(y.astype(BF16), w_ref[...], preferred_element_type=F32)
    o_ref[...] = x_ref[...] + mod_ref[5:6, :] * upd


def _rg_out_proj(y, w_out, x, mods_l):
    seg_per_tile = TM // SCAN_L
    tiles_per_blk = SCAN_V // seg_per_tile
    return pl.pallas_call(
        _rg_out_kernel,
        grid=(N_TOK // TM,),
        in_specs=[
            pl.BlockSpec((SCAN_L, seg_per_tile * D_RNN), lambda i: (i // tiles_per_blk, i % tiles_per_blk)),
            pl.BlockSpec((D_RNN, D_MODEL), lambda i: (0, 0), pipeline_mode=pl.Buffered(1)),
            pl.BlockSpec((TM, D_MODEL), lambda i: (i, 0)),
            pl.BlockSpec((None, N_MOD, D_MODEL), lambda i: (_group_of_tile(i, TM), 0, 0)),
        ],
        out_specs=pl.BlockSpec((TM, D_MODEL), lambda i: (i, 0)),
        out_shape=jax.ShapeDtypeStruct((N_TOK, D_MODEL), F32),
        compiler_params=_cparams("arbitrary"),
        name="rg_out_proj",
    )(y, w_out, x, mods_l)


def _chunk_tri():
    ri = lax.broadcasted_iota(jnp.int32, (GLA_C, GLA_C), 0)
    ci = lax.broadcasted_iota(jnp.int32, (GLA_C, GLA_C), 1)
    return (ri >= ci).astype(BF16), (ri <= ci).astype(BF16)


def _split3(x):
    hi = x.astype(BF16)
    r1 = x - hi.astype(F32)
    mid = r1.astype(BF16)
    lo = (r1 - mid.astype(F32)).astype(BF16)
    return hi, mid, lo


def _gla_proj_kernel(x_ref, mod_ref, gain_ref, w_ref, wz_ref, w2_ref, bg_ref,
                     q_ref, k_ref, v_ref, r_ref, cum_ref):
    hb = _modulated(x_ref[...], mod_ref, gain_ref, 1).astype(BF16)
    q_ref[...] = jnp.dot(hb, w_ref[:, :GLA_QK], preferred_element_type=F32) * (GLA_DK ** -0.5)
    k_ref[...] = jnp.dot(hb, w_ref[:, GLA_QK:2 * GLA_QK], preferred_element_type=F32)
    v_ref[...] = jnp.dot(hb, w_ref[:, 2 * GLA_QK:2 * GLA_QK + GLA_V], preferred_element_type=F32)
    r = jnp.dot(hb, w_ref[:, 2 * GLA_QK + GLA_V:], preferred_element_type=F32)
    r_ref[...] = r * _sigmoid(r)
    z = jnp.dot(hb, wz_ref[...], preferred_element_type=F32).astype(BF16)
    zz = jnp.dot(z, w2_ref[...], preferred_element_type=F32) + bg_ref[...]
    log_gate = -_softplus(-zz) * (1.0 / GLA_TAU)
    parts = _split3(log_gate)
    tri_f, tri_b = _chunk_tri()
    for ch in range(TM // GLA_C):
        rows = slice(ch * GLA_C, (ch + 1) * GLA_C)
        cum_ref[rows, :GLA_QK] = sum(jnp.dot(tri_f, p[rows, :GLA_QK], preferred_element_type=F32) for p in parts)
        cum_ref[rows, GLA_QK:] = sum(jnp.dot(tri_b, p[rows, GLA_QK:], preferred_element_type=F32) for p in parts)


def _gla_proj(x, mods_l, gains_l, w_main, w_z, w2cat, bg):
    def rows(w):
        return pl.BlockSpec((TM, w), lambda i: (i, 0))

    def whole(a):
        return pl.BlockSpec(a.shape, lambda i: (0,) * a.ndim, pipeline_mode=pl.Buffered(1))

    widths = (GLA_QK, GLA_QK, GLA_V, GLA_V, 2 * GLA_QK)
    return pl.pallas_call(
        _gla_proj_kernel,
        grid=(N_TOK // TM,),
        in_specs=[
            rows(D_MODEL),
            pl.BlockSpec((None, N_MOD, D_MODEL), lambda i: (_group_of_tile(i, TM), 0, 0)),
            pl.BlockSpec((3, D_MODEL), lambda i: (0, 0)),
            whole(w_main), whole(w_z), whole(w2cat), whole(bg),
        ],
        out_specs=[rows(w) for w in widths],
        out_shape=[jax.ShapeDtypeStruct((N_TOK, w), F32) for w in widths],
        compiler_params=_cparams("arbitrary"),
        name="gla_in_proj",
    )(x, mods_l, gains_l, w_main, w_z, w2cat, bg)


def _dot_tn(a, b):
    return lax.dot_general(a, b, (((0,), (0,)), ((), ())), preferred_element_type=F32)


def _dot_nt(a, b):
    return lax.dot_general(a, b, (((1,), (1,)), ((), ())), preferred_element_type=F32)


def _level_ref(cum_ref, row0, cum, m, backward):
    c = GLA_C
    pick = m - 1 if backward else m

    def bcast(r, n):
        return jnp.broadcast_to(cum_ref[pl.ds(row0 + r, 1), :], (n, GLA_DK))

    if m >= 4:
        return jnp.concatenate([bcast(g * 2 * m + pick, 2 * m) for g in range(c // (2 * m))], axis=0)
    pos = lax.broadcasted_iota(jnp.int32, (c, GLA_DK), 0)
    if m == 2:
        lo = jnp.concatenate([bcast(g * 8 + pick, 8) for g in range(c // 8)], axis=0)
        hi = jnp.concatenate([bcast(g * 8 + 4 + pick, 8) for g in range(c // 8)], axis=0)
        return jnp.where((pos & 4) == 0, lo, hi)
    if backward:
        return jnp.where((pos & 1) == 1, pltpu.roll(cum, 1, 0), cum)
    return jnp.where((pos & 1) == 0, pltpu.roll(cum, c - 1, 0), cum)


def _gla_chunk(q_ref, k_ref, v_ref, cum_ref, row0, st, masks, signs, backward):
    c = GLA_C
    rows = pl.ds(row0, c)
    q, k, cum = q_ref[rows, :], k_ref[rows, :], cum_ref[rows, :]
    tot = cum_ref[pl.ds(row0 + (0 if backward else c - 1), 1), :]
    qb, kb, vb = q.astype(BF16), k.astype(BF16), v_ref[rows, :].astype(BF16)
    o = _dot_nt((q * jnp.exp(cum)).astype(BF16), st.astype(BF16))
    kd = (k * jnp.exp(tot - cum)).astype(BF16)
    st_new = st * jnp.exp(tot) + _dot_tn(vb, kd)

    scores = jnp.where(masks[0], _dot_nt(qb, kb), 0.0)
    for lvl, m in enumerate(GLA_LEVELS):
        d = (cum - _level_ref(cum_ref, row0, cum, m, backward)) * signs[lvl]
        e = jnp.exp(d).astype(BF16)
        scores = jnp.where(masks[lvl + 1], _dot_nt(qb * e, kb * e), scores)
    o = o + jnp.dot(scores.astype(BF16), vb, preferred_element_type=F32)
    return o, st_new


def _gla_kernel(*refs, t_len, has_s0):
    if has_s0:
        q_ref, k_ref, v_ref, cf_ref, cb_ref, r_ref, gn_ref, s0_ref, y_ref, sf_ref, sb_ref = refs
    else:
        q_ref, k_ref, v_ref, cf_ref, cb_ref, r_ref, gn_ref, y_ref, sout_ref, sf_ref, sb_ref = refs
    c = GLA_C
    n_chunks = t_len // c
    ri = lax.broadcasted_iota(jnp.int32, (c, c), 0)
    ci = lax.broadcasted_iota(jnp.int32, (c, c), 1)
    x = ri ^ ci
    masks_f = [ri == ci] + [(ri > ci) & (x >= m) & (x < 2 * m) for m in GLA_LEVELS]
    masks_b = [ri == ci] + [(ri < ci) & (x >= m) & (x < 2 * m) for m in GLA_LEVELS]
    pos = lax.broadcasted_iota(jnp.int32, (c, GLA_DK), 0)
    signs_f = [jnp.where((pos & m) != 0, 1.0, -1.0).astype(F32) for m in GLA_LEVELS]
    signs_b = [-s for s in signs_f]

    if has_s0:
        sf_ref[...] = s0_ref[0].T
        sb_ref[...] = s0_ref[1].T
    else:
        sf_ref[...] = jnp.zeros_like(sf_ref)
        sb_ref[...] = jnp.zeros_like(sb_ref)
    y_ref[...] = jnp.zeros_like(y_ref)

    def body(i, carry):
        rf = pl.multiple_of(i * c, c)
        rb = pl.multiple_of((n_chunks - 1 - i) * c, c)
        o, st = _gla_chunk(q_ref, k_ref, v_ref, cf_ref, rf, sf_ref[...], masks_f, signs_f, False)
        y_ref[pl.ds(rf, c), :] += o
        sf_ref[...] = st
        o, st = _gla_chunk(q_ref, k_ref, v_ref, cb_ref, rb, sb_ref[...], masks_b, signs_b, True)
        y_ref[pl.ds(rb, c), :] += o
        sb_ref[...] = st
        return carry

    lax.fori_loop(0, n_chunks, body, 0)

    if not has_s0:
        sout_ref[0] = sf_ref[...].T
        sout_ref[1] = sb_ref[...].T
    o = y_ref[...]
    y_ref[...] = _rms(o) * gn_ref[...] * r_ref[...]


def _gla_mix(q, k, v, cum, r, gnorm, s0, *, t_len, row_block0, n_seq):
    has_s0 = s0 is not None

    def rows(w, off=0):
        return pl.BlockSpec((t_len, w), lambda s, h: (row_block0 + s, off + h))

    state_spec = pl.BlockSpec((None, 2, None, GLA_DK, GLA_DV), lambda s, h: (s, 0, h, 0, 0))
    in_specs = [rows(GLA_DK), rows(GLA_DK), rows(GLA_DV), rows(GLA_DK), rows(GLA_DK, GLA_HEADS),
                rows(GLA_DV), pl.BlockSpec((None, 1, GLA_DV), lambda s, h: (h, 0, 0))]
    args = [q, k, v, cum, cum, r, gnorm.reshape(GLA_HEADS, 1, GLA_DV)]
    y_spec = pl.BlockSpec((t_len, GLA_DV), lambda s, h: (s, h))
    y_shape = jax.ShapeDtypeStruct((n_seq * t_len, GLA_V), F32)
    if has_s0:
        in_specs.append(state_spec)
        args.append(s0)
        out_specs, out_shape = y_spec, y_shape
    else:
        out_specs = [y_spec, state_spec]
        out_shape = [y_shape, jax.ShapeDtypeStruct((n_seq, 2, GLA_HEADS, GLA_DK, GLA_DV), F32)]
    return pl.pallas_call(
        functools.partial(_gla_kernel, t_len=t_len, has_s0=has_s0),
        grid=(n_seq, GLA_HEADS),
        in_specs=in_specs,
        out_specs=out_specs,
        out_shape=out_shape,
        scratch_shapes=[pltpu.VMEM((GLA_DV, GLA_DK), F32)] * 2,
        compiler_params=_cparams("arbitrary", "arbitrary"),
        name="gla_mix_sample" if has_s0 else "gla_mix_prompt",
    )(*args)


def _gla_out_kernel(yp_ref, ys_ref, w_ref, x_ref, mod_ref, o_ref):
    is_prompt = pl.program_id(0) < N_PROMPT_TOK // TM
    y = jnp.where(is_prompt, yp_ref[...], ys_ref[...]).astype(BF16)
    o_ref[...] = x_ref[...] + mod_ref[5:6, :] * jnp.dot(y, w_ref[...], preferred_element_type=F32)


def _gla_out_proj(y_p, y_s, w_out, x, mods_l):
    n_p = N_PROMPT_TOK // TM
    return pl.pallas_call(
        _gla_out_kernel,
        grid=(N_TOK // TM,),
        in_specs=[
            pl.BlockSpec((TM, GLA_V), lambda i: (jnp.minimum(i, n_p - 1), 0)),
            pl.BlockSpec((TM, GLA_V), lambda i: (jnp.maximum(i - n_p, 0), 0)),
            pl.BlockSpec((GLA_V, D_MODEL), lambda i: (0, 0), pipeline_mode=pl.Buffered(1)),
            pl.BlockSpec((TM, D_MODEL), lambda i: (i, 0)),
            pl.BlockSpec((None, N_MOD, D_MODEL), lambda i: (_group_of_tile(i, TM), 0, 0)),
        ],
        out_specs=pl.BlockSpec((TM, D_MODEL), lambda i: (i, 0)),
        out_shape=jax.ShapeDtypeStruct((N_TOK, D_MODEL), F32),
        compiler_params=_cparams("arbitrary"),
        name="gla_out_proj",
    )(y_p, y_s, w_out, x, mods_l)


def _sample_to_col_major(x):
    xs = x[N_PROMPT_TOK:].reshape(DEC_BATCH, DEC_SEQ // GRID_W, GRID_W, D_MODEL)
    xs = xs.swapaxes(1, 2).reshape(DEC_BATCH * DEC_SEQ, D_MODEL)
    return jnp.concatenate([x[:N_PROMPT_TOK], xs], axis=0)


def _sample_from_col_major(xs):
    xs = xs.reshape(DEC_BATCH, GRID_W, DEC_SEQ // GRID_W, D_MODEL)
    return xs.swapaxes(1, 2).reshape(DEC_BATCH, DEC_SEQ, D_MODEL)


def kernel(x_prompt, x_sample, c, state_rglru, state_gla, c_ctx, ada_w, ada_b, norm_g, ffn_w_in, ffn_w_out,
           rg_w_in, rg_conv_w, rg_conv_b, rg_w_a, rg_b_a, rg_w_i, rg_b_i, rg_lambda, rg_w_out,
           gla_w_in, gla_w_g2, gla_b_g, gla_norm, gla_w_out, final_norm):
    depth = ada_w.shape[0]
    x = jnp.concatenate([x_prompt.reshape(N_PROMPT_TOK, D_MODEL),
                         x_sample.reshape(DEC_BATCH * DEC_SEQ, D_MODEL)], axis=0)
    cvecs = jnp.concatenate([c_ctx[None, :], c, jnp.zeros((8 - N_GROUPS, D_MODEL), F32)], axis=0)
    mods = _ada_mods(cvecs, ada_w, ada_b)
    fn = final_norm.reshape(1, D_MODEL)
    ffn_in, ffn_out = ffn_w_in.astype(BF16), ffn_w_out.astype(BF16)
    rg_states, gla_states = [], []

    for i in range(depth):
        mods_l, gains_l = mods[i], norm_g[i]
        j = i // 2
        if i % 2 == 1:
            x = _sample_to_col_major(x)
        x = _half_ffn(x, mods_l, gains_l, ffn_in, ffn_out, i, 0, fn, False)
        if i % 2 == 0:
            gate, xbr = _rg_proj(x, mods_l, gains_l, rg_w_in[j].astype(BF16))
            y, st_f, st_b = _rg_scan(xbr, gate, rg_conv_w[j], rg_conv_b[j].reshape(1, D_RNN),
                                     rg_w_a[j].astype(BF16), rg_b_a[j], rg_w_i[j].astype(BF16), rg_b_i[j],
                                     rg_lambda[j], state_rglru[:, j])
            rg_states.append(jnp.stack([st_f[:BATCH], st_b[:BATCH]], axis=1))
            x = _rg_out_proj(y, rg_w_out[j].astype(BF16), x, mods_l)
        else:
            w_in = gla_w_in[j]
            n_main = 2 * GLA_QK + 2 * GLA_V
            w_z = jnp.pad(w_in[:, n_main:], ((0, 0), (0, 128 - 2 * GLA_RANK))).astype(BF16)
            w2cat = jnp.zeros((128, 2 * GLA_QK), F32)
            w2cat = w2cat.at[:GLA_RANK, :GLA_QK].set(gla_w_g2[j, 0])
            w2cat = w2cat.at[GLA_RANK:2 * GLA_RANK, GLA_QK:].set(gla_w_g2[j, 1]).astype(BF16)
            q, k, v, r, cum = _gla_proj(x, mods_l, gains_l, w_in[:, :n_main].astype(BF16), w_z, w2cat,
                                        gla_b_g[j].reshape(1, 2 * GLA_QK))
            y_p, st = _gla_mix(q, k, v, cum, r, gla_norm[j], None, t_len=SEQ, row_block0=0, n_seq=BATCH)
            y_s = _gla_mix(q, k, v, cum, r, gla_norm[j], state_gla[:, j], t_len=DEC_SEQ,
                           row_block0=N_PROMPT_TOK // DEC_SEQ, n_seq=DEC_BATCH)
            gla_states.append(st)
            x = _gla_out_proj(y_p, y_s, gla_w_out[j].astype(BF16), x, mods_l)
        x = _half_ffn(x, mods_l, gains_l, ffn_in, ffn_out, i, 1, fn, i == depth - 1)
        if i % 2 == 1:
            x = jnp.concatenate(
                [x[:N_PROMPT_TOK], _sample_from_col_major(x[N_PROMPT_TOK:]).reshape(-1, D_MODEL)], axis=0)

    y_prompt = x[:N_PROMPT_TOK].reshape(BATCH, SEQ, D_MODEL)
    y_sample = x[N_PROMPT_TOK:].reshape(DEC_BATCH, DEC_SEQ, D_MODEL)
    return (y_prompt, y_sample, jnp.stack(rg_states, axis=1), jnp.stack(gla_states, axis=1))
```

```python
import functools

import jax
import jax.numpy as jnp
from jax import lax
from jax.experimental import pallas as pl
from jax.experimental.pallas import tpu as pltpu

F32 = jnp.float32
BF16 = jnp.bfloat16

D_MODEL = 1024
BATCH = 16
SEQ = 256
DEC_BATCH = 2
DEC_SEQ = 2048
GRID_W = 64
EPS = 1e-6
N_MOD = 9
D_FF = 2816
FFN_RES = 0.5
D_RNN = 1280
RG_BLOCKS = 10
RG_BLOCK = 128
CONV_W = 4
RG_C = 8.0
GLA_HEADS = 4
GLA_QK = 512
GLA_V = 1024
GLA_DK = 128
GLA_DV = 256
GLA_RANK = 16
GLA_TAU = 16.0
LOG2_E = 1.4426950408889634

N_PROMPT_TOK = BATCH * SEQ
N_TOK = N_PROMPT_TOK + DEC_BATCH * DEC_SEQ
N_GROUPS = 1 + DEC_BATCH
GROUP_ROWS = DEC_SEQ
PROMPT_BLOCKS = N_PROMPT_TOK // GROUP_ROWS

VMEM_LIMIT_BYTES = 56 * 1024 * 1024

TM = 512
HALF_TILES = N_PROMPT_TOK // TM
FF_CHUNK = 1408
RG_CT = 256
SCAN_L = 256
SCAN_V = GROUP_ROWS // SCAN_L
GLA_C = 128
GLA_LEVELS = tuple(GLA_C >> (l + 1) for l in range(GLA_C.bit_length() - 1))


def _cparams(*sem):
    return pltpu.CompilerParams(dimension_semantics=sem, vmem_limit_bytes=VMEM_LIMIT_BYTES)


def _group_of_tile(i, rows):
    return jnp.maximum((i * rows) // GROUP_ROWS - (PROMPT_BLOCKS - 1), 0)


def _sigmoid(x):
    return 1.0 / (1.0 + jnp.exp2(x * (-LOG2_E)))


def _softplus(x):
    return jnp.maximum(x, 0.0) + jnp.log1p(jnp.exp(-jnp.abs(x)))


def _rms(x):
    return x * lax.rsqrt(jnp.mean(x * x, axis=-1, keepdims=True) + EPS)


def _modulated(x, mod_ref, gain_ref, j):
    y = _rms(x) * gain_ref[j:j + 1, :]
    return y * (1.0 + mod_ref[3 * j + 1:3 * j + 2, :]) + mod_ref[3 * j:3 * j + 1, :]


def _ada_kernel(c_ref, w_ref, b_ref, o_ref):
    c = c_ref[...]
    s = (c * _sigmoid(c)).astype(BF16)
    o_ref[...] = jnp.dot(s, w_ref[...].astype(BF16), preferred_element_type=F32) + b_ref[...]


def _ada_mods(cvecs, ada_w, ada_b):
    depth = ada_w.shape[0]
    out = pl.pallas_call(
        _ada_kernel,
        grid=(depth, N_MOD),
        in_specs=[
            pl.BlockSpec((8, D_MODEL), lambda l, k: (0, 0)),
            pl.BlockSpec((None, D_MODEL, D_MODEL), lambda l, k: (l, 0, k)),
            pl.BlockSpec((None, None, 1, D_MODEL), lambda l, k: (l, k, 0, 0)),
        ],
        out_specs=pl.BlockSpec((None, None, 8, D_MODEL), lambda l, k: (l, k, 0, 0)),
        out_shape=jax.ShapeDtypeStruct((depth, N_MOD, 8, D_MODEL), F32),
        compiler_params=_cparams("arbitrary", "arbitrary"),
        name="ada_mod",
    )(cvecs, ada_w, ada_b.reshape(depth, N_MOD, 1, D_MODEL))
    return out[:, :, :N_GROUPS, :].transpose(0, 2, 1, 3)


def _ffn_kernel(*refs, j, final, split_in, split_out):
    n_in = 2 if split_in else 1
    x_refs, (mod_ref, gain_ref, win_ref, wout_ref, fn_ref), o_refs = refs[:n_in], refs[n_in:n_in + 5], refs[n_in + 5:]
    first_half = pl.program_id(0) < HALF_TILES
    x = jnp.where(first_half, x_refs[0][...], x_refs[1][...]) if split_in else x_refs[0][...]
    hb = _modulated(x, mod_ref, gain_ref, j).astype(BF16)
    acc = jnp.zeros((TM, D_MODEL), F32)
    for c in range(D_FF // FF_CHUNK):
        g = jnp.dot(hb, win_ref[:, c * FF_CHUNK:(c + 1) * FF_CHUNK], preferred_element_type=F32)
        u = jnp.dot(hb, win_ref[:, D_FF + c * FF_CHUNK:D_FF + (c + 1) * FF_CHUNK],
                    preferred_element_type=F32)
        a = (g * _sigmoid(g) * u).astype(BF16)
        acc = acc + jnp.dot(a, wout_ref[c * FF_CHUNK:(c + 1) * FF_CHUNK, :], preferred_element_type=F32)
    out = x + FFN_RES * mod_ref[3 * j + 2:3 * j + 3, :] * acc
    if final:
        out = _rms(out) * fn_ref[...]
    if split_out:
        @pl.when(first_half)
        def _():
            o_refs[0][...] = out

        @pl.when(jnp.logical_not(first_half))
        def _():
            o_refs[1][...] = out
    else:
        o_refs[0][...] = out


def _half_specs(split):
    if not split:
        return [pl.BlockSpec((TM, D_MODEL), lambda i: (i, 0))], [jax.ShapeDtypeStruct((N_TOK, D_MODEL), F32)]
    specs = [pl.BlockSpec((TM, D_MODEL), lambda i: (jnp.minimum(i, HALF_TILES - 1), 0)),
             pl.BlockSpec((TM, D_MODEL), lambda i: (jnp.maximum(i - HALF_TILES, 0), 0))]
    return specs, [jax.ShapeDtypeStruct((N_PROMPT_TOK, D_MODEL), F32)] * 2


def _half_ffn(xs, mods_l, gains_l, w_in, w_out, layer, half, final_norm, final, split_out):
    split_in = len(xs) == 2
    in_specs, _ = _half_specs(split_in)
    out_specs, out_shape = _half_specs(split_out)
    return pl.pallas_call(
        functools.partial(_ffn_kernel, j=2 * half, final=final, split_in=split_in, split_out=split_out),
        grid=(N_TOK // TM,),
        in_specs=in_specs + [
            pl.BlockSpec((None, N_MOD, D_MODEL), lambda i: (_group_of_tile(i, TM), 0, 0)),
            pl.BlockSpec((3, D_MODEL), lambda i: (0, 0)),
            pl.BlockSpec((None, None, D_MODEL, 2 * D_FF), lambda i: (layer, half, 0, 0),
                         pipeline_mode=pl.Buffered(1)),
            pl.BlockSpec((None, None, D_FF, D_MODEL), lambda i: (layer, half, 0, 0),
                         pipeline_mode=pl.Buffered(1)),
            pl.BlockSpec((1, D_MODEL), lambda i: (0, 0)),
        ],
        out_specs=out_specs,
        out_shape=out_shape,
        compiler_params=_cparams("arbitrary"),
        name="half_ffn",
    )(*xs, mods_l, gains_l, w_in, w_out, final_norm)


def _gelu_tanh(x):
    return 0.5 * x * (1.0 + jnp.tanh(0.7978845608028654 * (x + 0.044715 * (x * x * x))))


def _rg_proj_kernel(x_ref, mod_ref, gain_ref, w_ref, gate_ref, xbr_ref):
    hb = _modulated(x_ref[...], mod_ref, gain_ref, 1).astype(BF16)
    gate_ref[...] = _gelu_tanh(jnp.dot(hb, w_ref[:, :D_RNN], preferred_element_type=F32))
    xbr_ref[...] = jnp.dot(hb, w_ref[:, D_RNN:], preferred_element_type=F32)


def _rg_proj(x, mods_l, gains_l, w_in):
    return pl.pallas_call(
        _rg_proj_kernel,
        grid=(N_TOK // TM,),
        in_specs=[
            pl.BlockSpec((TM, D_MODEL), lambda i: (i, 0)),
            pl.BlockSpec((None, N_MOD, D_MODEL), lambda i: (_group_of_tile(i, TM), 0, 0)),
            pl.BlockSpec((3, D_MODEL), lambda i: (0, 0)),
            pl.BlockSpec((D_MODEL, 2 * D_RNN), lambda i: (0, 0), pipeline_mode=pl.Buffered(1)),
        ],
        out_specs=[pl.BlockSpec((TM, D_RNN), lambda i: (i, 0))] * 2,
        out_shape=[jax.ShapeDtypeStruct((N_TOK, D_RNN), F32)] * 2,
        compiler_params=_cparams("arbitrary"),
        name="rg_in_proj",
    )(x, mods_l, gains_l, w_in)


def _seg_shift(row, down):
    v = lax.broadcasted_iota(jnp.int32, row.shape, 0)
    if down:
        return jnp.where(v >= 1, pltpu.roll(row, 1, 0), 0.0)
    return jnp.where(v < SCAN_V - 1, pltpu.roll(row, SCAN_V - 1, 0), 0.0)


def _rg_scan_kernel(xbr_ref, gate_ref, cw_ref, cb_ref, wa_ref, ba_ref, wi_ref, bi_ref, lam_ref, h0_ref,
                    y_ref, stf_ref, stb_ref, af_ref, uf_ref, ab_ref, ub_ref):
    chain = jnp.where(pl.program_id(0) >= PROMPT_BLOCKS, 1.0, 0.0).astype(F32)
    nbt = RG_CT // RG_BLOCK
    rows = SCAN_L * SCAN_V
    v = SCAN_V

    for nb in range(nbt):
        ls = slice(nb * RG_BLOCK, (nb + 1) * RG_BLOCK)
        x = xbr_ref[:, ls]
        before = chain * _seg_shift(x[rows - v:], True)
        after0 = chain * _seg_shift(x[:v], False)
        after1 = chain * _seg_shift(x[v:2 * v], False)
        xm1 = jnp.concatenate([before, x[:rows - v]], axis=0)
        xp1 = jnp.concatenate([x[v:], after0], axis=0)
        xp2 = jnp.concatenate([x[2 * v:], after0, after1], axis=0)
        xc = (cw_ref[0:1, ls] * xm1 + cw_ref[1:2, ls] * x + cw_ref[2:3, ls] * xp1 + cw_ref[3:4, ls] * xp2
              + cb_ref[0:1, ls])
        xcb = xc.astype(BF16)
        for d, (a_ref, u_ref) in enumerate(((af_ref, uf_ref), (ab_ref, ub_ref))):
            r = _sigmoid(jnp.dot(xcb, wa_ref[d, nb], preferred_element_type=F32) + ba_ref[d:d + 1, ls])
            gi = _sigmoid(jnp.dot(xcb, wi_ref[d, nb], preferred_element_type=F32) + bi_ref[d:d + 1, ls])
            a = jnp.exp2(r * ((-RG_C * LOG2_E) * _softplus(-lam_ref[d:d + 1, ls])))
            a_ref[nb] = a
            u_ref[nb] = jnp.sqrt(1.0 - a * a) * (gi * xc)

    def step(i, carry):
        sf = pl.ds(pl.multiple_of(i * v, v), v)
        sb = pl.ds(pl.multiple_of((SCAN_L - 1 - i) * v, v), v)
        out = []
        for nb in range(nbt):
            hf, pf, hb, pb = carry[4 * nb:4 * nb + 4]
            a = af_ref[nb, sf, :]
            hf = a * hf + uf_ref[nb, sf, :]
            pf = a * pf
            uf_ref[nb, sf, :] = hf
            af_ref[nb, sf, :] = pf
            a = ab_ref[nb, sb, :]
            hb = a * hb + ub_ref[nb, sb, :]
            pb = a * pb
            ub_ref[nb, sb, :] = hb
            ab_ref[nb, sb, :] = pb
            out += [hf, pf, hb, pb]
        return tuple(out)

    zero = jnp.zeros((v, RG_BLOCK), F32)
    one = jnp.ones((v, RG_BLOCK), F32)
    lax.fori_loop(0, SCAN_L, step, (zero, one, zero, one) * nbt, unroll=4)

    for nb in range(nbt):
        ls = slice(nb * RG_BLOCK, (nb + 1) * RG_BLOCK)
        h_end, p_end = uf_ref[nb, rows - v:, :], af_ref[nb, rows - v:, :]
        entry = [chain * h0_ref[0:1, ls]]
        for s in range(v - 1):
            entry.append(chain * (h_end[s:s + 1] + p_end[s:s + 1] * entry[s]))
        ef = jnp.concatenate(entry, axis=0)
        h_start, p_start = ub_ref[nb, :v, :], ab_ref[nb, :v, :]
        entry = [chain * h0_ref[1:2, ls]]
        for s in range(v - 1, 0, -1):
            entry.append(chain * (h_start[s:s + 1] + p_start[s:s + 1] * entry[-1]))
        eb = jnp.concatenate(entry[::-1], axis=0)
        shape3 = (SCAN_L, v, RG_BLOCK)
        hf = uf_ref[nb].reshape(shape3) + af_ref[nb].reshape(shape3) * ef[None]
        hb = ub_ref[nb].reshape(shape3) + ab_ref[nb].reshape(shape3) * eb[None]
        stf_ref[:, ls] = hf[SCAN_L - 1]
        stb_ref[:, ls] = hb[0]
        y_ref[:, ls] = (hf + hb).reshape(rows, RG_BLOCK) * gate_ref[:, ls]


def _rg_scan(xbr, gate, conv_w, conv_b, w_a, b_a, w_i, b_i, lam, h0):
    n_blk = N_TOK // GROUP_ROWS
    nbt = RG_CT // RG_BLOCK
    tile = pl.BlockSpec((GROUP_ROWS, RG_CT), lambda b, c: (b, c))
    vec2 = pl.BlockSpec((2, RG_CT), lambda b, c: (0, c))
    wspec = pl.BlockSpec((2, nbt, RG_BLOCK, RG_BLOCK), lambda b, c: (0, c, 0, 0))
    return pl.pallas_call(
        _rg_scan_kernel,
        grid=(n_blk, D_RNN // RG_CT),
        in_specs=[
            tile, tile,
            pl.BlockSpec((CONV_W, RG_CT), lambda b, c: (0, c)),
            pl.BlockSpec((1, RG_CT), lambda b, c: (0, c)),
            wspec, vec2, wspec, vec2, vec2,
            pl.BlockSpec((None, 2, RG_CT), lambda b, c: (jnp.maximum(b - PROMPT_BLOCKS, 0), 0, c)),
        ],
        out_specs=[tile] + [pl.BlockSpec((SCAN_V, RG_CT), lambda b, c: (b, c))] * 2,
        out_shape=[jax.ShapeDtypeStruct((N_TOK, D_RNN), F32)]
        + [jax.ShapeDtypeStruct((n_blk * SCAN_V, D_RNN), F32)] * 2,
        scratch_shapes=[pltpu.VMEM((nbt, GROUP_ROWS, RG_BLOCK), F32)] * 4,
        compiler_params=_cparams("arbitrary", "arbitrary"),
        name="rg_scan",
    )(xbr, gate, conv_w, conv_b, w_a, b_a, w_i, b_i, lam, h0)


def _rg_out_kernel(y_ref, w_ref, x_ref, mod_ref, o_ref):
    upd = jnp.dot(y_ref[...].astype(BF16), w_ref[...], preferred_element_type=F32)
    o_ref[...] = x_ref[...] + mod_ref[5:6, :] * upd


def _rg_out_proj(y, w_out, x, mods_l):
    return pl.pallas_call(
        _rg_out_kernel,
        grid=(N_TOK // TM,),
        in_specs=[
            pl.BlockSpec((TM, D_RNN), lambda i: (i, 0)),
            pl.BlockSpec((D_RNN, D_MODEL), lambda i: (0, 0), pipeline_mode=pl.Buffered(1)),
            pl.BlockSpec((TM, D_MODEL), lambda i: (i, 0)),
            pl.BlockSpec((None, N_MOD, D_MODEL), lambda i: (_group_of_tile(i, TM), 0, 0)),
        ],
        out_specs=pl.BlockSpec((TM, D_MODEL), lambda i: (i, 0)),
        out_shape=jax.ShapeDtypeStruct((N_TOK, D_MODEL), F32),
        compiler_params=_cparams("arbitrary"),
        name="rg_out_proj",
    )(y, w_out, x, mods_l)


def _chunk_tri():
    ri = lax.broadcasted_iota(jnp.int32, (GLA_C, GLA_C), 0)
    ci = lax.broadcasted_iota(jnp.int32, (GLA_C, GLA_C), 1)
    return (ri >= ci).astype(BF16), (ri <= ci).astype(BF16)


def _split3(x):
    hi = x.astype(BF16)
    r1 = x - hi.astype(F32)
    mid = r1.astype(BF16)
    lo = (r1 - mid.astype(F32)).astype(BF16)
    return hi, mid, lo


def _gla_proj_kernel(x_ref, mod_ref, gain_ref, w_ref, wz_ref, w2_ref, bg_ref,
                     q_ref, k_ref, v_ref, r_ref, cum_ref):
    hb = _modulated(x_ref[...], mod_ref, gain_ref, 1).astype(BF16)
    q_ref[...] = jnp.dot(hb, w_ref[:, :GLA_QK], preferred_element_type=F32) * (GLA_DK ** -0.5)
    k_ref[...] = jnp.dot(hb, w_ref[:, GLA_QK:2 * GLA_QK], preferred_element_type=F32)
    v_ref[...] = jnp.dot(hb, w_ref[:, 2 * GLA_QK:2 * GLA_QK + GLA_V], preferred_element_type=F32)
    r = jnp.dot(hb, w_ref[:, 2 * GLA_QK + GLA_V:], preferred_element_type=F32)
    r_ref[...] = r * _sigmoid(r)
    z = jnp.dot(hb, wz_ref[...], preferred_element_type=F32).astype(BF16)
    zz = jnp.dot(z, w2_ref[...], preferred_element_type=F32) + bg_ref[...]
    log_gate = -_softplus(-zz) * (1.0 / GLA_TAU)
    parts = _split3(log_gate)
    tri_f, tri_b = _chunk_tri()
    for ch in range(TM // GLA_C):
        rows = slice(ch * GLA_C, (ch + 1) * GLA_C)
        cum_ref[rows, :GLA_QK] = sum(jnp.dot(tri_f, p[rows, :GLA_QK], preferred_element_type=F32) for p in parts)
        cum_ref[rows, GLA_QK:] = sum(jnp.dot(tri_b, p[rows, GLA_QK:], preferred_element_type=F32) for p in parts)


def _gla_proj(x, mods_l, gains_l, w_main, w_z, w2cat, bg):
    def rows(w):
        return pl.BlockSpec((TM, w), lambda i: (i, 0))

    def whole(a):
        return pl.BlockSpec(a.shape, lambda i: (0,) * a.ndim, pipeline_mode=pl.Buffered(1))

    widths = (GLA_QK, GLA_QK, GLA_V, GLA_V, 2 * GLA_QK)
    return pl.pallas_call(
        _gla_proj_kernel,
        grid=(N_TOK // TM,),
        in_specs=[
            rows(D_MODEL),
            pl.BlockSpec((None, N_MOD, D_MODEL), lambda i: (_group_of_tile(i, TM), 0, 0)),
            pl.BlockSpec((3, D_MODEL), lambda i: (0, 0)),
            whole(w_main), whole(w_z), whole(w2cat), whole(bg),
        ],
        out_specs=[rows(w) for w in widths],
        out_shape=[jax.ShapeDtypeStruct((N_TOK, w), F32) for w in widths],
        compiler_params=_cparams("arbitrary"),
        name="gla_in_proj",
    )(x, mods_l, gains_l, w_main, w_z, w2cat, bg)


def _dot_tn(a, b):
    return lax.dot_general(a, b, (((0,), (0,)), ((), ())), preferred_element_type=F32)


def _dot_nt(a, b):
    return lax.dot_general(a, b, (((1,), (1,)), ((), ())), preferred_element_type=F32)


def _level_ref(cum_ref, row0, cum, m, backward):
    c = GLA_C
    pick = m - 1 if backward else m

    def bcast(r, n):
        return jnp.broadcast_to(cum_ref[pl.ds(row0 + r, 1), :], (n, GLA_DK))

    if m >= 4:
        return jnp.concatenate([bcast(g * 2 * m + pick, 2 * m) for g in range(c // (2 * m))], axis=0)
    pos = lax.broadcasted_iota(jnp.int32, (c, GLA_DK), 0)
    if m == 2:
        lo = jnp.concatenate([bcast(g * 8 + pick, 8) for g in range(c // 8)], axis=0)
        hi = jnp.concatenate([bcast(g * 8 + 4 + pick, 8) for g in range(c // 8)], axis=0)
        return jnp.where((pos & 4) == 0, lo, hi)
    if backward:
        return jnp.where((pos & 1) == 1, pltpu.roll(cum, 1, 0), cum)
    return jnp.where((pos & 1) == 0, pltpu.roll(cum, c - 1, 0), cum)


def _gla_chunk(q_ref, k_ref, v_ref, cum_ref, row0, st, masks, signs, backward):
    c = GLA_C
    rows = pl.ds(row0, c)
    q, k, cum = q_ref[rows, :], k_ref[rows, :], cum_ref[rows, :]
    tot = cum_ref[pl.ds(row0 + (0 if backward else c - 1), 1), :]
    qb, kb, vb = q.astype(BF16), k.astype(BF16), v_ref[rows, :].astype(BF16)
    o = _dot_nt((q * jnp.exp(cum)).astype(BF16), st.astype(BF16))
    kd = (k * jnp.exp(tot - cum)).astype(BF16)
    st_new = st * jnp.exp(tot) + _dot_tn(vb, kd)

    scores = jnp.where(masks[0], _dot_nt(qb, kb), 0.0)
    for lvl, m in enumerate(GLA_LEVELS):
        d = (cum - _level_ref(cum_ref, row0, cum, m, backward)) * signs[lvl]
        e = jnp.exp(d).astype(BF16)
        scores = jnp.where(masks[lvl + 1], _dot_nt(qb * e, kb * e), scores)
    o = o + jnp.dot(scores.astype(BF16), vb, preferred_element_type=F32)
    return o, st_new


def _gla_kernel(*refs, t_len, has_s0):
    if has_s0:
        q_ref, k_ref, v_ref, cf_ref, cb_ref, r_ref, gn_ref, s0_ref, y_ref, sf_ref, sb_ref = refs
    else:
        q_ref, k_ref, v_ref, cf_ref, cb_ref, r_ref, gn_ref, y_ref, sout_ref, sf_ref, sb_ref = refs
    c = GLA_C
    n_chunks = t_len // c
    ri = lax.broadcasted_iota(jnp.int32, (c, c), 0)
    ci = lax.broadcasted_iota(jnp.int32, (c, c), 1)
    x = ri ^ ci
    masks_f = [ri == ci] + [(ri > ci) & (x >= m) & (x < 2 * m) for m in GLA_LEVELS]
    masks_b = [ri == ci] + [(ri < ci) & (x >= m) & (x < 2 * m) for m in GLA_LEVELS]
    pos = lax.broadcasted_iota(jnp.int32, (c, GLA_DK), 0)
    signs_f = [jnp.where((pos & m) != 0, 1.0, -1.0).astype(F32) for m in GLA_LEVELS]
    signs_b = [-s for s in signs_f]

    if has_s0:
        sf_ref[...] = s0_ref[0].T
        sb_ref[...] = s0_ref[1].T
    else:
        sf_ref[...] = jnp.zeros_like(sf_ref)
        sb_ref[...] = jnp.zeros_like(sb_ref)
    y_ref[...] = jnp.zeros_like(y_ref)

    def body(i, carry):
        rf = pl.multiple_of(i * c, c)
        rb = pl.multiple_of((n_chunks - 1 - i) * c, c)
        o, st = _gla_chunk(q_ref, k_ref, v_ref, cf_ref, rf, sf_ref[...], masks_f, signs_f, False)
        y_ref[pl.ds(rf, c), :] += o
        sf_ref[...] = st
        o, st = _gla_chunk(q_ref, k_ref, v_ref, cb_ref, rb, sb_ref[...], masks_b, signs_b, True)
        y_ref[pl.ds(rb, c), :] += o
        sb_ref[...] = st
        return carry

    lax.fori_loop(0, n_chunks, body, 0)

    if not has_s0:
        sout_ref[0] = sf_ref[...].T
        sout_ref[1] = sb_ref[...].T
    o = y_ref[...]
    y_ref[...] = _rms(o) * gn_ref[...] * r_ref[...]


def _gla_mix(q, k, v, cum, r, gnorm, s0, *, t_len, row_block0, n_seq):
    has_s0 = s0 is not None

    def rows(w, off=0):
        return pl.BlockSpec((t_len, w), lambda s, h: (row_block0 + s, off + h))

    state_spec = pl.BlockSpec((None, 2, None, GLA_DK, GLA_DV), lambda s, h: (s, 0, h, 0, 0))
    in_specs = [rows(GLA_DK), rows(GLA_DK), rows(GLA_DV), rows(GLA_DK), rows(GLA_DK, GLA_HEADS),
                rows(GLA_DV), pl.BlockSpec((None, 1, GLA_DV), lambda s, h: (h, 0, 0))]
    args = [q, k, v, cum, cum, r, gnorm.reshape(GLA_HEADS, 1, GLA_DV)]
    y_spec = pl.BlockSpec((t_len, GLA_DV), lambda s, h: (s, h))
    y_shape = jax.ShapeDtypeStruct((n_seq * t_len, GLA_V), F32)
    if has_s0:
        in_specs.append(state_spec)
        args.append(s0)
        out_specs, out_shape = y_spec, y_shape
    else:
        out_specs = [y_spec, state_spec]
        out_shape = [y_shape, jax.ShapeDtypeStruct((n_seq, 2, GLA_HEADS, GLA_DK, GLA_DV), F32)]
    return pl.pallas_call(
        functools.partial(_gla_kernel, t_len=t_len, has_s0=has_s0),
        grid=(n_seq, GLA_HEADS),
        in_specs=in_specs,
        out_specs=out_specs,
        out_shape=out_shape,
        scratch_shapes=[pltpu.VMEM((GLA_DV, GLA_DK), F32)] * 2,
        compiler_params=_cparams("arbitrary", "arbitrary"),
        name="gla_mix_sample" if has_s0 else "gla_mix_prompt",
    )(*args)


def _gla_out_kernel(yp_ref, ys_ref, w_ref, x_ref, mod_ref, o_ref):
    is_prompt = pl.program_id(0) < HALF_TILES
    y = jnp.where(is_prompt, yp_ref[...], ys_ref[...]).astype(BF16)
    o_ref[...] = x_ref[...] + mod_ref[5:6, :] * jnp.dot(y, w_ref[...], preferred_element_type=F32)


def _gla_out_proj(y_p, y_s, w_out, x, mods_l):
    return pl.pallas_call(
        _gla_out_kernel,
        grid=(N_TOK // TM,),
        in_specs=[
            pl.BlockSpec((TM, GLA_V), lambda i: (jnp.minimum(i, HALF_TILES - 1), 0)),
            pl.BlockSpec((TM, GLA_V), lambda i: (jnp.maximum(i - HALF_TILES, 0), 0)),
            pl.BlockSpec((GLA_V, D_MODEL), lambda i: (0, 0), pipeline_mode=pl.Buffered(1)),
            pl.BlockSpec((TM, D_MODEL), lambda i: (i, 0)),
            pl.BlockSpec((None, N_MOD, D_MODEL), lambda i: (_group_of_tile(i, TM), 0, 0)),
        ],
        out_specs=pl.BlockSpec((TM, D_MODEL), lambda i: (i, 0)),
        out_shape=jax.ShapeDtypeStruct((N_TOK, D_MODEL), F32),
        compiler_params=_cparams("arbitrary"),
        name="gla_out_proj",
    )(y_p, y_s, w_out, x, mods_l)


def _natural_to_scan(x_half):
    x4 = x_half.reshape(-1, SCAN_V, SCAN_L, D_MODEL)
    return x4.swapaxes(1, 2).reshape(-1, D_MODEL)


def _scan_to_natural(x_half):
    x4 = x_half.reshape(-1, SCAN_L, SCAN_V, D_MODEL)
    return x4.swapaxes(1, 2).reshape(-1, D_MODEL)


def _natural_to_col_major(x_half):
    x4 = x_half.reshape(DEC_BATCH, DEC_SEQ // GRID_W, GRID_W, D_MODEL)
    return x4.swapaxes(1, 2).reshape(-1, D_MODEL)


def _col_major_to_natural(x_half):
    x4 = x_half.reshape(DEC_BATCH, GRID_W, DEC_SEQ // GRID_W, D_MODEL)
    return x4.swapaxes(1, 2).reshape(-1, D_MODEL)


def kernel(x_prompt, x_sample, c, state_rglru, state_gla, c_ctx, ada_w, ada_b, norm_g, ffn_w_in, ffn_w_out,
           rg_w_in, rg_conv_w, rg_conv_b, rg_w_a, rg_b_a, rg_w_i, rg_b_i, rg_lambda, rg_w_out,
           gla_w_in, gla_w_g2, gla_b_g, gla_norm, gla_w_out, final_norm):
    depth = ada_w.shape[0]
    cvecs = jnp.concatenate([c_ctx[None, :], c, jnp.zeros((8 - N_GROUPS, D_MODEL), F32)], axis=0)
    mods = _ada_mods(cvecs, ada_w, ada_b)
    fn = final_norm.reshape(1, D_MODEL)
    ffn_in, ffn_out = ffn_w_in.astype(BF16), ffn_w_out.astype(BF16)
    rg_states, gla_states = [], []

    xp = x_prompt.reshape(N_PROMPT_TOK, D_MODEL)
    xs = x_sample.reshape(DEC_BATCH * DEC_SEQ, D_MODEL)
    for i in range(depth):
        mods_l, gains_l = mods[i], norm_g[i]
        j = i // 2
        last = i == depth - 1
        if i % 2 == 0:
            halves = (_natural_to_scan(xp), _natural_to_scan(xs))
            x = _half_ffn(halves, mods_l, gains_l, ffn_in, ffn_out, i, 0, fn, False, False)[0]
            gate, xbr = _rg_proj(x, mods_l, gains_l, rg_w_in[j].astype(BF16))
            y, st_f, st_b = _rg_scan(xbr, gate, rg_conv_w[j], rg_conv_b[j].reshape(1, D_RNN),
                                     rg_w_a[j].astype(BF16), rg_b_a[j], rg_w_i[j].astype(BF16), rg_b_i[j],
                                     rg_lambda[j], state_rglru[:, j])
            rg_states.append(jnp.stack([st_f[:BATCH], st_b[:BATCH]], axis=1))
            x = _rg_out_proj(y, rg_w_out[j].astype(BF16), x, mods_l)
            xp, xs = _half_ffn((x,), mods_l, gains_l, ffn_in, ffn_out, i, 1, fn, last, True)
            xp, xs = _scan_to_natural(xp), _scan_to_natural(xs)
        else:
            halves = (xp, _natural_to_col_major(xs))
            x = _half_ffn(halves, mods_l, gains_l, ffn_in, ffn_out, i, 0, fn, False, False)[0]
            w_in = gla_w_in[j]
            n_main = 2 * GLA_QK + 2 * GLA_V
            w_z = jnp.pad(w_in[:, n_main:], ((0, 0), (0, 128 - 2 * GLA_RANK))).astype(BF16)
            w2cat = jnp.zeros((128, 2 * GLA_QK), F32)
            w2cat = w2cat.at[:GLA_RANK, :GLA_QK].set(gla_w_g2[j, 0])
            w2cat = w2cat.at[GLA_RANK:2 * GLA_RANK, GLA_QK:].set(gla_w_g2[j, 1]).astype(BF16)
            q, k, v, r, cum = _gla_proj(x, mods_l, gains_l, w_in[:, :n_main].astype(BF16), w_z, w2cat,
                                        gla_b_g[j].reshape(1, 2 * GLA_QK))
            y_p, st = _gla_mix(q, k, v, cum, r, gla_norm[j], None, t_len=SEQ, row_block0=0, n_seq=BATCH)
            y_s = _gla_mix(q, k, v, cum, r, gla_norm[j], state_gla[:, j], t_len=DEC_SEQ,
                           row_block0=N_PROMPT_TOK // DEC_SEQ, n_seq=DEC_BATCH)
            gla_states.append(st)
            x = _gla_out_proj(y_p, y_s, gla_w_out[j].astype(BF16), x, mods_l)
            xp, xs = _half_ffn((x,), mods_l, gains_l, ffn_in, ffn_out, i, 1, fn, last, True)
            xs = _col_major_to_natural(xs)

    y_prompt = xp.reshape(BATCH, SEQ, D_MODEL)
    y_sample = xs.reshape(DEC_BATCH, DEC_SEQ, D_MODEL)
    return (y_prompt, y_sample, jnp.stack(rg_states, axis=1), jnp.stack(gla_states, axis=1))
```

```python
import functools

import jax
import jax.numpy as jnp
from jax import lax
from jax.experimental import pallas as pl
from jax.experimental.pallas import tpu as pltpu

F32 = jnp.float32
BF16 = jnp.bfloat16

D_MODEL = 1024
BATCH = 16
SEQ = 256
DEC_BATCH = 2
DEC_SEQ = 2048
GRID_W = 64
EPS = 1e-6
N_MOD = 9
D_FF = 2816
FFN_RES = 0.5
D_RNN = 1280
RG_BLOCKS = 10
RG_BLOCK = 128
CONV_W = 4
RG_C = 8.0
GLA_HEADS = 4
GLA_QK = 512
GLA_V = 1024
GLA_DK = 128
GLA_DV = 256
GLA_RANK = 16
GLA_TAU = 16.0
LOG2_E = 1.4426950408889634

N_PROMPT_TOK = BATCH * SEQ
N_TOK = N_PROMPT_TOK + DEC_BATCH * DEC_SEQ
N_GROUPS = 1 + DEC_BATCH
GROUP_ROWS = DEC_SEQ
PROMPT_BLOCKS = N_PROMPT_TOK // GROUP_ROWS

VMEM_LIMIT_BYTES = 56 * 1024 * 1024

TM = 512
HALF_TILES = N_PROMPT_TOK // TM
FF_CHUNK = 1408
RG_CT = 256
SCAN_L = 256
SCAN_V = GROUP_ROWS // SCAN_L
GLA_C = 128
GLA_LEVELS = tuple(GLA_C >> (l + 1) for l in range(GLA_C.bit_length() - 1))


def _cparams(*sem):
    return pltpu.CompilerParams(dimension_semantics=sem, vmem_limit_bytes=VMEM_LIMIT_BYTES)


def _group_of_tile(i, rows):
    return jnp.maximum((i * rows) // GROUP_ROWS - (PROMPT_BLOCKS - 1), 0)


def _sigmoid(x):
    return 1.0 / (1.0 + jnp.exp2(x * (-LOG2_E)))


def _softplus(x):
    return jnp.maximum(x, 0.0) + jnp.log1p(jnp.exp(-jnp.abs(x)))


def _rms(x):
    return x * lax.rsqrt(jnp.mean(x * x, axis=-1, keepdims=True) + EPS)


def _modulated(x, mod_ref, gain_ref, j):
    y = _rms(x) * gain_ref[j:j + 1, :]
    return y * (1.0 + mod_ref[3 * j + 1:3 * j + 2, :]) + mod_ref[3 * j:3 * j + 1, :]


def _ada_kernel(c_ref, w_ref, b_ref, o_ref):
    c = c_ref[...]
    s = (c * _sigmoid(c)).astype(BF16)
    o_ref[...] = jnp.dot(s, w_ref[...].astype(BF16), preferred_element_type=F32) + b_ref[...]


def _ada_mods(cvecs, ada_w, ada_b):
    depth = ada_w.shape[0]
    out = pl.pallas_call(
        _ada_kernel,
        grid=(depth, N_MOD),
        in_specs=[
            pl.BlockSpec((8, D_MODEL), lambda l, k: (0, 0)),
            pl.BlockSpec((None, D_MODEL, D_MODEL), lambda l, k: (l, 0, k)),
            pl.BlockSpec((None, None, 1, D_MODEL), lambda l, k: (l, k, 0, 0)),
        ],
        out_specs=pl.BlockSpec((None, None, 8, D_MODEL), lambda l, k: (l, k, 0, 0)),
        out_shape=jax.ShapeDtypeStruct((depth, N_MOD, 8, D_MODEL), F32),
        compiler_params=_cparams("arbitrary", "arbitrary"),
        name="ada_mod",
    )(cvecs, ada_w, ada_b.reshape(depth, N_MOD, 1, D_MODEL))
    return out[:, :, :N_GROUPS, :].transpose(0, 2, 1, 3)


def _ffn_kernel(*refs, j, final, split_in, split_out):
    n_in = 2 if split_in else 1
    x_refs, (mod_ref, gain_ref, win_ref, wout_ref, fn_ref), o_refs = refs[:n_in], refs[n_in:n_in + 5], refs[n_in + 5:]
    first_half = pl.program_id(0) < HALF_TILES
    x = jnp.where(first_half, x_refs[0][...], x_refs[1][...]) if split_in else x_refs[0][...]
    hb = _modulated(x, mod_ref, gain_ref, j).astype(BF16)
    acc = jnp.zeros((TM, D_MODEL), F32)
    for c in range(D_FF // FF_CHUNK):
        g = jnp.dot(hb, win_ref[:, c * FF_CHUNK:(c + 1) * FF_CHUNK], preferred_element_type=F32)
        u = jnp.dot(hb, win_ref[:, D_FF + c * FF_CHUNK:D_FF + (c + 1) * FF_CHUNK],
                    preferred_element_type=F32)
        a = (g * _sigmoid(g) * u).astype(BF16)
        acc = acc + jnp.dot(a, wout_ref[c * FF_CHUNK:(c + 1) * FF_CHUNK, :], preferred_element_type=F32)
    out = x + FFN_RES * mod_ref[3 * j + 2:3 * j + 3, :] * acc
    if final:
        out = _rms(out) * fn_ref[...]
    if split_out:
        @pl.when(first_half)
        def _():
            o_refs[0][...] = out

        @pl.when(jnp.logical_not(first_half))
        def _():
            o_refs[1][...] = out
    else:
        o_refs[0][...] = out


def _half_specs(split):
    if not split:
        return [pl.BlockSpec((TM, D_MODEL), lambda i: (i, 0))], [jax.ShapeDtypeStruct((N_TOK, D_MODEL), F32)]
    specs = [pl.BlockSpec((TM, D_MODEL), lambda i: (jnp.minimum(i, HALF_TILES - 1), 0)),
             pl.BlockSpec((TM, D_MODEL), lambda i: (jnp.maximum(i - HALF_TILES, 0), 0))]
    return specs, [jax.ShapeDtypeStruct((N_PROMPT_TOK, D_MODEL), F32)] * 2


def _half_ffn(xs, mods_l, gains_l, w_in, w_out, layer, half, final_norm, final, split_out):
    split_in = len(xs) == 2
    in_specs, _ = _half_specs(split_in)
    out_specs, out_shape = _half_specs(split_out)
    return pl.pallas_call(
        functools.partial(_ffn_kernel, j=2 * half, final=final, split_in=split_in, split_out=split_out),
        grid=(N_TOK // TM,),
        in_specs=in_specs + [
            pl.BlockSpec((None, N_MOD, D_MODEL), lambda i: (_group_of_tile(i, TM), 0, 0)),
            pl.BlockSpec((3, D_MODEL), lambda i: (0, 0)),
            pl.BlockSpec((None, None, D_MODEL, 2 * D_FF), lambda i: (layer, half, 0, 0),
                         pipeline_mode=pl.Buffered(1)),
            pl.BlockSpec((None, None, D_FF, D_MODEL), lambda i: (layer, half, 0, 0),
                         pipeline_mode=pl.Buffered(1)),
            pl.BlockSpec((1, D_MODEL), lambda i: (0, 0)),
        ],
        out_specs=out_specs,
        out_shape=out_shape,
        compiler_params=_cparams("arbitrary"),
        name="half_ffn",
    )(*xs, mods_l, gains_l, w_in, w_out, final_norm)


def _gelu_tanh(x):
    return 0.5 * x * (1.0 + jnp.tanh(0.7978845608028654 * (x + 0.044715 * (x * x * x))))


def _rg_proj_kernel(x_ref, mod_ref, gain_ref, w_ref, gate_ref, xbr_ref):
    hb = _modulated(x_ref[...], mod_ref, gain_ref, 1).astype(BF16)
    gate_ref[...] = _gelu_tanh(jnp.dot(hb, w_ref[:, :D_RNN], preferred_element_type=F32))
    xbr_ref[...] = jnp.dot(hb, w_ref[:, D_RNN:], preferred_element_type=F32)


def _rg_proj(x, mods_l, gains_l, w_in):
    return pl.pallas_call(
        _rg_proj_kernel,
        grid=(N_TOK // TM,),
        in_specs=[
            pl.BlockSpec((TM, D_MODEL), lambda i: (i, 0)),
            pl.BlockSpec((None, N_MOD, D_MODEL), lambda i: (_group_of_tile(i, TM), 0, 0)),
            pl.BlockSpec((3, D_MODEL), lambda i: (0, 0)),
            pl.BlockSpec((D_MODEL, 2 * D_RNN), lambda i: (0, 0), pipeline_mode=pl.Buffered(1)),
        ],
        out_specs=[pl.BlockSpec((TM, D_RNN), lambda i: (i, 0))] * 2,
        out_shape=[jax.ShapeDtypeStruct((N_TOK, D_RNN), F32)] * 2,
        compiler_params=_cparams("arbitrary"),
        name="rg_in_proj",
    )(x, mods_l, gains_l, w_in)


def _seg_shift(row, down):
    v = lax.broadcasted_iota(jnp.int32, row.shape, 0)
    if down:
        return jnp.where(v >= 1, pltpu.roll(row, 1, 0), 0.0)
    return jnp.where(v < SCAN_V - 1, pltpu.roll(row, SCAN_V - 1, 0), 0.0)


def _rg_scan_kernel(xbr_ref, gate_ref, cw_ref, cb_ref, wa_ref, ba_ref, wi_ref, bi_ref, lam_ref, h0_ref,
                    y_ref, stf_ref, stb_ref, af_ref, uf_ref, ab_ref, ub_ref):
    chain = jnp.where(pl.program_id(0) >= PROMPT_BLOCKS, 1.0, 0.0).astype(F32)
    nbt = RG_CT // RG_BLOCK
    rows = SCAN_L * SCAN_V
    v = SCAN_V

    for nb in range(nbt):
        ls = slice(nb * RG_BLOCK, (nb + 1) * RG_BLOCK)
        x = xbr_ref[:, ls]
        before = chain * _seg_shift(x[rows - v:], True)
        after0 = chain * _seg_shift(x[:v], False)
        after1 = chain * _seg_shift(x[v:2 * v], False)
        xm1 = jnp.concatenate([before, x[:rows - v]], axis=0)
        xp1 = jnp.concatenate([x[v:], after0], axis=0)
        xp2 = jnp.concatenate([x[2 * v:], after0, after1], axis=0)
        xc = (cw_ref[0:1, ls] * xm1 + cw_ref[1:2, ls] * x + cw_ref[2:3, ls] * xp1 + cw_ref[3:4, ls] * xp2
              + cb_ref[0:1, ls])
        xcb = xc.astype(BF16)
        for d, (a_ref, u_ref) in enumerate(((af_ref, uf_ref), (ab_ref, ub_ref))):
            r = _sigmoid(jnp.dot(xcb, wa_ref[d, nb], preferred_element_type=F32) + ba_ref[d:d + 1, ls])
            gi = _sigmoid(jnp.dot(xcb, wi_ref[d, nb], preferred_element_type=F32) + bi_ref[d:d + 1, ls])
            a = jnp.exp2(r * ((-RG_C * LOG2_E) * _softplus(-lam_ref[d:d + 1, ls])))
            a_ref[nb] = a
            u_ref[nb] = jnp.sqrt(1.0 - a * a) * (gi * xc)

    def step(i, carry):
        sf = pl.ds(pl.multiple_of(i * v, v), v)
        sb = pl.ds(pl.multiple_of((SCAN_L - 1 - i) * v, v), v)
        out = []
        for nb in range(nbt):
            hf, pf, hb, pb = carry[4 * nb:4 * nb + 4]
            a = af_ref[nb, sf, :]
            hf = a * hf + uf_ref[nb, sf, :]
            pf = a * pf
            uf_ref[nb, sf, :] = hf
            af_ref[nb, sf, :] = pf
            a = ab_ref[nb, sb, :]
            hb = a * hb + ub_ref[nb, sb, :]
            pb = a * pb
            ub_ref[nb, sb, :] = hb
            ab_ref[nb, sb, :] = pb
            out += [hf, pf, hb, pb]
        return tuple(out)

    zero = jnp.zeros((v, RG_BLOCK), F32)
    one = jnp.ones((v, RG_BLOCK), F32)
    lax.fori_loop(0, SCAN_L, step, (zero, one, zero, one) * nbt, unroll=4)

    for nb in range(nbt):
        ls = slice(nb * RG_BLOCK, (nb + 1) * RG_BLOCK)
        h_end, p_end = uf_ref[nb, rows - v:, :], af_ref[nb, rows - v:, :]
        entry = [chain * h0_ref[0:1, ls]]
        for s in range(v - 1):
            entry.append(chain * (h_end[s:s + 1] + p_end[s:s + 1] * entry[s]))
        ef = jnp.concatenate(entry, axis=0)
        h_start, p_start = ub_ref[nb, :v, :], ab_ref[nb, :v, :]
        entry = [chain * h0_ref[1:2, ls]]
        for s in range(v - 1, 0, -1):
            entry.append(chain * (h_start[s:s + 1] + p_start[s:s + 1] * entry[-1]))
        eb = jnp.concatenate(entry[::-1], axis=0)
        shape3 = (SCAN_L, v, RG_BLOCK)
        hf = uf_ref[nb].reshape(shape3) + af_ref[nb].reshape(shape3) * ef[None]
        hb = ub_ref[nb].reshape(shape3) + ab_ref[nb].reshape(shape3) * eb[None]
        stf_ref[:, ls] = hf[SCAN_L - 1]
        stb_ref[:, ls] = hb[0]
        y_ref[:, ls] = (hf + hb).reshape(rows, RG_BLOCK) * gate_ref[:, ls]


def _rg_scan(xbr, gate, conv_w, conv_b, w_a, b_a, w_i, b_i, lam, h0):
    n_blk = N_TOK // GROUP_ROWS
    nbt = RG_CT // RG_BLOCK
    tile = pl.BlockSpec((GROUP_ROWS, RG_CT), lambda b, c: (b, c))
    vec2 = pl.BlockSpec((2, RG_CT), lambda b, c: (0, c))
    wspec = pl.BlockSpec((2, nbt, RG_BLOCK, RG_BLOCK), lambda b, c: (0, c, 0, 0))
    return pl.pallas_call(
        _rg_scan_kernel,
        grid=(n_blk, D_RNN // RG_CT),
        in_specs=[
            tile, tile,
            pl.BlockSpec((CONV_W, RG_CT), lambda b, c: (0, c)),
            pl.BlockSpec((1, RG_CT), lambda b, c: (0, c)),
            wspec, vec2, wspec, vec2, vec2,
            pl.BlockSpec((None, 2, RG_CT), lambda b, c: (jnp.maximum(b - PROMPT_BLOCKS, 0), 0, c)),
        ],
        out_specs=[tile] + [pl.BlockSpec((SCAN_V, RG_CT), lambda b, c: (b, c))] * 2,
        out_shape=[jax.ShapeDtypeStruct((N_TOK, D_RNN), F32)]
        + [jax.ShapeDtypeStruct((n_blk * SCAN_V, D_RNN), F32)] * 2,
        scratch_shapes=[pltpu.VMEM((nbt, GROUP_ROWS, RG_BLOCK), F32)] * 4,
        compiler_params=_cparams("arbitrary", "arbitrary"),
        name="rg_scan",
    )(xbr, gate, conv_w, conv_b, w_a, b_a, w_i, b_i, lam, h0)


def _rg_out_kernel(y_ref, w_ref, x_ref, mod_ref, o_ref):
    upd = jnp.dot(y_ref[...].astype(BF16), w_ref[...], preferred_element_type=F32)
    o_ref[...] = x_ref[...] + mod_ref[5:6, :] * upd


def _rg_out_proj(y, w_out, x, mods_l):
    return pl.pallas_call(
        _rg_out_kernel,
        grid=(N_TOK // TM,),
        in_specs=[
            pl.BlockSpec((TM, D_RNN), lambda i: (i, 0)),
            pl.BlockSpec((D_RNN, D_MODEL), lambda i: (0, 0), pipeline_mode=pl.Buffered(1)),
            pl.BlockSpec((TM, D_MODEL), lambda i: (i, 0)),
            pl.BlockSpec((None, N_MOD, D_MODEL), lambda i: (_group_of_tile(i, TM), 0, 0)),
        ],
        out_specs=pl.BlockSpec((TM, D_MODEL), lambda i: (i, 0)),
        out_shape=jax.ShapeDtypeStruct((N_TOK, D_MODEL), F32),
        compiler_params=_cparams("arbitrary"),
        name="rg_out_proj",
    )(y, w_out, x, mods_l)


def _chunk_tri():
    ri = lax.broadcasted_iota(jnp.int32, (GLA_C, GLA_C), 0)
    ci = lax.broadcasted_iota(jnp.int32, (GLA_C, GLA_C), 1)
    return (ri >= ci).astype(BF16), (ri <= ci).astype(BF16)


def _split3(x):
    hi = x.astype(BF16)
    r1 = x - hi.astype(F32)
    mid = r1.astype(BF16)
    lo = (r1 - mid.astype(F32)).astype(BF16)
    return hi, mid, lo


def _gla_proj_kernel(x_ref, mod_ref, gain_ref, w_ref, wz_ref, w2_ref, bg_ref,
                     q_ref, k_ref, v_ref, r_ref, cum_ref):
    hb = _modulated(x_ref[...], mod_ref, gain_ref, 1).astype(BF16)
    q_ref[...] = jnp.dot(hb, w_ref[:, :GLA_QK], preferred_element_type=F32) * (GLA_DK ** -0.5)
    k_ref[...] = jnp.dot(hb, w_ref[:, GLA_QK:2 * GLA_QK], preferred_element_type=F32)
    v_ref[...] = jnp.dot(hb, w_ref[:, 2 * GLA_QK:2 * GLA_QK + GLA_V], preferred_element_type=F32).astype(BF16)
    r = jnp.dot(hb, w_ref[:, 2 * GLA_QK + GLA_V:], preferred_element_type=F32)
    r_ref[...] = r * _sigmoid(r)
    z = jnp.dot(hb, wz_ref[...], preferred_element_type=F32).astype(BF16)
    zz = jnp.dot(z, w2_ref[...], preferred_element_type=F32) + bg_ref[...]
    log_gate = -_softplus(-zz) * (1.0 / GLA_TAU)
    parts = _split3(log_gate)
    tri_f, tri_b = _chunk_tri()
    for ch in range(TM // GLA_C):
        rows = slice(ch * GLA_C, (ch + 1) * GLA_C)
        cum_ref[rows, :GLA_QK] = sum(jnp.dot(tri_f, p[rows, :GLA_QK], preferred_element_type=F32) for p in parts)
        cum_ref[rows, GLA_QK:] = sum(jnp.dot(tri_b, p[rows, GLA_QK:], preferred_element_type=F32) for p in parts)


def _gla_proj(x, mods_l, gains_l, w_main, w_z, w2cat, bg):
    def rows(w):
        return pl.BlockSpec((TM, w), lambda i: (i, 0))

    def whole(a):
        return pl.BlockSpec(a.shape, lambda i: (0,) * a.ndim, pipeline_mode=pl.Buffered(1))

    widths = (GLA_QK, GLA_QK, GLA_V, GLA_V, 2 * GLA_QK)
    dtypes = (F32, F32, BF16, F32, F32)
    return pl.pallas_call(
        _gla_proj_kernel,
        grid=(N_TOK // TM,),
        in_specs=[
            rows(D_MODEL),
            pl.BlockSpec((None, N_MOD, D_MODEL), lambda i: (_group_of_tile(i, TM), 0, 0)),
            pl.BlockSpec((3, D_MODEL), lambda i: (0, 0)),
            whole(w_main), whole(w_z), whole(w2cat), whole(bg),
        ],
        out_specs=[rows(w) for w in widths],
        out_shape=[jax.ShapeDtypeStruct((N_TOK, w), t) for w, t in zip(widths, dtypes)],
        compiler_params=_cparams("arbitrary"),
        name="gla_in_proj",
    )(x, mods_l, gains_l, w_main, w_z, w2cat, bg)


def _dot_tn(a, b):
    return lax.dot_general(a, b, (((0,), (0,)), ((), ())), preferred_element_type=F32)


def _dot_nt(a, b):
    return lax.dot_general(a, b, (((1,), (1,)), ((), ())), preferred_element_type=F32)


def _level_ref(cum_ref, row0, cum, m, backward):
    c = GLA_C
    pick = m - 1 if backward else m

    def bcast(r, n):
        return jnp.broadcast_to(cum_ref[pl.ds(row0 + r, 1), :], (n, GLA_DK))

    if m >= 4:
        return jnp.concatenate([bcast(g * 2 * m + pick, 2 * m) for g in range(c // (2 * m))], axis=0)
    pos = lax.broadcasted_iota(jnp.int32, (c, GLA_DK), 0)
    if m == 2:
        lo = jnp.concatenate([bcast(g * 8 + pick, 8) for g in range(c // 8)], axis=0)
        hi = jnp.concatenate([bcast(g * 8 + 4 + pick, 8) for g in range(c // 8)], axis=0)
        return jnp.where((pos & 4) == 0, lo, hi)
    if backward:
        return jnp.where((pos & 1) == 1, pltpu.roll(cum, 1, 0), cum)
    return jnp.where((pos & 1) == 0, pltpu.roll(cum, c - 1, 0), cum)


def _gla_intra(q_ref, k_ref, v_ref, cf_ref, cb_ref, row0, diag, masks, uppers):
    rows = pl.ds(row0, GLA_C)
    cf, cb = cf_ref[rows, :], cb_ref[rows, :]
    qb, kb = q_ref[rows, :].astype(BF16), k_ref[rows, :].astype(BF16)
    scores = jnp.where(diag, 2.0 * _dot_nt(qb, kb), 0.0)
    for lvl, m in enumerate(GLA_LEVELS):
        df = cf - _level_ref(cf_ref, row0, cf, m, False)
        db = cb - _level_ref(cb_ref, row0, cb, m, True)
        gq = jnp.exp2(jnp.where(uppers[lvl], df, db) * LOG2_E).astype(BF16)
        gk = jnp.exp2(jnp.where(uppers[lvl], db, df) * (-LOG2_E)).astype(BF16)
        scores = jnp.where(masks[lvl], _dot_nt(qb * gq, kb * gk), scores)
    return jnp.dot(scores.astype(BF16), v_ref[rows, :], preferred_element_type=F32)


def _gla_inter(q_ref, k_ref, v_ref, cum_ref, row0, st, backward):
    rows = pl.ds(row0, GLA_C)
    cum = cum_ref[rows, :]
    tot = cum_ref[pl.ds(row0 + (0 if backward else GLA_C - 1), 1), :]
    o = _dot_nt((q_ref[rows, :] * jnp.exp(cum)).astype(BF16), st.astype(BF16))
    kd = (k_ref[rows, :] * jnp.exp(tot - cum)).astype(BF16)
    return o, st * jnp.exp(tot) + _dot_tn(v_ref[rows, :], kd)


def _gla_kernel(*refs, t_len, has_s0):
    if has_s0:
        q_ref, k_ref, v_ref, cf_ref, cb_ref, r_ref, gn_ref, s0_ref, y_ref, sf_ref, sb_ref = refs
    else:
        q_ref, k_ref, v_ref, cf_ref, cb_ref, r_ref, gn_ref, y_ref, sout_ref, sf_ref, sb_ref = refs
    c = GLA_C
    n_chunks = t_len // c
    ri = lax.broadcasted_iota(jnp.int32, (c, c), 0)
    ci = lax.broadcasted_iota(jnp.int32, (c, c), 1)
    x = ri ^ ci
    masks = [(x >= m) & (x < 2 * m) for m in GLA_LEVELS]
    pos = lax.broadcasted_iota(jnp.int32, (c, GLA_DK), 0)
    uppers = [(pos & m) != 0 for m in GLA_LEVELS]

    def intra(i, carry):
        row0 = pl.multiple_of(i * c, c)
        y_ref[pl.ds(row0, c), :] = _gla_intra(q_ref, k_ref, v_ref, cf_ref, cb_ref, row0, ri == ci, masks, uppers)
        return carry

    lax.fori_loop(0, n_chunks, intra, 0, unroll=2)

    if has_s0:
        sf_ref[...] = s0_ref[0].T
        sb_ref[...] = s0_ref[1].T
    else:
        sf_ref[...] = jnp.zeros_like(sf_ref)
        sb_ref[...] = jnp.zeros_like(sb_ref)

    def inter(i, carry):
        rf = pl.multiple_of(i * c, c)
        rb = pl.multiple_of((n_chunks - 1 - i) * c, c)
        o, st = _gla_inter(q_ref, k_ref, v_ref, cf_ref, rf, sf_ref[...], False)
        y_ref[pl.ds(rf, c), :] += o
        sf_ref[...] = st
        o, st = _gla_inter(q_ref, k_ref, v_ref, cb_ref, rb, sb_ref[...], True)
        y_ref[pl.ds(rb, c), :] += o
        sb_ref[...] = st
        return carry

    lax.fori_loop(0, n_chunks, inter, 0, unroll=2)

    if not has_s0:
        sout_ref[0] = sf_ref[...].T
        sout_ref[1] = sb_ref[...].T
    o = y_ref[...]
    y_ref[...] = _rms(o) * gn_ref[...] * r_ref[...]


def _gla_mix(q, k, v, cum, r, gnorm, s0, *, t_len, row_block0, n_seq):
    has_s0 = s0 is not None

    def rows(w, off=0):
        return pl.BlockSpec((t_len, w), lambda s, h: (row_block0 + s, off + h))

    state_spec = pl.BlockSpec((None, 2, None, GLA_DK, GLA_DV), lambda s, h: (s, 0, h, 0, 0))
    in_specs = [rows(GLA_DK), rows(GLA_DK), rows(GLA_DV), rows(GLA_DK), rows(GLA_DK, GLA_HEADS),
                rows(GLA_DV), pl.BlockSpec((None, 1, GLA_DV), lambda s, h: (h, 0, 0))]
    args = [q, k, v, cum, cum, r, gnorm.reshape(GLA_HEADS, 1, GLA_DV)]
    y_spec = pl.BlockSpec((t_len, GLA_DV), lambda s, h: (s, h))
    y_shape = jax.ShapeDtypeStruct((n_seq * t_len, GLA_V), F32)
    if has_s0:
        in_specs.append(state_spec)
        args.append(s0)
        out_specs, out_shape = y_spec, y_shape
    else:
        out_specs = [y_spec, state_spec]
        out_shape = [y_shape, jax.ShapeDtypeStruct((n_seq, 2, GLA_HEADS, GLA_DK, GLA_DV), F32)]
    return pl.pallas_call(
        functools.partial(_gla_kernel, t_len=t_len, has_s0=has_s0),
        grid=(n_seq, GLA_HEADS),
        in_specs=in_specs,
        out_specs=out_specs,
        out_shape=out_shape,
        scratch_shapes=[pltpu.VMEM((GLA_DV, GLA_DK), F32)] * 2,
        compiler_params=_cparams("arbitrary", "arbitrary"),
        name="gla_mix_sample" if has_s0 else "gla_mix_prompt",
    )(*args)


def _gla_out_kernel(yp_ref, ys_ref, w_ref, x_ref, mod_ref, o_ref):
    is_prompt = pl.program_id(0) < HALF_TILES
    y = jnp.where(is_prompt, yp_ref[...], ys_ref[...]).astype(BF16)
    o_ref[...] = x_ref[...] + mod_ref[5:6, :] * jnp.dot(y, w_ref[...], preferred_element_type=F32)


def _gla_out_proj(y_p, y_s, w_out, x, mods_l):
    return pl.pallas_call(
        _gla_out_kernel,
        grid=(N_TOK // TM,),
        in_specs=[
            pl.BlockSpec((TM, GLA_V), lambda i: (jnp.minimum(i, HALF_TILES - 1), 0)),
            pl.BlockSpec((TM, GLA_V), lambda i: (jnp.maximum(i - HALF_TILES, 0), 0)),
            pl.BlockSpec((GLA_V, D_MODEL), lambda i: (0, 0), pipeline_mode=pl.Buffered(1)),
            pl.BlockSpec((TM, D_MODEL), lambda i: (i, 0)),
            pl.BlockSpec((None, N_MOD, D_MODEL), lambda i: (_group_of_tile(i, TM), 0, 0)),
        ],
        out_specs=pl.BlockSpec((TM, D_MODEL), lambda i: (i, 0)),
        out_shape=jax.ShapeDtypeStruct((N_TOK, D_MODEL), F32),
        compiler_params=_cparams("arbitrary"),
        name="gla_out_proj",
    )(y_p, y_s, w_out, x, mods_l)


def _natural_to_scan(x_half):
    x4 = x_half.reshape(-1, SCAN_V, SCAN_L, D_MODEL)
    return x4.swapaxes(1, 2).reshape(-1, D_MODEL)


def _scan_to_natural(x_half):
    x4 = x_half.reshape(-1, SCAN_L, SCAN_V, D_MODEL)
    return x4.swapaxes(1, 2).reshape(-1, D_MODEL)


def _natural_to_col_major(x_half):
    x4 = x_half.reshape(DEC_BATCH, DEC_SEQ // GRID_W, GRID_W, D_MODEL)
    return x4.swapaxes(1, 2).reshape(-1, D_MODEL)


def _col_major_to_natural(x_half):
    x4 = x_half.reshape(DEC_BATCH, GRID_W, DEC_SEQ // GRID_W, D_MODEL)
    return x4.swapaxes(1, 2).reshape(-1, D_MODEL)


def kernel(x_prompt, x_sample, c, state_rglru, state_gla, c_ctx, ada_w, ada_b, norm_g, ffn_w_in, ffn_w_out,
           rg_w_in, rg_conv_w, rg_conv_b, rg_w_a, rg_b_a, rg_w_i, rg_b_i, rg_lambda, rg_w_out,
           gla_w_in, gla_w_g2, gla_b_g, gla_norm, gla_w_out, final_norm):
    depth = ada_w.shape[0]
    cvecs = jnp.concatenate([c_ctx[None, :], c, jnp.zeros((8 - N_GROUPS, D_MODEL), F32)], axis=0)
    mods = _ada_mods(cvecs, ada_w, ada_b)
    fn = final_norm.reshape(1, D_MODEL)
    ffn_in, ffn_out = ffn_w_in.astype(BF16), ffn_w_out.astype(BF16)
    rg_states, gla_states = [], []

    xp = x_prompt.reshape(N_PROMPT_TOK, D_MODEL)
    xs = x_sample.reshape(DEC_BATCH * DEC_SEQ, D_MODEL)
    for i in range(depth):
        mods_l, gains_l = mods[i], norm_g[i]
        j = i // 2
        last = i == depth - 1
        if i % 2 == 0:
            halves = (_natural_to_scan(xp), _natural_to_scan(xs))
            x = _half_ffn(halves, mods_l, gains_l, ffn_in, ffn_out, i, 0, fn, False, False)[0]
            gate, xbr = _rg_proj(x, mods_l, gains_l, rg_w_in[j].astype(BF16))
            y, st_f, st_b = _rg_scan(xbr, gate, rg_conv_w[j], rg_conv_b[j].reshape(1, D_RNN),
                                     rg_w_a[j].astype(BF16), rg_b_a[j], rg_w_i[j].astype(BF16), rg_b_i[j],
                                     rg_lambda[j], state_rglru[:, j])
            rg_states.append(jnp.stack([st_f[:BATCH], st_b[:BATCH]], axis=1))
            x = _rg_out_proj(y, rg_w_out[j].astype(BF16), x, mods_l)
            xp, xs = _half_ffn((x,), mods_l, gains_l, ffn_in, ffn_out, i, 1, fn, last, True)
            xp, xs = _scan_to_natural(xp), _scan_to_natural(xs)
        else:
            halves = (xp, _natural_to_col_major(xs))
            x = _half_ffn(halves, mods_l, gains_l, ffn_in, ffn_out, i, 0, fn, False, False)[0]
            w_in = gla_w_in[j]
            n_main = 2 * GLA_QK + 2 * GLA_V
            w_z = jnp.pad(w_in[:, n_main:], ((0, 0), (0, 128 - 2 * GLA_RANK))).astype(BF16)
            w2cat = jnp.zeros((128, 2 * GLA_QK), F32)
            w2cat = w2cat.at[:GLA_RANK, :GLA_QK].set(gla_w_g2[j, 0])
            w2cat = w2cat.at[GLA_RANK:2 * GLA_RANK, GLA_QK:].set(gla_w_g2[j, 1]).astype(BF16)
            q, k, v, r, cum = _gla_proj(x, mods_l, gains_l, w_in[:, :n_main].astype(BF16), w_z, w2cat,
                                        gla_b_g[j].reshape(1, 2 * GLA_QK))
            y_p, st = _gla_mix(q, k, v, cum, r, gla_norm[j], None, t_len=SEQ, row_block0=0, n_seq=BATCH)
            y_s = _gla_mix(q, k, v, cum, r, gla_norm[j], state_gla[:, j], t_len=DEC_SEQ,
                           row_block0=N_PROMPT_TOK // DEC_SEQ, n_seq=DEC_BATCH)
            gla_states.append(st)
            x = _gla_out_proj(y_p, y_s, gla_w_out[j].astype(BF16), x, mods_l)
            xp, xs = _half_ffn((x,), mods_l, gains_l, ffn_in, ffn_out, i, 1, fn, last, True)
            xs = _col_major_to_natural(xs)

    y_prompt = xp.reshape(BATCH, SEQ, D_MODEL)
    y_sample = xs.reshape(DEC_BATCH, DEC_SEQ, D_MODEL)
    return (y_prompt, y_sample, jnp.stack(rg_states, axis=1), jnp.stack(gla_states, axis=1))
```

```python
import functools

import jax
import jax.numpy as jnp
from jax import lax
from jax.experimental import pallas as pl
from jax.experimental.pallas import tpu as pltpu

F32 = jnp.float32
BF16 = jnp.bfloat16

D_MODEL = 1024
BATCH = 16
SEQ = 256
DEC_BATCH = 2
DEC_SEQ = 2048
GRID_W = 64
EPS = 1e-6
N_MOD = 9
D_FF = 2816
FFN_RES = 0.5
D_RNN = 1280
RG_BLOCKS = 10
RG_BLOCK = 128
CONV_W = 4
RG_C = 8.0
GLA_HEADS = 4
GLA_QK = 512
GLA_V = 1024
GLA_DK = 128
GLA_DV = 256
GLA_RANK = 16
GLA_TAU = 16.0
LOG2_E = 1.4426950408889634

N_PROMPT_TOK = BATCH * SEQ
N_TOK = N_PROMPT_TOK + DEC_BATCH * DEC_SEQ
N_GROUPS = 1 + DEC_BATCH
GROUP_ROWS = DEC_SEQ
PROMPT_BLOCKS = N_PROMPT_TOK // GROUP_ROWS

VMEM_LIMIT_BYTES = 56 * 1024 * 1024

TM = 512
HALF_TILES = N_PROMPT_TOK // TM
ADA_SLAB = 256
MXU_DIM = 256
FF_CHUNKS = (0, 6 * MXU_DIM, D_FF)
RG_CT = 256
SCAN_L = 256
SCAN_V = GROUP_ROWS // SCAN_L
GLA_C = 128
GLA_LEVELS = tuple(GLA_C >> (l + 1) for l in range(GLA_C.bit_length() - 1))


def _cparams(*sem):
    return pltpu.CompilerParams(dimension_semantics=sem, vmem_limit_bytes=VMEM_LIMIT_BYTES)


def _group_of_tile(i, rows):
    return jnp.maximum((i * rows) // GROUP_ROWS - (PROMPT_BLOCKS - 1), 0)


def _sigmoid(x):
    return 1.0 / (1.0 + jnp.exp2(x * (-LOG2_E)))


def _softplus(x):
    return jnp.maximum(x, 0.0) + jnp.log1p(jnp.exp(-jnp.abs(x)))


def _rms(x):
    return x * lax.rsqrt(jnp.mean(x * x, axis=-1, keepdims=True) + EPS)


def _modulated(x, mod_ref, gain_ref, j):
    y = _rms(x) * gain_ref[j:j + 1, :]
    return y * (1.0 + mod_ref[3 * j + 1:3 * j + 2, :]) + mod_ref[3 * j:3 * j + 1, :]


def _ada_kernel(c_ref, w_ref, b_ref, o_ref):
    @pl.when(pl.program_id(1) == 0)
    def _():
        o_ref[...] = jnp.broadcast_to(b_ref[...], o_ref.shape)

    c = c_ref[...]
    s = (c * _sigmoid(c)).astype(BF16)
    o_ref[...] += jnp.dot(s, w_ref[...].astype(BF16), preferred_element_type=F32)


def _ada_mods(cvecs, ada_w, ada_b):
    depth = ada_w.shape[0]
    n_slab = D_MODEL // ADA_SLAB
    c_slabs = cvecs.reshape(8, n_slab, ADA_SLAB).swapaxes(0, 1)
    out = pl.pallas_call(
        _ada_kernel,
        grid=(depth, n_slab),
        in_specs=[
            pl.BlockSpec((None, 8, ADA_SLAB), lambda l, k: (k, 0, 0)),
            pl.BlockSpec((None, ADA_SLAB, N_MOD * D_MODEL), lambda l, k: (l, k, 0)),
            pl.BlockSpec((None, 1, N_MOD * D_MODEL), lambda l, k: (l, 0, 0)),
        ],
        out_specs=pl.BlockSpec((None, 8, N_MOD * D_MODEL), lambda l, k: (l, 0, 0)),
        out_shape=jax.ShapeDtypeStruct((depth, 8, N_MOD * D_MODEL), F32),
        compiler_params=_cparams("arbitrary", "arbitrary"),
        name="ada_mod",
    )(c_slabs, ada_w, ada_b.reshape(depth, 1, N_MOD * D_MODEL))
    return out[:, :N_GROUPS, :].reshape(depth, N_GROUPS, N_MOD, D_MODEL)


def _pick_half(refs):
    if len(refs) == 1:
        return refs[0][...]
    return jnp.where(pl.program_id(0) < HALF_TILES, refs[0][...], refs[1][...])


def _ffn_kernel(*refs, j, final, n_x, n_y, split_out):
    x_refs, y_refs, refs = refs[:n_x], refs[n_x:n_x + n_y], refs[n_x + n_y:]
    if n_y:
        wmix_ref, refs = refs[0], refs[1:]
    mod_ref, gain_ref, win_ref, wout_ref, fn_ref = refs[:5]
    o_refs = refs[5:]
    x = _pick_half(x_refs)
    if n_y:
        y = _pick_half(y_refs).astype(BF16)
        x = x + mod_ref[5:6, :] * jnp.dot(y, wmix_ref[...], preferred_element_type=F32)
    hb = _modulated(x, mod_ref, gain_ref, j).astype(BF16)
    acc = jnp.zeros((TM, D_MODEL), F32)
    for lo, hi in zip(FF_CHUNKS[:-1], FF_CHUNKS[1:]):
        g = jnp.dot(hb, win_ref[:, lo:hi], preferred_element_type=F32)
        u = jnp.dot(hb, win_ref[:, D_FF + lo:D_FF + hi], preferred_element_type=F32)
        a = (g * _sigmoid(g) * u).astype(BF16)
        acc = acc + jnp.dot(a, wout_ref[lo:hi, :], preferred_element_type=F32)
    out = x + FFN_RES * mod_ref[3 * j + 2:3 * j + 3, :] * acc
    if final:
        out = _rms(out) * fn_ref[...]
    if split_out:
        first_half = pl.program_id(0) < HALF_TILES

        @pl.when(first_half)
        def _():
            o_refs[0][...] = out

        @pl.when(jnp.logical_not(first_half))
        def _():
            o_refs[1][...] = out
    else:
        o_refs[0][...] = out


def _row_specs(width, split):
    if not split:
        return [pl.BlockSpec((TM, width), lambda i: (i, 0))]
    return [pl.BlockSpec((TM, width), lambda i: (jnp.minimum(i, HALF_TILES - 1), 0)),
            pl.BlockSpec((TM, width), lambda i: (jnp.maximum(i - HALF_TILES, 0), 0))]


def _half_ffn(xs, mods_l, gains_l, w_in, w_out, layer, half, final_norm, final, split_out, mix=None):
    ys, w_mix = mix if mix is not None else ((), None)
    in_specs = _row_specs(D_MODEL, len(xs) == 2)
    args = list(xs)
    if ys:
        in_specs += _row_specs(ys[0].shape[1], len(ys) == 2)
        in_specs.append(pl.BlockSpec(w_mix.shape, lambda i: (0, 0), pipeline_mode=pl.Buffered(1)))
        args += list(ys) + [w_mix]
    n_rows = N_PROMPT_TOK if split_out else N_TOK
    return pl.pallas_call(
        functools.partial(_ffn_kernel, j=2 * half, final=final, n_x=len(xs), n_y=len(ys), split_out=split_out),
        grid=(N_TOK // TM,),
        in_specs=in_specs + [
            pl.BlockSpec((None, N_MOD, D_MODEL), lambda i: (_group_of_tile(i, TM), 0, 0)),
            pl.BlockSpec((3, D_MODEL), lambda i: (0, 0)),
            pl.BlockSpec((None, None, D_MODEL, 2 * D_FF), lambda i: (layer, half, 0, 0),
                         pipeline_mode=pl.Buffered(1)),
            pl.BlockSpec((None, None, D_FF, D_MODEL), lambda i: (layer, half, 0, 0),
                         pipeline_mode=pl.Buffered(1)),
            pl.BlockSpec((1, D_MODEL), lambda i: (0, 0)),
        ],
        out_specs=_row_specs(D_MODEL, split_out),
        out_shape=[jax.ShapeDtypeStruct((n_rows, D_MODEL), F32)] * (2 if split_out else 1),
        compiler_params=_cparams("arbitrary"),
        name="half_ffn",
    )(*args, mods_l, gains_l, w_in, w_out, final_norm)


def _gelu_tanh(x):
    return 0.5 * x * (1.0 + jnp.tanh(0.7978845608028654 * (x + 0.044715 * (x * x * x))))


def _rg_proj_kernel(x_ref, mod_ref, gain_ref, w_ref, gate_ref, xbr_ref):
    hb = _modulated(x_ref[...], mod_ref, gain_ref, 1).astype(BF16)
    gate_ref[...] = _gelu_tanh(jnp.dot(hb, w_ref[:, :D_RNN], preferred_element_type=F32))
    xbr_ref[...] = jnp.dot(hb, w_ref[:, D_RNN:], preferred_element_type=F32)


def _rg_proj(x, mods_l, gains_l, w_in):
    return pl.pallas_call(
        _rg_proj_kernel,
        grid=(N_TOK // TM,),
        in_specs=[
            pl.BlockSpec((TM, D_MODEL), lambda i: (i, 0)),
            pl.BlockSpec((None, N_MOD, D_MODEL), lambda i: (_group_of_tile(i, TM), 0, 0)),
            pl.BlockSpec((3, D_MODEL), lambda i: (0, 0)),
            pl.BlockSpec((D_MODEL, 2 * D_RNN), lambda i: (0, 0), pipeline_mode=pl.Buffered(1)),
        ],
        out_specs=[pl.BlockSpec((TM, D_RNN), lambda i: (i, 0))] * 2,
        out_shape=[jax.ShapeDtypeStruct((N_TOK, D_RNN), F32)] * 2,
        compiler_params=_cparams("arbitrary"),
        name="rg_in_proj",
    )(x, mods_l, gains_l, w_in)


def _seg_shift(row, down):
    v = lax.broadcasted_iota(jnp.int32, row.shape, 0)
    if down:
        return jnp.where(v >= 1, pltpu.roll(row, 1, 0), 0.0)
    return jnp.where(v < SCAN_V - 1, pltpu.roll(row, SCAN_V - 1, 0), 0.0)


def _rg_scan_kernel(xbr_ref, gate_ref, cw_ref, cb_ref, wa_ref, ba_ref, wi_ref, bi_ref, lam_ref, h0_ref,
                    y_ref, stf_ref, stb_ref, af_ref, uf_ref, ab_ref, ub_ref):
    chain = jnp.where(pl.program_id(0) >= PROMPT_BLOCKS, 1.0, 0.0).astype(F32)
    nbt = RG_CT // RG_BLOCK
    rows = SCAN_L * SCAN_V
    v = SCAN_V

    for nb in range(nbt):
        ls = slice(nb * RG_BLOCK, (nb + 1) * RG_BLOCK)
        x = xbr_ref[:, ls]
        before = chain * _seg_shift(x[rows - v:], True)
        after0 = chain * _seg_shift(x[:v], False)
        after1 = chain * _seg_shift(x[v:2 * v], False)
        xm1 = jnp.concatenate([before, x[:rows - v]], axis=0)
        xp1 = jnp.concatenate([x[v:], after0], axis=0)
        xp2 = jnp.concatenate([x[2 * v:], after0, after1], axis=0)
        xc = (cw_ref[0:1, ls] * xm1 + cw_ref[1:2, ls] * x + cw_ref[2:3, ls] * xp1 + cw_ref[3:4, ls] * xp2
              + cb_ref[0:1, ls])
        xcb = xc.astype(BF16)
        for d, (a_ref, u_ref) in enumerate(((af_ref, uf_ref), (ab_ref, ub_ref))):
            r = _sigmoid(jnp.dot(xcb, wa_ref[d, nb], preferred_element_type=F32) + ba_ref[d:d + 1, ls])
            gi = _sigmoid(jnp.dot(xcb, wi_ref[d, nb], preferred_element_type=F32) + bi_ref[d:d + 1, ls])
            a = jnp.exp2(r * ((-RG_C * LOG2_E) * _softplus(-lam_ref[d:d + 1, ls])))
            a_ref[nb] = a
            z = 1.0 - a * a
            u_ref[nb] = jnp.where(z > 0.0, z * lax.rsqrt(z), 0.0) * (gi * xc)

    def step(i, carry):
        sf = pl.ds(pl.multiple_of(i * v, v), v)
        sb = pl.ds(pl.multiple_of((SCAN_L - 1 - i) * v, v), v)
        out = []
        for nb in range(nbt):
            hf, pf, hb, pb = carry[4 * nb:4 * nb + 4]
            a = af_ref[nb, sf, :]
            hf = a * hf + uf_ref[nb, sf, :]
            pf = a * pf
            uf_ref[nb, sf, :] = hf
            af_ref[nb, sf, :] = pf
            a = ab_ref[nb, sb, :]
            hb = a * hb + ub_ref[nb, sb, :]
            pb = a * pb
            ub_ref[nb, sb, :] = hb
            ab_ref[nb, sb, :] = pb
            out += [hf, pf, hb, pb]
        return tuple(out)

    zero = jnp.zeros((v, RG_BLOCK), F32)
    one = jnp.ones((v, RG_BLOCK), F32)
    lax.fori_loop(0, SCAN_L, step, (zero, one, zero, one) * nbt, unroll=4)

    for nb in range(nbt):
        ls = slice(nb * RG_BLOCK, (nb + 1) * RG_BLOCK)
        h_end, p_end = uf_ref[nb, rows - v:, :], af_ref[nb, rows - v:, :]
        entry = [chain * h0_ref[0:1, ls]]
        for s in range(v - 1):
            entry.append(chain * (h_end[s:s + 1] + p_end[s:s + 1] * entry[s]))
        ef = jnp.concatenate(entry, axis=0)
        h_start, p_start = ub_ref[nb, :v, :], ab_ref[nb, :v, :]
        entry = [chain * h0_ref[1:2, ls]]
        for s in range(v - 1, 0, -1):
            entry.append(chain * (h_start[s:s + 1] + p_start[s:s + 1] * entry[-1]))
        eb = jnp.concatenate(entry[::-1], axis=0)
        shape3 = (SCAN_L, v, RG_BLOCK)
        hf = uf_ref[nb].reshape(shape3) + af_ref[nb].reshape(shape3) * ef[None]
        hb = ub_ref[nb].reshape(shape3) + ab_ref[nb].reshape(shape3) * eb[None]
        stf_ref[:, ls] = hf[SCAN_L - 1]
        stb_ref[:, ls] = hb[0]
        y_ref[:, ls] = (hf + hb).reshape(rows, RG_BLOCK) * gate_ref[:, ls]


def _rg_scan(xbr, gate, conv_w, conv_b, w_a, b_a, w_i, b_i, lam, h0):
    n_blk = N_TOK // GROUP_ROWS
    nbt = RG_CT // RG_BLOCK
    tile = pl.BlockSpec((GROUP_ROWS, RG_CT), lambda b, c: (b, c))
    vec2 = pl.BlockSpec((2, RG_CT), lambda b, c: (0, c))
    wspec = pl.BlockSpec((2, nbt, RG_BLOCK, RG_BLOCK), lambda b, c: (0, c, 0, 0))
    return pl.pallas_call(
        _rg_scan_kernel,
        grid=(n_blk, D_RNN // RG_CT),
        in_specs=[
            tile, tile,
            pl.BlockSpec((CONV_W, RG_CT), lambda b, c: (0, c)),
            pl.BlockSpec((1, RG_CT), lambda b, c: (0, c)),
            wspec, vec2, wspec, vec2, vec2,
            pl.BlockSpec((None, 2, RG_CT), lambda b, c: (jnp.maximum(b - PROMPT_BLOCKS, 0), 0, c)),
        ],
        out_specs=[tile] + [pl.BlockSpec((SCAN_V, RG_CT), lambda b, c: (b, c))] * 2,
        out_shape=[jax.ShapeDtypeStruct((N_TOK, D_RNN), F32)]
        + [jax.ShapeDtypeStruct((n_blk * SCAN_V, D_RNN), F32)] * 2,
        scratch_shapes=[pltpu.VMEM((nbt, GROUP_ROWS, RG_BLOCK), F32)] * 4,
        compiler_params=_cparams("arbitrary", "arbitrary"),
        name="rg_scan",
    )(xbr, gate, conv_w, conv_b, w_a, b_a, w_i, b_i, lam, h0)


def _chunk_tri():
    ri = lax.broadcasted_iota(jnp.int32, (GLA_C, GLA_C), 0)
    ci = lax.broadcasted_iota(jnp.int32, (GLA_C, GLA_C), 1)
    return (ri >= ci).astype(BF16), (ri <= ci).astype(BF16)


def _split3(x):
    hi = x.astype(BF16)
    r1 = x - hi.astype(F32)
    mid = r1.astype(BF16)
    lo = (r1 - mid.astype(F32)).astype(BF16)
    return hi, mid, lo


def _gla_proj_kernel(x_ref, mod_ref, gain_ref, w_ref, wz_ref, w2_ref, bg_ref,
                     q_ref, k_ref, v_ref, r_ref, cum_ref):
    hb = _modulated(x_ref[...], mod_ref, gain_ref, 1).astype(BF16)
    q_ref[...] = jnp.dot(hb, w_ref[:, :GLA_QK], preferred_element_type=F32) * (GLA_DK ** -0.5)
    k_ref[...] = jnp.dot(hb, w_ref[:, GLA_QK:2 * GLA_QK], preferred_element_type=F32)
    v_ref[...] = jnp.dot(hb, w_ref[:, 2 * GLA_QK:2 * GLA_QK + GLA_V], preferred_element_type=F32).astype(BF16)
    r = jnp.dot(hb, w_ref[:, 2 * GLA_QK + GLA_V:], preferred_element_type=F32)
    r_ref[...] = r * _sigmoid(r)
    z = jnp.dot(hb, wz_ref[...], preferred_element_type=F32).astype(BF16)
    zz = jnp.dot(z, w2_ref[...], preferred_element_type=F32) + bg_ref[...]
    log_gate = -_softplus(-zz) * (1.0 / GLA_TAU)
    parts = _split3(log_gate)
    tri_f, tri_b = _chunk_tri()
    for ch in range(TM // GLA_C):
        rows = slice(ch * GLA_C, (ch + 1) * GLA_C)
        cum_ref[rows, :GLA_QK] = sum(jnp.dot(tri_f, p[rows, :GLA_QK], preferred_element_type=F32) for p in parts)
        cum_ref[rows, GLA_QK:] = sum(jnp.dot(tri_b, p[rows, GLA_QK:], preferred_element_type=F32) for p in parts)


def _gla_proj(x, mods_l, gains_l, w_main, w_z, w2cat, bg):
    def rows(w):
        return pl.BlockSpec((TM, w), lambda i: (i, 0))

    def whole(a):
        return pl.BlockSpec(a.shape, lambda i: (0,) * a.ndim, pipeline_mode=pl.Buffered(1))

    widths = (GLA_QK, GLA_QK, GLA_V, GLA_V, 2 * GLA_QK)
    dtypes = (F32, F32, BF16, F32, F32)
    return pl.pallas_call(
        _gla_proj_kernel,
        grid=(N_TOK // TM,),
        in_specs=[
            rows(D_MODEL),
            pl.BlockSpec((None, N_MOD, D_MODEL), lambda i: (_group_of_tile(i, TM), 0, 0)),
            pl.BlockSpec((3, D_MODEL), lambda i: (0, 0)),
            whole(w_main), whole(w_z), whole(w2cat), whole(bg),
        ],
        out_specs=[rows(w) for w in widths],
        out_shape=[jax.ShapeDtypeStruct((N_TOK, w), t) for w, t in zip(widths, dtypes)],
        compiler_params=_cparams("arbitrary"),
        name="gla_in_proj",
    )(x, mods_l, gains_l, w_main, w_z, w2cat, bg)


def _dot_tn(a, b):
    return lax.dot_general(a, b, (((0,), (0,)), ((), ())), preferred_element_type=F32)


def _dot_nt(a, b):
    return lax.dot_general(a, b, (((1,), (1,)), ((), ())), preferred_element_type=F32)


def _level_ref(cum_ref, row0, cum, m, backward):
    c = GLA_C
    pick = m - 1 if backward else m

    def bcast(r, n):
        return jnp.broadcast_to(cum_ref[pl.ds(row0 + r, 1), :], (n, GLA_DK))

    if m >= 4:
        return jnp.concatenate([bcast(g * 2 * m + pick, 2 * m) for g in range(c // (2 * m))], axis=0)
    pos = lax.broadcasted_iota(jnp.int32, (c, GLA_DK), 0)
    if m == 2:
        lo = jnp.concatenate([bcast(g * 8 + pick, 8) for g in range(c // 8)], axis=0)
        hi = jnp.concatenate([bcast(g * 8 + 4 + pick, 8) for g in range(c // 8)], axis=0)
        return jnp.where((pos & 4) == 0, lo, hi)
    if backward:
        return jnp.where((pos & 1) == 1, pltpu.roll(cum, 1, 0), cum)
    return jnp.where((pos & 1) == 0, pltpu.roll(cum, c - 1, 0), cum)


def _gla_intra(q_ref, k_ref, v_ref, cf_ref, cb_ref, row0, diag, masks, uppers):
    rows = pl.ds(row0, GLA_C)
    cf, cb = cf_ref[rows, :], cb_ref[rows, :]
    qb, kb = q_ref[rows, :].astype(BF16), k_ref[rows, :].astype(BF16)
    scores = jnp.where(diag, 2.0 * _dot_nt(qb, kb), 0.0)
    for lvl, m in enumerate(GLA_LEVELS):
        df = cf - _level_ref(cf_ref, row0, cf, m, False)
        db = cb - _level_ref(cb_ref, row0, cb, m, True)
        gq = jnp.exp2(jnp.where(uppers[lvl], df, db) * LOG2_E).astype(BF16)
        gk = jnp.exp2(jnp.where(uppers[lvl], db, df) * (-LOG2_E)).astype(BF16)
        scores = jnp.where(masks[lvl], _dot_nt(qb * gq, kb * gk), scores)
    return jnp.dot(scores.astype(BF16), v_ref[rows, :], preferred_element_type=F32)


def _gla_inter(q_ref, k_ref, v_ref, cum_ref, row0, st, backward):
    rows = pl.ds(row0, GLA_C)
    cum = cum_ref[rows, :]
    tot = cum_ref[pl.ds(row0 + (0 if backward else GLA_C - 1), 1), :]
    o = _dot_nt((q_ref[rows, :] * jnp.exp(cum)).astype(BF16), st.astype(BF16))
    kd = (k_ref[rows, :] * jnp.exp(tot - cum)).astype(BF16)
    return o, st * jnp.exp(tot) + _dot_tn(v_ref[rows, :], kd)


def _gla_kernel(*refs, t_len, has_s0):
    if has_s0:
        q_ref, k_ref, v_ref, cf_ref, cb_ref, r_ref, gn_ref, s0_ref, y_ref, sf_ref, sb_ref = refs
    else:
        q_ref, k_ref, v_ref, cf_ref, cb_ref, r_ref, gn_ref, y_ref, sout_ref, sf_ref, sb_ref = refs
    c = GLA_C
    n_chunks = t_len // c
    ri = lax.broadcasted_iota(jnp.int32, (c, c), 0)
    ci = lax.broadcasted_iota(jnp.int32, (c, c), 1)
    x = ri ^ ci
    masks = [(x >= m) & (x < 2 * m) for m in GLA_LEVELS]
    pos = lax.broadcasted_iota(jnp.int32, (c, GLA_DK), 0)
    uppers = [(pos & m) != 0 for m in GLA_LEVELS]

    def intra(i, carry):
        row0 = pl.multiple_of(i * c, c)
        y_ref[pl.ds(row0, c), :] = _gla_intra(q_ref, k_ref, v_ref, cf_ref, cb_ref, row0, ri == ci, masks, uppers)
        return carry

    lax.fori_loop(0, n_chunks, intra, 0, unroll=2)

    if has_s0:
        sf_ref[...] = s0_ref[0].T
        sb_ref[...] = s0_ref[1].T
    else:
        sf_ref[...] = jnp.zeros_like(sf_ref)
        sb_ref[...] = jnp.zeros_like(sb_ref)

    def inter(i, carry):
        rf = pl.multiple_of(i * c, c)
        rb = pl.multiple_of((n_chunks - 1 - i) * c, c)
        o, st = _gla_inter(q_ref, k_ref, v_ref, cf_ref, rf, sf_ref[...], False)
        y_ref[pl.ds(rf, c), :] += o
        sf_ref[...] = st
        o, st = _gla_inter(q_ref, k_ref, v_ref, cb_ref, rb, sb_ref[...], True)
        y_ref[pl.ds(rb, c), :] += o
        sb_ref[...] = st
        return carry

    lax.fori_loop(0, n_chunks, inter, 0, unroll=2)

    if not has_s0:
        sout_ref[0] = sf_ref[...].T
        sout_ref[1] = sb_ref[...].T
    o = y_ref[...]
    y_ref[...] = _rms(o) * gn_ref[...] * r_ref[...]


def _gla_mix(q, k, v, cum, r, gnorm, s0, *, t_len, row_block0, n_seq):
    has_s0 = s0 is not None

    def rows(w, off=0):
        return pl.BlockSpec((t_len, w), lambda s, h: (row_block0 + s, off + h))

    state_spec = pl.BlockSpec((None, 2, None, GLA_DK, GLA_DV), lambda s, h: (s, 0, h, 0, 0))
    in_specs = [rows(GLA_DK), rows(GLA_DK), rows(GLA_DV), rows(GLA_DK), rows(GLA_DK, GLA_HEADS),
                rows(GLA_DV), pl.BlockSpec((None, 1, GLA_DV), lambda s, h: (h, 0, 0))]
    args = [q, k, v, cum, cum, r, gnorm.reshape(GLA_HEADS, 1, GLA_DV)]
    y_spec = pl.BlockSpec((t_len, GLA_DV), lambda s, h: (s, h))
    y_shape = jax.ShapeDtypeStruct((n_seq * t_len, GLA_V), F32)
    if has_s0:
        in_specs.append(state_spec)
        args.append(s0)
        out_specs, out_shape = y_spec, y_shape
    else:
        out_specs = [y_spec, state_spec]
        out_shape = [y_shape, jax.ShapeDtypeStruct((n_seq, 2, GLA_HEADS, GLA_DK, GLA_DV), F32)]
    return pl.pallas_call(
        functools.partial(_gla_kernel, t_len=t_len, has_s0=has_s0),
        grid=(n_seq, GLA_HEADS),
        in_specs=in_specs,
        out_specs=out_specs,
        out_shape=out_shape,
        scratch_shapes=[pltpu.VMEM((GLA_DV, GLA_DK), F32)] * 2,
        compiler_params=_cparams("arbitrary", "arbitrary"),
        name="gla_mix_sample" if has_s0 else "gla_mix_prompt",
    )(*args)


def _natural_to_scan(x_half):
    x4 = x_half.reshape(-1, SCAN_V, SCAN_L, D_MODEL)
    return x4.swapaxes(1, 2).reshape(-1, D_MODEL)


def _scan_to_natural(x_half):
    x4 = x_half.reshape(-1, SCAN_L, SCAN_V, D_MODEL)
    return x4.swapaxes(1, 2).reshape(-1, D_MODEL)


def _natural_to_col_major(x_half):
    x4 = x_half.reshape(DEC_BATCH, DEC_SEQ // GRID_W, GRID_W, D_MODEL)
    return x4.swapaxes(1, 2).reshape(-1, D_MODEL)


def _col_major_to_natural(x_half):
    x4 = x_half.reshape(DEC_BATCH, GRID_W, DEC_SEQ // GRID_W, D_MODEL)
    return x4.swapaxes(1, 2).reshape(-1, D_MODEL)


def kernel(x_prompt, x_sample, c, state_rglru, state_gla, c_ctx, ada_w, ada_b, norm_g, ffn_w_in, ffn_w_out,
           rg_w_in, rg_conv_w, rg_conv_b, rg_w_a, rg_b_a, rg_w_i, rg_b_i, rg_lambda, rg_w_out,
           gla_w_in, gla_w_g2, gla_b_g, gla_norm, gla_w_out, final_norm):
    depth = ada_w.shape[0]
    cvecs = jnp.concatenate([c_ctx[None, :], c, jnp.zeros((8 - N_GROUPS, D_MODEL), F32)], axis=0)
    mods = _ada_mods(cvecs, ada_w, ada_b)
    fn = final_norm.reshape(1, D_MODEL)
    ffn_in, ffn_out = ffn_w_in.astype(BF16), ffn_w_out.astype(BF16)
    rg_states, gla_states = [], []

    xp = x_prompt.reshape(N_PROMPT_TOK, D_MODEL)
    xs = x_sample.reshape(DEC_BATCH * DEC_SEQ, D_MODEL)
    for i in range(depth):
        mods_l, gains_l = mods[i], norm_g[i]
        j = i // 2
        last = i == depth - 1
        if i % 2 == 0:
            halves = (_natural_to_scan(xp), _natural_to_scan(xs))
            x = _half_ffn(halves, mods_l, gains_l, ffn_in, ffn_out, i, 0, fn, False, False)[0]
            gate, xbr = _rg_proj(x, mods_l, gains_l, rg_w_in[j].astype(BF16))
            y, st_f, st_b = _rg_scan(xbr, gate, rg_conv_w[j], rg_conv_b[j].reshape(1, D_RNN),
                                     rg_w_a[j].astype(BF16), rg_b_a[j], rg_w_i[j].astype(BF16), rg_b_i[j],
                                     rg_lambda[j], state_rglru[:, j])
            rg_states.append(jnp.stack([st_f[:BATCH], st_b[:BATCH]], axis=1))
            xp, xs = _half_ffn((x,), mods_l, gains_l, ffn_in, ffn_out, i, 1, fn, last, True,
                               mix=((y,), rg_w_out[j].astype(BF16)))
            xp, xs = _scan_to_natural(xp), _scan_to_natural(xs)
        else:
            halves = (xp, _natural_to_col_major(xs))
            x = _half_ffn(halves, mods_l, gains_l, ffn_in, ffn_out, i, 0, fn, False, False)[0]
            w_in = gla_w_in[j]
            n_main = 2 * GLA_QK + 2 * GLA_V
            w_z = jnp.pad(w_in[:, n_main:], ((0, 0), (0, 128 - 2 * GLA_RANK))).astype(BF16)
            w2cat = jnp.zeros((128, 2 * GLA_QK), F32)
            w2cat = w2cat.at[:GLA_RANK, :GLA_QK].set(gla_w_g2[j, 0])
            w2cat = w2cat.at[GLA_RANK:2 * GLA_RANK, GLA_QK:].set(gla_w_g2[j, 1]).astype(BF16)
            q, k, v, r, cum = _gla_proj(x, mods_l, gains_l, w_in[:, :n_main].astype(BF16), w_z, w2cat,
                                        gla_b_g[j].reshape(1, 2 * GLA_QK))
            y_p, st = _gla_mix(q, k, v, cum, r, gla_norm[j], None, t_len=SEQ, row_block0=0, n_seq=BATCH)
            y_s = _gla_mix(q, k, v, cum, r, gla_norm[j], state_gla[:, j], t_len=DEC_SEQ,
                           row_block0=N_PROMPT_TOK // DEC_SEQ, n_seq=DEC_BATCH)
            gla_states.append(st)
            xp, xs = _half_ffn((x,), mods_l, gains_l, ffn_in, ffn_out, i, 1, fn, last, True,
                               mix=((y_p, y_s), gla_w_out[j].astype(BF16)))
            xs = _col_major_to_natural(xs)

    y_prompt = xp.reshape(BATCH, SEQ, D_MODEL)
    y_sample = xs.reshape(DEC_BATCH, DEC_SEQ, D_MODEL)
    return (y_prompt, y_sample, jnp.stack(rg_states, axis=1), jnp.stack(gla_states, axis=1))
```

```python
import functools

import jax
import jax.numpy as jnp
from jax import lax
from jax.experimental import pallas as pl
from jax.experimental.pallas import tpu as pltpu

F32 = jnp.float32
BF16 = jnp.bfloat16

D_MODEL = 1024
BATCH = 16
SEQ = 256
DEC_BATCH = 2
DEC_SEQ = 2048
GRID_W = 64
EPS = 1e-6
N_MOD = 9
D_FF = 2816
FFN_RES = 0.5
D_RNN = 1280
RG_BLOCKS = 10
RG_BLOCK = 128
CONV_W = 4
RG_C = 8.0
GLA_HEADS = 4
GLA_QK = 512
GLA_V = 1024
GLA_DK = 128
GLA_DV = 256
GLA_RANK = 16
GLA_TAU = 16.0
LOG2_E = 1.4426950408889634

N_PROMPT_TOK = BATCH * SEQ
N_TOK = N_PROMPT_TOK + DEC_BATCH * DEC_SEQ
N_GROUPS = 1 + DEC_BATCH
GROUP_ROWS = DEC_SEQ
PROMPT_BLOCKS = N_PROMPT_TOK // GROUP_ROWS

VMEM_LIMIT_BYTES = 56 * 1024 * 1024

TM = 512
HALF_TILES = N_PROMPT_TOK // TM
ADA_SLAB = 256
MXU_DIM = 256
FF_CHUNKS = (0, 6 * MXU_DIM, D_FF)
RG_CT = 256
SCAN_L = 256
SCAN_V = GROUP_ROWS // SCAN_L
GLA_C = 128
GLA_LEVELS = tuple(GLA_C >> (l + 1) for l in range(GLA_C.bit_length() - 1))


def _cparams(*sem):
    return pltpu.CompilerParams(dimension_semantics=sem, vmem_limit_bytes=VMEM_LIMIT_BYTES)


def _group_of_tile(i, rows):
    return jnp.maximum((i * rows) // GROUP_ROWS - (PROMPT_BLOCKS - 1), 0)


def _sigmoid(x):
    return 1.0 / (1.0 + jnp.exp2(x * (-LOG2_E)))


def _softplus(x):
    return jnp.maximum(x, 0.0) + jnp.log1p(jnp.exp(-jnp.abs(x)))


def _rms(x):
    return x * lax.rsqrt(jnp.mean(x * x, axis=-1, keepdims=True) + EPS)


def _modulated(x, mod_ref, gain_ref, j):
    y = _rms(x) * gain_ref[j:j + 1, :]
    return y * (1.0 + mod_ref[3 * j + 1:3 * j + 2, :]) + mod_ref[3 * j:3 * j + 1, :]


def _ada_kernel(c_ref, w_ref, b_ref, o_ref):
    @pl.when(pl.program_id(1) == 0)
    def _():
        o_ref[...] = jnp.broadcast_to(b_ref[...], o_ref.shape)

    c = c_ref[...]
    s = (c * _sigmoid(c)).astype(BF16)
    o_ref[...] += jnp.dot(s, w_ref[...].astype(BF16), preferred_element_type=F32)


def _ada_mods(cvecs, ada_w, ada_b):
    depth = ada_w.shape[0]
    n_slab = D_MODEL // ADA_SLAB
    c_slabs = cvecs.reshape(8, n_slab, ADA_SLAB).swapaxes(0, 1)
    out = pl.pallas_call(
        _ada_kernel,
        grid=(depth, n_slab),
        in_specs=[
            pl.BlockSpec((None, 8, ADA_SLAB), lambda l, k: (k, 0, 0)),
            pl.BlockSpec((None, ADA_SLAB, N_MOD * D_MODEL), lambda l, k: (l, k, 0)),
            pl.BlockSpec((None, 1, N_MOD * D_MODEL), lambda l, k: (l, 0, 0)),
        ],
        out_specs=pl.BlockSpec((None, 8, N_MOD * D_MODEL), lambda l, k: (l, 0, 0)),
        out_shape=jax.ShapeDtypeStruct((depth, 8, N_MOD * D_MODEL), F32),
        compiler_params=_cparams("arbitrary", "arbitrary"),
        name="ada_mod",
    )(c_slabs, ada_w, ada_b.reshape(depth, 1, N_MOD * D_MODEL))
    return out[:, :N_GROUPS, :].reshape(depth, N_GROUPS, N_MOD, D_MODEL)


def _pick_half(refs):
    if len(refs) == 1:
        return refs[0][...]
    return jnp.where(pl.program_id(0) < HALF_TILES, refs[0][...], refs[1][...])


def _ffn_kernel(*refs, j, final, n_x, n_y, split_out, cast_next):
    x_refs, y_refs, refs = refs[:n_x], refs[n_x:n_x + n_y], refs[n_x + n_y:]
    if n_y:
        wmix_ref, refs = refs[0], refs[1:]
    mod_ref, gain_ref, win_ref, wout_ref, fn_ref = refs[:5]
    refs = refs[5:]
    if cast_next:
        (next_in_ref, next_out_ref), refs = refs[:2], refs[2:]
        refs[-2][...] = next_in_ref[...].astype(BF16)
        refs[-1][...] = next_out_ref[...].astype(BF16)
    o_refs = refs
    x = _pick_half(x_refs)
    if n_y:
        y = _pick_half(y_refs).astype(BF16)
        x = x + mod_ref[5:6, :] * jnp.dot(y, wmix_ref[...], preferred_element_type=F32)
    hb = _modulated(x, mod_ref, gain_ref, j).astype(BF16)
    acc = jnp.zeros((TM, D_MODEL), F32)
    for lo, hi in zip(FF_CHUNKS[:-1], FF_CHUNKS[1:]):
        g = jnp.dot(hb, win_ref[:, lo:hi], preferred_element_type=F32)
        u = jnp.dot(hb, win_ref[:, D_FF + lo:D_FF + hi], preferred_element_type=F32)
        a = (g * _sigmoid(g) * u).astype(BF16)
        acc = acc + jnp.dot(a, wout_ref[lo:hi, :], preferred_element_type=F32)
    out = x + FFN_RES * mod_ref[3 * j + 2:3 * j + 3, :] * acc
    if final:
        out = _rms(out) * fn_ref[...]
    if split_out:
        first_half = pl.program_id(0) < HALF_TILES

        @pl.when(first_half)
        def _():
            o_refs[0][...] = out

        @pl.when(jnp.logical_not(first_half))
        def _():
            o_refs[1][...] = out
    else:
        o_refs[0][...] = out


def _row_specs(width, split):
    if not split:
        return [pl.BlockSpec((TM, width), lambda i: (i, 0))]
    return [pl.BlockSpec((TM, width), lambda i: (jnp.minimum(i, HALF_TILES - 1), 0)),
            pl.BlockSpec((TM, width), lambda i: (jnp.maximum(i - HALF_TILES, 0), 0))]


def _half_ffn(xs, mods_l, gains_l, w_in, w_out, half, final_norm, final, split_out, mix=None, next_w=None):
    n_steps = N_TOK // TM
    ys, w_mix = mix if mix is not None else ((), None)
    in_specs = _row_specs(D_MODEL, len(xs) == 2)
    args = list(xs)
    if ys:
        in_specs += _row_specs(ys[0].shape[1], len(ys) == 2)
        in_specs.append(pl.BlockSpec(w_mix.shape, lambda i: (0, 0), pipeline_mode=pl.Buffered(1)))
        args += list(ys) + [w_mix]
    in_specs += [
        pl.BlockSpec((None, N_MOD, D_MODEL), lambda i: (_group_of_tile(i, TM), 0, 0)),
        pl.BlockSpec((3, D_MODEL), lambda i: (0, 0)),
        pl.BlockSpec((D_MODEL, 2 * D_FF), lambda i: (0, 0), pipeline_mode=pl.Buffered(1)),
        pl.BlockSpec((D_FF, D_MODEL), lambda i: (0, 0), pipeline_mode=pl.Buffered(1)),
        pl.BlockSpec((1, D_MODEL), lambda i: (0, 0)),
    ]
    args += [mods_l, gains_l, w_in, w_out, final_norm]
    n_rows = N_PROMPT_TOK if split_out else N_TOK
    out_specs = _row_specs(D_MODEL, split_out)
    out_shape = [jax.ShapeDtypeStruct((n_rows, D_MODEL), F32)] * (2 if split_out else 1)
    if next_w is not None:
        f_in, f_out, nl, nh = next_w
        slab_in, slab_out = D_MODEL // n_steps, D_FF // n_steps
        in_specs += [pl.BlockSpec((None, None, slab_in, 2 * D_FF), lambda i: (nl, nh, i, 0)),
                     pl.BlockSpec((None, None, slab_out, D_MODEL), lambda i: (nl, nh, i, 0))]
        args += [f_in, f_out]
        out_specs += [pl.BlockSpec((slab_in, 2 * D_FF), lambda i: (i, 0)),
                      pl.BlockSpec((slab_out, D_MODEL), lambda i: (i, 0))]
        out_shape += [jax.ShapeDtypeStruct((D_MODEL, 2 * D_FF), BF16), jax.ShapeDtypeStruct((D_FF, D_MODEL), BF16)]
    return pl.pallas_call(
        functools.partial(_ffn_kernel, j=2 * half, final=final, n_x=len(xs), n_y=len(ys), split_out=split_out,
                          cast_next=next_w is not None),
        grid=(n_steps,),
        in_specs=in_specs,
        out_specs=out_specs,
        out_shape=out_shape,
        compiler_params=_cparams("arbitrary"),
        name="half_ffn",
    )(*args)


def _gelu_tanh(x):
    return 0.5 * x * (1.0 + jnp.tanh(0.7978845608028654 * (x + 0.044715 * (x * x * x))))


def _rg_proj_kernel(x_ref, mod_ref, gain_ref, w_ref, gate_ref, xbr_ref):
    hb = _modulated(x_ref[...], mod_ref, gain_ref, 1).astype(BF16)
    gate_ref[...] = _gelu_tanh(jnp.dot(hb, w_ref[:, :D_RNN], preferred_element_type=F32))
    xbr_ref[...] = jnp.dot(hb, w_ref[:, D_RNN:], preferred_element_type=F32)


def _rg_proj(x, mods_l, gains_l, w_in):
    return pl.pallas_call(
        _rg_proj_kernel,
        grid=(N_TOK // TM,),
        in_specs=[
            pl.BlockSpec((TM, D_MODEL), lambda i: (i, 0)),
            pl.BlockSpec((None, N_MOD, D_MODEL), lambda i: (_group_of_tile(i, TM), 0, 0)),
            pl.BlockSpec((3, D_MODEL), lambda i: (0, 0)),
            pl.BlockSpec((D_MODEL, 2 * D_RNN), lambda i: (0, 0), pipeline_mode=pl.Buffered(1)),
        ],
        out_specs=[pl.BlockSpec((TM, D_RNN), lambda i: (i, 0))] * 2,
        out_shape=[jax.ShapeDtypeStruct((N_TOK, D_RNN), F32)] * 2,
        compiler_params=_cparams("arbitrary"),
        name="rg_in_proj",
    )(x, mods_l, gains_l, w_in)


def _seg_shift(row, down):
    v = lax.broadcasted_iota(jnp.int32, row.shape, 0)
    if down:
        return jnp.where(v >= 1, pltpu.roll(row, 1, 0), 0.0)
    return jnp.where(v < SCAN_V - 1, pltpu.roll(row, SCAN_V - 1, 0), 0.0)


def _rg_scan_kernel(xbr_ref, gate_ref, cw_ref, cb_ref, wa_ref, ba_ref, wi_ref, bi_ref, lam_ref, h0_ref,
                    y_ref, stf_ref, stb_ref, af_ref, uf_ref, ab_ref, ub_ref):
    chain = jnp.where(pl.program_id(0) >= PROMPT_BLOCKS, 1.0, 0.0).astype(F32)
    nbt = RG_CT // RG_BLOCK
    rows = SCAN_L * SCAN_V
    v = SCAN_V

    for nb in range(nbt):
        ls = slice(nb * RG_BLOCK, (nb + 1) * RG_BLOCK)
        x = xbr_ref[:, ls]
        before = chain * _seg_shift(x[rows - v:], True)
        after0 = chain * _seg_shift(x[:v], False)
        after1 = chain * _seg_shift(x[v:2 * v], False)
        xm1 = jnp.concatenate([before, x[:rows - v]], axis=0)
        xp1 = jnp.concatenate([x[v:], after0], axis=0)
        xp2 = jnp.concatenate([x[2 * v:], after0, after1], axis=0)
        xc = (cw_ref[0:1, ls] * xm1 + cw_ref[1:2, ls] * x + cw_ref[2:3, ls] * xp1 + cw_ref[3:4, ls] * xp2
              + cb_ref[0:1, ls])
        xcb = xc.astype(BF16)
        for d, (a_ref, u_ref) in enumerate(((af_ref, uf_ref), (ab_ref, ub_ref))):
            r = _sigmoid(jnp.dot(xcb, wa_ref[d, nb], preferred_element_type=F32) + ba_ref[d:d + 1, ls])
            gi = _sigmoid(jnp.dot(xcb, wi_ref[d, nb], preferred_element_type=F32) + bi_ref[d:d + 1, ls])
            a = jnp.exp2(r * ((-RG_C * LOG2_E) * _softplus(-lam_ref[d:d + 1, ls])))
            a_ref[nb] = a
            z = 1.0 - a * a
            u_ref[nb] = jnp.where(z > 0.0, z * lax.rsqrt(z), 0.0) * (gi * xc)

    def step(i, carry):
        sf = pl.ds(pl.multiple_of(i * v, v), v)
        sb = pl.ds(pl.multiple_of((SCAN_L - 1 - i) * v, v), v)
        out = []
        for nb in range(nbt):
            hf, pf, hb, pb = carry[4 * nb:4 * nb + 4]
            a = af_ref[nb, sf, :]
            hf = a * hf + uf_ref[nb, sf, :]
            pf = a * pf
            uf_ref[nb, sf, :] = hf
            af_ref[nb, sf, :] = pf
            a = ab_ref[nb, sb, :]
            hb = a * hb + ub_ref[nb, sb, :]
            pb = a * pb
            ub_ref[nb, sb, :] = hb
            ab_ref[nb, sb, :] = pb
            out += [hf, pf, hb, pb]
        return tuple(out)

    zero = jnp.zeros((v, RG_BLOCK), F32)
    one = jnp.ones((v, RG_BLOCK), F32)
    lax.fori_loop(0, SCAN_L, step, (zero, one, zero, one) * nbt, unroll=4)

    for nb in range(nbt):
        ls = slice(nb * RG_BLOCK, (nb + 1) * RG_BLOCK)
        h_end, p_end = uf_ref[nb, rows - v:, :], af_ref[nb, rows - v:, :]
        entry = [chain * h0_ref[0:1, ls]]
        for s in range(v - 1):
            entry.append(chain * (h_end[s:s + 1] + p_end[s:s + 1] * entry[s]))
        ef = jnp.concatenate(entry, axis=0)
        h_start, p_start = ub_ref[nb, :v, :], ab_ref[nb, :v, :]
        entry = [chain * h0_ref[1:2, ls]]
        for s in range(v - 1, 0, -1):
            entry.append(chain * (h_start[s:s + 1] + p_start[s:s + 1] * entry[-1]))
        eb = jnp.concatenate(entry[::-1], axis=0)
        shape3 = (SCAN_L, v, RG_BLOCK)
        hf = uf_ref[nb].reshape(shape3) + af_ref[nb].reshape(shape3) * ef[None]
        hb = ub_ref[nb].reshape(shape3) + ab_ref[nb].reshape(shape3) * eb[None]
        stf_ref[:, ls] = hf[SCAN_L - 1]
        stb_ref[:, ls] = hb[0]
        y_ref[:, ls] = (hf + hb).reshape(rows, RG_BLOCK) * gate_ref[:, ls]


def _rg_scan(xbr, gate, conv_w, conv_b, w_a, b_a, w_i, b_i, lam, h0):
    n_blk = N_TOK // GROUP_ROWS
    nbt = RG_CT // RG_BLOCK
    tile = pl.BlockSpec((GROUP_ROWS, RG_CT), lambda b, c: (b, c))
    vec2 = pl.BlockSpec((2, RG_CT), lambda b, c: (0, c))
    wspec = pl.BlockSpec((2, nbt, RG_BLOCK, RG_BLOCK), lambda b, c: (0, c, 0, 0))
    return pl.pallas_call(
        _rg_scan_kernel,
        grid=(n_blk, D_RNN // RG_CT),
        in_specs=[
            tile, tile,
            pl.BlockSpec((CONV_W, RG_CT), lambda b, c: (0, c)),
            pl.BlockSpec((1, RG_CT), lambda b, c: (0, c)),
            wspec, vec2, wspec, vec2, vec2,
            pl.BlockSpec((None, 2, RG_CT), lambda b, c: (jnp.maximum(b - PROMPT_BLOCKS, 0), 0, c)),
        ],
        out_specs=[tile] + [pl.BlockSpec((SCAN_V, RG_CT), lambda b, c: (b, c))] * 2,
        out_shape=[jax.ShapeDtypeStruct((N_TOK, D_RNN), F32)]
        + [jax.ShapeDtypeStruct((n_blk * SCAN_V, D_RNN), F32)] * 2,
        scratch_shapes=[pltpu.VMEM((nbt, GROUP_ROWS, RG_BLOCK), F32)] * 4,
        compiler_params=_cparams("arbitrary", "arbitrary"),
        name="rg_scan",
    )(xbr, gate, conv_w, conv_b, w_a, b_a, w_i, b_i, lam, h0)


def _chunk_tri():
    ri = lax.broadcasted_iota(jnp.int32, (GLA_C, GLA_C), 0)
    ci = lax.broadcasted_iota(jnp.int32, (GLA_C, GLA_C), 1)
    return (ri >= ci).astype(BF16), (ri <= ci).astype(BF16)


def _split3(x):
    hi = x.astype(BF16)
    r1 = x - hi.astype(F32)
    mid = r1.astype(BF16)
    lo = (r1 - mid.astype(F32)).astype(BF16)
    return hi, mid, lo


def _gla_proj_kernel(x_ref, mod_ref, gain_ref, w_ref, wz_ref, w2_ref, bg_ref,
                     q_ref, k_ref, v_ref, r_ref, cum_ref):
    hb = _modulated(x_ref[...], mod_ref, gain_ref, 1).astype(BF16)
    q_ref[...] = jnp.dot(hb, w_ref[:, :GLA_QK], preferred_element_type=F32) * (GLA_DK ** -0.5)
    k_ref[...] = jnp.dot(hb, w_ref[:, GLA_QK:2 * GLA_QK], preferred_element_type=F32)
    v_ref[...] = jnp.dot(hb, w_ref[:, 2 * GLA_QK:2 * GLA_QK + GLA_V], preferred_element_type=F32).astype(BF16)
    r = jnp.dot(hb, w_ref[:, 2 * GLA_QK + GLA_V:], preferred_element_type=F32)
    r_ref[...] = r * _sigmoid(r)
    z = jnp.dot(hb, wz_ref[...], preferred_element_type=F32).astype(BF16)
    zz = jnp.dot(z, w2_ref[...], preferred_element_type=F32) + bg_ref[...]
    log_gate = -_softplus(-zz) * (1.0 / GLA_TAU)
    parts = _split3(log_gate)
    tri_f, tri_b = _chunk_tri()
    for ch in range(TM // GLA_C):
        rows = slice(ch * GLA_C, (ch + 1) * GLA_C)
        cum_ref[rows, :GLA_QK] = sum(jnp.dot(tri_f, p[rows, :GLA_QK], preferred_element_type=F32) for p in parts)
        cum_ref[rows, GLA_QK:] = sum(jnp.dot(tri_b, p[rows, GLA_QK:], preferred_element_type=F32) for p in parts)


def _gla_proj(x, mods_l, gains_l, w_main, w_z, w2cat, bg):
    def rows(w):
        return pl.BlockSpec((TM, w), lambda i: (i, 0))

    def whole(a):
        return pl.BlockSpec(a.shape, lambda i: (0,) * a.ndim, pipeline_mode=pl.Buffered(1))

    widths = (GLA_QK, GLA_QK, GLA_V, GLA_V, 2 * GLA_QK)
    dtypes = (F32, F32, BF16, F32, F32)
    return pl.pallas_call(
        _gla_proj_kernel,
        grid=(N_TOK // TM,),
        in_specs=[
            rows(D_MODEL),
            pl.BlockSpec((None, N_MOD, D_MODEL), lambda i: (_group_of_tile(i, TM), 0, 0)),
            pl.BlockSpec((3, D_MODEL), lambda i: (0, 0)),
            whole(w_main), whole(w_z), whole(w2cat), whole(bg),
        ],
        out_specs=[rows(w) for w in widths],
        out_shape=[jax.ShapeDtypeStruct((N_TOK, w), t) for w, t in zip(widths, dtypes)],
        compiler_params=_cparams("arbitrary"),
        name="gla_in_proj",
    )(x, mods_l, gains_l, w_main, w_z, w2cat, bg)


def _dot_tn(a, b):
    return lax.dot_general(a, b, (((0,), (0,)), ((), ())), preferred_element_type=F32)


def _dot_nt(a, b):
    return lax.dot_general(a, b, (((1,), (1,)), ((), ())), preferred_element_type=F32)


def _level_ref(cum_ref, row0, cum, m, backward):
    c = GLA_C
    pick = m - 1 if backward else m

    def bcast(r, n):
        return jnp.broadcast_to(cum_ref[pl.ds(row0 + r, 1), :], (n, GLA_DK))

    if m >= 4:
        return jnp.concatenate([bcast(g * 2 * m + pick, 2 * m) for g in range(c // (2 * m))], axis=0)
    pos = lax.broadcasted_iota(jnp.int32, (c, GLA_DK), 0)
    if m == 2:
        lo = jnp.concatenate([bcast(g * 8 + pick, 8) for g in range(c // 8)], axis=0)
        hi = jnp.concatenate([bcast(g * 8 + 4 + pick, 8) for g in range(c // 8)], axis=0)
        return jnp.where((pos & 4) == 0, lo, hi)
    if backward:
        return jnp.where((pos & 1) == 1, pltpu.roll(cum, 1, 0), cum)
    return jnp.where((pos & 1) == 0, pltpu.roll(cum, c - 1, 0), cum)


def _gla_intra(q_ref, k_ref, v_ref, cf_ref, cb_ref, row0, diag, masks, uppers):
    rows = pl.ds(row0, GLA_C)
    cf, cb = cf_ref[rows, :], cb_ref[rows, :]
    qb, kb = q_ref[rows, :].astype(BF16), k_ref[rows, :].astype(BF16)
    scores = jnp.where(diag, 2.0 * _dot_nt(qb, kb), 0.0)
    for lvl, m in enumerate(GLA_LEVELS):
        df = cf - _level_ref(cf_ref, row0, cf, m, False)
        db = cb - _level_ref(cb_ref, row0, cb, m, True)
        gq = jnp.exp2(jnp.where(uppers[lvl], df, db) * LOG2_E).astype(BF16)
        gk = jnp.exp2(jnp.where(uppers[lvl], db, df) * (-LOG2_E)).astype(BF16)
        scores = jnp.where(masks[lvl], _dot_nt(qb * gq, kb * gk), scores)
    return jnp.dot(scores.astype(BF16), v_ref[rows, :], preferred_element_type=F32)


def _gla_inter(q_ref, k_ref, v_ref, cum_ref, row0, st, backward):
    rows = pl.ds(row0, GLA_C)
    cum = cum_ref[rows, :]
    tot = cum_ref[pl.ds(row0 + (0 if backward else GLA_C - 1), 1), :]
    o = _dot_nt((q_ref[rows, :] * jnp.exp(cum)).astype(BF16), st.astype(BF16))
    kd = (k_ref[rows, :] * jnp.exp(tot - cum)).astype(BF16)
    return o, st * jnp.exp(tot) + _dot_tn(v_ref[rows, :], kd)


def _gla_kernel(*refs, t_len, has_s0):
    if has_s0:
        q_ref, k_ref, v_ref, cf_ref, cb_ref, r_ref, gn_ref, s0_ref, y_ref, sf_ref, sb_ref = refs
    else:
        q_ref, k_ref, v_ref, cf_ref, cb_ref, r_ref, gn_ref, y_ref, sout_ref, sf_ref, sb_ref = refs
    c = GLA_C
    n_chunks = t_len // c
    ri = lax.broadcasted_iota(jnp.int32, (c, c), 0)
    ci = lax.broadcasted_iota(jnp.int32, (c, c), 1)
    x = ri ^ ci
    masks = [(x >= m) & (x < 2 * m) for m in GLA_LEVELS]
    pos = lax.broadcasted_iota(jnp.int32, (c, GLA_DK), 0)
    uppers = [(pos & m) != 0 for m in GLA_LEVELS]

    def intra(i, carry):
        row0 = pl.multiple_of(i * c, c)
        y_ref[pl.ds(row0, c), :] = _gla_intra(q_ref, k_ref, v_ref, cf_ref, cb_ref, row0, ri == ci, masks, uppers)
        return carry

    lax.fori_loop(0, n_chunks, intra, 0, unroll=2)

    if has_s0:
        sf_ref[...] = s0_ref[0].T
        sb_ref[...] = s0_ref[1].T
    else:
        sf_ref[...] = jnp.zeros_like(sf_ref)
        sb_ref[...] = jnp.zeros_like(sb_ref)

    def inter(i, carry):
        rf = pl.multiple_of(i * c, c)
        rb = pl.multiple_of((n_chunks - 1 - i) * c, c)
        o, st = _gla_inter(q_ref, k_ref, v_ref, cf_ref, rf, sf_ref[...], False)
        y_ref[pl.ds(rf, c), :] += o
        sf_ref[...] = st
        o, st = _gla_inter(q_ref, k_ref, v_ref, cb_ref, rb, sb_ref[...], True)
        y_ref[pl.ds(rb, c), :] += o
        sb_ref[...] = st
        return carry

    lax.fori_loop(0, n_chunks, inter, 0, unroll=2)

    if not has_s0:
        sout_ref[0] = sf_ref[...].T
        sout_ref[1] = sb_ref[...].T
    o = y_ref[...]
    y_ref[...] = _rms(o) * gn_ref[...] * r_ref[...]


def _gla_mix(q, k, v, cum, r, gnorm, s0, *, t_len, row_block0, n_seq):
    has_s0 = s0 is not None

    def rows(w, off=0):
        return pl.BlockSpec((t_len, w), lambda s, h: (row_block0 + s, off + h))

    state_spec = pl.BlockSpec((None, 2, None, GLA_DK, GLA_DV), lambda s, h: (s, 0, h, 0, 0))
    in_specs = [rows(GLA_DK), rows(GLA_DK), rows(GLA_DV), rows(GLA_DK), rows(GLA_DK, GLA_HEADS),
                rows(GLA_DV), pl.BlockSpec((None, 1, GLA_DV), lambda s, h: (h, 0, 0))]
    args = [q, k, v, cum, cum, r, gnorm.reshape(GLA_HEADS, 1, GLA_DV)]
    y_spec = pl.BlockSpec((t_len, GLA_DV), lambda s, h: (s, h))
    y_shape = jax.ShapeDtypeStruct((n_seq * t_len, GLA_V), F32)
    if has_s0:
        in_specs.append(state_spec)
        args.append(s0)
        out_specs, out_shape = y_spec, y_shape
    else:
        out_specs = [y_spec, state_spec]
        out_shape = [y_shape, jax.ShapeDtypeStruct((n_seq, 2, GLA_HEADS, GLA_DK, GLA_DV), F32)]
    return pl.pallas_call(
        functools.partial(_gla_kernel, t_len=t_len, has_s0=has_s0),
        grid=(n_seq, GLA_HEADS),
        in_specs=in_specs,
        out_specs=out_specs,
        out_shape=out_shape,
        scratch_shapes=[pltpu.VMEM((GLA_DV, GLA_DK), F32)] * 2,
        compiler_params=_cparams("arbitrary", "arbitrary"),
        name="gla_mix_sample" if has_s0 else "gla_mix_prompt",
    )(*args)


def _natural_to_scan(x_half):
    x4 = x_half.reshape(-1, SCAN_V, SCAN_L, D_MODEL)
    return x4.swapaxes(1, 2).reshape(-1, D_MODEL)


def _scan_to_natural(x_half):
    x4 = x_half.reshape(-1, SCAN_L, SCAN_V, D_MODEL)
    return x4.swapaxes(1, 2).reshape(-1, D_MODEL)


def _natural_to_col_major(x_half):
    x4 = x_half.reshape(DEC_BATCH, DEC_SEQ // GRID_W, GRID_W, D_MODEL)
    return x4.swapaxes(1, 2).reshape(-1, D_MODEL)


def _col_major_to_natural(x_half):
    x4 = x_half.reshape(DEC_BATCH, GRID_W, DEC_SEQ // GRID_W, D_MODEL)
    return x4.swapaxes(1, 2).reshape(-1, D_MODEL)


def kernel(x_prompt, x_sample, c, state_rglru, state_gla, c_ctx, ada_w, ada_b, norm_g, ffn_w_in, ffn_w_out,
           rg_w_in, rg_conv_w, rg_conv_b, rg_w_a, rg_b_a, rg_w_i, rg_b_i, rg_lambda, rg_w_out,
           gla_w_in, gla_w_g2, gla_b_g, gla_norm, gla_w_out, final_norm):
    depth = ada_w.shape[0]
    cvecs = jnp.concatenate([c_ctx[None, :], c, jnp.zeros((8 - N_GROUPS, D_MODEL), F32)], axis=0)
    mods = _ada_mods(cvecs, ada_w, ada_b)
    fn = final_norm.reshape(1, D_MODEL)
    rg_states, gla_states = [], []

    w_in, w_out = ffn_w_in[0, 0].astype(BF16), ffn_w_out[0, 0].astype(BF16)

    def next_w(layer, half):
        nxt = 2 * layer + half + 1
        return (ffn_w_in, ffn_w_out, nxt // 2, nxt % 2) if nxt < 2 * depth else None

    xp = x_prompt.reshape(N_PROMPT_TOK, D_MODEL)
    xs = x_sample.reshape(DEC_BATCH * DEC_SEQ, D_MODEL)
    for i in range(depth):
        mods_l, gains_l = mods[i], norm_g[i]
        j = i // 2
        last = i == depth - 1
        if i % 2 == 0:
            halves = (_natural_to_scan(xp), _natural_to_scan(xs))
            x, w_in, w_out = _half_ffn(halves, mods_l, gains_l, w_in, w_out, 0, fn, False, False,
                                       next_w=next_w(i, 0))
            gate, xbr = _rg_proj(x, mods_l, gains_l, rg_w_in[j].astype(BF16))
            y, st_f, st_b = _rg_scan(xbr, gate, rg_conv_w[j], rg_conv_b[j].reshape(1, D_RNN),
                                     rg_w_a[j].astype(BF16), rg_b_a[j], rg_w_i[j].astype(BF16), rg_b_i[j],
                                     rg_lambda[j], state_rglru[:, j])
            rg_states.append(jnp.stack([st_f[:BATCH], st_b[:BATCH]], axis=1))
            xp, xs, *nw = _half_ffn((x,), mods_l, gains_l, w_in, w_out, 1, fn, last, True,
                                    mix=((y,), rg_w_out[j].astype(BF16)), next_w=next_w(i, 1))
            w_in, w_out = nw if nw else (None, None)
            xp, xs = _scan_to_natural(xp), _scan_to_natural(xs)
        else:
            halves = (xp, _natural_to_col_major(xs))
            x, w_in, w_out = _half_ffn(halves, mods_l, gains_l, w_in, w_out, 0, fn, False, False,
                                       next_w=next_w(i, 0))
            gw_in = gla_w_in[j]
            n_main = 2 * GLA_QK + 2 * GLA_V
            w_z = jnp.pad(gw_in[:, n_main:], ((0, 0), (0, 128 - 2 * GLA_RANK))).astype(BF16)
            w2cat = jnp.zeros((128, 2 * GLA_QK), F32)
            w2cat = w2cat.at[:GLA_RANK, :GLA_QK].set(gla_w_g2[j, 0])
            w2cat = w2cat.at[GLA_RANK:2 * GLA_RANK, GLA_QK:].set(gla_w_g2[j, 1]).astype(BF16)
            q, k, v, r, cum = _gla_proj(x, mods_l, gains_l, gw_in[:, :n_main].astype(BF16), w_z, w2cat,
                                        gla_b_g[j].reshape(1, 2 * GLA_QK))
            y_p, st = _gla_mix(q, k, v, cum, r, gla_norm[j], None, t_len=SEQ, row_block0=0, n_seq=BATCH)
            y_s = _gla_mix(q, k, v, cum, r, gla_norm[j], state_gla[:, j], t_len=DEC_SEQ,
                           row_block0=N_PROMPT_TOK // DEC_SEQ, n_seq=DEC_BATCH)
            gla_states.append(st)
            xp, xs, *nw = _half_ffn((x,), mods_l, gains_l, w_in, w_out, 1, fn, last, True,
                                    mix=((y_p, y_s), gla_w_out[j].astype(BF16)), next_w=next_w(i, 1))
            w_in, w_out = nw if nw else (None, None)
            xs = _col_major_to_natural(xs)

    y_prompt = xp.reshape(BATCH, SEQ, D_MODEL)
    y_sample = xs.reshape(DEC_BATCH, DEC_SEQ, D_MODEL)
    return (y_prompt, y_sample, jnp.stack(rg_states, axis=1), jnp.stack(gla_states, axis=1))
```

```python
import functools

import jax
import jax.numpy as jnp
from jax import lax
from jax.experimental import pallas as pl
from jax.experimental.pallas import tpu as pltpu

F32 = jnp.float32
BF16 = jnp.bfloat16

D_MODEL = 1024
BATCH = 16
SEQ = 256
DEC_BATCH = 2
DEC_SEQ = 2048
GRID_W = 64
EPS = 1e-6
N_MOD = 9
D_FF = 2816
FFN_RES = 0.5
D_RNN = 1280
RG_BLOCKS = 10
RG_BLOCK = 128
CONV_W = 4
RG_C = 8.0
GLA_HEADS = 4
GLA_QK = 512
GLA_V = 1024
GLA_DK = 128
GLA_DV = 256
GLA_RANK = 16
GLA_TAU = 16.0
LOG2_E = 1.4426950408889634

N_PROMPT_TOK = BATCH * SEQ
N_TOK = N_PROMPT_TOK + DEC_BATCH * DEC_SEQ
N_GROUPS = 1 + DEC_BATCH
GROUP_ROWS = DEC_SEQ
PROMPT_BLOCKS = N_PROMPT_TOK // GROUP_ROWS

VMEM_LIMIT_BYTES = 56 * 1024 * 1024

TM = 512
HALF_TILES = N_PROMPT_TOK // TM
ADA_SLAB = 256
MXU_DIM = 256
BF16_SUBLANES = 16
FF_CHUNKS = (0, 6 * MXU_DIM, D_FF)
RG_CT = 256
SCAN_L = 256
SCAN_V = GROUP_ROWS // SCAN_L
GLA_C = 128
GLA_PROMPT_SEQS = 4
GLA_INTRA_UNROLL = 4
GLA_INTER_UNROLL = 8
GLA_LEVELS = tuple(GLA_C >> (l + 1) for l in range(GLA_C.bit_length() - 1))


def _cparams(*sem):
    return pltpu.CompilerParams(dimension_semantics=sem, vmem_limit_bytes=VMEM_LIMIT_BYTES)


def _group_of_tile(i, rows):
    return jnp.maximum((i * rows) // GROUP_ROWS - (PROMPT_BLOCKS - 1), 0)


def _sigmoid(x):
    return 1.0 / (1.0 + jnp.exp2(x * (-LOG2_E)))


def _softplus(x):
    return jnp.maximum(x, 0.0) + jnp.log1p(jnp.exp(-jnp.abs(x)))


def _rms(x):
    return x * lax.rsqrt(jnp.mean(x * x, axis=-1, keepdims=True) + EPS)


def _modulated(x, mod_ref, gain_ref, j):
    y = _rms(x) * gain_ref[j:j + 1, :]
    return y * (1.0 + mod_ref[3 * j + 1:3 * j + 2, :]) + mod_ref[3 * j:3 * j + 1, :]


def _ada_kernel(c_ref, w_ref, b_ref, o_ref):
    @pl.when(pl.program_id(1) == 0)
    def _():
        o_ref[...] = jnp.broadcast_to(b_ref[...], o_ref.shape)

    c = c_ref[...]
    s = (c * _sigmoid(c)).astype(BF16)
    o_ref[...] += jnp.dot(s, w_ref[...].astype(BF16), preferred_element_type=F32)


def _ada_mods(cvecs, ada_w, ada_b):
    depth = ada_w.shape[0]
    n_slab = D_MODEL // ADA_SLAB
    c_slabs = cvecs.reshape(8, n_slab, ADA_SLAB).swapaxes(0, 1)
    out = pl.pallas_call(
        _ada_kernel,
        grid=(depth, n_slab),
        in_specs=[
            pl.BlockSpec((None, 8, ADA_SLAB), lambda l, k: (k, 0, 0)),
            pl.BlockSpec((None, ADA_SLAB, N_MOD * D_MODEL), lambda l, k: (l, k, 0)),
            pl.BlockSpec((None, 1, N_MOD * D_MODEL), lambda l, k: (l, 0, 0)),
        ],
        out_specs=pl.BlockSpec((None, 8, N_MOD * D_MODEL), lambda l, k: (l, 0, 0)),
        out_shape=jax.ShapeDtypeStruct((depth, 8, N_MOD * D_MODEL), F32),
        compiler_params=_cparams("arbitrary", "arbitrary"),
        name="ada_mod",
    )(c_slabs, ada_w, ada_b.reshape(depth, 1, N_MOD * D_MODEL))
    return out[:, :N_GROUPS, :].reshape(depth, N_GROUPS, N_MOD, D_MODEL)


def _pick_half(refs):
    if len(refs) == 1:
        return refs[0][...]
    return jnp.where(pl.program_id(0) < HALF_TILES, refs[0][...], refs[1][...])


def _ffn_kernel(*refs, j, final, n_x, n_y, split_out, n_cast):
    x_refs, y_refs, refs = refs[:n_x], refs[n_x:n_x + n_y], refs[n_x + n_y:]
    if n_y:
        wmix_ref, refs = refs[0], refs[1:]
    mod_ref, gain_ref, win_ref, wout_ref, fn_ref = refs[:5]
    refs = refs[5:]
    if n_cast:
        f32_refs, refs = refs[:n_cast], refs[n_cast:]
        for src_ref, dst_ref in zip(f32_refs, refs[len(refs) - n_cast:]):
            dst_ref[...] = src_ref[...].astype(BF16)
    o_refs = refs
    x = _pick_half(x_refs)
    if n_y:
        y = _pick_half(y_refs).astype(BF16)
        x = x + mod_ref[5:6, :] * jnp.dot(y, wmix_ref[...], preferred_element_type=F32)
    hb = _modulated(x, mod_ref, gain_ref, j).astype(BF16)
    acc = jnp.zeros((TM, D_MODEL), F32)
    for lo, hi in zip(FF_CHUNKS[:-1], FF_CHUNKS[1:]):
        g = jnp.dot(hb, win_ref[:, lo:hi], preferred_element_type=F32)
        u = jnp.dot(hb, win_ref[:, D_FF + lo:D_FF + hi], preferred_element_type=F32)
        a = (g * _sigmoid(g) * u).astype(BF16)
        acc = acc + jnp.dot(a, wout_ref[lo:hi, :], preferred_element_type=F32)
    out = x + FFN_RES * mod_ref[3 * j + 2:3 * j + 3, :] * acc
    if final:
        out = _rms(out) * fn_ref[...]
    if split_out:
        first_half = pl.program_id(0) < HALF_TILES

        @pl.when(first_half)
        def _():
            o_refs[0][...] = out

        @pl.when(jnp.logical_not(first_half))
        def _():
            o_refs[1][...] = out
    else:
        o_refs[0][...] = out


def _row_specs(width, split):
    if not split:
        return [pl.BlockSpec((TM, width), lambda i: (i, 0))]
    return [pl.BlockSpec((TM, width), lambda i: (jnp.minimum(i, HALF_TILES - 1), 0)),
            pl.BlockSpec((TM, width), lambda i: (jnp.maximum(i - HALF_TILES, 0), 0))]


def _half_ffn(xs, mods_l, gains_l, w_in, w_out, half, final_norm, final, split_out, mix=None, casts=()):
    n_steps = N_TOK // TM
    ys, w_mix = mix if mix is not None else ((), None)
    in_specs = _row_specs(D_MODEL, len(xs) == 2)
    args = list(xs)
    if ys:
        in_specs += _row_specs(ys[0].shape[1], len(ys) == 2)
        in_specs.append(pl.BlockSpec(w_mix.shape, lambda i: (0, 0), pipeline_mode=pl.Buffered(1)))
        args += list(ys) + [w_mix]
    in_specs += [
        pl.BlockSpec((None, N_MOD, D_MODEL), lambda i: (_group_of_tile(i, TM), 0, 0)),
        pl.BlockSpec((3, D_MODEL), lambda i: (0, 0)),
        pl.BlockSpec((D_MODEL, 2 * D_FF), lambda i: (0, 0), pipeline_mode=pl.Buffered(1)),
        pl.BlockSpec((D_FF, D_MODEL), lambda i: (0, 0), pipeline_mode=pl.Buffered(1)),
        pl.BlockSpec((1, D_MODEL), lambda i: (0, 0)),
    ]
    args += [mods_l, gains_l, w_in, w_out, final_norm]
    n_rows = N_PROMPT_TOK if split_out else N_TOK
    out_specs = _row_specs(D_MODEL, split_out)
    out_shape = [jax.ShapeDtypeStruct((n_rows, D_MODEL), F32)] * (2 if split_out else 1)
    for arr, lead, cols in casts:
        rows = arr.shape[len(lead)]
        slab = rows // n_steps
        assert slab * n_steps == rows and slab % BF16_SUBLANES == 0, (rows, n_steps)
        in_specs.append(pl.BlockSpec((None,) * len(lead) + (slab, cols), lambda i, lead=lead: (*lead, i, 0)))
        args.append(arr)
        out_specs.append(pl.BlockSpec((slab, cols), lambda i: (i, 0)))
        out_shape.append(jax.ShapeDtypeStruct((rows, cols), BF16))
    return pl.pallas_call(
        functools.partial(_ffn_kernel, j=2 * half, final=final, n_x=len(xs), n_y=len(ys), split_out=split_out,
                          n_cast=len(casts)),
        grid=(n_steps,),
        in_specs=in_specs,
        out_specs=out_specs,
        out_shape=out_shape,
        compiler_params=_cparams("arbitrary"),
        name="half_ffn",
    )(*args)


def _gelu_tanh(x):
    return 0.5 * x * (1.0 + jnp.tanh(0.7978845608028654 * (x + 0.044715 * (x * x * x))))


def _rg_proj_kernel(x_ref, mod_ref, gain_ref, w_ref, gate_ref, xbr_ref):
    hb = _modulated(x_ref[...], mod_ref, gain_ref, 1).astype(BF16)
    gate_ref[...] = _gelu_tanh(jnp.dot(hb, w_ref[:, :D_RNN], preferred_element_type=F32))
    xbr_ref[...] = jnp.dot(hb, w_ref[:, D_RNN:], preferred_element_type=F32)


def _rg_proj(x, mods_l, gains_l, w_in):
    return pl.pallas_call(
        _rg_proj_kernel,
        grid=(N_TOK // TM,),
        in_specs=[
            pl.BlockSpec((TM, D_MODEL), lambda i: (i, 0)),
            pl.BlockSpec((None, N_MOD, D_MODEL), lambda i: (_group_of_tile(i, TM), 0, 0)),
            pl.BlockSpec((3, D_MODEL), lambda i: (0, 0)),
            pl.BlockSpec((D_MODEL, 2 * D_RNN), lambda i: (0, 0), pipeline_mode=pl.Buffered(1)),
        ],
        out_specs=[pl.BlockSpec((TM, D_RNN), lambda i: (i, 0))] * 2,
        out_shape=[jax.ShapeDtypeStruct((N_TOK, D_RNN), F32)] * 2,
        compiler_params=_cparams("arbitrary"),
        name="rg_in_proj",
    )(x, mods_l, gains_l, w_in)


def _seg_shift(row, down):
    v = lax.broadcasted_iota(jnp.int32, row.shape, 0)
    if down:
        return jnp.where(v >= 1, pltpu.roll(row, 1, 0), 0.0)
    return jnp.where(v < SCAN_V - 1, pltpu.roll(row, SCAN_V - 1, 0), 0.0)


def _rg_scan_kernel(xbr_ref, gate_ref, cw_ref, cb_ref, wa_ref, ba_ref, wi_ref, bi_ref, lam_ref, h0_ref,
                    y_ref, stf_ref, stb_ref, af_ref, uf_ref, ab_ref, ub_ref):
    chain = jnp.where(pl.program_id(0) >= PROMPT_BLOCKS, 1.0, 0.0).astype(F32)
    nbt = RG_CT // RG_BLOCK
    rows = SCAN_L * SCAN_V
    v = SCAN_V

    for nb in range(nbt):
        ls = slice(nb * RG_BLOCK, (nb + 1) * RG_BLOCK)
        x = xbr_ref[:, ls]
        before = chain * _seg_shift(x[rows - v:], True)
        after0 = chain * _seg_shift(x[:v], False)
        after1 = chain * _seg_shift(x[v:2 * v], False)
        xm1 = jnp.concatenate([before, x[:rows - v]], axis=0)
        xp1 = jnp.concatenate([x[v:], after0], axis=0)
        xp2 = jnp.concatenate([x[2 * v:], after0, after1], axis=0)
        xc = (cw_ref[0:1, ls] * xm1 + cw_ref[1:2, ls] * x + cw_ref[2:3, ls] * xp1 + cw_ref[3:4, ls] * xp2
              + cb_ref[0:1, ls])
        xcb = xc.astype(BF16)
        for d, (a_ref, u_ref) in enumerate(((af_ref, uf_ref), (ab_ref, ub_ref))):
            r = _sigmoid(jnp.dot(xcb, wa_ref[d, nb], preferred_element_type=F32) + ba_ref[d:d + 1, ls])
            gi = _sigmoid(jnp.dot(xcb, wi_ref[d, nb], preferred_element_type=F32) + bi_ref[d:d + 1, ls])
            a = jnp.exp2(r * ((-RG_C * LOG2_E) * _softplus(-lam_ref[d:d + 1, ls])))
            a_ref[nb] = a
            z = 1.0 - a * a
            u_ref[nb] = jnp.where(z > 0.0, z * lax.rsqrt(z), 0.0) * (gi * xc)

    def step(i, carry):
        sf = pl.ds(pl.multiple_of(i * v, v), v)
        sb = pl.ds(pl.multiple_of((SCAN_L - 1 - i) * v, v), v)
        out = []
        for nb in range(nbt):
            hf, pf, hb, pb = carry[4 * nb:4 * nb + 4]
            a = af_ref[nb, sf, :]
            hf = a * hf + uf_ref[nb, sf, :]
            pf = a * pf
            uf_ref[nb, sf, :] = hf
            af_ref[nb, sf, :] = pf
            a = ab_ref[nb, sb, :]
            hb = a * hb + ub_ref[nb, sb, :]
            pb = a * pb
            ub_ref[nb, sb, :] = hb
            ab_ref[nb, sb, :] = pb
            out += [hf, pf, hb, pb]
        return tuple(out)

    zero = jnp.zeros((v, RG_BLOCK), F32)
    one = jnp.ones((v, RG_BLOCK), F32)
    lax.fori_loop(0, SCAN_L, step, (zero, one, zero, one) * nbt, unroll=4)

    for nb in range(nbt):
        ls = slice(nb * RG_BLOCK, (nb + 1) * RG_BLOCK)
        h_end, p_end = uf_ref[nb, rows - v:, :], af_ref[nb, rows - v:, :]
        entry = [chain * h0_ref[0:1, ls]]
        for s in range(v - 1):
            entry.append(chain * (h_end[s:s + 1] + p_end[s:s + 1] * entry[s]))
        ef = jnp.concatenate(entry, axis=0)
        h_start, p_start = ub_ref[nb, :v, :], ab_ref[nb, :v, :]
        entry = [chain * h0_ref[1:2, ls]]
        for s in range(v - 1, 0, -1):
            entry.append(chain * (h_start[s:s + 1] + p_start[s:s + 1] * entry[-1]))
        eb = jnp.concatenate(entry[::-1], axis=0)
        shape3 = (SCAN_L, v, RG_BLOCK)
        hf = uf_ref[nb].reshape(shape3) + af_ref[nb].reshape(shape3) * ef[None]
        hb = ub_ref[nb].reshape(shape3) + ab_ref[nb].reshape(shape3) * eb[None]
        stf_ref[:, ls] = hf[SCAN_L - 1]
        stb_ref[:, ls] = hb[0]
        y_ref[:, ls] = (hf + hb).reshape(rows, RG_BLOCK) * gate_ref[:, ls]


def _rg_scan(xbr, gate, conv_w, conv_b, w_a, b_a, w_i, b_i, lam, h0):
    n_blk = N_TOK // GROUP_ROWS
    nbt = RG_CT // RG_BLOCK
    tile = pl.BlockSpec((GROUP_ROWS, RG_CT), lambda b, c: (b, c))
    vec2 = pl.BlockSpec((2, RG_CT), lambda b, c: (0, c))
    wspec = pl.BlockSpec((2, nbt, RG_BLOCK, RG_BLOCK), lambda b, c: (0, c, 0, 0))
    return pl.pallas_call(
        _rg_scan_kernel,
        grid=(n_blk, D_RNN // RG_CT),
        in_specs=[
            tile, tile,
            pl.BlockSpec((CONV_W, RG_CT), lambda b, c: (0, c)),
            pl.BlockSpec((1, RG_CT), lambda b, c: (0, c)),
            wspec, vec2, wspec, vec2, vec2,
            pl.BlockSpec((None, 2, RG_CT), lambda b, c: (jnp.maximum(b - PROMPT_BLOCKS, 0), 0, c)),
        ],
        out_specs=[tile] + [pl.BlockSpec((SCAN_V, RG_CT), lambda b, c: (b, c))] * 2,
        out_shape=[jax.ShapeDtypeStruct((N_TOK, D_RNN), F32)]
        + [jax.ShapeDtypeStruct((n_blk * SCAN_V, D_RNN), F32)] * 2,
        scratch_shapes=[pltpu.VMEM((nbt, GROUP_ROWS, RG_BLOCK), F32)] * 4,
        compiler_params=_cparams("arbitrary", "arbitrary"),
        name="rg_scan",
    )(xbr, gate, conv_w, conv_b, w_a, b_a, w_i, b_i, lam, h0)


def _chunk_tri():
    ri = lax.broadcasted_iota(jnp.int32, (GLA_C, GLA_C), 0)
    ci = lax.broadcasted_iota(jnp.int32, (GLA_C, GLA_C), 1)
    return (ri >= ci).astype(BF16), (ri <= ci).astype(BF16)


def _split3(x):
    hi = x.astype(BF16)
    r1 = x - hi.astype(F32)
    mid = r1.astype(BF16)
    lo = (r1 - mid.astype(F32)).astype(BF16)
    return hi, mid, lo


def _gla_proj_kernel(x_ref, mod_ref, gain_ref, w_ref, wz_ref, w2_ref, bg_ref,
                     q_ref, k_ref, v_ref, r_ref, cum_ref):
    hb = _modulated(x_ref[...], mod_ref, gain_ref, 1).astype(BF16)
    q_ref[...] = jnp.dot(hb, w_ref[:, :GLA_QK], preferred_element_type=F32) * (GLA_DK ** -0.5)
    k_ref[...] = jnp.dot(hb, w_ref[:, GLA_QK:2 * GLA_QK], preferred_element_type=F32)
    v_ref[...] = jnp.dot(hb, w_ref[:, 2 * GLA_QK:2 * GLA_QK + GLA_V], preferred_element_type=F32).astype(BF16)
    r = jnp.dot(hb, w_ref[:, 2 * GLA_QK + GLA_V:], preferred_element_type=F32)
    r_ref[...] = r * _sigmoid(r)
    z = jnp.dot(hb, wz_ref[...], preferred_element_type=F32).astype(BF16)
    zz = jnp.dot(z, w2_ref[...], preferred_element_type=F32) + bg_ref[...]
    log_gate = -_softplus(-zz) * (1.0 / GLA_TAU)
    parts = _split3(log_gate)
    tri_f, tri_b = _chunk_tri()
    for ch in range(TM // GLA_C):
        rows = slice(ch * GLA_C, (ch + 1) * GLA_C)
        cum_ref[rows, :GLA_QK] = sum(jnp.dot(tri_f, p[rows, :GLA_QK], preferred_element_type=F32) for p in parts)
        cum_ref[rows, GLA_QK:] = sum(jnp.dot(tri_b, p[rows, GLA_QK:], preferred_element_type=F32) for p in parts)


def _gla_proj(x, mods_l, gains_l, w_main, w_z, w2cat, bg):
    def rows(w):
        return pl.BlockSpec((TM, w), lambda i: (i, 0))

    def whole(a):
        return pl.BlockSpec(a.shape, lambda i: (0,) * a.ndim, pipeline_mode=pl.Buffered(1))

    widths = (GLA_QK, GLA_QK, GLA_V, GLA_V, 2 * GLA_QK)
    dtypes = (F32, F32, BF16, F32, F32)
    return pl.pallas_call(
        _gla_proj_kernel,
        grid=(N_TOK // TM,),
        in_specs=[
            rows(D_MODEL),
            pl.BlockSpec((None, N_MOD, D_MODEL), lambda i: (_group_of_tile(i, TM), 0, 0)),
            pl.BlockSpec((3, D_MODEL), lambda i: (0, 0)),
            whole(w_main), whole(w_z), whole(w2cat), whole(bg),
        ],
        out_specs=[rows(w) for w in widths],
        out_shape=[jax.ShapeDtypeStruct((N_TOK, w), t) for w, t in zip(widths, dtypes)],
        compiler_params=_cparams("arbitrary"),
        name="gla_in_proj",
    )(x, mods_l, gains_l, w_main, w_z, w2cat, bg)


def _dot_tn(a, b):
    return lax.dot_general(a, b, (((0,), (0,)), ((), ())), preferred_element_type=F32)


def _dot_nt(a, b):
    return lax.dot_general(a, b, (((1,), (1,)), ((), ())), preferred_element_type=F32)


def _level_ref(cum_ref, row0, cum, m, backward):
    c = GLA_C
    pick = m - 1 if backward else m

    def bcast(r, n):
        return jnp.broadcast_to(cum_ref[pl.ds(row0 + r, 1), :], (n, GLA_DK))

    if m >= 4:
        return jnp.concatenate([bcast(g * 2 * m + pick, 2 * m) for g in range(c // (2 * m))], axis=0)
    pos = lax.broadcasted_iota(jnp.int32, (c, GLA_DK), 0)
    if m == 2:
        lo = jnp.concatenate([bcast(g * 8 + pick, 8) for g in range(c // 8)], axis=0)
        hi = jnp.concatenate([bcast(g * 8 + 4 + pick, 8) for g in range(c // 8)], axis=0)
        return jnp.where((pos & 4) == 0, lo, hi)
    if backward:
        return jnp.where((pos & 1) == 1, pltpu.roll(cum, 1, 0), cum)
    return jnp.where((pos & 1) == 0, pltpu.roll(cum, c - 1, 0), cum)


def _gla_intra(q_ref, k_ref, v_ref, cf_ref, cb_ref, row0, diag, masks, uppers):
    rows = pl.ds(row0, GLA_C)
    cf, cb = cf_ref[rows, :], cb_ref[rows, :]
    qb, kb = q_ref[rows, :].astype(BF16), k_ref[rows, :].astype(BF16)
    scores = jnp.where(diag, 2.0 * _dot_nt(qb, kb), 0.0)
    for lvl, m in enumerate(GLA_LEVELS):
        df = cf - _level_ref(cf_ref, row0, cf, m, False)
        db = cb - _level_ref(cb_ref, row0, cb, m, True)
        gq = jnp.exp2(jnp.where(uppers[lvl], df, db) * LOG2_E).astype(BF16)
        gk = jnp.exp2(jnp.where(uppers[lvl], db, df) * (-LOG2_E)).astype(BF16)
        scores = jnp.where(masks[lvl], _dot_nt(qb * gq, kb * gk), scores)
    return jnp.dot(scores.astype(BF16), v_ref[rows, :], preferred_element_type=F32)


def _gla_inter(q_ref, k_ref, v_ref, cum_ref, row0, st, backward):
    rows = pl.ds(row0, GLA_C)
    cum = cum_ref[rows, :]
    tot = cum_ref[pl.ds(row0 + (0 if backward else GLA_C - 1), 1), :]
    o = _dot_nt((q_ref[rows, :] * jnp.exp(cum)).astype(BF16), st.astype(BF16))
    kd = (k_ref[rows, :] * jnp.exp(tot - cum)).astype(BF16)
    return o, st * jnp.exp(tot) + _dot_tn(v_ref[rows, :], kd)


def _gla_kernel(*refs, t_len, seqs, has_s0):
    if has_s0:
        q_ref, k_ref, v_ref, cf_ref, cb_ref, r_ref, gn_ref, s0_ref, y_ref, sf_ref, sb_ref = refs
    else:
        q_ref, k_ref, v_ref, cf_ref, cb_ref, r_ref, gn_ref, y_ref, sout_ref, sf_ref, sb_ref = refs
    c = GLA_C
    n_chunks = t_len // c
    ri = lax.broadcasted_iota(jnp.int32, (c, c), 0)
    ci = lax.broadcasted_iota(jnp.int32, (c, c), 1)
    x = ri ^ ci
    masks = [(x >= m) & (x < 2 * m) for m in GLA_LEVELS]
    pos = lax.broadcasted_iota(jnp.int32, (c, GLA_DK), 0)
    uppers = [(pos & m) != 0 for m in GLA_LEVELS]

    def intra(i, carry):
        row0 = pl.multiple_of(i * c, c)
        y_ref[pl.ds(row0, c), :] = _gla_intra(q_ref, k_ref, v_ref, cf_ref, cb_ref, row0, ri == ci, masks, uppers)
        return carry

    lax.fori_loop(0, seqs * n_chunks, intra, 0, unroll=min(seqs * n_chunks, GLA_INTRA_UNROLL))

    for s in range(seqs):
        if has_s0:
            sf_ref[s] = s0_ref[s, 0].T
            sb_ref[s] = s0_ref[s, 1].T
        else:
            sf_ref[s] = jnp.zeros((GLA_DV, GLA_DK), F32)
            sb_ref[s] = jnp.zeros((GLA_DV, GLA_DK), F32)

        def inter(i, carry, s=s):
            rf = pl.multiple_of(s * t_len + i * c, c)
            rb = pl.multiple_of(s * t_len + (n_chunks - 1 - i) * c, c)
            o, st = _gla_inter(q_ref, k_ref, v_ref, cf_ref, rf, sf_ref[s], False)
            y_ref[pl.ds(rf, c), :] += o
            sf_ref[s] = st
            o, st = _gla_inter(q_ref, k_ref, v_ref, cb_ref, rb, sb_ref[s], True)
            y_ref[pl.ds(rb, c), :] += o
            sb_ref[s] = st
            return carry

        lax.fori_loop(0, n_chunks, inter, 0, unroll=min(n_chunks, GLA_INTER_UNROLL))
        if not has_s0:
            sout_ref[s, 0] = sf_ref[s].T
            sout_ref[s, 1] = sb_ref[s].T
    o = y_ref[...]
    y_ref[...] = _rms(o) * gn_ref[...] * r_ref[...]


def _gla_mix(q, k, v, cum, r, gnorm, s0, *, t_len, seqs, row0, n_seq):
    has_s0 = s0 is not None
    blk = seqs * t_len
    blk0 = row0 // blk
    assert blk0 * blk == row0 and n_seq % seqs == 0

    def rows(w, off=0):
        return pl.BlockSpec((blk, w), lambda s, h: (blk0 + s, off + h))

    state_spec = pl.BlockSpec((seqs, 2, None, GLA_DK, GLA_DV), lambda s, h: (s, 0, h, 0, 0))
    in_specs = [rows(GLA_DK), rows(GLA_DK), rows(GLA_DV), rows(GLA_DK), rows(GLA_DK, GLA_HEADS),
                rows(GLA_DV), pl.BlockSpec((None, 1, GLA_DV), lambda s, h: (h, 0, 0))]
    args = [q, k, v, cum, cum, r, gnorm.reshape(GLA_HEADS, 1, GLA_DV)]
    y_spec = pl.BlockSpec((blk, GLA_DV), lambda s, h: (s, h))
    y_shape = jax.ShapeDtypeStruct((n_seq * t_len, GLA_V), F32)
    if has_s0:
        in_specs.append(state_spec)
        args.append(s0)
        out_specs, out_shape = y_spec, y_shape
    else:
        out_specs = [y_spec, state_spec]
        out_shape = [y_shape, jax.ShapeDtypeStruct((n_seq, 2, GLA_HEADS, GLA_DK, GLA_DV), F32)]
    return pl.pallas_call(
        functools.partial(_gla_kernel, t_len=t_len, seqs=seqs, has_s0=has_s0),
        grid=(n_seq // seqs, GLA_HEADS),
        in_specs=in_specs,
        out_specs=out_specs,
        out_shape=out_shape,
        scratch_shapes=[pltpu.VMEM((seqs, GLA_DV, GLA_DK), F32)] * 2,
        compiler_params=_cparams("arbitrary", "arbitrary"),
        name="gla_mix_sample" if has_s0 else "gla_mix_prompt",
    )(*args)


def _natural_to_scan(x_half):
    x4 = x_half.reshape(-1, SCAN_V, SCAN_L, D_MODEL)
    return x4.swapaxes(1, 2).reshape(-1, D_MODEL)


def _scan_to_natural(x_half):
    x4 = x_half.reshape(-1, SCAN_L, SCAN_V, D_MODEL)
    return x4.swapaxes(1, 2).reshape(-1, D_MODEL)


def _natural_to_col_major(x_half):
    x4 = x_half.reshape(DEC_BATCH, DEC_SEQ // GRID_W, GRID_W, D_MODEL)
    return x4.swapaxes(1, 2).reshape(-1, D_MODEL)


def _col_major_to_natural(x_half):
    x4 = x_half.reshape(DEC_BATCH, GRID_W, DEC_SEQ // GRID_W, D_MODEL)
    return x4.swapaxes(1, 2).reshape(-1, D_MODEL)


def kernel(x_prompt, x_sample, c, state_rglru, state_gla, c_ctx, ada_w, ada_b, norm_g, ffn_w_in, ffn_w_out,
           rg_w_in, rg_conv_w, rg_conv_b, rg_w_a, rg_b_a, rg_w_i, rg_b_i, rg_lambda, rg_w_out,
           gla_w_in, gla_w_g2, gla_b_g, gla_norm, gla_w_out, final_norm):
    depth = ada_w.shape[0]
    cvecs = jnp.concatenate([c_ctx[None, :], c, jnp.zeros((8 - N_GROUPS, D_MODEL), F32)], axis=0)
    mods = _ada_mods(cvecs, ada_w, ada_b)
    fn = final_norm.reshape(1, D_MODEL)
    rg_states, gla_states = [], []

    w_in, w_out = ffn_w_in[0, 0].astype(BF16), ffn_w_out[0, 0].astype(BF16)
    n_main = 2 * GLA_QK + 2 * GLA_V

    def ffn_casts(layer, half):
        return ((ffn_w_in, (layer, half), 2 * D_FF), (ffn_w_out, (layer, half), D_MODEL))

    xp = x_prompt.reshape(N_PROMPT_TOK, D_MODEL)
    xs = x_sample.reshape(DEC_BATCH * DEC_SEQ, D_MODEL)
    for i in range(depth):
        mods_l, gains_l = mods[i], norm_g[i]
        j = i // 2
        last = i == depth - 1
        if i % 2 == 0:
            halves = (_natural_to_scan(xp), _natural_to_scan(xs))
            mixer_casts = ((rg_w_in, (j,), 2 * D_RNN), (rg_w_out, (j,), D_MODEL))
        else:
            halves = (xp, _natural_to_col_major(xs))
            mixer_casts = ((gla_w_in, (j,), n_main), (gla_w_out, (j,), D_MODEL))
        x, w_in, w_out, wm_in, wm_out = _half_ffn(halves, mods_l, gains_l, w_in, w_out, 0, fn, False, False,
                                                  casts=ffn_casts(i, 1) + mixer_casts)
        if i % 2 == 0:
            gate, xbr = _rg_proj(x, mods_l, gains_l, wm_in)
            y, st_f, st_b = _rg_scan(xbr, gate, rg_conv_w[j], rg_conv_b[j].reshape(1, D_RNN),
                                     rg_w_a[j].astype(BF16), rg_b_a[j], rg_w_i[j].astype(BF16), rg_b_i[j],
                                     rg_lambda[j], state_rglru[:, j])
            rg_states.append(jnp.stack([st_f[:BATCH], st_b[:BATCH]], axis=1))
            ys = (y,)
        else:
            w_z = jnp.pad(gla_w_in[j][:, n_main:], ((0, 0), (0, 128 - 2 * GLA_RANK))).astype(BF16)
            w2cat = jnp.zeros((128, 2 * GLA_QK), F32)
            w2cat = w2cat.at[:GLA_RANK, :GLA_QK].set(gla_w_g2[j, 0])
            w2cat = w2cat.at[GLA_RANK:2 * GLA_RANK, GLA_QK:].set(gla_w_g2[j, 1]).astype(BF16)
            q, k, v, r, cum = _gla_proj(x, mods_l, gains_l, wm_in, w_z, w2cat, gla_b_g[j].reshape(1, 2 * GLA_QK))
            y_p, st = _gla_mix(q, k, v, cum, r, gla_norm[j], None, t_len=SEQ, seqs=GLA_PROMPT_SEQS, row0=0,
                               n_seq=BATCH)
            y_s = _gla_mix(q, k, v, cum, r, gla_norm[j], state_gla[:, j], t_len=DEC_SEQ, seqs=1,
                           row0=N_PROMPT_TOK, n_seq=DEC_BATCH)
            gla_states.append(st)
            ys = (y_p, y_s)
        xp, xs, *next_pair = _half_ffn((x,), mods_l, gains_l, w_in, w_out, 1, fn, last, True, mix=(ys, wm_out),
                                       casts=() if last else ffn_casts(i + 1, 0))
        if not last:
            w_in, w_out = next_pair
        if i % 2 == 0:
            xp, xs = _scan_to_natural(xp), _scan_to_natural(xs)
        else:
            xs = _col_major_to_natural(xs)

    y_prompt = xp.reshape(BATCH, SEQ, D_MODEL)
    y_sample = xs.reshape(DEC_BATCH, DEC_SEQ, D_MODEL)
    return (y_prompt, y_sample, jnp.stack(rg_states, axis=1), jnp.stack(gla_states, axis=1))
```

```python
import functools

import jax
import jax.numpy as jnp
from jax import lax
from jax.experimental import pallas as pl
from jax.experimental.pallas import tpu as pltpu

F32 = jnp.float32
BF16 = jnp.bfloat16

D_MODEL = 1024
BATCH = 16
SEQ = 256
DEC_BATCH = 2
DEC_SEQ = 2048
GRID_W = 64
EPS = 1e-6
N_MOD = 9
D_FF = 2816
FFN_RES = 0.5
D_RNN = 1280
RG_BLOCKS = 10
RG_BLOCK = 128
CONV_W = 4
RG_C = 8.0
GLA_HEADS = 4
GLA_QK = 512
GLA_V = 1024
GLA_DK = 128
GLA_DV = 256
GLA_RANK = 16
GLA_TAU = 16.0
LOG2_E = 1.4426950408889634

N_PROMPT_TOK = BATCH * SEQ
N_TOK = N_PROMPT_TOK + DEC_BATCH * DEC_SEQ
N_GROUPS = 1 + DEC_BATCH
GROUP_ROWS = DEC_SEQ
PROMPT_BLOCKS = N_PROMPT_TOK // GROUP_ROWS

VMEM_LIMIT_BYTES = 56 * 1024 * 1024

TM = 512
HALF_TILES = N_PROMPT_TOK // TM
TM_PROJ = 1024
ADA_SLAB = 256
MXU_DIM = 256
BF16_SUBLANES = 16
FF_CHUNKS = (0, 6 * MXU_DIM, D_FF)
RG_CT = 256
SCAN_L = 256
SCAN_V = GROUP_ROWS // SCAN_L
GLA_C = 128
GLA_PROMPT_SEQS = 4
GLA_INTRA_UNROLL = 4
GLA_INTER_UNROLL = 8
GLA_LEVELS = tuple(GLA_C >> (l + 1) for l in range(GLA_C.bit_length() - 1))


def _cparams(*sem):
    return pltpu.CompilerParams(dimension_semantics=sem, vmem_limit_bytes=VMEM_LIMIT_BYTES)


def _group_of_tile(i, rows):
    return jnp.maximum((i * rows) // GROUP_ROWS - (PROMPT_BLOCKS - 1), 0)


def _sigmoid(x):
    return 1.0 / (1.0 + jnp.exp2(x * (-LOG2_E)))


def _softplus(x):
    return jnp.maximum(x, 0.0) + jnp.log1p(jnp.exp(-jnp.abs(x)))


def _rms(x):
    return x * lax.rsqrt(jnp.mean(x * x, axis=-1, keepdims=True) + EPS)


def _modulated(x, mod_ref, gain_ref, j):
    y = _rms(x) * gain_ref[j:j + 1, :]
    return y * (1.0 + mod_ref[3 * j + 1:3 * j + 2, :]) + mod_ref[3 * j:3 * j + 1, :]


def _ada_kernel(c_ref, w_ref, b_ref, o_ref):
    @pl.when(pl.program_id(1) == 0)
    def _():
        o_ref[...] = jnp.broadcast_to(b_ref[...], o_ref.shape)

    c = c_ref[...]
    s = (c * _sigmoid(c)).astype(BF16)
    o_ref[...] += jnp.dot(s, w_ref[...].astype(BF16), preferred_element_type=F32)


def _ada_mods(cvecs, ada_w, ada_b):
    depth = ada_w.shape[0]
    n_slab = D_MODEL // ADA_SLAB
    c_slabs = cvecs.reshape(8, n_slab, ADA_SLAB).swapaxes(0, 1)
    out = pl.pallas_call(
        _ada_kernel,
        grid=(depth, n_slab),
        in_specs=[
            pl.BlockSpec((None, 8, ADA_SLAB), lambda l, k: (k, 0, 0)),
            pl.BlockSpec((None, ADA_SLAB, N_MOD * D_MODEL), lambda l, k: (l, k, 0)),
            pl.BlockSpec((None, 1, N_MOD * D_MODEL), lambda l, k: (l, 0, 0)),
        ],
        out_specs=pl.BlockSpec((None, 8, N_MOD * D_MODEL), lambda l, k: (l, 0, 0)),
        out_shape=jax.ShapeDtypeStruct((depth, 8, N_MOD * D_MODEL), F32),
        compiler_params=_cparams("arbitrary", "arbitrary"),
        name="ada_mod",
    )(c_slabs, ada_w, ada_b.reshape(depth, 1, N_MOD * D_MODEL))
    return out[:, :N_GROUPS, :].reshape(depth, N_GROUPS, N_MOD, D_MODEL)


def _pick_half(refs):
    if len(refs) == 1:
        return refs[0][...]
    return jnp.where(pl.program_id(0) < HALF_TILES, refs[0][...], refs[1][...])


def _ffn_kernel(*refs, j, final, n_x, n_y, split_out, n_cast):
    x_refs, y_refs, refs = refs[:n_x], refs[n_x:n_x + n_y], refs[n_x + n_y:]
    if n_y:
        wmix_ref, refs = refs[0], refs[1:]
    mod_ref, gain_ref, win_ref, wout_ref, fn_ref = refs[:5]
    refs = refs[5:]
    if n_cast:
        f32_refs, refs = refs[:n_cast], refs[n_cast:]
        for src_ref, dst_ref in zip(f32_refs, refs[len(refs) - n_cast:]):
            dst_ref[...] = src_ref[...].astype(BF16)
    o_refs = refs
    x = _pick_half(x_refs)
    if n_y:
        y = _pick_half(y_refs).astype(BF16)
        x = x + mod_ref[5:6, :] * jnp.dot(y, wmix_ref[...], preferred_element_type=F32)
    hb = _modulated(x, mod_ref, gain_ref, j).astype(BF16)
    acc = jnp.zeros((TM, D_MODEL), F32)
    for lo, hi in zip(FF_CHUNKS[:-1], FF_CHUNKS[1:]):
        g = jnp.dot(hb, win_ref[:, lo:hi], preferred_element_type=F32)
        u = jnp.dot(hb, win_ref[:, D_FF + lo:D_FF + hi], preferred_element_type=F32)
        a = (g * _sigmoid(g) * u).astype(BF16)
        acc = acc + jnp.dot(a, wout_ref[lo:hi, :], preferred_element_type=F32)
    out = x + FFN_RES * mod_ref[3 * j + 2:3 * j + 3, :] * acc
    if final:
        out = _rms(out) * fn_ref[...]
    if split_out:
        first_half = pl.program_id(0) < HALF_TILES

        @pl.when(first_half)
        def _():
            o_refs[0][...] = out

        @pl.when(jnp.logical_not(first_half))
        def _():
            o_refs[1][...] = out
    else:
        o_refs[0][...] = out


def _row_specs(width, split):
    if not split:
        return [pl.BlockSpec((TM, width), lambda i: (i, 0))]
    return [pl.BlockSpec((TM, width), lambda i: (jnp.minimum(i, HALF_TILES - 1), 0)),
            pl.BlockSpec((TM, width), lambda i: (jnp.maximum(i - HALF_TILES, 0), 0))]


def _half_ffn(xs, mods_l, gains_l, w_in, w_out, half, final_norm, final, split_out, mix=None, casts=()):
    n_steps = N_TOK // TM
    ys, w_mix = mix if mix is not None else ((), None)
    in_specs = _row_specs(D_MODEL, len(xs) == 2)
    args = list(xs)
    if ys:
        in_specs += _row_specs(ys[0].shape[1], len(ys) == 2)
        in_specs.append(pl.BlockSpec(w_mix.shape, lambda i: (0, 0), pipeline_mode=pl.Buffered(1)))
        args += list(ys) + [w_mix]
    in_specs += [
        pl.BlockSpec((None, N_MOD, D_MODEL), lambda i: (_group_of_tile(i, TM), 0, 0)),
        pl.BlockSpec((3, D_MODEL), lambda i: (0, 0)),
        pl.BlockSpec((D_MODEL, 2 * D_FF), lambda i: (0, 0), pipeline_mode=pl.Buffered(1)),
        pl.BlockSpec((D_FF, D_MODEL), lambda i: (0, 0), pipeline_mode=pl.Buffered(1)),
        pl.BlockSpec((1, D_MODEL), lambda i: (0, 0)),
    ]
    args += [mods_l, gains_l, w_in, w_out, final_norm]
    n_rows = N_PROMPT_TOK if split_out else N_TOK
    out_specs = _row_specs(D_MODEL, split_out)
    out_shape = [jax.ShapeDtypeStruct((n_rows, D_MODEL), F32)] * (2 if split_out else 1)
    for arr, lead, cols in casts:
        rows = arr.shape[len(lead)]
        slab = rows // n_steps
        assert slab * n_steps == rows and slab % BF16_SUBLANES == 0, (rows, n_steps)
        in_specs.append(pl.BlockSpec((None,) * len(lead) + (slab, cols), lambda i, lead=lead: (*lead, i, 0)))
        args.append(arr)
        out_specs.append(pl.BlockSpec((slab, cols), lambda i: (i, 0)))
        out_shape.append(jax.ShapeDtypeStruct((rows, cols), BF16))
    return pl.pallas_call(
        functools.partial(_ffn_kernel, j=2 * half, final=final, n_x=len(xs), n_y=len(ys), split_out=split_out,
                          n_cast=len(casts)),
        grid=(n_steps,),
        in_specs=in_specs,
        out_specs=out_specs,
        out_shape=out_shape,
        compiler_params=_cparams("arbitrary"),
        name="half_ffn",
    )(*args)


def _gelu_tanh(x):
    return 0.5 * x * (1.0 + jnp.tanh(0.7978845608028654 * (x + 0.044715 * (x * x * x))))


def _rg_proj_kernel(x_ref, mod_ref, gain_ref, w_ref, gate_ref, xbr_ref):
    hb = _modulated(x_ref[...], mod_ref, gain_ref, 1).astype(BF16)
    gate_ref[...] = _gelu_tanh(jnp.dot(hb, w_ref[:, :D_RNN], preferred_element_type=F32))
    xbr_ref[...] = jnp.dot(hb, w_ref[:, D_RNN:], preferred_element_type=F32)


def _rg_proj(x, mods_l, gains_l, w_in):
    return pl.pallas_call(
        _rg_proj_kernel,
        grid=(N_TOK // TM_PROJ,),
        in_specs=[
            pl.BlockSpec((TM_PROJ, D_MODEL), lambda i: (i, 0)),
            pl.BlockSpec((None, N_MOD, D_MODEL), lambda i: (_group_of_tile(i, TM_PROJ), 0, 0)),
            pl.BlockSpec((3, D_MODEL), lambda i: (0, 0)),
            pl.BlockSpec((D_MODEL, 2 * D_RNN), lambda i: (0, 0), pipeline_mode=pl.Buffered(1)),
        ],
        out_specs=[pl.BlockSpec((TM_PROJ, D_RNN), lambda i: (i, 0))] * 2,
        out_shape=[jax.ShapeDtypeStruct((N_TOK, D_RNN), F32)] * 2,
        compiler_params=_cparams("arbitrary"),
        name="rg_in_proj",
    )(x, mods_l, gains_l, w_in)


def _seg_shift(row, down):
    v = lax.broadcasted_iota(jnp.int32, row.shape, 0)
    if down:
        return jnp.where(v >= 1, pltpu.roll(row, 1, 0), 0.0)
    return jnp.where(v < SCAN_V - 1, pltpu.roll(row, SCAN_V - 1, 0), 0.0)


def _rg_scan_kernel(xbr_ref, gate_ref, cw_ref, cb_ref, wa_ref, ba_ref, wi_ref, bi_ref, lam_ref, h0_ref,
                    y_ref, stf_ref, stb_ref, af_ref, uf_ref, ab_ref, ub_ref):
    chain = jnp.where(pl.program_id(0) >= PROMPT_BLOCKS, 1.0, 0.0).astype(F32)
    nbt = RG_CT // RG_BLOCK
    rows = SCAN_L * SCAN_V
    v = SCAN_V

    for nb in range(nbt):
        ls = slice(nb * RG_BLOCK, (nb + 1) * RG_BLOCK)
        x = xbr_ref[:, ls]
        before = chain * _seg_shift(x[rows - v:], True)
        after0 = chain * _seg_shift(x[:v], False)
        after1 = chain * _seg_shift(x[v:2 * v], False)
        xm1 = jnp.concatenate([before, x[:rows - v]], axis=0)
        xp1 = jnp.concatenate([x[v:], after0], axis=0)
        xp2 = jnp.concatenate([x[2 * v:], after0, after1], axis=0)
        xc = (cw_ref[0:1, ls] * xm1 + cw_ref[1:2, ls] * x + cw_ref[2:3, ls] * xp1 + cw_ref[3:4, ls] * xp2
              + cb_ref[0:1, ls])
        xcb = xc.astype(BF16)
        half_xc = 0.5 * xc
        for d, (a_ref, u_ref) in enumerate(((af_ref, uf_ref), (ab_ref, ub_ref))):
            tr = jnp.tanh(0.5 * (jnp.dot(xcb, wa_ref[d, nb], preferred_element_type=F32) + ba_ref[d:d + 1, ls]))
            ti = jnp.tanh(0.5 * (jnp.dot(xcb, wi_ref[d, nb], preferred_element_type=F32) + bi_ref[d:d + 1, ls]))
            half_k = (-0.5 * RG_C * LOG2_E) * _softplus(-lam_ref[d:d + 1, ls])
            a = jnp.exp2(tr * half_k + half_k)
            a_ref[nb] = a
            z = 1.0 - a * a
            u_ref[nb] = jnp.where(z > 0.0, z * lax.rsqrt(z), 0.0) * (ti * half_xc + half_xc)

    def step(i, carry):
        sf = pl.ds(pl.multiple_of(i * v, v), v)
        sb = pl.ds(pl.multiple_of((SCAN_L - 1 - i) * v, v), v)
        out = []
        for nb in range(nbt):
            hf, pf, hb, pb = carry[4 * nb:4 * nb + 4]
            a = af_ref[nb, sf, :]
            hf = a * hf + uf_ref[nb, sf, :]
            pf = a * pf
            uf_ref[nb, sf, :] = hf
            af_ref[nb, sf, :] = pf
            a = ab_ref[nb, sb, :]
            hb = a * hb + ub_ref[nb, sb, :]
            pb = a * pb
            ub_ref[nb, sb, :] = hb
            ab_ref[nb, sb, :] = pb
            out += [hf, pf, hb, pb]
        return tuple(out)

    zero = jnp.zeros((v, RG_BLOCK), F32)
    one = jnp.ones((v, RG_BLOCK), F32)
    lax.fori_loop(0, SCAN_L, step, (zero, one, zero, one) * nbt, unroll=4)

    for nb in range(nbt):
        ls = slice(nb * RG_BLOCK, (nb + 1) * RG_BLOCK)
        h_end, p_end = uf_ref[nb, rows - v:, :], af_ref[nb, rows - v:, :]
        entry = [chain * h0_ref[0:1, ls]]
        for s in range(v - 1):
            entry.append(chain * (h_end[s:s + 1] + p_end[s:s + 1] * entry[s]))
        ef = jnp.concatenate(entry, axis=0)
        h_start, p_start = ub_ref[nb, :v, :], ab_ref[nb, :v, :]
        entry = [chain * h0_ref[1:2, ls]]
        for s in range(v - 1, 0, -1):
            entry.append(chain * (h_start[s:s + 1] + p_start[s:s + 1] * entry[-1]))
        eb = jnp.concatenate(entry[::-1], axis=0)
        shape3 = (SCAN_L, v, RG_BLOCK)
        hf = uf_ref[nb].reshape(shape3) + af_ref[nb].reshape(shape3) * ef[None]
        hb = ub_ref[nb].reshape(shape3) + ab_ref[nb].reshape(shape3) * eb[None]
        stf_ref[:, ls] = hf[SCAN_L - 1]
        stb_ref[:, ls] = hb[0]
        y_ref[:, ls] = (hf + hb).reshape(rows, RG_BLOCK) * gate_ref[:, ls]


def _rg_scan(xbr, gate, conv_w, conv_b, w_a, b_a, w_i, b_i, lam, h0):
    n_blk = N_TOK // GROUP_ROWS
    nbt = RG_CT // RG_BLOCK
    tile = pl.BlockSpec((GROUP_ROWS, RG_CT), lambda b, c: (b, c))
    vec2 = pl.BlockSpec((2, RG_CT), lambda b, c: (0, c))
    wspec = pl.BlockSpec((2, nbt, RG_BLOCK, RG_BLOCK), lambda b, c: (0, c, 0, 0))
    return pl.pallas_call(
        _rg_scan_kernel,
        grid=(n_blk, D_RNN // RG_CT),
        in_specs=[
            tile, tile,
            pl.BlockSpec((CONV_W, RG_CT), lambda b, c: (0, c)),
            pl.BlockSpec((1, RG_CT), lambda b, c: (0, c)),
            wspec, vec2, wspec, vec2, vec2,
            pl.BlockSpec((None, 2, RG_CT), lambda b, c: (jnp.maximum(b - PROMPT_BLOCKS, 0), 0, c)),
        ],
        out_specs=[tile] + [pl.BlockSpec((SCAN_V, RG_CT), lambda b, c: (b, c))] * 2,
        out_shape=[jax.ShapeDtypeStruct((N_TOK, D_RNN), F32)]
        + [jax.ShapeDtypeStruct((n_blk * SCAN_V, D_RNN), F32)] * 2,
        scratch_shapes=[pltpu.VMEM((nbt, GROUP_ROWS, RG_BLOCK), F32)] * 4,
        compiler_params=_cparams("arbitrary", "arbitrary"),
        name="rg_scan",
    )(xbr, gate, conv_w, conv_b, w_a, b_a, w_i, b_i, lam, h0)


def _chunk_tri():
    ri = lax.broadcasted_iota(jnp.int32, (GLA_C, GLA_C), 0)
    ci = lax.broadcasted_iota(jnp.int32, (GLA_C, GLA_C), 1)
    return (ri >= ci).astype(BF16), (ri <= ci).astype(BF16)


def _split3(x):
    hi = x.astype(BF16)
    r1 = x - hi.astype(F32)
    mid = r1.astype(BF16)
    lo = (r1 - mid.astype(F32)).astype(BF16)
    return hi, mid, lo


def _gla_proj_kernel(x_ref, mod_ref, gain_ref, w_ref, wz_ref, w2_ref, bg_ref,
                     q_ref, k_ref, v_ref, r_ref, cum_ref):
    hb = _modulated(x_ref[...], mod_ref, gain_ref, 1).astype(BF16)
    z = jnp.dot(hb, wz_ref[...].astype(BF16), preferred_element_type=F32).astype(BF16)
    zz = jnp.dot(z, w2_ref[...], preferred_element_type=F32) + bg_ref[...]
    log_gate = -_softplus(-zz) * (LOG2_E / GLA_TAU)
    parts = _split3(log_gate)
    r = jnp.dot(hb, w_ref[:, 2 * GLA_QK + GLA_V:], preferred_element_type=F32)
    r_ref[...] = r * _sigmoid(r)
    q_ref[...] = jnp.dot(hb, w_ref[:, :GLA_QK], preferred_element_type=F32) * (GLA_DK ** -0.5)
    k_ref[...] = jnp.dot(hb, w_ref[:, GLA_QK:2 * GLA_QK], preferred_element_type=F32)
    v_ref[...] = jnp.dot(hb, w_ref[:, 2 * GLA_QK:2 * GLA_QK + GLA_V], preferred_element_type=F32).astype(BF16)
    tri_f, tri_b = _chunk_tri()
    for ch in range(TM_PROJ // GLA_C):
        rows = slice(ch * GLA_C, (ch + 1) * GLA_C)
        cum_ref[rows, :GLA_QK] = sum(jnp.dot(tri_f, p[rows, :GLA_QK], preferred_element_type=F32) for p in parts)
        cum_ref[rows, GLA_QK:] = sum(jnp.dot(tri_b, p[rows, GLA_QK:], preferred_element_type=F32) for p in parts)


def _gla_proj(x, mods_l, gains_l, w_main, w_z, w2cat, bg):
    def rows(w):
        return pl.BlockSpec((TM_PROJ, w), lambda i: (i, 0))

    def whole(a):
        return pl.BlockSpec(a.shape, lambda i: (0,) * a.ndim, pipeline_mode=pl.Buffered(1))

    widths = (GLA_QK, GLA_QK, GLA_V, GLA_V, 2 * GLA_QK)
    dtypes = (F32, F32, BF16, F32, F32)
    return pl.pallas_call(
        _gla_proj_kernel,
        grid=(N_TOK // TM_PROJ,),
        in_specs=[
            rows(D_MODEL),
            pl.BlockSpec((None, N_MOD, D_MODEL), lambda i: (_group_of_tile(i, TM_PROJ), 0, 0)),
            pl.BlockSpec((3, D_MODEL), lambda i: (0, 0)),
            whole(w_main), whole(w_z), whole(w2cat), whole(bg),
        ],
        out_specs=[rows(w) for w in widths],
        out_shape=[jax.ShapeDtypeStruct((N_TOK, w), t) for w, t in zip(widths, dtypes)],
        compiler_params=_cparams("arbitrary"),
        name="gla_in_proj",
    )(x, mods_l, gains_l, w_main, w_z, w2cat, bg)


def _dot_tn(a, b):
    return lax.dot_general(a, b, (((0,), (0,)), ((), ())), preferred_element_type=F32)


def _dot_nt(a, b):
    return lax.dot_general(a, b, (((1,), (1,)), ((), ())), preferred_element_type=F32)


def _level_ref(cum_ref, row0, cum, m, backward):
    c = GLA_C
    pick = m - 1 if backward else m

    def bcast(r, n):
        return jnp.broadcast_to(cum_ref[pl.ds(row0 + r, 1), :], (n, GLA_DK))

    if m >= 4:
        return jnp.concatenate([bcast(g * 2 * m + pick, 2 * m) for g in range(c // (2 * m))], axis=0)
    pos = lax.broadcasted_iota(jnp.int32, (c, GLA_DK), 0)
    if m == 2:
        lo = jnp.concatenate([bcast(g * 8 + pick, 8) for g in range(c // 8)], axis=0)
        hi = jnp.concatenate([bcast(g * 8 + 4 + pick, 8) for g in range(c // 8)], axis=0)
        return jnp.where((pos & 4) == 0, lo, hi)
    if backward:
        return jnp.where((pos & 1) == 1, pltpu.roll(cum, 1, 0), cum)
    return jnp.where((pos & 1) == 0, pltpu.roll(cum, c - 1, 0), cum)


def _gla_intra(q_ref, k_ref, v_ref, cf_ref, cb_ref, row0, diag, masks, uppers):
    rows = pl.ds(row0, GLA_C)
    cf, cb = cf_ref[rows, :], cb_ref[rows, :]
    qb, kb = q_ref[rows, :].astype(BF16), k_ref[rows, :].astype(BF16)
    scores = jnp.where(diag, 2.0 * _dot_nt(qb, kb), 0.0)
    for lvl, m in enumerate(GLA_LEVELS):
        df = cf - _level_ref(cf_ref, row0, cf, m, False)
        db = cb - _level_ref(cb_ref, row0, cb, m, True)
        gq = jnp.exp2(jnp.where(uppers[lvl], df, db)).astype(BF16)
        gk = jnp.exp2(-jnp.where(uppers[lvl], db, df)).astype(BF16)
        scores = jnp.where(masks[lvl], _dot_nt(qb * gq, kb * gk), scores)
    return jnp.dot(scores.astype(BF16), v_ref[rows, :], preferred_element_type=F32)


def _gla_inter(q_ref, k_ref, v_ref, cum_ref, row0, st, backward):
    rows = pl.ds(row0, GLA_C)
    cum = cum_ref[rows, :]
    tot = cum_ref[pl.ds(row0 + (0 if backward else GLA_C - 1), 1), :]
    o = _dot_nt((q_ref[rows, :] * jnp.exp2(cum)).astype(BF16), st.astype(BF16))
    kd = (k_ref[rows, :] * jnp.exp2(tot - cum)).astype(BF16)
    return o, st * jnp.exp2(tot) + _dot_tn(v_ref[rows, :], kd)


def _gla_kernel(*refs, t_len, seqs, has_s0):
    if has_s0:
        q_ref, k_ref, v_ref, cf_ref, cb_ref, r_ref, gn_ref, s0_ref, y_ref, sf_ref, sb_ref = refs
    else:
        q_ref, k_ref, v_ref, cf_ref, cb_ref, r_ref, gn_ref, y_ref, sout_ref, sf_ref, sb_ref = refs
    c = GLA_C
    n_chunks = t_len // c
    ri = lax.broadcasted_iota(jnp.int32, (c, c), 0)
    ci = lax.broadcasted_iota(jnp.int32, (c, c), 1)
    x = ri ^ ci
    masks = [(x >= m) & (x < 2 * m) for m in GLA_LEVELS]
    pos = lax.broadcasted_iota(jnp.int32, (c, GLA_DK), 0)
    uppers = [(pos & m) != 0 for m in GLA_LEVELS]

    def intra(i, carry):
        row0 = pl.multiple_of(i * c, c)
        y_ref[pl.ds(row0, c), :] = _gla_intra(q_ref, k_ref, v_ref, cf_ref, cb_ref, row0, ri == ci, masks, uppers)
        return carry

    lax.fori_loop(0, seqs * n_chunks, intra, 0, unroll=min(seqs * n_chunks, GLA_INTRA_UNROLL))

    for s in range(seqs):
        if has_s0:
            sf_ref[s] = s0_ref[s, 0].T
            sb_ref[s] = s0_ref[s, 1].T
        else:
            sf_ref[s] = jnp.zeros((GLA_DV, GLA_DK), F32)
            sb_ref[s] = jnp.zeros((GLA_DV, GLA_DK), F32)

        def inter(i, carry, s=s):
            rf = pl.multiple_of(s * t_len + i * c, c)
            rb = pl.multiple_of(s * t_len + (n_chunks - 1 - i) * c, c)
            o, st = _gla_inter(q_ref, k_ref, v_ref, cf_ref, rf, sf_ref[s], False)
            y_ref[pl.ds(rf, c), :] += o
            sf_ref[s] = st
            o, st = _gla_inter(q_ref, k_ref, v_ref, cb_ref, rb, sb_ref[s], True)
            y_ref[pl.ds(rb, c), :] += o
            sb_ref[s] = st
            return carry

        lax.fori_loop(0, n_chunks, inter, 0, unroll=min(n_chunks, GLA_INTER_UNROLL))
        if not has_s0:
            sout_ref[s, 0] = sf_ref[s].T
            sout_ref[s, 1] = sb_ref[s].T
    o = y_ref[...]
    y_ref[...] = _rms(o) * gn_ref[...] * r_ref[...]


def _gla_mix(q, k, v, cum, r, gnorm, s0, *, t_len, seqs, row0, n_seq):
    has_s0 = s0 is not None
    blk = seqs * t_len
    blk0 = row0 // blk
    assert blk0 * blk == row0 and n_seq % seqs == 0

    def rows(w, off=0):
        return pl.BlockSpec((blk, w), lambda s, h: (blk0 + s, off + h))

    state_spec = pl.BlockSpec((seqs, 2, None, GLA_DK, GLA_DV), lambda s, h: (s, 0, h, 0, 0))
    in_specs = [rows(GLA_DK), rows(GLA_DK), rows(GLA_DV), rows(GLA_DK), rows(GLA_DK, GLA_HEADS),
                rows(GLA_DV), pl.BlockSpec((None, 1, GLA_DV), lambda s, h: (h, 0, 0))]
    args = [q, k, v, cum, cum, r, gnorm.reshape(GLA_HEADS, 1, GLA_DV)]
    y_spec = pl.BlockSpec((blk, GLA_DV), lambda s, h: (s, h))
    y_shape = jax.ShapeDtypeStruct((n_seq * t_len, GLA_V), F32)
    if has_s0:
        in_specs.append(state_spec)
        args.append(s0)
        out_specs, out_shape = y_spec, y_shape
    else:
        out_specs = [y_spec, state_spec]
        out_shape = [y_shape, jax.ShapeDtypeStruct((n_seq, 2, GLA_HEADS, GLA_DK, GLA_DV), F32)]
    return pl.pallas_call(
        functools.partial(_gla_kernel, t_len=t_len, seqs=seqs, has_s0=has_s0),
        grid=(n_seq // seqs, GLA_HEADS),
        in_specs=in_specs,
        out_specs=out_specs,
        out_shape=out_shape,
        scratch_shapes=[pltpu.VMEM((seqs, GLA_DV, GLA_DK), F32)] * 2,
        compiler_params=_cparams("arbitrary", "arbitrary"),
        name="gla_mix_sample" if has_s0 else "gla_mix_prompt",
    )(*args)


def _natural_to_scan(x_half):
    x4 = x_half.reshape(-1, SCAN_V, SCAN_L, D_MODEL)
    return x4.swapaxes(1, 2).reshape(-1, D_MODEL)


def _scan_to_natural(x_half):
    x4 = x_half.reshape(-1, SCAN_L, SCAN_V, D_MODEL)
    return x4.swapaxes(1, 2).reshape(-1, D_MODEL)


def _natural_to_col_major(x_half):
    x4 = x_half.reshape(DEC_BATCH, DEC_SEQ // GRID_W, GRID_W, D_MODEL)
    return x4.swapaxes(1, 2).reshape(-1, D_MODEL)


def _col_major_to_natural(x_half):
    x4 = x_half.reshape(DEC_BATCH, GRID_W, DEC_SEQ // GRID_W, D_MODEL)
    return x4.swapaxes(1, 2).reshape(-1, D_MODEL)


def kernel(x_prompt, x_sample, c, state_rglru, state_gla, c_ctx, ada_w, ada_b, norm_g, ffn_w_in, ffn_w_out,
           rg_w_in, rg_conv_w, rg_conv_b, rg_w_a, rg_b_a, rg_w_i, rg_b_i, rg_lambda, rg_w_out,
           gla_w_in, gla_w_g2, gla_b_g, gla_norm, gla_w_out, final_norm):
    depth = ada_w.shape[0]
    cvecs = jnp.concatenate([c_ctx[None, :], c, jnp.zeros((8 - N_GROUPS, D_MODEL), F32)], axis=0)
    mods = _ada_mods(cvecs, ada_w, ada_b)
    fn = final_norm.reshape(1, D_MODEL)
    rg_states, gla_states = [], []

    w_in, w_out = ffn_w_in[0, 0].astype(BF16), ffn_w_out[0, 0].astype(BF16)
    n_main = 2 * GLA_QK + 2 * GLA_V

    def ffn_casts(layer, half):
        return ((ffn_w_in, (layer, half), 2 * D_FF), (ffn_w_out, (layer, half), D_MODEL))

    xp = x_prompt.reshape(N_PROMPT_TOK, D_MODEL)
    xs = x_sample.reshape(DEC_BATCH * DEC_SEQ, D_MODEL)
    for i in range(depth):
        mods_l, gains_l = mods[i], norm_g[i]
        j = i // 2
        last = i == depth - 1
        if i % 2 == 0:
            halves = (_natural_to_scan(xp), _natural_to_scan(xs))
            mixer_casts = ((rg_w_in, (j,), 2 * D_RNN), (rg_w_out, (j,), D_MODEL))
        else:
            halves = (xp, _natural_to_col_major(xs))
            mixer_casts = ((gla_w_in, (j,), n_main), (gla_w_out, (j,), D_MODEL))
        x, w_in, w_out, wm_in, wm_out = _half_ffn(halves, mods_l, gains_l, w_in, w_out, 0, fn, False, False,
                                                  casts=ffn_casts(i, 1) + mixer_casts)
        if i % 2 == 0:
            gate, xbr = _rg_proj(x, mods_l, gains_l, wm_in)
            y, st_f, st_b = _rg_scan(xbr, gate, rg_conv_w[j], rg_conv_b[j].reshape(1, D_RNN),
                                     rg_w_a[j].astype(BF16), rg_b_a[j], rg_w_i[j].astype(BF16), rg_b_i[j],
                                     rg_lambda[j], state_rglru[:, j])
            rg_states.append(jnp.stack([st_f[:BATCH], st_b[:BATCH]], axis=1))
            ys = (y,)
        else:
            w_z = jnp.pad(gla_w_in[j, :, n_main:], ((0, 0), (0, 128 - 2 * GLA_RANK)))
            w2cat = jnp.zeros((128, 2 * GLA_QK), F32)
            w2cat = w2cat.at[:GLA_RANK, :GLA_QK].set(gla_w_g2[j, 0])
            w2cat = w2cat.at[GLA_RANK:2 * GLA_RANK, GLA_QK:].set(gla_w_g2[j, 1]).astype(BF16)
            q, k, v, r, cum = _gla_proj(x, mods_l, gains_l, wm_in, w_z, w2cat, gla_b_g[j].reshape(1, 2 * GLA_QK))
            y_p, st = _gla_mix(q, k, v, cum, r, gla_norm[j], None, t_len=SEQ, seqs=GLA_PROMPT_SEQS, row0=0,
                               n_seq=BATCH)
            y_s = _gla_mix(q, k, v, cum, r, gla_norm[j], state_gla[:, j], t_len=DEC_SEQ, seqs=1,
                           row0=N_PROMPT_TOK, n_seq=DEC_BATCH)
            gla_states.append(st)
            ys = (y_p, y_s)
        xp, xs, *next_pair = _half_ffn((x,), mods_l, gains_l, w_in, w_out, 1, fn, last, True, mix=(ys, wm_out),
                                       casts=() if last else ffn_casts(i + 1, 0))
        if not last:
            w_in, w_out = next_pair
        if i % 2 == 0:
            xp, xs = _scan_to_natural(xp), _scan_to_natural(xs)
        else:
            xs = _col_major_to_natural(xs)

    y_prompt = xp.reshape(BATCH, SEQ, D_MODEL)
    y_sample = xs.reshape(DEC_BATCH, DEC_SEQ, D_MODEL)
    return (y_prompt, y_sample, jnp.stack(rg_states, axis=1), jnp.stack(gla_states, axis=1))
```

```python
import functools

import jax
import jax.numpy as jnp
from jax import lax
from jax.experimental import pallas as pl
from jax.experimental.pallas import tpu as pltpu

F32 = jnp.float32
BF16 = jnp.bfloat16

D_MODEL = 1024
BATCH = 16
SEQ = 256
DEC_BATCH = 2
DEC_SEQ = 2048
GRID_W = 64
EPS = 1e-6
N_MOD = 9
D_FF = 2816
FFN_RES = 0.5
D_RNN = 1280
RG_BLOCKS = 10
RG_BLOCK = 128
CONV_W = 4
RG_C = 8.0
GLA_HEADS = 4
GLA_QK = 512
GLA_V = 1024
GLA_DK = 128
GLA_DV = 256
GLA_RANK = 16
GLA_TAU = 16.0
LOG2_E = 1.4426950408889634

N_PROMPT_TOK = BATCH * SEQ
N_TOK = N_PROMPT_TOK + DEC_BATCH * DEC_SEQ
N_GROUPS = 1 + DEC_BATCH
GROUP_ROWS = DEC_SEQ
PROMPT_BLOCKS = N_PROMPT_TOK // GROUP_ROWS

VMEM_LIMIT_BYTES = 56 * 1024 * 1024

TM = 512
HALF_TILES = N_PROMPT_TOK // TM
TM_PROJ = 1024
ADA_SLAB = 256
MXU_DIM = 256
BF16_SUBLANES = 16
FF_CHUNKS = (0, 6 * MXU_DIM, D_FF)
RG_CT = 256
SCAN_L = 256
SCAN_V = GROUP_ROWS // SCAN_L
GLA_C = 128
GLA_PROMPT_SEQS = 4
GLA_INTRA_UNROLL = 4
GLA_INTER_UNROLL = 8
GLA_LEVELS = tuple(GLA_C >> (l + 1) for l in range(GLA_C.bit_length() - 1))


def _cparams(*sem):
    return pltpu.CompilerParams(dimension_semantics=sem, vmem_limit_bytes=VMEM_LIMIT_BYTES)


def _group_of_tile(i, rows):
    return jnp.maximum((i * rows) // GROUP_ROWS - (PROMPT_BLOCKS - 1), 0)


def _sigmoid(x):
    return 1.0 / (1.0 + jnp.exp2(x * (-LOG2_E)))


def _softplus(x):
    return jnp.maximum(x, 0.0) + jnp.log1p(jnp.exp(-jnp.abs(x)))


def _rms(x):
    return x * lax.rsqrt(jnp.mean(x * x, axis=-1, keepdims=True) + EPS)


def _modulated(x, mod_ref, gain_ref, j):
    y = _rms(x) * gain_ref[j:j + 1, :]
    return y * (1.0 + mod_ref[3 * j + 1:3 * j + 2, :]) + mod_ref[3 * j:3 * j + 1, :]


def _ada_kernel(c_ref, w_ref, b_ref, o_ref):
    @pl.when(pl.program_id(1) == 0)
    def _():
        o_ref[...] = jnp.broadcast_to(b_ref[...], o_ref.shape)

    c = c_ref[...]
    s = (c * _sigmoid(c)).astype(BF16)
    o_ref[...] += jnp.dot(s, w_ref[...].astype(BF16), preferred_element_type=F32)


def _ada_mods(cvecs, ada_w, ada_b):
    depth = ada_w.shape[0]
    n_slab = D_MODEL // ADA_SLAB
    c_slabs = cvecs.reshape(8, n_slab, ADA_SLAB).swapaxes(0, 1)
    out = pl.pallas_call(
        _ada_kernel,
        grid=(depth, n_slab),
        in_specs=[
            pl.BlockSpec((None, 8, ADA_SLAB), lambda l, k: (k, 0, 0)),
            pl.BlockSpec((None, ADA_SLAB, N_MOD * D_MODEL), lambda l, k: (l, k, 0)),
            pl.BlockSpec((None, 1, N_MOD * D_MODEL), lambda l, k: (l, 0, 0)),
        ],
        out_specs=pl.BlockSpec((None, 8, N_MOD * D_MODEL), lambda l, k: (l, 0, 0)),
        out_shape=jax.ShapeDtypeStruct((depth, 8, N_MOD * D_MODEL), F32),
        compiler_params=_cparams("arbitrary", "arbitrary"),
        name="ada_mod",
    )(c_slabs, ada_w, ada_b.reshape(depth, 1, N_MOD * D_MODEL))
    return out[:, :N_GROUPS, :].reshape(depth, N_GROUPS, N_MOD, D_MODEL)


def _pick_half(refs):
    if len(refs) == 1:
        return refs[0][...]
    return jnp.where(pl.program_id(0) < HALF_TILES, refs[0][...], refs[1][...])


def _ffn_kernel(*refs, j, final, n_x, n_y, split_out, n_cast):
    x_refs, y_refs, refs = refs[:n_x], refs[n_x:n_x + n_y], refs[n_x + n_y:]
    if n_y:
        wmix_ref, refs = refs[0], refs[1:]
    mod_ref, gain_ref, win_ref, wout_ref, fn_ref = refs[:5]
    refs = refs[5:]
    if n_cast:
        f32_refs, refs = refs[:n_cast], refs[n_cast:]
        for src_ref, dst_ref in zip(f32_refs, refs[len(refs) - n_cast:]):
            dst_ref[...] = src_ref[...].astype(BF16)
    o_refs = refs
    x = _pick_half(x_refs)
    if n_y:
        y = _pick_half(y_refs).astype(BF16)
        x = x + mod_ref[5:6, :] * jnp.dot(y, wmix_ref[...], preferred_element_type=F32)
    hb = _modulated(x, mod_ref, gain_ref, j).astype(BF16)
    acc = jnp.zeros((TM, D_MODEL), F32)
    for lo, hi in zip(FF_CHUNKS[:-1], FF_CHUNKS[1:]):
        g = jnp.dot(hb, win_ref[:, lo:hi], preferred_element_type=F32)
        u = jnp.dot(hb, win_ref[:, D_FF + lo:D_FF + hi], preferred_element_type=F32)
        a = (g * _sigmoid(g) * u).astype(BF16)
        acc = acc + jnp.dot(a, wout_ref[lo:hi, :], preferred_element_type=F32)
    out = x + FFN_RES * mod_ref[3 * j + 2:3 * j + 3, :] * acc
    if final:
        out = _rms(out) * fn_ref[...]
    if split_out:
        first_half = pl.program_id(0) < HALF_TILES

        @pl.when(first_half)
        def _():
            o_refs[0][...] = out

        @pl.when(jnp.logical_not(first_half))
        def _():
            o_refs[1][...] = out
    else:
        o_refs[0][...] = out


def _row_specs(width, split):
    if not split:
        return [pl.BlockSpec((TM, width), lambda i: (i, 0))]
    return [pl.BlockSpec((TM, width), lambda i: (jnp.minimum(i, HALF_TILES - 1), 0)),
            pl.BlockSpec((TM, width), lambda i: (jnp.maximum(i - HALF_TILES, 0), 0))]


def _half_ffn(xs, mods_l, gains_l, w_in, w_out, half, final_norm, final, split_out, mix=None, casts=()):
    n_steps = N_TOK // TM
    ys, w_mix = mix if mix is not None else ((), None)
    in_specs = _row_specs(D_MODEL, len(xs) == 2)
    args = list(xs)
    if ys:
        in_specs += _row_specs(ys[0].shape[1], len(ys) == 2)
        in_specs.append(pl.BlockSpec(w_mix.shape, lambda i: (0, 0), pipeline_mode=pl.Buffered(1)))
        args += list(ys) + [w_mix]
    in_specs += [
        pl.BlockSpec((None, N_MOD, D_MODEL), lambda i: (_group_of_tile(i, TM), 0, 0)),
        pl.BlockSpec((3, D_MODEL), lambda i: (0, 0)),
        pl.BlockSpec((D_MODEL, 2 * D_FF), lambda i: (0, 0), pipeline_mode=pl.Buffered(1)),
        pl.BlockSpec((D_FF, D_MODEL), lambda i: (0, 0), pipeline_mode=pl.Buffered(1)),
        pl.BlockSpec((1, D_MODEL), lambda i: (0, 0)),
    ]
    args += [mods_l, gains_l, w_in, w_out, final_norm]
    n_rows = N_PROMPT_TOK if split_out else N_TOK
    out_specs = _row_specs(D_MODEL, split_out)
    out_shape = [jax.ShapeDtypeStruct((n_rows, D_MODEL), F32)] * (2 if split_out else 1)
    for arr, lead, cols in casts:
        rows = arr.shape[len(lead)]
        slab = rows // n_steps
        assert slab * n_steps == rows and slab % BF16_SUBLANES == 0, (rows, n_steps)
        in_specs.append(pl.BlockSpec((None,) * len(lead) + (slab, cols), lambda i, lead=lead: (*lead, i, 0)))
        args.append(arr)
        out_specs.append(pl.BlockSpec((slab, cols), lambda i: (i, 0)))
        out_shape.append(jax.ShapeDtypeStruct((rows, cols), BF16))
    return pl.pallas_call(
        functools.partial(_ffn_kernel, j=2 * half, final=final, n_x=len(xs), n_y=len(ys), split_out=split_out,
                          n_cast=len(casts)),
        grid=(n_steps,),
        in_specs=in_specs,
        out_specs=out_specs,
        out_shape=out_shape,
        compiler_params=_cparams("arbitrary"),
        name="half_ffn",
    )(*args)


def _gelu_tanh(x):
    return 0.5 * x * (1.0 + jnp.tanh(0.7978845608028654 * (x + 0.044715 * (x * x * x))))


def _rg_proj_kernel(x_ref, mod_ref, gain_ref, w_ref, gate_ref, xbr_ref):
    hb = _modulated(x_ref[...], mod_ref, gain_ref, 1).astype(BF16)
    gate_ref[...] = _gelu_tanh(jnp.dot(hb, w_ref[:, :D_RNN], preferred_element_type=F32))
    xbr_ref[...] = jnp.dot(hb, w_ref[:, D_RNN:], preferred_element_type=F32)


def _rg_proj(x, mods_l, gains_l, w_in):
    return pl.pallas_call(
        _rg_proj_kernel,
        grid=(N_TOK // TM_PROJ,),
        in_specs=[
            pl.BlockSpec((TM_PROJ, D_MODEL), lambda i: (i, 0)),
            pl.BlockSpec((None, N_MOD, D_MODEL), lambda i: (_group_of_tile(i, TM_PROJ), 0, 0)),
            pl.BlockSpec((3, D_MODEL), lambda i: (0, 0)),
            pl.BlockSpec((D_MODEL, 2 * D_RNN), lambda i: (0, 0), pipeline_mode=pl.Buffered(1)),
        ],
        out_specs=[pl.BlockSpec((TM_PROJ, D_RNN), lambda i: (i, 0))] * 2,
        out_shape=[jax.ShapeDtypeStruct((N_TOK, D_RNN), F32)] * 2,
        compiler_params=_cparams("arbitrary"),
        name="rg_in_proj",
    )(x, mods_l, gains_l, w_in)


def _seg_shift(row, down):
    v = lax.broadcasted_iota(jnp.int32, row.shape, 0)
    if down:
        return jnp.where(v >= 1, pltpu.roll(row, 1, 0), 0.0)
    return jnp.where(v < SCAN_V - 1, pltpu.roll(row, SCAN_V - 1, 0), 0.0)


def _rg_scan_kernel(xbr_ref, gate_ref, cw_ref, cb_ref, wa_ref, ba_ref, wi_ref, bi_ref, lam_ref, h0_ref,
                    y_ref, stf_ref, stb_ref, af_ref, uf_ref, ab_ref, ub_ref):
    chain = jnp.where(pl.program_id(0) >= PROMPT_BLOCKS, 1.0, 0.0).astype(F32)
    nbt = RG_CT // RG_BLOCK
    rows = SCAN_L * SCAN_V
    v = SCAN_V

    for nb in range(nbt):
        ls = slice(nb * RG_BLOCK, (nb + 1) * RG_BLOCK)
        x = xbr_ref[:, ls]
        before = chain * _seg_shift(x[rows - v:], True)
        after0 = chain * _seg_shift(x[:v], False)
        after1 = chain * _seg_shift(x[v:2 * v], False)
        xm1 = jnp.concatenate([before, x[:rows - v]], axis=0)
        xp1 = jnp.concatenate([x[v:], after0], axis=0)
        xp2 = jnp.concatenate([x[2 * v:], after0, after1], axis=0)
        xc = (cw_ref[0:1, ls] * xm1 + cw_ref[1:2, ls] * x + cw_ref[2:3, ls] * xp1 + cw_ref[3:4, ls] * xp2
              + cb_ref[0:1, ls])
        xcb = xc.astype(BF16)
        half_xc = 0.5 * xc
        for d, (a_ref, u_ref) in enumerate(((af_ref, uf_ref), (ab_ref, ub_ref))):
            tr = jnp.tanh(0.5 * (jnp.dot(xcb, wa_ref[d, nb], preferred_element_type=F32) + ba_ref[d:d + 1, ls]))
            ti = jnp.tanh(0.5 * (jnp.dot(xcb, wi_ref[d, nb], preferred_element_type=F32) + bi_ref[d:d + 1, ls]))
            half_k = (-0.5 * RG_C * LOG2_E) * _softplus(-lam_ref[d:d + 1, ls])
            a = jnp.exp2(tr * half_k + half_k)
            a_ref[nb] = a
            z = 1.0 - a * a
            u_ref[nb] = jnp.where(z > 0.0, z * lax.rsqrt(z), 0.0) * (ti * half_xc + half_xc)

    def step(i, carry):
        sf = pl.ds(pl.multiple_of(i * v, v), v)
        sb = pl.ds(pl.multiple_of((SCAN_L - 1 - i) * v, v), v)
        out = []
        for nb in range(nbt):
            hf, pf, hb, pb = carry[4 * nb:4 * nb + 4]
            a = af_ref[nb, sf, :]
            hf = a * hf + uf_ref[nb, sf, :]
            pf = a * pf
            uf_ref[nb, sf, :] = hf
            af_ref[nb, sf, :] = pf
            a = ab_ref[nb, sb, :]
            hb = a * hb + ub_ref[nb, sb, :]
            pb = a * pb
            ub_ref[nb, sb, :] = hb
            ab_ref[nb, sb, :] = pb
            out += [hf, pf, hb, pb]
        return tuple(out)

    zero = jnp.zeros((v, RG_BLOCK), F32)
    one = jnp.ones((v, RG_BLOCK), F32)
    lax.fori_loop(0, SCAN_L, step, (zero, one, zero, one) * nbt, unroll=4)

    for nb in range(nbt):
        ls = slice(nb * RG_BLOCK, (nb + 1) * RG_BLOCK)
        h_end, p_end = uf_ref[nb, rows - v:, :], af_ref[nb, rows - v:, :]
        entry = [chain * h0_ref[0:1, ls]]
        for s in range(v - 1):
            entry.append(chain * (h_end[s:s + 1] + p_end[s:s + 1] * entry[s]))
        ef = jnp.concatenate(entry, axis=0)
        h_start, p_start = ub_ref[nb, :v, :], ab_ref[nb, :v, :]
        entry = [chain * h0_ref[1:2, ls]]
        for s in range(v - 1, 0, -1):
            entry.append(chain * (h_start[s:s + 1] + p_start[s:s + 1] * entry[-1]))
        eb = jnp.concatenate(entry[::-1], axis=0)
        shape3 = (SCAN_L, v, RG_BLOCK)
        hf = uf_ref[nb].reshape(shape3) + af_ref[nb].reshape(shape3) * ef[None]
        hb = ub_ref[nb].reshape(shape3) + ab_ref[nb].reshape(shape3) * eb[None]
        stf_ref[:, ls] = hf[SCAN_L - 1]
        stb_ref[:, ls] = hb[0]
        y_ref[:, ls] = (hf + hb).reshape(rows, RG_BLOCK) * gate_ref[:, ls]


def _rg_scan(xbr, gate, conv_w, conv_b, w_a, b_a, w_i, b_i, lam, h0):
    n_blk = N_TOK // GROUP_ROWS
    nbt = RG_CT // RG_BLOCK
    tile = pl.BlockSpec((GROUP_ROWS, RG_CT), lambda b, c: (b, c))
    vec2 = pl.BlockSpec((2, RG_CT), lambda b, c: (0, c))
    wspec = pl.BlockSpec((2, nbt, RG_BLOCK, RG_BLOCK), lambda b, c: (0, c, 0, 0))
    return pl.pallas_call(
        _rg_scan_kernel,
        grid=(n_blk, D_RNN // RG_CT),
        in_specs=[
            tile, tile,
            pl.BlockSpec((CONV_W, RG_CT), lambda b, c: (0, c)),
            pl.BlockSpec((1, RG_CT), lambda b, c: (0, c)),
            wspec, vec2, wspec, vec2, vec2,
            pl.BlockSpec((None, 2, RG_CT), lambda b, c: (jnp.maximum(b - PROMPT_BLOCKS, 0), 0, c)),
        ],
        out_specs=[tile] + [pl.BlockSpec((SCAN_V, RG_CT), lambda b, c: (b, c))] * 2,
        out_shape=[jax.ShapeDtypeStruct((N_TOK, D_RNN), F32)]
        + [jax.ShapeDtypeStruct((n_blk * SCAN_V, D_RNN), F32)] * 2,
        scratch_shapes=[pltpu.VMEM((nbt, GROUP_ROWS, RG_BLOCK), F32)] * 4,
        compiler_params=_cparams("arbitrary", "arbitrary"),
        name="rg_scan",
    )(xbr, gate, conv_w, conv_b, w_a, b_a, w_i, b_i, lam, h0)


def _chunk_tri():
    ri = lax.broadcasted_iota(jnp.int32, (GLA_C, GLA_C), 0)
    ci = lax.broadcasted_iota(jnp.int32, (GLA_C, GLA_C), 1)
    return (ri >= ci).astype(BF16), (ri <= ci).astype(BF16)


def _split3(x):
    hi = x.astype(BF16)
    r1 = x - hi.astype(F32)
    mid = r1.astype(BF16)
    lo = (r1 - mid.astype(F32)).astype(BF16)
    return hi, mid, lo


def _gla_proj_kernel(x_ref, mod_ref, gain_ref, w_ref, wz_ref, w2_ref, bg_ref,
                     q_ref, k_ref, v_ref, r_ref, cum_ref):
    hb = _modulated(x_ref[...], mod_ref, gain_ref, 1).astype(BF16)
    z = jnp.dot(hb, wz_ref[...].astype(BF16), preferred_element_type=F32).astype(BF16)
    zz = jnp.dot(z, w2_ref[...], preferred_element_type=F32) + bg_ref[...]
    log_gate = -_softplus(-zz) * (LOG2_E / GLA_TAU)
    parts = _split3(log_gate)
    r = jnp.dot(hb, w_ref[:, 2 * GLA_QK + GLA_V:], preferred_element_type=F32)
    r_ref[...] = r * _sigmoid(r)
    q_ref[...] = jnp.dot(hb, w_ref[:, :GLA_QK], preferred_element_type=F32) * (GLA_DK ** -0.5)
    k_ref[...] = jnp.dot(hb, w_ref[:, GLA_QK:2 * GLA_QK], preferred_element_type=F32)
    v_ref[...] = jnp.dot(hb, w_ref[:, 2 * GLA_QK:2 * GLA_QK + GLA_V], preferred_element_type=F32).astype(BF16)
    tri_f, tri_b = _chunk_tri()
    for ch in range(TM_PROJ // GLA_C):
        rows = slice(ch * GLA_C, (ch + 1) * GLA_C)
        cum_ref[rows, :GLA_QK] = sum(jnp.dot(tri_f, p[rows, :GLA_QK], preferred_element_type=F32) for p in parts)
        cum_ref[rows, GLA_QK:] = sum(jnp.dot(tri_b, p[rows, GLA_QK:], preferred_element_type=F32) for p in parts)


def _gla_proj(x, mods_l, gains_l, w_main, w_z, w2cat, bg):
    def rows(w):
        return pl.BlockSpec((TM_PROJ, w), lambda i: (i, 0))

    def whole(a):
        return pl.BlockSpec(a.shape, lambda i: (0,) * a.ndim, pipeline_mode=pl.Buffered(1))

    widths = (GLA_QK, GLA_QK, GLA_V, GLA_V, 2 * GLA_QK)
    dtypes = (F32, F32, BF16, F32, F32)
    return pl.pallas_call(
        _gla_proj_kernel,
        grid=(N_TOK // TM_PROJ,),
        in_specs=[
            rows(D_MODEL),
            pl.BlockSpec((None, N_MOD, D_MODEL), lambda i: (_group_of_tile(i, TM_PROJ), 0, 0)),
            pl.BlockSpec((3, D_MODEL), lambda i: (0, 0)),
            whole(w_main), whole(w_z), whole(w2cat), whole(bg),
        ],
        out_specs=[rows(w) for w in widths],
        out_shape=[jax.ShapeDtypeStruct((N_TOK, w), t) for w, t in zip(widths, dtypes)],
        compiler_params=_cparams("arbitrary"),
        name="gla_in_proj",
    )(x, mods_l, gains_l, w_main, w_z, w2cat, bg)


def _dot_tn(a, b):
    return lax.dot_general(a, b, (((0,), (0,)), ((), ())), preferred_element_type=F32)


def _dot_nt(a, b):
    return lax.dot_general(a, b, (((1,), (1,)), ((), ())), preferred_element_type=F32)


def _level_ref(cum_ref, row0, cum, m, backward):
    c = GLA_C
    pick = m - 1 if backward else m

    def bcast(r, n):
        return jnp.broadcast_to(cum_ref[pl.ds(row0 + r, 1), :], (n, GLA_DK))

    if m >= 4:
        return jnp.concatenate([bcast(g * 2 * m + pick, 2 * m) for g in range(c // (2 * m))], axis=0)
    pos = lax.broadcasted_iota(jnp.int32, (c, GLA_DK), 0)
    if m == 2:
        lo = jnp.concatenate([bcast(g * 8 + pick, 8) for g in range(c // 8)], axis=0)
        hi = jnp.concatenate([bcast(g * 8 + 4 + pick, 8) for g in range(c // 8)], axis=0)
        return jnp.where((pos & 4) == 0, lo, hi)
    if backward:
        return jnp.where((pos & 1) == 1, pltpu.roll(cum, 1, 0), cum)
    return jnp.where((pos & 1) == 0, pltpu.roll(cum, c - 1, 0), cum)


def _gla_intra(q_ref, k_ref, v_ref, cf_ref, cb_ref, row0, diag, masks, uppers):
    rows = pl.ds(row0, GLA_C)
    cf, cb = cf_ref[rows, :], cb_ref[rows, :]
    qb, kb = q_ref[rows, :].astype(BF16), k_ref[rows, :].astype(BF16)
    scores = jnp.where(diag, 2.0 * _dot_nt(qb, kb), 0.0)
    for lvl, m in enumerate(GLA_LEVELS):
        df = cf - _level_ref(cf_ref, row0, cf, m, False)
        db = cb - _level_ref(cb_ref, row0, cb, m, True)
        gq = jnp.exp2(jnp.where(uppers[lvl], df, db)).astype(BF16)
        gk = jnp.exp2(-jnp.where(uppers[lvl], db, df)).astype(BF16)
        scores = jnp.where(masks[lvl], _dot_nt(qb * gq, kb * gk), scores)
    return jnp.dot(scores.astype(BF16), v_ref[rows, :], preferred_element_type=F32)


def _gla_inter(q_ref, k_ref, v_ref, cum_ref, row0, st, backward):
    rows = pl.ds(row0, GLA_C)
    cum = cum_ref[rows, :]
    tot = cum_ref[pl.ds(row0 + (0 if backward else GLA_C - 1), 1), :]
    o = _dot_nt((q_ref[rows, :] * jnp.exp2(cum)).astype(BF16), st.astype(BF16))
    kd = (k_ref[rows, :] * jnp.exp2(tot - cum)).astype(BF16)
    return o, st * jnp.exp2(tot) + _dot_tn(v_ref[rows, :], kd)


def _gla_kernel(*refs, t_len, seqs, has_s0):
    if has_s0:
        q_ref, k_ref, v_ref, cf_ref, cb_ref, r_ref, gn_ref, s0_ref, y_ref, sf_ref, sb_ref = refs
    else:
        q_ref, k_ref, v_ref, cf_ref, cb_ref, r_ref, gn_ref, y_ref, sout_ref, sf_ref, sb_ref = refs
    c = GLA_C
    n_chunks = t_len // c
    ri = lax.broadcasted_iota(jnp.int32, (c, c), 0)
    ci = lax.broadcasted_iota(jnp.int32, (c, c), 1)
    x = ri ^ ci
    masks = [(x >= m) & (x < 2 * m) for m in GLA_LEVELS]
    pos = lax.broadcasted_iota(jnp.int32, (c, GLA_DK), 0)
    uppers = [(pos & m) != 0 for m in GLA_LEVELS]

    def intra(i, carry):
        row0 = pl.multiple_of(i * c, c)
        y_ref[pl.ds(row0, c), :] = _gla_intra(q_ref, k_ref, v_ref, cf_ref, cb_ref, row0, ri == ci, masks, uppers)
        return carry

    lax.fori_loop(0, seqs * n_chunks, intra, 0, unroll=min(seqs * n_chunks, GLA_INTRA_UNROLL))

    for s in range(seqs):
        if has_s0:
            sf_ref[s] = s0_ref[s, 0].T
            sb_ref[s] = s0_ref[s, 1].T
        else:
            sf_ref[s] = jnp.zeros((GLA_DV, GLA_DK), F32)
            sb_ref[s] = jnp.zeros((GLA_DV, GLA_DK), F32)

        def inter(i, carry, s=s):
            rf = pl.multiple_of(s * t_len + i * c, c)
            rb = pl.multiple_of(s * t_len + (n_chunks - 1 - i) * c, c)
            o, st = _gla_inter(q_ref, k_ref, v_ref, cf_ref, rf, sf_ref[s], False)
            y_ref[pl.ds(rf, c), :] += o
            sf_ref[s] = st
            o, st = _gla_inter(q_ref, k_ref, v_ref, cb_ref, rb, sb_ref[s], True)
            y_ref[pl.ds(rb, c), :] += o
            sb_ref[s] = st
            return carry

        lax.fori_loop(0, n_chunks, inter, 0, unroll=min(n_chunks, GLA_INTER_UNROLL))
        if not has_s0:
            sout_ref[s, 0] = sf_ref[s].T
            sout_ref[s, 1] = sb_ref[s].T
    o = y_ref[...]
    y_ref[...] = _rms(o) * gn_ref[...] * r_ref[...]


def _gla_mix(q, k, v, cum, r, gnorm, s0, *, t_len, seqs, row0, n_seq):
    has_s0 = s0 is not None
    blk = seqs * t_len
    blk0 = row0 // blk
    assert blk0 * blk == row0 and n_seq % seqs == 0

    def rows(w, off=0):
        return pl.BlockSpec((blk, w), lambda s, h: (blk0 + s, off + h))

    state_spec = pl.BlockSpec((seqs, 2, None, GLA_DK, GLA_DV), lambda s, h: (s, 0, h, 0, 0))
    in_specs = [rows(GLA_DK), rows(GLA_DK), rows(GLA_DV), rows(GLA_DK), rows(GLA_DK, GLA_HEADS),
                rows(GLA_DV), pl.BlockSpec((None, 1, GLA_DV), lambda s, h: (h, 0, 0))]
    args = [q, k, v, cum, cum, r, gnorm.reshape(GLA_HEADS, 1, GLA_DV)]
    y_spec = pl.BlockSpec((blk, GLA_DV), lambda s, h: (s, h))
    y_shape = jax.ShapeDtypeStruct((n_seq * t_len, GLA_V), F32)
    if has_s0:
        in_specs.append(state_spec)
        args.append(s0)
        out_specs, out_shape = y_spec, y_shape
    else:
        out_specs = [y_spec, state_spec]
        out_shape = [y_shape, jax.ShapeDtypeStruct((n_seq, 2, GLA_HEADS, GLA_DK, GLA_DV), F32)]
    return pl.pallas_call(
        functools.partial(_gla_kernel, t_len=t_len, seqs=seqs, has_s0=has_s0),
        grid=(n_seq // seqs, GLA_HEADS),
        in_specs=in_specs,
        out_specs=out_specs,
        out_shape=out_shape,
        scratch_shapes=[pltpu.VMEM((seqs, GLA_DV, GLA_DK), F32)] * 2,
        compiler_params=_cparams("arbitrary", "arbitrary"),
        name="gla_mix_sample" if has_s0 else "gla_mix_prompt",
    )(*args)


ORDER_AXES = {"natural": "bvhw", "scan": "bhwv", "cols": "bwvh"}
AXIS_SIZE = {"b": N_PROMPT_TOK // GROUP_ROWS, "v": SCAN_V, "h": SCAN_L // GRID_W, "w": GRID_W}


def _reorder(x_half, src, dst):
    if src == dst:
        return x_half
    s_ax, d_ax = ORDER_AXES[src], ORDER_AXES[dst]
    x5 = x_half.reshape([AXIS_SIZE[a] for a in s_ax] + [D_MODEL])
    return x5.transpose([s_ax.index(a) for a in d_ax] + [len(s_ax)]).reshape(-1, D_MODEL)


def kernel(x_prompt, x_sample, c, state_rglru, state_gla, c_ctx, ada_w, ada_b, norm_g, ffn_w_in, ffn_w_out,
           rg_w_in, rg_conv_w, rg_conv_b, rg_w_a, rg_b_a, rg_w_i, rg_b_i, rg_lambda, rg_w_out,
           gla_w_in, gla_w_g2, gla_b_g, gla_norm, gla_w_out, final_norm):
    depth = ada_w.shape[0]
    cvecs = jnp.concatenate([c_ctx[None, :], c, jnp.zeros((8 - N_GROUPS, D_MODEL), F32)], axis=0)
    mods = _ada_mods(cvecs, ada_w, ada_b)
    fn = final_norm.reshape(1, D_MODEL)
    rg_states, gla_states = [], []

    w_in, w_out = ffn_w_in[0, 0].astype(BF16), ffn_w_out[0, 0].astype(BF16)
    n_main = 2 * GLA_QK + 2 * GLA_V

    def ffn_casts(layer, half):
        return ((ffn_w_in, (layer, half), 2 * D_FF), (ffn_w_out, (layer, half), D_MODEL))

    xp = x_prompt.reshape(N_PROMPT_TOK, D_MODEL)
    xs = x_sample.reshape(DEC_BATCH * DEC_SEQ, D_MODEL)
    order_p = order_s = "natural"
    for i in range(depth):
        mods_l, gains_l = mods[i], norm_g[i]
        j = i // 2
        last = i == depth - 1
        if i % 2 == 0:
            want_p, want_s = "scan", "scan"
            mixer_casts = ((rg_w_in, (j,), 2 * D_RNN), (rg_w_out, (j,), D_MODEL))
        else:
            want_p, want_s = "natural", "cols"
            mixer_casts = ((gla_w_in, (j,), n_main), (gla_w_out, (j,), D_MODEL))
        halves = (_reorder(xp, order_p, want_p), _reorder(xs, order_s, want_s))
        order_p, order_s = want_p, want_s
        x, w_in, w_out, wm_in, wm_out = _half_ffn(halves, mods_l, gains_l, w_in, w_out, 0, fn, False, False,
                                                  casts=ffn_casts(i, 1) + mixer_casts)
        if i % 2 == 0:
            gate, xbr = _rg_proj(x, mods_l, gains_l, wm_in)
            y, st_f, st_b = _rg_scan(xbr, gate, rg_conv_w[j], rg_conv_b[j].reshape(1, D_RNN),
                                     rg_w_a[j].astype(BF16), rg_b_a[j], rg_w_i[j].astype(BF16), rg_b_i[j],
                                     rg_lambda[j], state_rglru[:, j])
            rg_states.append(jnp.stack([st_f[:BATCH], st_b[:BATCH]], axis=1))
            ys = (y,)
        else:
            w_z = jnp.pad(gla_w_in[j, :, n_main:], ((0, 0), (0, 128 - 2 * GLA_RANK)))
            w2cat = jnp.zeros((128, 2 * GLA_QK), F32)
            w2cat = w2cat.at[:GLA_RANK, :GLA_QK].set(gla_w_g2[j, 0])
            w2cat = w2cat.at[GLA_RANK:2 * GLA_RANK, GLA_QK:].set(gla_w_g2[j, 1]).astype(BF16)
            q, k, v, r, cum = _gla_proj(x, mods_l, gains_l, wm_in, w_z, w2cat, gla_b_g[j].reshape(1, 2 * GLA_QK))
            y_p, st = _gla_mix(q, k, v, cum, r, gla_norm[j], None, t_len=SEQ, seqs=GLA_PROMPT_SEQS, row0=0,
                               n_seq=BATCH)
            y_s = _gla_mix(q, k, v, cum, r, gla_norm[j], state_gla[:, j], t_len=DEC_SEQ, seqs=1,
                           row0=N_PROMPT_TOK, n_seq=DEC_BATCH)
            gla_states.append(st)
            ys = (y_p, y_s)
        xp, xs, *next_pair = _half_ffn((x,), mods_l, gains_l, w_in, w_out, 1, fn, last, True, mix=(ys, wm_out),
                                       casts=() if last else ffn_casts(i + 1, 0))
        if not last:
            w_in, w_out = next_pair

    y_prompt = _reorder(xp, order_p, "natural").reshape(BATCH, SEQ, D_MODEL)
    y_sample = _reorder(xs, order_s, "natural").reshape(DEC_BATCH, DEC_SEQ, D_MODEL)
    return (y_prompt, y_sample, jnp.stack(rg_states, axis=1), jnp.stack(gla_states, axis=1))
```

```python
import functools

import jax
import jax.numpy as jnp
from jax import lax
from jax.experimental import pallas as pl
from jax.experimental.pallas import tpu as pltpu

F32 = jnp.float32
BF16 = jnp.bfloat16

D_MODEL = 1024
BATCH = 16
SEQ = 256
DEC_BATCH = 2
DEC_SEQ = 2048
GRID_W = 64
EPS = 1e-6
N_MOD = 9
D_FF = 2816
FFN_RES = 0.5
D_RNN = 1280
RG_BLOCKS = 10
RG_BLOCK = 128
CONV_W = 4
RG_C = 8.0
GLA_HEADS = 4
GLA_QK = 512
GLA_V = 1024
GLA_DK = 128
GLA_DV = 256
GLA_RANK = 16
GLA_TAU = 16.0
LOG2_E = 1.4426950408889634

N_PROMPT_TOK = BATCH * SEQ
N_TOK = N_PROMPT_TOK + DEC_BATCH * DEC_SEQ
N_GROUPS = 1 + DEC_BATCH
GROUP_ROWS = DEC_SEQ
PROMPT_BLOCKS = N_PROMPT_TOK // GROUP_ROWS

VMEM_LIMIT_BYTES = 56 * 1024 * 1024

TM = 512
HALF_TILES = N_PROMPT_TOK // TM
TM_PROJ = 1024
ADA_SLAB = 256
MXU_DIM = 256
BF16_SUBLANES = 16
FF_CHUNKS = (0, 6 * MXU_DIM, D_FF)
RG_CT = 256
SCAN_L = 256
SCAN_V = GROUP_ROWS // SCAN_L
GLA_C = 128
GLA_PROMPT_SEQS = 4
GLA_INTRA_UNROLL = 4
GLA_INTER_UNROLL = 8
GLA_LEVELS = tuple(GLA_C >> (l + 1) for l in range(GLA_C.bit_length() - 1))


def _cparams(*sem):
    return pltpu.CompilerParams(dimension_semantics=sem, vmem_limit_bytes=VMEM_LIMIT_BYTES)


def _group_of_tile(i, rows):
    return jnp.maximum((i * rows) // GROUP_ROWS - (PROMPT_BLOCKS - 1), 0)


def _sigmoid(x):
    return 1.0 / (1.0 + jnp.exp2(x * (-LOG2_E)))


def _softplus(x):
    return jnp.maximum(x, 0.0) + jnp.log1p(jnp.exp(-jnp.abs(x)))


def _rms(x):
    return x * lax.rsqrt(jnp.mean(x * x, axis=-1, keepdims=True) + EPS)


def _modulated(x, mod_ref, gain_ref, j):
    y = _rms(x) * gain_ref[j:j + 1, :]
    return y * (1.0 + mod_ref[3 * j + 1:3 * j + 2, :]) + mod_ref[3 * j:3 * j + 1, :]


def _ada_kernel(c_ref, w_ref, b_ref, o_ref):
    @pl.when(pl.program_id(1) == 0)
    def _():
        o_ref[...] = jnp.broadcast_to(b_ref[...], o_ref.shape)

    c = c_ref[...]
    s = (c * _sigmoid(c)).astype(BF16)
    o_ref[...] += jnp.dot(s, w_ref[...].astype(BF16), preferred_element_type=F32)


def _ada_mods(cvecs, ada_w, ada_b):
    depth = ada_w.shape[0]
    n_slab = D_MODEL // ADA_SLAB
    c_slabs = cvecs.reshape(8, n_slab, ADA_SLAB).swapaxes(0, 1)
    out = pl.pallas_call(
        _ada_kernel,
        grid=(depth, n_slab),
        in_specs=[
            pl.BlockSpec((None, 8, ADA_SLAB), lambda l, k: (k, 0, 0)),
            pl.BlockSpec((None, ADA_SLAB, N_MOD * D_MODEL), lambda l, k: (l, k, 0)),
            pl.BlockSpec((None, 1, N_MOD * D_MODEL), lambda l, k: (l, 0, 0)),
        ],
        out_specs=pl.BlockSpec((None, 8, N_MOD * D_MODEL), lambda l, k: (l, 0, 0)),
        out_shape=jax.ShapeDtypeStruct((depth, 8, N_MOD * D_MODEL), F32),
        compiler_params=_cparams("arbitrary", "arbitrary"),
        name="ada_mod",
    )(c_slabs, ada_w, ada_b.reshape(depth, 1, N_MOD * D_MODEL))
    return out[:, :N_GROUPS, :].reshape(depth, N_GROUPS, N_MOD, D_MODEL)


def _pick_half(refs):
    if len(refs) == 1:
        return refs[0][...]
    return jnp.where(pl.program_id(0) < HALF_TILES, refs[0][...], refs[1][...])


def _ffn_kernel(*refs, j, final, n_x, n_y, split_out, n_cast):
    x_refs, y_refs, refs = refs[:n_x], refs[n_x:n_x + n_y], refs[n_x + n_y:]
    if n_y:
        wmix_ref, refs = refs[0], refs[1:]
    mod_ref, gain_ref, win_ref, wout_ref, fn_ref = refs[:5]
    refs = refs[5:]
    if n_cast:
        f32_refs, refs = refs[:n_cast], refs[n_cast:]
        for src_ref, dst_ref in zip(f32_refs, refs[len(refs) - n_cast:]):
            dst_ref[...] = src_ref[...].astype(BF16)
    o_refs = refs
    x = _pick_half(x_refs)
    if n_y:
        y = _pick_half(y_refs).astype(BF16)
        x = x + mod_ref[5:6, :] * jnp.dot(y, wmix_ref[...], preferred_element_type=F32)
    hb = _modulated(x, mod_ref, gain_ref, j).astype(BF16)
    acc = jnp.zeros((TM, D_MODEL), F32)
    for lo, hi in zip(FF_CHUNKS[:-1], FF_CHUNKS[1:]):
        g = jnp.dot(hb, win_ref[:, lo:hi], preferred_element_type=F32)
        u = jnp.dot(hb, win_ref[:, D_FF + lo:D_FF + hi], preferred_element_type=F32)
        a = (g * _sigmoid(g) * u).astype(BF16)
        acc = acc + jnp.dot(a, wout_ref[lo:hi, :], preferred_element_type=F32)
    out = x + FFN_RES * mod_ref[3 * j + 2:3 * j + 3, :] * acc
    if final:
        out = _rms(out) * fn_ref[...]
    if split_out:
        first_half = pl.program_id(0) < HALF_TILES

        @pl.when(first_half)
        def _():
            o_refs[0][...] = out

        @pl.when(jnp.logical_not(first_half))
        def _():
            o_refs[1][...] = out
    else:
        o_refs[0][...] = out


def _row_specs(width, split):
    if not split:
        return [pl.BlockSpec((TM, width), lambda i: (i, 0))]
    return [pl.BlockSpec((TM, width), lambda i: (jnp.minimum(i, HALF_TILES - 1), 0)),
            pl.BlockSpec((TM, width), lambda i: (jnp.maximum(i - HALF_TILES, 0), 0))]


def _half_ffn(xs, mods_l, gains_l, w_in, w_out, half, final_norm, final, split_out, mix=None, casts=()):
    n_steps = N_TOK // TM
    ys, w_mix = mix if mix is not None else ((), None)
    in_specs = _row_specs(D_MODEL, len(xs) == 2)
    args = list(xs)
    if ys:
        in_specs += _row_specs(ys[0].shape[1], len(ys) == 2)
        in_specs.append(pl.BlockSpec(w_mix.shape, lambda i: (0, 0), pipeline_mode=pl.Buffered(1)))
        args += list(ys) + [w_mix]
    in_specs += [
        pl.BlockSpec((None, N_MOD, D_MODEL), lambda i: (_group_of_tile(i, TM), 0, 0)),
        pl.BlockSpec((3, D_MODEL), lambda i: (0, 0)),
        pl.BlockSpec((D_MODEL, 2 * D_FF), lambda i: (0, 0), pipeline_mode=pl.Buffered(1)),
        pl.BlockSpec((D_FF, D_MODEL), lambda i: (0, 0), pipeline_mode=pl.Buffered(1)),
        pl.BlockSpec((1, D_MODEL), lambda i: (0, 0)),
    ]
    args += [mods_l, gains_l, w_in, w_out, final_norm]
    n_rows = N_PROMPT_TOK if split_out else N_TOK
    out_specs = _row_specs(D_MODEL, split_out)
    out_shape = [jax.ShapeDtypeStruct((n_rows, D_MODEL), F32)] * (2 if split_out else 1)
    for arr, lead, cols in casts:
        rows = arr.shape[len(lead)]
        slab = rows // n_steps
        assert slab * n_steps == rows and slab % BF16_SUBLANES == 0, (rows, n_steps)
        in_specs.append(pl.BlockSpec((None,) * len(lead) + (slab, cols), lambda i, lead=lead: (*lead, i, 0)))
        args.append(arr)
        out_specs.append(pl.BlockSpec((slab, cols), lambda i: (i, 0)))
        out_shape.append(jax.ShapeDtypeStruct((rows, cols), BF16))
    return pl.pallas_call(
        functools.partial(_ffn_kernel, j=2 * half, final=final, n_x=len(xs), n_y=len(ys), split_out=split_out,
                          n_cast=len(casts)),
        grid=(n_steps,),
        in_specs=in_specs,
        out_specs=out_specs,
        out_shape=out_shape,
        compiler_params=_cparams("arbitrary"),
        name="half_ffn",
    )(*args)


def _gelu_tanh(x):
    return 0.5 * x * (1.0 + jnp.tanh(0.7978845608028654 * (x + 0.044715 * (x * x * x))))


def _rg_proj_kernel(x_ref, mod_ref, gain_ref, w_ref, gate_ref, xbr_ref):
    hb = _modulated(x_ref[...], mod_ref, gain_ref, 1).astype(BF16)
    gate_ref[...] = _gelu_tanh(jnp.dot(hb, w_ref[:, :D_RNN], preferred_element_type=F32))
    xbr_ref[...] = jnp.dot(hb, w_ref[:, D_RNN:], preferred_element_type=F32)


def _rg_proj(x, mods_l, gains_l, w_in):
    return pl.pallas_call(
        _rg_proj_kernel,
        grid=(N_TOK // TM_PROJ,),
        in_specs=[
            pl.BlockSpec((TM_PROJ, D_MODEL), lambda i: (i, 0)),
            pl.BlockSpec((None, N_MOD, D_MODEL), lambda i: (_group_of_tile(i, TM_PROJ), 0, 0)),
            pl.BlockSpec((3, D_MODEL), lambda i: (0, 0)),
            pl.BlockSpec((D_MODEL, 2 * D_RNN), lambda i: (0, 0), pipeline_mode=pl.Buffered(1)),
        ],
        out_specs=[pl.BlockSpec((TM_PROJ, D_RNN), lambda i: (i, 0))] * 2,
        out_shape=[jax.ShapeDtypeStruct((N_TOK, D_RNN), F32)] * 2,
        compiler_params=_cparams("arbitrary"),
        name="rg_in_proj",
    )(x, mods_l, gains_l, w_in)


def _seg_shift(row, down):
    v = lax.broadcasted_iota(jnp.int32, row.shape, 0)
    if down:
        return jnp.where(v >= 1, pltpu.roll(row, 1, 0), 0.0)
    return jnp.where(v < SCAN_V - 1, pltpu.roll(row, SCAN_V - 1, 0), 0.0)


def _rg_scan_kernel(xbr_ref, gate_ref, cw_ref, cb_ref, wa_ref, ba_ref, wi_ref, bi_ref, lam_ref, h0_ref,
                    y_ref, stf_ref, stb_ref, af_ref, uf_ref, ab_ref, ub_ref):
    chain = jnp.where(pl.program_id(0) >= PROMPT_BLOCKS, 1.0, 0.0).astype(F32)
    nbt = RG_CT // RG_BLOCK
    rows = SCAN_L * SCAN_V
    v = SCAN_V

    for nb in range(nbt):
        ls = slice(nb * RG_BLOCK, (nb + 1) * RG_BLOCK)
        x = xbr_ref[:, ls]
        before = chain * _seg_shift(x[rows - v:], True)
        after0 = chain * _seg_shift(x[:v], False)
        after1 = chain * _seg_shift(x[v:2 * v], False)
        xm1 = jnp.concatenate([before, x[:rows - v]], axis=0)
        xp1 = jnp.concatenate([x[v:], after0], axis=0)
        xp2 = jnp.concatenate([x[2 * v:], after0, after1], axis=0)
        xc = (cw_ref[0:1, ls] * xm1 + cw_ref[1:2, ls] * x + cw_ref[2:3, ls] * xp1 + cw_ref[3:4, ls] * xp2
              + cb_ref[0:1, ls])
        xcb = xc.astype(BF16)
        half_xc = 0.5 * xc
        for d, (a_ref, u_ref) in enumerate(((af_ref, uf_ref), (ab_ref, ub_ref))):
            tr = jnp.tanh(0.5 * (jnp.dot(xcb, wa_ref[d, nb], preferred_element_type=F32) + ba_ref[d:d + 1, ls]))
            ti = jnp.tanh(0.5 * (jnp.dot(xcb, wi_ref[d, nb], preferred_element_type=F32) + bi_ref[d:d + 1, ls]))
            half_k = (-0.5 * RG_C * LOG2_E) * _softplus(-lam_ref[d:d + 1, ls])
            a = jnp.exp2(tr * half_k + half_k)
            a_ref[nb] = a
            z = 1.0 - a * a
            u_ref[nb] = jnp.where(z > 0.0, z * lax.rsqrt(z), 0.0) * (ti * half_xc + half_xc)

    def step(i, carry):
        sf = pl.ds(pl.multiple_of(i * v, v), v)
        sb = pl.ds(pl.multiple_of((SCAN_L - 1 - i) * v, v), v)
        out = []
        for nb in range(nbt):
            hf, pf, hb, pb = carry[4 * nb:4 * nb + 4]
            a = af_ref[nb, sf, :]
            hf = a * hf + uf_ref[nb, sf, :]
            pf = a * pf
            uf_ref[nb, sf, :] = hf
            af_ref[nb, sf, :] = pf
            a = ab_ref[nb, sb, :]
            hb = a * hb + ub_ref[nb, sb, :]
            pb = a * pb
            ub_ref[nb, sb, :] = hb
            ab_ref[nb, sb, :] = pb
            out += [hf, pf, hb, pb]
        return tuple(out)

    zero = jnp.zeros((v, RG_BLOCK), F32)
    one = jnp.ones((v, RG_BLOCK), F32)
    lax.fori_loop(0, SCAN_L, step, (zero, one, zero, one) * nbt, unroll=4)

    for nb in range(nbt):
        ls = slice(nb * RG_BLOCK, (nb + 1) * RG_BLOCK)
        h_end, p_end = uf_ref[nb, rows - v:, :], af_ref[nb, rows - v:, :]
        entry = [chain * h0_ref[0:1, ls]]
        for s in range(v - 1):
            entry.append(chain * (h_end[s:s + 1] + p_end[s:s + 1] * entry[s]))
        ef = jnp.concatenate(entry, axis=0)
        h_start, p_start = ub_ref[nb, :v, :], ab_ref[nb, :v, :]
        entry = [chain * h0_ref[1:2, ls]]
        for s in range(v - 1, 0, -1):
            entry.append(chain * (h_start[s:s + 1] + p_start[s:s + 1] * entry[-1]))
        eb = jnp.concatenate(entry[::-1], axis=0)
        shape3 = (SCAN_L, v, RG_BLOCK)
        hf = uf_ref[nb].reshape(shape3) + af_ref[nb].reshape(shape3) * ef[None]
        hb = ub_ref[nb].reshape(shape3) + ab_ref[nb].reshape(shape3) * eb[None]
        stf_ref[:, ls] = hf[SCAN_L - 1]
        stb_ref[:, ls] = hb[0]
        y_ref[:, ls] = (hf + hb).reshape(rows, RG_BLOCK) * gate_ref[:, ls]


def _rg_scan(xbr, gate, conv_w, conv_b, w_a, b_a, w_i, b_i, lam, h0):
    n_blk = N_TOK // GROUP_ROWS
    nbt = RG_CT // RG_BLOCK
    tile = pl.BlockSpec((GROUP_ROWS, RG_CT), lambda b, c: (b, c))
    vec2 = pl.BlockSpec((2, RG_CT), lambda b, c: (0, c))
    wspec = pl.BlockSpec((2, nbt, RG_BLOCK, RG_BLOCK), lambda b, c: (0, c, 0, 0))
    return pl.pallas_call(
        _rg_scan_kernel,
        grid=(n_blk, D_RNN // RG_CT),
        in_specs=[
            tile, tile,
            pl.BlockSpec((CONV_W, RG_CT), lambda b, c: (0, c)),
            pl.BlockSpec((1, RG_CT), lambda b, c: (0, c)),
            wspec, vec2, wspec, vec2, vec2,
            pl.BlockSpec((None, 2, RG_CT), lambda b, c: (jnp.maximum(b - PROMPT_BLOCKS, 0), 0, c)),
        ],
        out_specs=[tile] + [pl.BlockSpec((SCAN_V, RG_CT), lambda b, c: (b, c))] * 2,
        out_shape=[jax.ShapeDtypeStruct((N_TOK, D_RNN), F32)]
        + [jax.ShapeDtypeStruct((n_blk * SCAN_V, D_RNN), F32)] * 2,
        scratch_shapes=[pltpu.VMEM((nbt, GROUP_ROWS, RG_BLOCK), F32)] * 4,
        compiler_params=_cparams("arbitrary", "arbitrary"),
        name="rg_scan",
    )(xbr, gate, conv_w, conv_b, w_a, b_a, w_i, b_i, lam, h0)


def _chunk_tri():
    ri = lax.broadcasted_iota(jnp.int32, (GLA_C, GLA_C), 0)
    ci = lax.broadcasted_iota(jnp.int32, (GLA_C, GLA_C), 1)
    return (ri >= ci).astype(BF16), (ri <= ci).astype(BF16)


def _split3(x):
    hi = x.astype(BF16)
    r1 = x - hi.astype(F32)
    mid = r1.astype(BF16)
    lo = (r1 - mid.astype(F32)).astype(BF16)
    return hi, mid, lo


def _gla_proj_kernel(x_ref, mod_ref, gain_ref, w_ref, wz_ref, w2_ref, bg_ref,
                     q_ref, k_ref, cf_ref, cb_ref, v_ref, r_ref):
    dk, dv = GLA_DK, GLA_DV
    hb = _modulated(x_ref[...], mod_ref, gain_ref, 1).astype(BF16)
    z = jnp.dot(hb, wz_ref[...].astype(BF16), preferred_element_type=F32).astype(BF16)
    zz = jnp.dot(z, w2_ref[...], preferred_element_type=F32) + bg_ref[...]
    log_gate = -_softplus(-zz) * (LOG2_E / GLA_TAU)
    parts = _split3(log_gate)
    r = jnp.dot(hb, w_ref[:, 2 * GLA_QK + GLA_V:], preferred_element_type=F32)
    r = r * _sigmoid(r)
    q = jnp.dot(hb, w_ref[:, :GLA_QK], preferred_element_type=F32) * (GLA_DK ** -0.5)
    k = jnp.dot(hb, w_ref[:, GLA_QK:2 * GLA_QK], preferred_element_type=F32)
    v = jnp.dot(hb, w_ref[:, 2 * GLA_QK:2 * GLA_QK + GLA_V], preferred_element_type=F32).astype(BF16)
    for h in range(GLA_HEADS):
        r_ref[h] = r[:, h * dv:(h + 1) * dv]
        v_ref[h] = v[:, h * dv:(h + 1) * dv]
        q_ref[h] = q[:, h * dk:(h + 1) * dk]
        k_ref[h] = k[:, h * dk:(h + 1) * dk]
    tri_f, tri_b = _chunk_tri()
    for ch in range(TM_PROJ // GLA_C):
        rows = slice(ch * GLA_C, (ch + 1) * GLA_C)
        cum_f = sum(jnp.dot(tri_f, p[rows, :GLA_QK], preferred_element_type=F32) for p in parts)
        cum_b = sum(jnp.dot(tri_b, p[rows, GLA_QK:], preferred_element_type=F32) for p in parts)
        for h in range(GLA_HEADS):
            cf_ref[h, rows, :] = cum_f[:, h * dk:(h + 1) * dk]
            cb_ref[h, rows, :] = cum_b[:, h * dk:(h + 1) * dk]


def _gla_proj(x, mods_l, gains_l, w_main, w_z, w2cat, bg):
    def whole(a):
        return pl.BlockSpec(a.shape, lambda i: (0,) * a.ndim, pipeline_mode=pl.Buffered(1))

    def heads(w, dtype):
        return (pl.BlockSpec((GLA_HEADS, TM_PROJ, w), lambda i: (0, i, 0)),
                jax.ShapeDtypeStruct((GLA_HEADS, N_TOK, w), dtype))

    outs = [heads(GLA_DK, F32)] * 4 + [heads(GLA_DV, BF16), heads(GLA_DV, F32)]
    return pl.pallas_call(
        _gla_proj_kernel,
        grid=(N_TOK // TM_PROJ,),
        in_specs=[
            pl.BlockSpec((TM_PROJ, D_MODEL), lambda i: (i, 0)),
            pl.BlockSpec((None, N_MOD, D_MODEL), lambda i: (_group_of_tile(i, TM_PROJ), 0, 0)),
            pl.BlockSpec((3, D_MODEL), lambda i: (0, 0)),
            whole(w_main), whole(w_z), whole(w2cat), whole(bg),
        ],
        out_specs=[spec for spec, _ in outs],
        out_shape=[shape for _, shape in outs],
        compiler_params=_cparams("arbitrary"),
        name="gla_in_proj",
    )(x, mods_l, gains_l, w_main, w_z, w2cat, bg)


def _dot_tn(a, b):
    return lax.dot_general(a, b, (((0,), (0,)), ((), ())), preferred_element_type=F32)


def _dot_nt(a, b):
    return lax.dot_general(a, b, (((1,), (1,)), ((), ())), preferred_element_type=F32)


def _level_ref(cum_ref, row0, cum, m, backward):
    c = GLA_C
    pick = m - 1 if backward else m

    def bcast(r, n):
        return jnp.broadcast_to(cum_ref[pl.ds(row0 + r, 1), :], (n, GLA_DK))

    if m >= 4:
        return jnp.concatenate([bcast(g * 2 * m + pick, 2 * m) for g in range(c // (2 * m))], axis=0)
    pos = lax.broadcasted_iota(jnp.int32, (c, GLA_DK), 0)
    if m == 2:
        lo = jnp.concatenate([bcast(g * 8 + pick, 8) for g in range(c // 8)], axis=0)
        hi = jnp.concatenate([bcast(g * 8 + 4 + pick, 8) for g in range(c // 8)], axis=0)
        return jnp.where((pos & 4) == 0, lo, hi)
    if backward:
        return jnp.where((pos & 1) == 1, pltpu.roll(cum, 1, 0), cum)
    return jnp.where((pos & 1) == 0, pltpu.roll(cum, c - 1, 0), cum)


def _gla_intra(q_ref, k_ref, v_ref, cf_ref, cb_ref, row0, diag, masks, uppers):
    rows = pl.ds(row0, GLA_C)
    cf, cb = cf_ref[rows, :], cb_ref[rows, :]
    qb, kb = q_ref[rows, :].astype(BF16), k_ref[rows, :].astype(BF16)
    scores = jnp.where(diag, 2.0 * _dot_nt(qb, kb), 0.0)
    for lvl, m in enumerate(GLA_LEVELS):
        df = cf - _level_ref(cf_ref, row0, cf, m, False)
        db = cb - _level_ref(cb_ref, row0, cb, m, True)
        gq = jnp.exp2(jnp.where(uppers[lvl], df, db)).astype(BF16)
        gk = jnp.exp2(-jnp.where(uppers[lvl], db, df)).astype(BF16)
        scores = jnp.where(masks[lvl], _dot_nt(qb * gq, kb * gk), scores)
    return jnp.dot(scores.astype(BF16), v_ref[rows, :], preferred_element_type=F32)


def _gla_inter(q_ref, k_ref, v_ref, cum_ref, row0, st, backward):
    rows = pl.ds(row0, GLA_C)
    cum = cum_ref[rows, :]
    tot = cum_ref[pl.ds(row0 + (0 if backward else GLA_C - 1), 1), :]
    o = _dot_nt((q_ref[rows, :] * jnp.exp2(cum)).astype(BF16), st.astype(BF16))
    kd = (k_ref[rows, :] * jnp.exp2(tot - cum)).astype(BF16)
    return o, st * jnp.exp2(tot) + _dot_tn(v_ref[rows, :], kd)


def _gla_kernel(*refs, t_len, seqs, has_s0):
    if has_s0:
        q_ref, k_ref, cf_ref, cb_ref, v_ref, r_ref, gn_ref, s0_ref, y_ref, sf_ref, sb_ref = refs
    else:
        q_ref, k_ref, cf_ref, cb_ref, v_ref, r_ref, gn_ref, y_ref, sout_ref, sf_ref, sb_ref = refs
    c = GLA_C
    n_chunks = t_len // c
    ri = lax.broadcasted_iota(jnp.int32, (c, c), 0)
    ci = lax.broadcasted_iota(jnp.int32, (c, c), 1)
    x = ri ^ ci
    masks = [(x >= m) & (x < 2 * m) for m in GLA_LEVELS]
    pos = lax.broadcasted_iota(jnp.int32, (c, GLA_DK), 0)
    uppers = [(pos & m) != 0 for m in GLA_LEVELS]

    def intra(i, carry):
        row0 = pl.multiple_of(i * c, c)
        y_ref[pl.ds(row0, c), :] = _gla_intra(q_ref, k_ref, v_ref, cf_ref, cb_ref, row0, ri == ci, masks, uppers)
        return carry

    lax.fori_loop(0, seqs * n_chunks, intra, 0, unroll=min(seqs * n_chunks, GLA_INTRA_UNROLL))

    for s in range(seqs):
        if has_s0:
            sf_ref[s] = s0_ref[s, 0].T
            sb_ref[s] = s0_ref[s, 1].T
        else:
            sf_ref[s] = jnp.zeros((GLA_DV, GLA_DK), F32)
            sb_ref[s] = jnp.zeros((GLA_DV, GLA_DK), F32)

        def inter(i, carry, s=s):
            rf = pl.multiple_of(s * t_len + i * c, c)
            rb = pl.multiple_of(s * t_len + (n_chunks - 1 - i) * c, c)
            o, st = _gla_inter(q_ref, k_ref, v_ref, cf_ref, rf, sf_ref[s], False)
            y_ref[pl.ds(rf, c), :] += o
            sf_ref[s] = st
            o, st = _gla_inter(q_ref, k_ref, v_ref, cb_ref, rb, sb_ref[s], True)
            y_ref[pl.ds(rb, c), :] += o
            sb_ref[s] = st
            return carry

        lax.fori_loop(0, n_chunks, inter, 0, unroll=min(n_chunks, GLA_INTER_UNROLL))
        if not has_s0:
            sout_ref[s, 0] = sf_ref[s].T
            sout_ref[s, 1] = sb_ref[s].T
    o = y_ref[...]
    y_ref[...] = _rms(o) * gn_ref[...] * r_ref[...]


def _gla_mix(qkcc, v, r, gnorm, s0, *, t_len, seqs, row0, n_seq):
    has_s0 = s0 is not None
    blk = seqs * t_len
    blk0 = row0 // blk
    assert blk0 * blk == row0 and n_seq % seqs == 0

    def head_rows(w):
        return pl.BlockSpec((None, blk, w), lambda s, h: (h, blk0 + s, 0))

    state_spec = pl.BlockSpec((seqs, 2, None, GLA_DK, GLA_DV), lambda s, h: (s, 0, h, 0, 0))
    in_specs = [head_rows(GLA_DK)] * 4 + [head_rows(GLA_DV), head_rows(GLA_DV),
                                          pl.BlockSpec((None, 1, GLA_DV), lambda s, h: (h, 0, 0))]
    args = [*qkcc, v, r, gnorm.reshape(GLA_HEADS, 1, GLA_DV)]
    y_spec = pl.BlockSpec((blk, GLA_DV), lambda s, h: (s, h))
    y_shape = jax.ShapeDtypeStruct((n_seq * t_len, GLA_V), F32)
    if has_s0:
        in_specs.append(state_spec)
        args.append(s0)
        out_specs, out_shape = y_spec, y_shape
    else:
        out_specs = [y_spec, state_spec]
        out_shape = [y_shape, jax.ShapeDtypeStruct((n_seq, 2, GLA_HEADS, GLA_DK, GLA_DV), F32)]
    return pl.pallas_call(
        functools.partial(_gla_kernel, t_len=t_len, seqs=seqs, has_s0=has_s0),
        grid=(n_seq // seqs, GLA_HEADS),
        in_specs=in_specs,
        out_specs=out_specs,
        out_shape=out_shape,
        scratch_shapes=[pltpu.VMEM((seqs, GLA_DV, GLA_DK), F32)] * 2,
        compiler_params=_cparams("arbitrary", "arbitrary"),
        name="gla_mix_sample" if has_s0 else "gla_mix_prompt",
    )(*args)


def _natural_to_scan(x_half):
    x4 = x_half.reshape(-1, SCAN_V, SCAN_L, D_MODEL)
    return x4.swapaxes(1, 2).reshape(-1, D_MODEL)


def _scan_to_natural(x_half):
    x4 = x_half.reshape(-1, SCAN_L, SCAN_V, D_MODEL)
    return x4.swapaxes(1, 2).reshape(-1, D_MODEL)


def _natural_to_col_major(x_half):
    x4 = x_half.reshape(DEC_BATCH, DEC_SEQ // GRID_W, GRID_W, D_MODEL)
    return x4.swapaxes(1, 2).reshape(-1, D_MODEL)


def _col_major_to_natural(x_half):
    x4 = x_half.reshape(DEC_BATCH, GRID_W, DEC_SEQ // GRID_W, D_MODEL)
    return x4.swapaxes(1, 2).reshape(-1, D_MODEL)


def kernel(x_prompt, x_sample, c, state_rglru, state_gla, c_ctx, ada_w, ada_b, norm_g, ffn_w_in, ffn_w_out,
           rg_w_in, rg_conv_w, rg_conv_b, rg_w_a, rg_b_a, rg_w_i, rg_b_i, rg_lambda, rg_w_out,
           gla_w_in, gla_w_g2, gla_b_g, gla_norm, gla_w_out, final_norm):
    depth = ada_w.shape[0]
    cvecs = jnp.concatenate([c_ctx[None, :], c, jnp.zeros((8 - N_GROUPS, D_MODEL), F32)], axis=0)
    mods = _ada_mods(cvecs, ada_w, ada_b)
    fn = final_norm.reshape(1, D_MODEL)
    rg_states, gla_states = [], []

    w_in, w_out = ffn_w_in[0, 0].astype(BF16), ffn_w_out[0, 0].astype(BF16)
    n_main = 2 * GLA_QK + 2 * GLA_V

    def ffn_casts(layer, half):
        return ((ffn_w_in, (layer, half), 2 * D_FF), (ffn_w_out, (layer, half), D_MODEL))

    xp = x_prompt.reshape(N_PROMPT_TOK, D_MODEL)
    xs = x_sample.reshape(DEC_BATCH * DEC_SEQ, D_MODEL)
    for i in range(depth):
        mods_l, gains_l = mods[i], norm_g[i]
        j = i // 2
        last = i == depth - 1
        if i % 2 == 0:
            halves = (_natural_to_scan(xp), _natural_to_scan(xs))
            mixer_casts = ((rg_w_in, (j,), 2 * D_RNN), (rg_w_out, (j,), D_MODEL))
        else:
            halves = (xp, _natural_to_col_major(xs))
            mixer_casts = ((gla_w_in, (j,), n_main), (gla_w_out, (j,), D_MODEL))
        x, w_in, w_out, wm_in, wm_out = _half_ffn(halves, mods_l, gains_l, w_in, w_out, 0, fn, False, False,
                                                  casts=ffn_casts(i, 1) + mixer_casts)
        if i % 2 == 0:
            gate, xbr = _rg_proj(x, mods_l, gains_l, wm_in)
            y, st_f, st_b = _rg_scan(xbr, gate, rg_conv_w[j], rg_conv_b[j].reshape(1, D_RNN),
                                     rg_w_a[j].astype(BF16), rg_b_a[j], rg_w_i[j].astype(BF16), rg_b_i[j],
                                     rg_lambda[j], state_rglru[:, j])
            rg_states.append(jnp.stack([st_f[:BATCH], st_b[:BATCH]], axis=1))
            ys = (y,)
        else:
            w_z = jnp.pad(gla_w_in[j, :, n_main:], ((0, 0), (0, 128 - 2 * GLA_RANK)))
            w2cat = jnp.zeros((128, 2 * GLA_QK), F32)
            w2cat = w2cat.at[:GLA_RANK, :GLA_QK].set(gla_w_g2[j, 0])
            w2cat = w2cat.at[GLA_RANK:2 * GLA_RANK, GLA_QK:].set(gla_w_g2[j, 1]).astype(BF16)
            *qkcc, v, r = _gla_proj(x, mods_l, gains_l, wm_in, w_z, w2cat, gla_b_g[j].reshape(1, 2 * GLA_QK))
            y_p, st = _gla_mix(qkcc, v, r, gla_norm[j], None, t_len=SEQ, seqs=GLA_PROMPT_SEQS, row0=0, n_seq=BATCH)
            y_s = _gla_mix(qkcc, v, r, gla_norm[j], state_gla[:, j], t_len=DEC_SEQ, seqs=1, row0=N_PROMPT_TOK,
                           n_seq=DEC_BATCH)
            gla_states.append(st)
            ys = (y_p, y_s)
        xp, xs, *next_pair = _half_ffn((x,), mods_l, gains_l, w_in, w_out, 1, fn, last, True, mix=(ys, wm_out),
                                       casts=() if last else ffn_casts(i + 1, 0))
        if not last:
            w_in, w_out = next_pair
        if i % 2 == 0:
            xp, xs = _scan_to_natural(xp), _scan_to_natural(xs)
        else:
            xs = _col_major_to_natural(xs)

    y_prompt = xp.reshape(BATCH, SEQ, D_MODEL)
    y_sample = xs.reshape(DEC_BATCH, DEC_SEQ, D_MODEL)
    return (y_prompt, y_sample, jnp.stack(rg_states, axis=1), jnp.stack(gla_states, axis=1))
```

```python
import functools

import jax
import jax.numpy as jnp
from jax import lax
from jax.experimental import pallas as pl
from jax.experimental.pallas import tpu as pltpu

F32 = jnp.float32
BF16 = jnp.bfloat16

D_MODEL = 1024
BATCH = 16
SEQ = 256
DEC_BATCH = 2
DEC_SEQ = 2048
GRID_W = 64
EPS = 1e-6
N_MOD = 9
D_FF = 2816
FFN_RES = 0.5
D_RNN = 1280
RG_BLOCKS = 10
RG_BLOCK = 128
CONV_W = 4
RG_C = 8.0
GLA_HEADS = 4
GLA_QK = 512
GLA_V = 1024
GLA_DK = 128
GLA_DV = 256
GLA_RANK = 16
GLA_TAU = 16.0
LOG2_E = 1.4426950408889634

N_PROMPT_TOK = BATCH * SEQ
N_TOK = N_PROMPT_TOK + DEC_BATCH * DEC_SEQ
N_GROUPS = 1 + DEC_BATCH
GROUP_ROWS = DEC_SEQ
PROMPT_BLOCKS = N_PROMPT_TOK // GROUP_ROWS

VMEM_LIMIT_BYTES = 56 * 1024 * 1024

TM = 512
HALF_TILES = N_PROMPT_TOK // TM
TM_PROJ = 1024
ADA_SLAB = 256
MXU_DIM = 256
BF16_SUBLANES = 16
FF_CHUNKS = (0, 6 * MXU_DIM, D_FF)
RG_CT = 256
SCAN_L = 256
SCAN_V = GROUP_ROWS // SCAN_L
GLA_C = 128
GLA_PROMPT_SEQS = 8
GLA_INTRA_UNROLL = 4
GLA_INTER_UNROLL = 8
GLA_LEVELS = tuple(GLA_C >> (l + 1) for l in range(GLA_C.bit_length() - 1))


def _cparams(*sem):
    return pltpu.CompilerParams(dimension_semantics=sem, vmem_limit_bytes=VMEM_LIMIT_BYTES)


def _group_of_tile(i, rows):
    return jnp.maximum((i * rows) // GROUP_ROWS - (PROMPT_BLOCKS - 1), 0)


def _sigmoid(x):
    return 1.0 / (1.0 + jnp.exp2(x * (-LOG2_E)))


def _softplus(x):
    return jnp.maximum(x, 0.0) + jnp.log1p(jnp.exp(-jnp.abs(x)))


def _rms(x):
    return x * lax.rsqrt(jnp.mean(x * x, axis=-1, keepdims=True) + EPS)


def _modulated(x, mod_ref, gain_ref, j):
    y = _rms(x) * gain_ref[j:j + 1, :]
    return y * (1.0 + mod_ref[3 * j + 1:3 * j + 2, :]) + mod_ref[3 * j:3 * j + 1, :]


def _ada_kernel(c_ref, w_ref, b_ref, o_ref):
    @pl.when(pl.program_id(1) == 0)
    def _():
        o_ref[...] = jnp.broadcast_to(b_ref[...], o_ref.shape)

    c = c_ref[...]
    s = (c * _sigmoid(c)).astype(BF16)
    o_ref[...] += jnp.dot(s, w_ref[...].astype(BF16), preferred_element_type=F32)


def _ada_mods(cvecs, ada_w, ada_b):
    depth = ada_w.shape[0]
    n_slab = D_MODEL // ADA_SLAB
    c_slabs = cvecs.reshape(8, n_slab, ADA_SLAB).swapaxes(0, 1)
    out = pl.pallas_call(
        _ada_kernel,
        grid=(depth, n_slab),
        in_specs=[
            pl.BlockSpec((None, 8, ADA_SLAB), lambda l, k: (k, 0, 0)),
            pl.BlockSpec((None, ADA_SLAB, N_MOD * D_MODEL), lambda l, k: (l, k, 0)),
            pl.BlockSpec((None, 1, N_MOD * D_MODEL), lambda l, k: (l, 0, 0)),
        ],
        out_specs=pl.BlockSpec((None, 8, N_MOD * D_MODEL), lambda l, k: (l, 0, 0)),
        out_shape=jax.ShapeDtypeStruct((depth, 8, N_MOD * D_MODEL), F32),
        compiler_params=_cparams("arbitrary", "arbitrary"),
        name="ada_mod",
    )(c_slabs, ada_w, ada_b.reshape(depth, 1, N_MOD * D_MODEL))
    return out[:, :N_GROUPS, :].reshape(depth, N_GROUPS, N_MOD, D_MODEL)


def _pick_half(refs):
    if len(refs) == 1:
        return refs[0][...]
    return jnp.where(pl.program_id(0) < HALF_TILES, refs[0][...], refs[1][...])


def _ffn_kernel(*refs, j, final, n_x, n_y, split_out, n_cast):
    x_refs, y_refs, refs = refs[:n_x], refs[n_x:n_x + n_y], refs[n_x + n_y:]
    if n_y:
        wmix_ref, refs = refs[0], refs[1:]
    mod_ref, gain_ref, win_ref, wout_ref, fn_ref = refs[:5]
    refs = refs[5:]
    if n_cast:
        f32_refs, refs = refs[:n_cast], refs[n_cast:]
        for src_ref, dst_ref in zip(f32_refs, refs[len(refs) - n_cast:]):
            dst_ref[...] = src_ref[...].astype(BF16)
    o_refs = refs
    x = _pick_half(x_refs)
    if n_y:
        y = _pick_half(y_refs).astype(BF16)
        x = x + mod_ref[5:6, :] * jnp.dot(y, wmix_ref[...], preferred_element_type=F32)
    hb = _modulated(x, mod_ref, gain_ref, j).astype(BF16)
    acc = jnp.zeros((TM, D_MODEL), F32)
    for lo, hi in zip(FF_CHUNKS[:-1], FF_CHUNKS[1:]):
        g = jnp.dot(hb, win_ref[:, lo:hi], preferred_element_type=F32)
        u = jnp.dot(hb, win_ref[:, D_FF + lo:D_FF + hi], preferred_element_type=F32)
        a = (g * _sigmoid(g) * u).astype(BF16)
        acc = acc + jnp.dot(a, wout_ref[lo:hi, :], preferred_element_type=F32)
    out = x + FFN_RES * mod_ref[3 * j + 2:3 * j + 3, :] * acc
    if final:
        out = _rms(out) * fn_ref[...]
    if split_out:
        first_half = pl.program_id(0) < HALF_TILES

        @pl.when(first_half)
        def _():
            o_refs[0][...] = out

        @pl.when(jnp.logical_not(first_half))
        def _():
            o_refs[1][...] = out
    else:
        o_refs[0][...] = out


def _row_specs(width, split):
    if not split:
        return [pl.BlockSpec((TM, width), lambda i: (i, 0))]
    return [pl.BlockSpec((TM, width), lambda i: (jnp.minimum(i, HALF_TILES - 1), 0)),
            pl.BlockSpec((TM, width), lambda i: (jnp.maximum(i - HALF_TILES, 0), 0))]


def _half_ffn(xs, mods_l, gains_l, w_in, w_out, half, final_norm, final, split_out, mix=None, casts=()):
    n_steps = N_TOK // TM
    ys, w_mix = mix if mix is not None else ((), None)
    in_specs = _row_specs(D_MODEL, len(xs) == 2)
    args = list(xs)
    if ys:
        in_specs += _row_specs(ys[0].shape[1], len(ys) == 2)
        in_specs.append(pl.BlockSpec(w_mix.shape, lambda i: (0, 0), pipeline_mode=pl.Buffered(1)))
        args += list(ys) + [w_mix]
    in_specs += [
        pl.BlockSpec((None, N_MOD, D_MODEL), lambda i: (_group_of_tile(i, TM), 0, 0)),
        pl.BlockSpec((3, D_MODEL), lambda i: (0, 0)),
        pl.BlockSpec((D_MODEL, 2 * D_FF), lambda i: (0, 0), pipeline_mode=pl.Buffered(1)),
        pl.BlockSpec((D_FF, D_MODEL), lambda i: (0, 0), pipeline_mode=pl.Buffered(1)),
        pl.BlockSpec((1, D_MODEL), lambda i: (0, 0)),
    ]
    args += [mods_l, gains_l, w_in, w_out, final_norm]
    n_rows = N_PROMPT_TOK if split_out else N_TOK
    out_specs = _row_specs(D_MODEL, split_out)
    out_shape = [jax.ShapeDtypeStruct((n_rows, D_MODEL), F32)] * (2 if split_out else 1)
    for arr, lead, cols in casts:
        rows = arr.shape[len(lead)]
        slab = rows // n_steps
        assert slab * n_steps == rows and slab % BF16_SUBLANES == 0, (rows, n_steps)
        in_specs.append(pl.BlockSpec((None,) * len(lead) + (slab, cols), lambda i, lead=lead: (*lead, i, 0)))
        args.append(arr)
        out_specs.append(pl.BlockSpec((slab, cols), lambda i: (i, 0)))
        out_shape.append(jax.ShapeDtypeStruct((rows, cols), BF16))
    return pl.pallas_call(
        functools.partial(_ffn_kernel, j=2 * half, final=final, n_x=len(xs), n_y=len(ys), split_out=split_out,
                          n_cast=len(casts)),
        grid=(n_steps,),
        in_specs=in_specs,
        out_specs=out_specs,
        out_shape=out_shape,
        compiler_params=_cparams("arbitrary"),
        name="half_ffn",
    )(*args)


def _gelu_tanh(x):
    return 0.5 * x * (1.0 + jnp.tanh(0.7978845608028654 * (x + 0.044715 * (x * x * x))))


def _rg_proj_kernel(x_ref, mod_ref, gain_ref, w_ref, gate_ref, xbr_ref):
    hb = _modulated(x_ref[...], mod_ref, gain_ref, 1).astype(BF16)
    gate_ref[...] = _gelu_tanh(jnp.dot(hb, w_ref[:, :D_RNN], preferred_element_type=F32))
    xbr_ref[...] = jnp.dot(hb, w_ref[:, D_RNN:], preferred_element_type=F32)


def _rg_proj(x, mods_l, gains_l, w_in):
    return pl.pallas_call(
        _rg_proj_kernel,
        grid=(N_TOK // TM_PROJ,),
        in_specs=[
            pl.BlockSpec((TM_PROJ, D_MODEL), lambda i: (i, 0)),
            pl.BlockSpec((None, N_MOD, D_MODEL), lambda i: (_group_of_tile(i, TM_PROJ), 0, 0)),
            pl.BlockSpec((3, D_MODEL), lambda i: (0, 0)),
            pl.BlockSpec((D_MODEL, 2 * D_RNN), lambda i: (0, 0), pipeline_mode=pl.Buffered(1)),
        ],
        out_specs=[pl.BlockSpec((TM_PROJ, D_RNN), lambda i: (i, 0))] * 2,
        out_shape=[jax.ShapeDtypeStruct((N_TOK, D_RNN), F32)] * 2,
        compiler_params=_cparams("arbitrary"),
        name="rg_in_proj",
    )(x, mods_l, gains_l, w_in)


def _seg_shift(row, down):
    v = lax.broadcasted_iota(jnp.int32, row.shape, 0)
    if down:
        return jnp.where(v >= 1, pltpu.roll(row, 1, 0), 0.0)
    return jnp.where(v < SCAN_V - 1, pltpu.roll(row, SCAN_V - 1, 0), 0.0)


def _rg_scan_kernel(xbr_ref, gate_ref, cw_ref, cb_ref, wa_ref, ba_ref, wi_ref, bi_ref, lam_ref, h0_ref,
                    y_ref, stf_ref, stb_ref, af_ref, uf_ref, ab_ref, ub_ref):
    chain = jnp.where(pl.program_id(0) >= PROMPT_BLOCKS, 1.0, 0.0).astype(F32)
    nbt = RG_CT // RG_BLOCK
    rows = SCAN_L * SCAN_V
    v = SCAN_V

    for nb in range(nbt):
        ls = slice(nb * RG_BLOCK, (nb + 1) * RG_BLOCK)
        x = xbr_ref[:, ls]
        before = chain * _seg_shift(x[rows - v:], True)
        after0 = chain * _seg_shift(x[:v], False)
        after1 = chain * _seg_shift(x[v:2 * v], False)
        xm1 = jnp.concatenate([before, x[:rows - v]], axis=0)
        xp1 = jnp.concatenate([x[v:], after0], axis=0)
        xp2 = jnp.concatenate([x[2 * v:], after0, after1], axis=0)
        xc = (cw_ref[0:1, ls] * xm1 + cw_ref[1:2, ls] * x + cw_ref[2:3, ls] * xp1 + cw_ref[3:4, ls] * xp2
              + cb_ref[0:1, ls])
        xcb = xc.astype(BF16)
        xc_r2 = xc * (2.0 ** 0.5)
        for d, (a_ref, u_ref) in enumerate(((af_ref, uf_ref), (ab_ref, ub_ref))):
            r = _sigmoid(jnp.dot(xcb, wa_ref[d, nb], preferred_element_type=F32) + ba_ref[d:d + 1, ls])
            gi = _sigmoid(jnp.dot(xcb, wi_ref[d, nb], preferred_element_type=F32) + bi_ref[d:d + 1, ls])
            decay = RG_C * _softplus(-lam_ref[d:d + 1, ls])
            a_ref[nb] = jnp.exp2(r * (decay * (-LOG2_E)))
            t = jnp.tanh(r * decay)
            prod = t * (1.0 + t)
            u_ref[nb] = jnp.where(prod > 0.0, t * lax.rsqrt(prod), 0.0) * (gi * xc_r2)

    def step(i, carry):
        sf = pl.ds(pl.multiple_of(i * v, v), v)
        sb = pl.ds(pl.multiple_of((SCAN_L - 1 - i) * v, v), v)
        out = []
        for nb in range(nbt):
            hf, pf, hb, pb = carry[4 * nb:4 * nb + 4]
            a = af_ref[nb, sf, :]
            hf = a * hf + uf_ref[nb, sf, :]
            pf = a * pf
            uf_ref[nb, sf, :] = hf
            af_ref[nb, sf, :] = pf
            a = ab_ref[nb, sb, :]
            hb = a * hb + ub_ref[nb, sb, :]
            pb = a * pb
            ub_ref[nb, sb, :] = hb
            ab_ref[nb, sb, :] = pb
            out += [hf, pf, hb, pb]
        return tuple(out)

    zero = jnp.zeros((v, RG_BLOCK), F32)
    one = jnp.ones((v, RG_BLOCK), F32)
    lax.fori_loop(0, SCAN_L, step, (zero, one, zero, one) * nbt, unroll=4)

    for nb in range(nbt):
        ls = slice(nb * RG_BLOCK, (nb + 1) * RG_BLOCK)
        h_end, p_end = uf_ref[nb, rows - v:, :], af_ref[nb, rows - v:, :]
        entry = [chain * h0_ref[0:1, ls]]
        for s in range(v - 1):
            entry.append(chain * (h_end[s:s + 1] + p_end[s:s + 1] * entry[s]))
        ef = jnp.concatenate(entry, axis=0)
        h_start, p_start = ub_ref[nb, :v, :], ab_ref[nb, :v, :]
        entry = [chain * h0_ref[1:2, ls]]
        for s in range(v - 1, 0, -1):
            entry.append(chain * (h_start[s:s + 1] + p_start[s:s + 1] * entry[-1]))
        eb = jnp.concatenate(entry[::-1], axis=0)
        shape3 = (SCAN_L, v, RG_BLOCK)
        hf = uf_ref[nb].reshape(shape3) + af_ref[nb].reshape(shape3) * ef[None]
        hb = ub_ref[nb].reshape(shape3) + ab_ref[nb].reshape(shape3) * eb[None]
        stf_ref[:, ls] = hf[SCAN_L - 1]
        stb_ref[:, ls] = hb[0]
        y_ref[:, ls] = (hf + hb).reshape(rows, RG_BLOCK) * gate_ref[:, ls]


def _rg_scan(xbr, gate, conv_w, conv_b, w_a, b_a, w_i, b_i, lam, h0):
    n_blk = N_TOK // GROUP_ROWS
    nbt = RG_CT // RG_BLOCK
    tile = pl.BlockSpec((GROUP_ROWS, RG_CT), lambda b, c: (b, c))
    vec2 = pl.BlockSpec((2, RG_CT), lambda b, c: (0, c))
    wspec = pl.BlockSpec((2, nbt, RG_BLOCK, RG_BLOCK), lambda b, c: (0, c, 0, 0))
    return pl.pallas_call(
        _rg_scan_kernel,
        grid=(n_blk, D_RNN // RG_CT),
        in_specs=[
            tile, tile,
            pl.BlockSpec((CONV_W, RG_CT), lambda b, c: (0, c)),
            pl.BlockSpec((1, RG_CT), lambda b, c: (0, c)),
            wspec, vec2, wspec, vec2, vec2,
            pl.BlockSpec((None, 2, RG_CT), lambda b, c: (jnp.maximum(b - PROMPT_BLOCKS, 0), 0, c)),
        ],
        out_specs=[tile] + [pl.BlockSpec((SCAN_V, RG_CT), lambda b, c: (b, c))] * 2,
        out_shape=[jax.ShapeDtypeStruct((N_TOK, D_RNN), F32)]
        + [jax.ShapeDtypeStruct((n_blk * SCAN_V, D_RNN), F32)] * 2,
        scratch_shapes=[pltpu.VMEM((nbt, GROUP_ROWS, RG_BLOCK), F32)] * 4,
        compiler_params=_cparams("arbitrary", "arbitrary"),
        name="rg_scan",
    )(xbr, gate, conv_w, conv_b, w_a, b_a, w_i, b_i, lam, h0)


def _chunk_tri():
    ri = lax.broadcasted_iota(jnp.int32, (GLA_C, GLA_C), 0)
    ci = lax.broadcasted_iota(jnp.int32, (GLA_C, GLA_C), 1)
    return (ri >= ci).astype(BF16), (ri <= ci).astype(BF16)


def _split3(x):
    hi = x.astype(BF16)
    r1 = x - hi.astype(F32)
    mid = r1.astype(BF16)
    lo = (r1 - mid.astype(F32)).astype(BF16)
    return hi, mid, lo


def _gla_proj_kernel(x_ref, mod_ref, gain_ref, w_ref, wz_ref, w2_ref, bg_ref,
                     q_ref, k_ref, v_ref, r_ref, cum_ref):
    hb = _modulated(x_ref[...], mod_ref, gain_ref, 1).astype(BF16)
    z = jnp.dot(hb, wz_ref[...].astype(BF16), preferred_element_type=F32).astype(BF16)
    zz = jnp.dot(z, w2_ref[...], preferred_element_type=F32) + bg_ref[...]
    log_gate = -_softplus(-zz) * (LOG2_E / GLA_TAU)
    parts = _split3(log_gate)
    r = jnp.dot(hb, w_ref[:, 2 * GLA_QK + GLA_V:], preferred_element_type=F32)
    r_ref[...] = r * _sigmoid(r)
    q_ref[...] = jnp.dot(hb, w_ref[:, :GLA_QK], preferred_element_type=F32) * (GLA_DK ** -0.5)
    k_ref[...] = jnp.dot(hb, w_ref[:, GLA_QK:2 * GLA_QK], preferred_element_type=F32)
    v_ref[...] = jnp.dot(hb, w_ref[:, 2 * GLA_QK:2 * GLA_QK + GLA_V], preferred_element_type=F32).astype(BF16)
    tri_f, tri_b = _chunk_tri()
    for ch in range(TM_PROJ // GLA_C):
        rows = slice(ch * GLA_C, (ch + 1) * GLA_C)
        cum_ref[rows, :GLA_QK] = sum(jnp.dot(tri_f, p[rows, :GLA_QK], preferred_element_type=F32) for p in parts)
        cum_ref[rows, GLA_QK:] = sum(jnp.dot(tri_b, p[rows, GLA_QK:], preferred_element_type=F32) for p in parts)


def _gla_proj(x, mods_l, gains_l, w_main, w_z, w2cat, bg):
    def rows(w):
        return pl.BlockSpec((TM_PROJ, w), lambda i: (i, 0))

    def whole(a):
        return pl.BlockSpec(a.shape, lambda i: (0,) * a.ndim, pipeline_mode=pl.Buffered(1))

    widths = (GLA_QK, GLA_QK, GLA_V, GLA_V, 2 * GLA_QK)
    dtypes = (F32, F32, BF16, F32, F32)
    return pl.pallas_call(
        _gla_proj_kernel,
        grid=(N_TOK // TM_PROJ,),
        in_specs=[
            rows(D_MODEL),
            pl.BlockSpec((None, N_MOD, D_MODEL), lambda i: (_group_of_tile(i, TM_PROJ), 0, 0)),
            pl.BlockSpec((3, D_MODEL), lambda i: (0, 0)),
            whole(w_main), whole(w_z), whole(w2cat), whole(bg),
        ],
        out_specs=[rows(w) for w in widths],
        out_shape=[jax.ShapeDtypeStruct((N_TOK, w), t) for w, t in zip(widths, dtypes)],
        compiler_params=_cparams("arbitrary"),
        name="gla_in_proj",
    )(x, mods_l, gains_l, w_main, w_z, w2cat, bg)


def _dot_tn(a, b):
    return lax.dot_general(a, b, (((0,), (0,)), ((), ())), preferred_element_type=F32)


def _dot_nt(a, b):
    return lax.dot_general(a, b, (((1,), (1,)), ((), ())), preferred_element_type=F32)


def _level_ref(cum_ref, row0, cum, m, backward):
    c = GLA_C
    pick = m - 1 if backward else m

    def bcast(r, n):
        return jnp.broadcast_to(cum_ref[pl.ds(row0 + r, 1), :], (n, GLA_DK))

    if m >= 4:
        return jnp.concatenate([bcast(g * 2 * m + pick, 2 * m) for g in range(c // (2 * m))], axis=0)
    pos = lax.broadcasted_iota(jnp.int32, (c, GLA_DK), 0)
    if m == 2:
        lo = jnp.concatenate([bcast(g * 8 + pick, 8) for g in range(c // 8)], axis=0)
        hi = jnp.concatenate([bcast(g * 8 + 4 + pick, 8) for g in range(c // 8)], axis=0)
        return jnp.where((pos & 4) == 0, lo, hi)
    if backward:
        return jnp.where((pos & 1) == 1, pltpu.roll(cum, 1, 0), cum)
    return jnp.where((pos & 1) == 0, pltpu.roll(cum, c - 1, 0), cum)


def _gla_intra(q_ref, k_ref, v_ref, cf_ref, cb_ref, row0, diag, masks, uppers):
    rows = pl.ds(row0, GLA_C)
    cf, cb = cf_ref[rows, :], cb_ref[rows, :]
    qb, kb = q_ref[rows, :].astype(BF16), k_ref[rows, :].astype(BF16)
    scores = jnp.where(diag, 2.0 * _dot_nt(qb, kb), 0.0)
    for lvl, m in enumerate(GLA_LEVELS):
        df = cf - _level_ref(cf_ref, row0, cf, m, False)
        db = cb - _level_ref(cb_ref, row0, cb, m, True)
        gq = jnp.exp2(jnp.where(uppers[lvl], df, db)).astype(BF16)
        gk = jnp.exp2(-jnp.where(uppers[lvl], db, df)).astype(BF16)
        scores = jnp.where(masks[lvl], _dot_nt(qb * gq, kb * gk), scores)
    return jnp.dot(scores.astype(BF16), v_ref[rows, :], preferred_element_type=F32)


def _gla_inter(q_ref, k_ref, v_ref, cum_ref, row0, st, backward):
    rows = pl.ds(row0, GLA_C)
    cum = cum_ref[rows, :]
    tot = cum_ref[pl.ds(row0 + (0 if backward else GLA_C - 1), 1), :]
    o = _dot_nt((q_ref[rows, :] * jnp.exp2(cum)).astype(BF16), st.astype(BF16))
    kd = (k_ref[rows, :] * jnp.exp2(tot - cum)).astype(BF16)
    return o, st * jnp.exp2(tot) + _dot_tn(v_ref[rows, :], kd)


def _gla_kernel(*refs, t_len, seqs, has_s0):
    if has_s0:
        q_ref, k_ref, v_ref, cf_ref, cb_ref, r_ref, gn_ref, s0_ref, y_ref, sf_ref, sb_ref = refs
    else:
        q_ref, k_ref, v_ref, cf_ref, cb_ref, r_ref, gn_ref, y_ref, sout_ref, sf_ref, sb_ref = refs
    c = GLA_C
    n_chunks = t_len // c
    ri = lax.broadcasted_iota(jnp.int32, (c, c), 0)
    ci = lax.broadcasted_iota(jnp.int32, (c, c), 1)
    x = ri ^ ci
    masks = [(x >= m) & (x < 2 * m) for m in GLA_LEVELS]
    pos = lax.broadcasted_iota(jnp.int32, (c, GLA_DK), 0)
    uppers = [(pos & m) != 0 for m in GLA_LEVELS]

    def intra(i, carry):
        row0 = pl.multiple_of(i * c, c)
        y_ref[pl.ds(row0, c), :] = _gla_intra(q_ref, k_ref, v_ref, cf_ref, cb_ref, row0, ri == ci, masks, uppers)
        return carry

    lax.fori_loop(0, seqs * n_chunks, intra, 0, unroll=min(seqs * n_chunks, GLA_INTRA_UNROLL))

    for s in range(seqs):
        if has_s0:
            sf_ref[s] = s0_ref[s, 0].T
            sb_ref[s] = s0_ref[s, 1].T
        else:
            sf_ref[s] = jnp.zeros((GLA_DV, GLA_DK), F32)
            sb_ref[s] = jnp.zeros((GLA_DV, GLA_DK), F32)

        def inter(i, carry, s=s):
            rf = pl.multiple_of(s * t_len + i * c, c)
            rb = pl.multiple_of(s * t_len + (n_chunks - 1 - i) * c, c)
            o, st = _gla_inter(q_ref, k_ref, v_ref, cf_ref, rf, sf_ref[s], False)
            y_ref[pl.ds(rf, c), :] += o
            sf_ref[s] = st
            o, st = _gla_inter(q_ref, k_ref, v_ref, cb_ref, rb, sb_ref[s], True)
            y_ref[pl.ds(rb, c), :] += o
            sb_ref[s] = st
            return carry

        lax.fori_loop(0, n_chunks, inter, 0, unroll=min(n_chunks, GLA_INTER_UNROLL))
        if not has_s0:
            sout_ref[s, 0] = sf_ref[s].T
            sout_ref[s, 1] = sb_ref[s].T
    o = y_ref[...]
    y_ref[...] = _rms(o) * gn_ref[...] * r_ref[...]


def _gla_mix(q, k, v, cum, r, gnorm, s0, *, t_len, seqs, row0, n_seq):
    has_s0 = s0 is not None
    blk = seqs * t_len
    blk0 = row0 // blk
    assert blk0 * blk == row0 and n_seq % seqs == 0

    def rows(w, off=0):
        return pl.BlockSpec((blk, w), lambda s, h: (blk0 + s, off + h))

    state_spec = pl.BlockSpec((seqs, 2, None, GLA_DK, GLA_DV), lambda s, h: (s, 0, h, 0, 0))
    in_specs = [rows(GLA_DK), rows(GLA_DK), rows(GLA_DV), rows(GLA_DK), rows(GLA_DK, GLA_HEADS),
                rows(GLA_DV), pl.BlockSpec((None, 1, GLA_DV), lambda s, h: (h, 0, 0))]
    args = [q, k, v, cum, cum, r, gnorm.reshape(GLA_HEADS, 1, GLA_DV)]
    y_spec = pl.BlockSpec((blk, GLA_DV), lambda s, h: (s, h))
    y_shape = jax.ShapeDtypeStruct((n_seq * t_len, GLA_V), F32)
    if has_s0:
        in_specs.append(state_spec)
        args.append(s0)
        out_specs, out_shape = y_spec, y_shape
    else:
        out_specs = [y_spec, state_spec]
        out_shape = [y_shape, jax.ShapeDtypeStruct((n_seq, 2, GLA_HEADS, GLA_DK, GLA_DV), F32)]
    return pl.pallas_call(
        functools.partial(_gla_kernel, t_len=t_len, seqs=seqs, has_s0=has_s0),
        grid=(n_seq // seqs, GLA_HEADS),
        in_specs=in_specs,
        out_specs=out_specs,
        out_shape=out_shape,
        scratch_shapes=[pltpu.VMEM((seqs, GLA_DV, GLA_DK), F32)] * 2,
        compiler_params=_cparams("arbitrary", "arbitrary"),
        name="gla_mix_sample" if has_s0 else "gla_mix_prompt",
    )(*args)


def _natural_to_scan(x_half):
    x4 = x_half.reshape(-1, SCAN_V, SCAN_L, D_MODEL)
    return x4.swapaxes(1, 2).reshape(-1, D_MODEL)


def _scan_to_natural(x_half):
    x4 = x_half.reshape(-1, SCAN_L, SCAN_V, D_MODEL)
    return x4.swapaxes(1, 2).reshape(-1, D_MODEL)


def _natural_to_col_major(x_half):
    x4 = x_half.reshape(DEC_BATCH, DEC_SEQ // GRID_W, GRID_W, D_MODEL)
    return x4.swapaxes(1, 2).reshape(-1, D_MODEL)


def _col_major_to_natural(x_half):
    x4 = x_half.reshape(DEC_BATCH, GRID_W, DEC_SEQ // GRID_W, D_MODEL)
    return x4.swapaxes(1, 2).reshape(-1, D_MODEL)


def kernel(x_prompt, x_sample, c, state_rglru, state_gla, c_ctx, ada_w, ada_b, norm_g, ffn_w_in, ffn_w_out,
           rg_w_in, rg_conv_w, rg_conv_b, rg_w_a, rg_b_a, rg_w_i, rg_b_i, rg_lambda, rg_w_out,
           gla_w_in, gla_w_g2, gla_b_g, gla_norm, gla_w_out, final_norm):
    depth = ada_w.shape[0]
    cvecs = jnp.concatenate([c_ctx[None, :], c, jnp.zeros((8 - N_GROUPS, D_MODEL), F32)], axis=0)
    mods = _ada_mods(cvecs, ada_w, ada_b)
    fn = final_norm.reshape(1, D_MODEL)
    rg_states, gla_states = [], []

    w_in, w_out = ffn_w_in[0, 0].astype(BF16), ffn_w_out[0, 0].astype(BF16)
    n_main = 2 * GLA_QK + 2 * GLA_V

    def ffn_casts(layer, half):
        return ((ffn_w_in, (layer, half), 2 * D_FF), (ffn_w_out, (layer, half), D_MODEL))

    xp = x_prompt.reshape(N_PROMPT_TOK, D_MODEL)
    xs = x_sample.reshape(DEC_BATCH * DEC_SEQ, D_MODEL)
    for i in range(depth):
        mods_l, gains_l = mods[i], norm_g[i]
        j = i // 2
        last = i == depth - 1
        if i % 2 == 0:
            halves = (_natural_to_scan(xp), _natural_to_scan(xs))
            mixer_casts = ((rg_w_in, (j,), 2 * D_RNN), (rg_w_out, (j,), D_MODEL))
        else:
            halves = (xp, _natural_to_col_major(xs))
            mixer_casts = ((gla_w_in[j, :, :n_main], (), n_main), (gla_w_out, (j,), D_MODEL))
        x, w_in, w_out, wm_in, wm_out = _half_ffn(halves, mods_l, gains_l, w_in, w_out, 0, fn, False, False,
                                                  casts=ffn_casts(i, 1) + mixer_casts)
        if i % 2 == 0:
            gate, xbr = _rg_proj(x, mods_l, gains_l, wm_in)
            y, st_f, st_b = _rg_scan(xbr, gate, rg_conv_w[j], rg_conv_b[j].reshape(1, D_RNN),
                                     rg_w_a[j].astype(BF16), rg_b_a[j], rg_w_i[j].astype(BF16), rg_b_i[j],
                                     rg_lambda[j], state_rglru[:, j])
            rg_states.append(jnp.stack([st_f[:BATCH], st_b[:BATCH]], axis=1))
            ys = (y,)
        else:
            w_z = jnp.pad(gla_w_in[j, :, n_main:], ((0, 0), (0, 128 - 2 * GLA_RANK)))
            w2cat = jnp.zeros((128, 2 * GLA_QK), F32)
            w2cat = w2cat.at[:GLA_RANK, :GLA_QK].set(gla_w_g2[j, 0])
            w2cat = w2cat.at[GLA_RANK:2 * GLA_RANK, GLA_QK:].set(gla_w_g2[j, 1]).astype(BF16)
            q, k, v, r, cum = _gla_proj(x, mods_l, gains_l, wm_in, w_z, w2cat, gla_b_g[j].reshape(1, 2 * GLA_QK))
            y_p, st = _gla_mix(q, k, v, cum, r, gla_norm[j], None, t_len=SEQ, seqs=GLA_PROMPT_SEQS, row0=0,
                               n_seq=BATCH)
            y_s = _gla_mix(q, k, v, cum, r, gla_norm[j], state_gla[:, j], t_len=DEC_SEQ, seqs=1,
                           row0=N_PROMPT_TOK, n_seq=DEC_BATCH)
            gla_states.append(st)
            ys = (y_p, y_s)
        xp, xs, *next_pair = _half_ffn((x,), mods_l, gains_l, w_in, w_out, 1, fn, last, True, mix=(ys, wm_out),
                                       casts=() if last else ffn_casts(i + 1, 0))
        if not last:
            w_in, w_out = next_pair
        if i % 2 == 0:
            xp, xs = _scan_to_natural(xp), _scan_to_natural(xs)
        else:
            xs = _col_major_to_natural(xs)

    y_prompt = xp.reshape(BATCH, SEQ, D_MODEL)
    y_sample = xs.reshape(DEC_BATCH, DEC_SEQ, D_MODEL)
    return (y_prompt, y_sample, jnp.stack(rg_states, axis=1), jnp.stack(gla_states, axis=1))
```

```python
import functools

import jax
import jax.numpy as jnp
from jax import lax
from jax.experimental import pallas as pl
from jax.experimental.pallas import tpu as pltpu

F32 = jnp.float32
BF16 = jnp.bfloat16

D_MODEL = 1024
BATCH = 16
SEQ = 256
DEC_BATCH = 2
DEC_SEQ = 2048
GRID_W = 64
EPS = 1e-6
N_MOD = 9
D_FF = 2816
FFN_RES = 0.5
D_RNN = 1280
RG_BLOCKS = 10
RG_BLOCK = 128
CONV_W = 4
RG_C = 8.0
GLA_HEADS = 4
GLA_QK = 512
GLA_V = 1024
GLA_DK = 128
GLA_DV = 256
GLA_RANK = 16
GLA_TAU = 16.0
LOG2_E = 1.4426950408889634

N_PROMPT_TOK = BATCH * SEQ
N_TOK = N_PROMPT_TOK + DEC_BATCH * DEC_SEQ
N_GROUPS = 1 + DEC_BATCH
GROUP_ROWS = DEC_SEQ
PROMPT_BLOCKS = N_PROMPT_TOK // GROUP_ROWS

VMEM_LIMIT_BYTES = 56 * 1024 * 1024

TM = 512
HALF_TILES = N_PROMPT_TOK // TM
TM_PROJ = 1024
ADA_SLAB = 256
MXU_DIM = 256
BF16_SUBLANES = 16
FF_CHUNKS = (0, 6 * MXU_DIM, D_FF)
RG_CT = 256
SCAN_L = 256
SCAN_V = GROUP_ROWS // SCAN_L
GLA_C = 128
GLA_PROMPT_SEQS = 8
GLA_INTRA_UNROLL = 4
GLA_INTER_UNROLL = 8
GLA_LEVELS = tuple(GLA_C >> (l + 1) for l in range(GLA_C.bit_length() - 1))


def _cparams(*sem):
    return pltpu.CompilerParams(dimension_semantics=sem, vmem_limit_bytes=VMEM_LIMIT_BYTES)


def _group_of_tile(i, rows):
    return jnp.maximum((i * rows) // GROUP_ROWS - (PROMPT_BLOCKS - 1), 0)


def _sigmoid(x):
    return 1.0 / (1.0 + jnp.exp2(x * (-LOG2_E)))


def _softplus(x):
    return jnp.maximum(x, 0.0) + jnp.log1p(jnp.exp(-jnp.abs(x)))


def _rms(x):
    return x * lax.rsqrt(jnp.mean(x * x, axis=-1, keepdims=True) + EPS)


def _modulated(x, mod_ref, gain_ref, j):
    y = _rms(x) * gain_ref[j:j + 1, :]
    return y * (1.0 + mod_ref[3 * j + 1:3 * j + 2, :]) + mod_ref[3 * j:3 * j + 1, :]


def _ada_kernel(c_ref, w_ref, b_ref, o_ref):
    @pl.when(pl.program_id(1) == 0)
    def _():
        o_ref[...] = jnp.broadcast_to(b_ref[...], o_ref.shape)

    c = c_ref[...]
    s = (c * _sigmoid(c)).astype(BF16)
    o_ref[...] += jnp.dot(s, w_ref[...].astype(BF16), preferred_element_type=F32)


def _ada_mods(cvecs, ada_w, ada_b):
    depth = ada_w.shape[0]
    n_slab = D_MODEL // ADA_SLAB
    c_slabs = cvecs.reshape(8, n_slab, ADA_SLAB).swapaxes(0, 1)
    out = pl.pallas_call(
        _ada_kernel,
        grid=(depth, n_slab),
        in_specs=[
            pl.BlockSpec((None, 8, ADA_SLAB), lambda l, k: (k, 0, 0)),
            pl.BlockSpec((None, ADA_SLAB, N_MOD * D_MODEL), lambda l, k: (l, k, 0)),
            pl.BlockSpec((None, 1, N_MOD * D_MODEL), lambda l, k: (l, 0, 0)),
        ],
        out_specs=pl.BlockSpec((None, 8, N_MOD * D_MODEL), lambda l, k: (l, 0, 0)),
        out_shape=jax.ShapeDtypeStruct((depth, 8, N_MOD * D_MODEL), F32),
        compiler_params=_cparams("arbitrary", "arbitrary"),
        name="ada_mod",
    )(c_slabs, ada_w, ada_b.reshape(depth, 1, N_MOD * D_MODEL))
    return out[:, :N_GROUPS, :].reshape(depth, N_GROUPS, N_MOD, D_MODEL)


def _pick_half(refs):
    if len(refs) == 1:
        return refs[0][...]
    return jnp.where(pl.program_id(0) < HALF_TILES, refs[0][...], refs[1][...])


def _ffn_kernel(*refs, j, final, n_x, n_y, split_out, n_cast):
    x_refs, y_refs, refs = refs[:n_x], refs[n_x:n_x + n_y], refs[n_x + n_y:]
    if n_y:
        wmix_ref, refs = refs[0], refs[1:]
    mod_ref, gain_ref, win_ref, wout_ref, fn_ref = refs[:5]
    refs = refs[5:]
    if n_cast:
        f32_refs, refs = refs[:n_cast], refs[n_cast:]
        for src_ref, dst_ref in zip(f32_refs, refs[len(refs) - n_cast:]):
            dst_ref[...] = src_ref[...].astype(BF16)
    o_refs = refs
    x = _pick_half(x_refs)
    if n_y:
        y = _pick_half(y_refs).astype(BF16)
        x = x + mod_ref[5:6, :] * jnp.dot(y, wmix_ref[...], preferred_element_type=F32)
    hb = _modulated(x, mod_ref, gain_ref, j).astype(BF16)
    acc = jnp.zeros((TM, D_MODEL), F32)
    for lo, hi in zip(FF_CHUNKS[:-1], FF_CHUNKS[1:]):
        g = jnp.dot(hb, win_ref[:, lo:hi], preferred_element_type=F32)
        u = jnp.dot(hb, win_ref[:, D_FF + lo:D_FF + hi], preferred_element_type=F32)
        a = (g * _sigmoid(g) * u).astype(BF16)
        acc = acc + jnp.dot(a, wout_ref[lo:hi, :], preferred_element_type=F32)
    out = x + FFN_RES * mod_ref[3 * j + 2:3 * j + 3, :] * acc
    if final:
        out = _rms(out) * fn_ref[...]
    if split_out:
        first_half = pl.program_id(0) < HALF_TILES

        @pl.when(first_half)
        def _():
            o_refs[0][...] = out

        @pl.when(jnp.logical_not(first_half))
        def _():
            o_refs[1][...] = out
    else:
        o_refs[0][...] = out


def _row_specs(width, split):
    if not split:
        return [pl.BlockSpec((TM, width), lambda i: (i, 0))]
    return [pl.BlockSpec((TM, width), lambda i: (jnp.minimum(i, HALF_TILES - 1), 0)),
            pl.BlockSpec((TM, width), lambda i: (jnp.maximum(i - HALF_TILES, 0), 0))]


def _half_ffn(xs, mods_l, gains_l, w_in, w_out, half, final_norm, final, split_out, mix=None, casts=()):
    n_steps = N_TOK // TM
    ys, w_mix = mix if mix is not None else ((), None)
    in_specs = _row_specs(D_MODEL, len(xs) == 2)
    args = list(xs)
    if ys:
        in_specs += _row_specs(ys[0].shape[1], len(ys) == 2)
        in_specs.append(pl.BlockSpec(w_mix.shape, lambda i: (0, 0), pipeline_mode=pl.Buffered(1)))
        args += list(ys) + [w_mix]
    in_specs += [
        pl.BlockSpec((None, N_MOD, D_MODEL), lambda i: (_group_of_tile(i, TM), 0, 0)),
        pl.BlockSpec((3, D_MODEL), lambda i: (0, 0)),
        pl.BlockSpec((D_MODEL, 2 * D_FF), lambda i: (0, 0), pipeline_mode=pl.Buffered(1)),
        pl.BlockSpec((D_FF, D_MODEL), lambda i: (0, 0), pipeline_mode=pl.Buffered(1)),
        pl.BlockSpec((1, D_MODEL), lambda i: (0, 0)),
    ]
    args += [mods_l, gains_l, w_in, w_out, final_norm]
    n_rows = N_PROMPT_TOK if split_out else N_TOK
    out_specs = _row_specs(D_MODEL, split_out)
    out_shape = [jax.ShapeDtypeStruct((n_rows, D_MODEL), F32)] * (2 if split_out else 1)
    for arr, lead, cols in casts:
        rows = arr.shape[len(lead)]
        slab = rows // n_steps
        assert slab * n_steps == rows and slab % BF16_SUBLANES == 0, (rows, n_steps)
        in_specs.append(pl.BlockSpec((None,) * len(lead) + (slab, cols), lambda i, lead=lead: (*lead, i, 0)))
        args.append(arr)
        out_specs.append(pl.BlockSpec((slab, cols), lambda i: (i, 0)))
        out_shape.append(jax.ShapeDtypeStruct((rows, cols), BF16))
    return pl.pallas_call(
        functools.partial(_ffn_kernel, j=2 * half, final=final, n_x=len(xs), n_y=len(ys), split_out=split_out,
                          n_cast=len(casts)),
        grid=(n_steps,),
        in_specs=in_specs,
        out_specs=out_specs,
        out_shape=out_shape,
        compiler_params=_cparams("arbitrary"),
        name="half_ffn",
    )(*args)


def _gelu_tanh(x):
    return 0.5 * x * (1.0 + jnp.tanh(0.7978845608028654 * (x + 0.044715 * (x * x * x))))


def _rg_proj_kernel(x_ref, mod_ref, gain_ref, w_ref, gate_ref, xbr_ref):
    hb = _modulated(x_ref[...], mod_ref, gain_ref, 1).astype(BF16)
    gate_ref[...] = _gelu_tanh(jnp.dot(hb, w_ref[:, :D_RNN], preferred_element_type=F32))
    xbr_ref[...] = jnp.dot(hb, w_ref[:, D_RNN:], preferred_element_type=F32)


def _rg_proj(x, mods_l, gains_l, w_in):
    return pl.pallas_call(
        _rg_proj_kernel,
        grid=(N_TOK // TM_PROJ,),
        in_specs=[
            pl.BlockSpec((TM_PROJ, D_MODEL), lambda i: (i, 0)),
            pl.BlockSpec((None, N_MOD, D_MODEL), lambda i: (_group_of_tile(i, TM_PROJ), 0, 0)),
            pl.BlockSpec((3, D_MODEL), lambda i: (0, 0)),
            pl.BlockSpec((D_MODEL, 2 * D_RNN), lambda i: (0, 0), pipeline_mode=pl.Buffered(1)),
        ],
        out_specs=[pl.BlockSpec((TM_PROJ, D_RNN), lambda i: (i, 0))] * 2,
        out_shape=[jax.ShapeDtypeStruct((N_TOK, D_RNN), F32)] * 2,
        compiler_params=_cparams("arbitrary"),
        name="rg_in_proj",
    )(x, mods_l, gains_l, w_in)


def _seg_shift(row, down):
    v = lax.broadcasted_iota(jnp.int32, row.shape, 0)
    if down:
        return jnp.where(v >= 1, pltpu.roll(row, 1, 0), 0.0)
    return jnp.where(v < SCAN_V - 1, pltpu.roll(row, SCAN_V - 1, 0), 0.0)


def _rg_scan_kernel(xbr_ref, gate_ref, cw_ref, cb_ref, wa_ref, ba_ref, wi_ref, bi_ref, lam_ref, h0_ref,
                    y_ref, stf_ref, stb_ref, af_ref, uf_ref, ab_ref, ub_ref, pf_ref, hf_ref, pb_ref, hb_ref):
    chain = jnp.where(pl.program_id(0) >= PROMPT_BLOCKS, 1.0, 0.0).astype(F32)
    nbt = RG_CT // RG_BLOCK
    rows = SCAN_L * SCAN_V
    v = SCAN_V

    for nb in range(nbt):
        ls = slice(nb * RG_BLOCK, (nb + 1) * RG_BLOCK)
        x = xbr_ref[:, ls]
        before = chain * _seg_shift(x[rows - v:], True)
        after0 = chain * _seg_shift(x[:v], False)
        after1 = chain * _seg_shift(x[v:2 * v], False)
        xm1 = jnp.concatenate([before, x[:rows - v]], axis=0)
        xp1 = jnp.concatenate([x[v:], after0], axis=0)
        xp2 = jnp.concatenate([x[2 * v:], after0, after1], axis=0)
        xc = (cw_ref[0:1, ls] * xm1 + cw_ref[1:2, ls] * x + cw_ref[2:3, ls] * xp1 + cw_ref[3:4, ls] * xp2
              + cb_ref[0:1, ls])
        xcb = xc.astype(BF16)
        xc_r2 = xc * (2.0 ** 0.5)
        for d, (a_ref, u_ref) in enumerate(((af_ref, uf_ref), (ab_ref, ub_ref))):
            r = _sigmoid(jnp.dot(xcb, wa_ref[d, nb], preferred_element_type=F32) + ba_ref[d:d + 1, ls])
            gi = _sigmoid(jnp.dot(xcb, wi_ref[d, nb], preferred_element_type=F32) + bi_ref[d:d + 1, ls])
            decay = RG_C * _softplus(-lam_ref[d:d + 1, ls])
            a_ref[nb] = jnp.exp2(r * (decay * (-LOG2_E)))
            t = jnp.tanh(r * decay)
            prod = t * (1.0 + t)
            u_ref[nb] = jnp.where(prod > 0.0, t * lax.rsqrt(prod), 0.0) * (gi * xc_r2)

    def step(i, carry):
        sf = pl.ds(pl.multiple_of(i * v, v), v)
        sb = pl.ds(pl.multiple_of((SCAN_L - 1 - i) * v, v), v)
        out = []
        for nb in range(nbt):
            hf, pf, hb, pb = carry[4 * nb:4 * nb + 4]
            a = af_ref[nb, sf, :]
            hf = a * hf + uf_ref[nb, sf, :]
            pf = a * pf
            hf_ref[nb, sf, :] = hf
            pf_ref[nb, sf, :] = pf
            a = ab_ref[nb, sb, :]
            hb = a * hb + ub_ref[nb, sb, :]
            pb = a * pb
            hb_ref[nb, sb, :] = hb
            pb_ref[nb, sb, :] = pb
            out += [hf, pf, hb, pb]
        return tuple(out)

    zero = jnp.zeros((v, RG_BLOCK), F32)
    one = jnp.ones((v, RG_BLOCK), F32)
    lax.fori_loop(0, SCAN_L, step, (zero, one, zero, one) * nbt, unroll=8)

    for nb in range(nbt):
        ls = slice(nb * RG_BLOCK, (nb + 1) * RG_BLOCK)
        h_end, p_end = hf_ref[nb, rows - v:, :], pf_ref[nb, rows - v:, :]
        entry = [chain * h0_ref[0:1, ls]]
        for s in range(v - 1):
            entry.append(chain * (h_end[s:s + 1] + p_end[s:s + 1] * entry[s]))
        ef = jnp.concatenate(entry, axis=0)
        h_start, p_start = hb_ref[nb, :v, :], pb_ref[nb, :v, :]
        entry = [chain * h0_ref[1:2, ls]]
        for s in range(v - 1, 0, -1):
            entry.append(chain * (h_start[s:s + 1] + p_start[s:s + 1] * entry[-1]))
        eb = jnp.concatenate(entry[::-1], axis=0)
        shape3 = (SCAN_L, v, RG_BLOCK)
        hf = hf_ref[nb].reshape(shape3) + pf_ref[nb].reshape(shape3) * ef[None]
        hb = hb_ref[nb].reshape(shape3) + pb_ref[nb].reshape(shape3) * eb[None]
        stf_ref[:, ls] = hf[SCAN_L - 1]
        stb_ref[:, ls] = hb[0]
        y_ref[:, ls] = (hf + hb).reshape(rows, RG_BLOCK) * gate_ref[:, ls]


def _rg_scan(xbr, gate, conv_w, conv_b, w_a, b_a, w_i, b_i, lam, h0):
    n_blk = N_TOK // GROUP_ROWS
    nbt = RG_CT // RG_BLOCK
    tile = pl.BlockSpec((GROUP_ROWS, RG_CT), lambda b, c: (b, c))
    vec2 = pl.BlockSpec((2, RG_CT), lambda b, c: (0, c))
    wspec = pl.BlockSpec((2, nbt, RG_BLOCK, RG_BLOCK), lambda b, c: (0, c, 0, 0))
    return pl.pallas_call(
        _rg_scan_kernel,
        grid=(n_blk, D_RNN // RG_CT),
        in_specs=[
            tile, tile,
            pl.BlockSpec((CONV_W, RG_CT), lambda b, c: (0, c)),
            pl.BlockSpec((1, RG_CT), lambda b, c: (0, c)),
            wspec, vec2, wspec, vec2, vec2,
            pl.BlockSpec((None, 2, RG_CT), lambda b, c: (jnp.maximum(b - PROMPT_BLOCKS, 0), 0, c)),
        ],
        out_specs=[tile] + [pl.BlockSpec((SCAN_V, RG_CT), lambda b, c: (b, c))] * 2,
        out_shape=[jax.ShapeDtypeStruct((N_TOK, D_RNN), F32)]
        + [jax.ShapeDtypeStruct((n_blk * SCAN_V, D_RNN), F32)] * 2,
        scratch_shapes=[pltpu.VMEM((nbt, GROUP_ROWS, RG_BLOCK), F32)] * 8,
        compiler_params=_cparams("arbitrary", "arbitrary"),
        name="rg_scan",
    )(xbr, gate, conv_w, conv_b, w_a, b_a, w_i, b_i, lam, h0)


def _chunk_tri():
    ri = lax.broadcasted_iota(jnp.int32, (GLA_C, GLA_C), 0)
    ci = lax.broadcasted_iota(jnp.int32, (GLA_C, GLA_C), 1)
    return (ri >= ci).astype(BF16), (ri <= ci).astype(BF16)


def _split3(x):
    hi = x.astype(BF16)
    r1 = x - hi.astype(F32)
    mid = r1.astype(BF16)
    lo = (r1 - mid.astype(F32)).astype(BF16)
    return hi, mid, lo


def _gla_proj_kernel(x_ref, mod_ref, gain_ref, w_ref, wz_ref, w2_ref, bg_ref,
                     q_ref, k_ref, v_ref, r_ref, cum_ref):
    hb = _modulated(x_ref[...], mod_ref, gain_ref, 1).astype(BF16)
    z = jnp.dot(hb, wz_ref[...].astype(BF16), preferred_element_type=F32).astype(BF16)
    zz = jnp.dot(z, w2_ref[...], preferred_element_type=F32) + bg_ref[...]
    log_gate = -_softplus(-zz) * (LOG2_E / GLA_TAU)
    parts = _split3(log_gate)
    r = jnp.dot(hb, w_ref[:, 2 * GLA_QK + GLA_V:], preferred_element_type=F32)
    r_ref[...] = r * _sigmoid(r)
    q_ref[...] = jnp.dot(hb, w_ref[:, :GLA_QK], preferred_element_type=F32) * (GLA_DK ** -0.5)
    k_ref[...] = jnp.dot(hb, w_ref[:, GLA_QK:2 * GLA_QK], preferred_element_type=F32)
    v_ref[...] = jnp.dot(hb, w_ref[:, 2 * GLA_QK:2 * GLA_QK + GLA_V], preferred_element_type=F32).astype(BF16)
    tri_f, tri_b = _chunk_tri()
    for ch in range(TM_PROJ // GLA_C):
        rows = slice(ch * GLA_C, (ch + 1) * GLA_C)
        cum_ref[rows, :GLA_QK] = sum(jnp.dot(tri_f, p[rows, :GLA_QK], preferred_element_type=F32) for p in parts)
        cum_ref[rows, GLA_QK:] = sum(jnp.dot(tri_b, p[rows, GLA_QK:], preferred_element_type=F32) for p in parts)


def _gla_proj(x, mods_l, gains_l, w_main, w_z, w2cat, bg):
    def rows(w):
        return pl.BlockSpec((TM_PROJ, w), lambda i: (i, 0))

    def whole(a):
        return pl.BlockSpec(a.shape, lambda i: (0,) * a.ndim, pipeline_mode=pl.Buffered(1))

    widths = (GLA_QK, GLA_QK, GLA_V, GLA_V, 2 * GLA_QK)
    dtypes = (F32, F32, BF16, F32, F32)
    return pl.pallas_call(
        _gla_proj_kernel,
        grid=(N_TOK // TM_PROJ,),
        in_specs=[
            rows(D_MODEL),
            pl.BlockSpec((None, N_MOD, D_MODEL), lambda i: (_group_of_tile(i, TM_PROJ), 0, 0)),
            pl.BlockSpec((3, D_MODEL), lambda i: (0, 0)),
            whole(w_main), whole(w_z), whole(w2cat), whole(bg),
        ],
        out_specs=[rows(w) for w in widths],
        out_shape=[jax.ShapeDtypeStruct((N_TOK, w), t) for w, t in zip(widths, dtypes)],
        compiler_params=_cparams("arbitrary"),
        name="gla_in_proj",
    )(x, mods_l, gains_l, w_main, w_z, w2cat, bg)


def _dot_tn(a, b):
    return lax.dot_general(a, b, (((0,), (0,)), ((), ())), preferred_element_type=F32)


def _dot_nt(a, b):
    return lax.dot_general(a, b, (((1,), (1,)), ((), ())), preferred_element_type=F32)


def _level_ref(cum_ref, row0, cum, m, backward):
    c = GLA_C
    pick = m - 1 if backward else m

    def bcast(r, n):
        return jnp.broadcast_to(cum_ref[pl.ds(row0 + r, 1), :], (n, GLA_DK))

    if m >= 4:
        return jnp.concatenate([bcast(g * 2 * m + pick, 2 * m) for g in range(c // (2 * m))], axis=0)
    pos = lax.broadcasted_iota(jnp.int32, (c, GLA_DK), 0)
    if m == 2:
        lo = jnp.concatenate([bcast(g * 8 + pick, 8) for g in range(c // 8)], axis=0)
        hi = jnp.concatenate([bcast(g * 8 + 4 + pick, 8) for g in range(c // 8)], axis=0)
        return jnp.where((pos & 4) == 0, lo, hi)
    if backward:
        return jnp.where((pos & 1) == 1, pltpu.roll(cum, 1, 0), cum)
    return jnp.where((pos & 1) == 0, pltpu.roll(cum, c - 1, 0), cum)


def _gla_intra(q_ref, k_ref, v_ref, cf_ref, cb_ref, row0, diag, masks, uppers):
    rows = pl.ds(row0, GLA_C)
    cf, cb = cf_ref[rows, :], cb_ref[rows, :]
    qb, kb = q_ref[rows, :].astype(BF16), k_ref[rows, :].astype(BF16)
    scores = jnp.where(diag, 2.0 * _dot_nt(qb, kb), 0.0)
    for lvl, m in enumerate(GLA_LEVELS):
        df = cf - _level_ref(cf_ref, row0, cf, m, False)
        db = cb - _level_ref(cb_ref, row0, cb, m, True)
        gq = jnp.exp2(jnp.where(uppers[lvl], df, db)).astype(BF16)
        gk = jnp.exp2(-jnp.where(uppers[lvl], db, df)).astype(BF16)
        scores = jnp.where(masks[lvl], _dot_nt(qb * gq, kb * gk), scores)
    return jnp.dot(scores.astype(BF16), v_ref[rows, :], preferred_element_type=F32)


def _gla_inter(q_ref, k_ref, v_ref, cum_ref, row0, st, backward):
    rows = pl.ds(row0, GLA_C)
    cum = cum_ref[rows, :]
    tot = cum_ref[pl.ds(row0 + (0 if backward else GLA_C - 1), 1), :]
    o = _dot_nt((q_ref[rows, :] * jnp.exp2(cum)).astype(BF16), st.astype(BF16))
    kd = (k_ref[rows, :] * jnp.exp2(tot - cum)).astype(BF16)
    return o, st * jnp.exp2(tot) + _dot_tn(v_ref[rows, :], kd)


def _gla_kernel(*refs, t_len, seqs, has_s0):
    if has_s0:
        q_ref, k_ref, v_ref, cf_ref, cb_ref, r_ref, gn_ref, s0_ref, y_ref, sf_ref, sb_ref = refs
    else:
        q_ref, k_ref, v_ref, cf_ref, cb_ref, r_ref, gn_ref, y_ref, sout_ref, sf_ref, sb_ref = refs
    c = GLA_C
    n_chunks = t_len // c
    ri = lax.broadcasted_iota(jnp.int32, (c, c), 0)
    ci = lax.broadcasted_iota(jnp.int32, (c, c), 1)
    x = ri ^ ci
    masks = [(x >= m) & (x < 2 * m) for m in GLA_LEVELS]
    pos = lax.broadcasted_iota(jnp.int32, (c, GLA_DK), 0)
    uppers = [(pos & m) != 0 for m in GLA_LEVELS]

    def intra(i, carry):
        row0 = pl.multiple_of(i * c, c)
        y_ref[pl.ds(row0, c), :] = _gla_intra(q_ref, k_ref, v_ref, cf_ref, cb_ref, row0, ri == ci, masks, uppers)
        return carry

    lax.fori_loop(0, seqs * n_chunks, intra, 0, unroll=min(seqs * n_chunks, GLA_INTRA_UNROLL))

    for s in range(seqs):
        if has_s0:
            sf_ref[s] = s0_ref[s, 0].T
            sb_ref[s] = s0_ref[s, 1].T
        else:
            sf_ref[s] = jnp.zeros((GLA_DV, GLA_DK), F32)
            sb_ref[s] = jnp.zeros((GLA_DV, GLA_DK), F32)

        def inter(i, carry, s=s):
            rf = pl.multiple_of(s * t_len + i * c, c)
            rb = pl.multiple_of(s * t_len + (n_chunks - 1 - i) * c, c)
            o, st = _gla_inter(q_ref, k_ref, v_ref, cf_ref, rf, sf_ref[s], False)
            y_ref[pl.ds(rf, c), :] += o
            sf_ref[s] = st
            o, st = _gla_inter(q_ref, k_ref, v_ref, cb_ref, rb, sb_ref[s], True)
            y_ref[pl.ds(rb, c), :] += o
            sb_ref[s] = st
            return carry

        lax.fori_loop(0, n_chunks, inter, 0, unroll=min(n_chunks, GLA_INTER_UNROLL))
        if not has_s0:
            sout_ref[s, 0] = sf_ref[s].T
            sout_ref[s, 1] = sb_ref[s].T
    o = y_ref[...]
    y_ref[...] = _rms(o) * gn_ref[...] * r_ref[...]


def _gla_mix(q, k, v, cum, r, gnorm, s0, *, t_len, seqs, row0, n_seq):
    has_s0 = s0 is not None
    blk = seqs * t_len
    blk0 = row0 // blk
    assert blk0 * blk == row0 and n_seq % seqs == 0

    def rows(w, off=0):
        return pl.BlockSpec((blk, w), lambda s, h: (blk0 + s, off + h))

    state_spec = pl.BlockSpec((seqs, 2, None, GLA_DK, GLA_DV), lambda s, h: (s, 0, h, 0, 0))
    in_specs = [rows(GLA_DK), rows(GLA_DK), rows(GLA_DV), rows(GLA_DK), rows(GLA_DK, GLA_HEADS),
                rows(GLA_DV), pl.BlockSpec((None, 1, GLA_DV), lambda s, h: (h, 0, 0))]
    args = [q, k, v, cum, cum, r, gnorm.reshape(GLA_HEADS, 1, GLA_DV)]
    y_spec = pl.BlockSpec((blk, GLA_DV), lambda s, h: (s, h))
    y_shape = jax.ShapeDtypeStruct((n_seq * t_len, GLA_V), F32)
    if has_s0:
        in_specs.append(state_spec)
        args.append(s0)
        out_specs, out_shape = y_spec, y_shape
    else:
        out_specs = [y_spec, state_spec]
        out_shape = [y_shape, jax.ShapeDtypeStruct((n_seq, 2, GLA_HEADS, GLA_DK, GLA_DV), F32)]
    return pl.pallas_call(
        functools.partial(_gla_kernel, t_len=t_len, seqs=seqs, has_s0=has_s0),
        grid=(n_seq // seqs, GLA_HEADS),
        in_specs=in_specs,
        out_specs=out_specs,
        out_shape=out_shape,
        scratch_shapes=[pltpu.VMEM((seqs, GLA_DV, GLA_DK), F32)] * 2,
        compiler_params=_cparams("arbitrary", "arbitrary"),
        name="gla_mix_sample" if has_s0 else "gla_mix_prompt",
    )(*args)


def _natural_to_scan(x_half):
    x4 = x_half.reshape(-1, SCAN_V, SCAN_L, D_MODEL)
    return x4.swapaxes(1, 2).reshape(-1, D_MODEL)


def _scan_to_natural(x_half):
    x4 = x_half.reshape(-1, SCAN_L, SCAN_V, D_MODEL)
    return x4.swapaxes(1, 2).reshape(-1, D_MODEL)


def _natural_to_col_major(x_half):
    x4 = x_half.reshape(DEC_BATCH, DEC_SEQ // GRID_W, GRID_W, D_MODEL)
    return x4.swapaxes(1, 2).reshape(-1, D_MODEL)


def _col_major_to_natural(x_half):
    x4 = x_half.reshape(DEC_BATCH, GRID_W, DEC_SEQ // GRID_W, D_MODEL)
    return x4.swapaxes(1, 2).reshape(-1, D_MODEL)


def kernel(x_prompt, x_sample, c, state_rglru, state_gla, c_ctx, ada_w, ada_b, norm_g, ffn_w_in, ffn_w_out,
           rg_w_in, rg_conv_w, rg_conv_b, rg_w_a, rg_b_a, rg_w_i, rg_b_i, rg_lambda, rg_w_out,
           gla_w_in, gla_w_g2, gla_b_g, gla_norm, gla_w_out, final_norm):
    depth = ada_w.shape[0]
    cvecs = jnp.concatenate([c_ctx[None, :], c, jnp.zeros((8 - N_GROUPS, D_MODEL), F32)], axis=0)
    mods = _ada_mods(cvecs, ada_w, ada_b)
    fn = final_norm.reshape(1, D_MODEL)
    rg_states, gla_states = [], []

    w_in, w_out = ffn_w_in[0, 0].astype(BF16), ffn_w_out[0, 0].astype(BF16)
    n_main = 2 * GLA_QK + 2 * GLA_V

    def ffn_casts(layer, half):
        return ((ffn_w_in, (layer, half), 2 * D_FF), (ffn_w_out, (layer, half), D_MODEL))

    xp = x_prompt.reshape(N_PROMPT_TOK, D_MODEL)
    xs = x_sample.reshape(DEC_BATCH * DEC_SEQ, D_MODEL)
    for i in range(depth):
        mods_l, gains_l = mods[i], norm_g[i]
        j = i // 2
        last = i == depth - 1
        if i % 2 == 0:
            halves = (_natural_to_scan(xp), _natural_to_scan(xs))
            mixer_casts = ((rg_w_in, (j,), 2 * D_RNN), (rg_w_out, (j,), D_MODEL))
        else:
            halves = (xp, _natural_to_col_major(xs))
            mixer_casts = ((gla_w_in, (j,), n_main), (gla_w_out, (j,), D_MODEL))
        x, w_in, w_out, wm_in, wm_out = _half_ffn(halves, mods_l, gains_l, w_in, w_out, 0, fn, False, False,
                                                  casts=ffn_casts(i, 1) + mixer_casts)
        if i % 2 == 0:
            gate, xbr = _rg_proj(x, mods_l, gains_l, wm_in)
            y, st_f, st_b = _rg_scan(xbr, gate, rg_conv_w[j], rg_conv_b[j].reshape(1, D_RNN),
                                     rg_w_a[j].astype(BF16), rg_b_a[j], rg_w_i[j].astype(BF16), rg_b_i[j],
                                     rg_lambda[j], state_rglru[:, j])
            rg_states.append(jnp.stack([st_f[:BATCH], st_b[:BATCH]], axis=1))
            ys = (y,)
        else:
            w_z = jnp.pad(gla_w_in[j, :, n_main:], ((0, 0), (0, 128 - 2 * GLA_RANK)))
            w2cat = jnp.zeros((128, 2 * GLA_QK), F32)
            w2cat = w2cat.at[:GLA_RANK, :GLA_QK].set(gla_w_g2[j, 0])
            w2cat = w2cat.at[GLA_RANK:2 * GLA_RANK, GLA_QK:].set(gla_w_g2[j, 1]).astype(BF16)
            q, k, v, r, cum = _gla_proj(x, mods_l, gains_l, wm_in, w_z, w2cat, gla_b_g[j].reshape(1, 2 * GLA_QK))
            y_p, st = _gla_mix(q, k, v, cum, r, gla_norm[j], None, t_len=SEQ, seqs=GLA_PROMPT_SEQS, row0=0,
                               n_seq=BATCH)
            y_s = _gla_mix(q, k, v, cum, r, gla_norm[j], state_gla[:, j], t_len=DEC_SEQ, seqs=1,
                           row0=N_PROMPT_TOK, n_seq=DEC_BATCH)
            gla_states.append(st)
            ys = (y_p, y_s)
        xp, xs, *next_pair = _half_ffn((x,), mods_l, gains_l, w_in, w_out, 1, fn, last, True, mix=(ys, wm_out),
                                       casts=() if last else ffn_casts(i + 1, 0))
        if not last:
            w_in, w_out = next_pair
        if i % 2 == 0:
            xp, xs = _scan_to_natural(xp), _scan_to_natural(xs)
        else:
            xs = _col_major_to_natural(xs)

    y_prompt = xp.reshape(BATCH, SEQ, D_MODEL)
    y_sample = xs.reshape(DEC_BATCH, DEC_SEQ, D_MODEL)
    return (y_prompt, y_sample, jnp.stack(rg_states, axis=1), jnp.stack(gla_states, axis=1))
```

```python
import functools

import jax
import jax.numpy as jnp
from jax import lax
from jax.experimental import pallas as pl
from jax.experimental.pallas import tpu as pltpu

F32 = jnp.float32
BF16 = jnp.bfloat16

D_MODEL = 1024
BATCH = 16
SEQ = 256
DEC_BATCH = 2
DEC_SEQ = 2048
GRID_W = 64
EPS = 1e-6
N_MOD = 9
D_FF = 2816
FFN_RES = 0.5
D_RNN = 1280
RG_BLOCK = 128
CONV_W = 4
RG_C = 8.0
GLA_HEADS = 4
GLA_QK = 512
GLA_V = 1024
GLA_DK = 128
GLA_DV = 256
GLA_RANK = 16
GLA_TAU = 16.0
LOG2_E = 1.4426950408889634

N_PROMPT_TOK = BATCH * SEQ
N_TOK = N_PROMPT_TOK + DEC_BATCH * DEC_SEQ
N_GROUPS = 1 + DEC_BATCH
GROUP_ROWS = DEC_SEQ
PROMPT_BLOCKS = N_PROMPT_TOK // GROUP_ROWS

VMEM_LIMIT_BYTES = 56 * 1024 * 1024

TM = 512
HALF_TILES = N_PROMPT_TOK // TM
TM_PROJ = 1024
ADA_SLAB = 256
MXU_DIM = 256
BF16_SUBLANES = 16
FF_CHUNKS = (0, 6 * MXU_DIM, D_FF)
RG_CT = 256
SCAN_L = 256
SCAN_V = GROUP_ROWS // SCAN_L
SCAN_UNROLL = 8
GLA_C = 128
GLA_PROMPT_SEQS = 8
GLA_INTRA_UNROLL = 4
GLA_INTER_UNROLL = 8
GLA_LEVELS = tuple(GLA_C >> (l + 1) for l in range(GLA_C.bit_length() - 1))


def _cparams(*sem):
    return pltpu.CompilerParams(dimension_semantics=sem, vmem_limit_bytes=VMEM_LIMIT_BYTES)


def _group_of_tile(i, rows):
    return jnp.maximum((i * rows) // GROUP_ROWS - (PROMPT_BLOCKS - 1), 0)


def _sigmoid(x):
    return 1.0 / (1.0 + jnp.exp2(x * (-LOG2_E)))


def _softplus(x):
    return jnp.maximum(x, 0.0) + jnp.log1p(jnp.exp(-jnp.abs(x)))


def _rms(x):
    return x * lax.rsqrt(jnp.mean(x * x, axis=-1, keepdims=True) + EPS)


def _modulated(x, mod_ref, gain_ref, j):
    y = _rms(x) * gain_ref[j:j + 1, :]
    return y * (1.0 + mod_ref[3 * j + 1:3 * j + 2, :]) + mod_ref[3 * j:3 * j + 1, :]


def _ada_kernel(c_ref, w_ref, b_ref, o_ref):
    @pl.when(pl.program_id(1) == 0)
    def _():
        o_ref[...] = jnp.broadcast_to(b_ref[...], o_ref.shape)

    c = c_ref[...]
    s = (c * _sigmoid(c)).astype(BF16)
    o_ref[...] += jnp.dot(s, w_ref[...].astype(BF16), preferred_element_type=F32)


def _ada_mods(cvecs, ada_w, ada_b):
    depth = ada_w.shape[0]
    n_slab = D_MODEL // ADA_SLAB
    c_slabs = cvecs.reshape(8, n_slab, ADA_SLAB).swapaxes(0, 1)
    out = pl.pallas_call(
        _ada_kernel,
        grid=(depth, n_slab),
        in_specs=[
            pl.BlockSpec((None, 8, ADA_SLAB), lambda l, k: (k, 0, 0)),
            pl.BlockSpec((None, ADA_SLAB, N_MOD * D_MODEL), lambda l, k: (l, k, 0)),
            pl.BlockSpec((None, 1, N_MOD * D_MODEL), lambda l, k: (l, 0, 0)),
        ],
        out_specs=pl.BlockSpec((None, 8, N_MOD * D_MODEL), lambda l, k: (l, 0, 0)),
        out_shape=jax.ShapeDtypeStruct((depth, 8, N_MOD * D_MODEL), F32),
        compiler_params=_cparams("arbitrary", "arbitrary"),
        name="ada_mod",
    )(c_slabs, ada_w, ada_b.reshape(depth, 1, N_MOD * D_MODEL))
    return out[:, :N_GROUPS, :].reshape(depth, N_GROUPS, N_MOD, D_MODEL)


def _pick_half(refs):
    if len(refs) == 1:
        return refs[0][...]
    return jnp.where(pl.program_id(0) < HALF_TILES, refs[0][...], refs[1][...])


def _ffn_kernel(*refs, j, final, n_x, n_y, split_out, n_cast):
    x_refs, y_refs, refs = refs[:n_x], refs[n_x:n_x + n_y], refs[n_x + n_y:]
    if n_y:
        wmix_ref, refs = refs[0], refs[1:]
    mod_ref, gain_ref, win_ref, wout_ref, fn_ref = refs[:5]
    refs = refs[5:]
    if n_cast:
        f32_refs, refs = refs[:n_cast], refs[n_cast:]
        for src_ref, dst_ref in zip(f32_refs, refs[len(refs) - n_cast:]):
            dst_ref[...] = src_ref[...].astype(BF16)
    o_refs = refs
    x = _pick_half(x_refs)
    if n_y:
        y = _pick_half(y_refs).astype(BF16)
        x = x + mod_ref[5:6, :] * jnp.dot(y, wmix_ref[...], preferred_element_type=F32)
    hb = _modulated(x, mod_ref, gain_ref, j).astype(BF16)
    acc = jnp.zeros((TM, D_MODEL), F32)
    for lo, hi in zip(FF_CHUNKS[:-1], FF_CHUNKS[1:]):
        g = jnp.dot(hb, win_ref[:, lo:hi], preferred_element_type=F32)
        u = jnp.dot(hb, win_ref[:, D_FF + lo:D_FF + hi], preferred_element_type=F32)
        a = (g * _sigmoid(g) * u).astype(BF16)
        acc = acc + jnp.dot(a, wout_ref[lo:hi, :], preferred_element_type=F32)
    out = x + FFN_RES * mod_ref[3 * j + 2:3 * j + 3, :] * acc
    if final:
        out = _rms(out) * fn_ref[...]
    if split_out:
        first_half = pl.program_id(0) < HALF_TILES

        @pl.when(first_half)
        def _():
            o_refs[0][...] = out

        @pl.when(jnp.logical_not(first_half))
        def _():
            o_refs[1][...] = out
    else:
        o_refs[0][...] = out


def _row_specs(width, split):
    if not split:
        return [pl.BlockSpec((TM, width), lambda i: (i, 0))]
    return [pl.BlockSpec((TM, width), lambda i: (jnp.minimum(i, HALF_TILES - 1), 0)),
            pl.BlockSpec((TM, width), lambda i: (jnp.maximum(i - HALF_TILES, 0), 0))]


def _half_ffn(xs, mods_l, gains_l, w_in, w_out, half, final_norm, final, split_out, mix=None, casts=()):
    n_steps = N_TOK // TM
    ys, w_mix = mix if mix is not None else ((), None)
    in_specs = _row_specs(D_MODEL, len(xs) == 2)
    args = list(xs)
    if ys:
        in_specs += _row_specs(ys[0].shape[1], len(ys) == 2)
        in_specs.append(pl.BlockSpec(w_mix.shape, lambda i: (0, 0), pipeline_mode=pl.Buffered(1)))
        args += list(ys) + [w_mix]
    in_specs += [
        pl.BlockSpec((None, N_MOD, D_MODEL), lambda i: (_group_of_tile(i, TM), 0, 0)),
        pl.BlockSpec((3, D_MODEL), lambda i: (0, 0)),
        pl.BlockSpec((D_MODEL, 2 * D_FF), lambda i: (0, 0), pipeline_mode=pl.Buffered(1)),
        pl.BlockSpec((D_FF, D_MODEL), lambda i: (0, 0), pipeline_mode=pl.Buffered(1)),
        pl.BlockSpec((1, D_MODEL), lambda i: (0, 0)),
    ]
    args += [mods_l, gains_l, w_in, w_out, final_norm]
    n_rows = N_PROMPT_TOK if split_out else N_TOK
    out_specs = _row_specs(D_MODEL, split_out)
    out_shape = [jax.ShapeDtypeStruct((n_rows, D_MODEL), F32)] * (2 if split_out else 1)
    for arr, lead, cols in casts:
        rows = arr.shape[len(lead)]
        slab = rows // n_steps
        assert slab * n_steps == rows and slab % BF16_SUBLANES == 0, (rows, n_steps)
        in_specs.append(pl.BlockSpec((None,) * len(lead) + (slab, cols), lambda i, lead=lead: (*lead, i, 0)))
        args.append(arr)
        out_specs.append(pl.BlockSpec((slab, cols), lambda i: (i, 0)))
        out_shape.append(jax.ShapeDtypeStruct((rows, cols), BF16))
    return pl.pallas_call(
        functools.partial(_ffn_kernel, j=2 * half, final=final, n_x=len(xs), n_y=len(ys), split_out=split_out,
                          n_cast=len(casts)),
        grid=(n_steps,),
        in_specs=in_specs,
        out_specs=out_specs,
        out_shape=out_shape,
        compiler_params=_cparams("arbitrary"),
        name="half_ffn",
    )(*args)


def _gelu_tanh(x):
    return 0.5 * x * (1.0 + jnp.tanh(0.7978845608028654 * (x + 0.044715 * (x * x * x))))


def _rg_proj_kernel(x_ref, mod_ref, gain_ref, w_ref, gate_ref, xbr_ref):
    hb = _modulated(x_ref[...], mod_ref, gain_ref, 1).astype(BF16)
    gate_ref[...] = _gelu_tanh(jnp.dot(hb, w_ref[:, :D_RNN], preferred_element_type=F32))
    xbr_ref[...] = jnp.dot(hb, w_ref[:, D_RNN:], preferred_element_type=F32)


def _rg_proj(x, mods_l, gains_l, w_in):
    return pl.pallas_call(
        _rg_proj_kernel,
        grid=(N_TOK // TM_PROJ,),
        in_specs=[
            pl.BlockSpec((TM_PROJ, D_MODEL), lambda i: (i, 0)),
            pl.BlockSpec((None, N_MOD, D_MODEL), lambda i: (_group_of_tile(i, TM_PROJ), 0, 0)),
            pl.BlockSpec((3, D_MODEL), lambda i: (0, 0)),
            pl.BlockSpec((D_MODEL, 2 * D_RNN), lambda i: (0, 0), pipeline_mode=pl.Buffered(1)),
        ],
        out_specs=[pl.BlockSpec((TM_PROJ, D_RNN), lambda i: (i, 0))] * 2,
        out_shape=[jax.ShapeDtypeStruct((N_TOK, D_RNN), F32)] * 2,
        compiler_params=_cparams("arbitrary"),
        name="rg_in_proj",
    )(x, mods_l, gains_l, w_in)


def _seg_shift(row, down):
    v = lax.broadcasted_iota(jnp.int32, row.shape, 0)
    if down:
        return jnp.where(v >= 1, pltpu.roll(row, 1, 0), 0.0)
    return jnp.where(v < SCAN_V - 1, pltpu.roll(row, SCAN_V - 1, 0), 0.0)


def _rg_scan_kernel(xbr_ref, gate_ref, cw_ref, cb_ref, wa_ref, ba_ref, wi_ref, bi_ref, lam_ref, h0_ref,
                    y_ref, stf_ref, stb_ref, af_ref, uf_ref, ab_ref, ub_ref, pf_ref, hf_ref, pb_ref, hb_ref):
    chain = jnp.where(pl.program_id(0) >= PROMPT_BLOCKS, 1.0, 0.0).astype(F32)
    nbt = RG_CT // RG_BLOCK
    rows = SCAN_L * SCAN_V
    v = SCAN_V

    for nb in range(nbt):
        ls = slice(nb * RG_BLOCK, (nb + 1) * RG_BLOCK)
        x = xbr_ref[:, ls]
        before = chain * _seg_shift(x[rows - v:], True)
        after0 = chain * _seg_shift(x[:v], False)
        after1 = chain * _seg_shift(x[v:2 * v], False)
        xm1 = jnp.concatenate([before, x[:rows - v]], axis=0)
        xp1 = jnp.concatenate([x[v:], after0], axis=0)
        xp2 = jnp.concatenate([x[2 * v:], after0, after1], axis=0)
        xc = (cw_ref[0:1, ls] * xm1 + cw_ref[1:2, ls] * x + cw_ref[2:3, ls] * xp1 + cw_ref[3:4, ls] * xp2
              + cb_ref[0:1, ls])
        xcb = xc.astype(BF16)
        xc_r2 = xc * (2.0 ** 0.5)
        for d, (a_ref, u_ref) in enumerate(((af_ref, uf_ref), (ab_ref, ub_ref))):
            r = _sigmoid(jnp.dot(xcb, wa_ref[d, nb], preferred_element_type=F32) + ba_ref[d:d + 1, ls])
            gi = _sigmoid(jnp.dot(xcb, wi_ref[d, nb], preferred_element_type=F32) + bi_ref[d:d + 1, ls])
            decay = RG_C * _softplus(-lam_ref[d:d + 1, ls])
            a_ref[nb] = jnp.exp2(r * (decay * (-LOG2_E)))
            t = jnp.tanh(r * decay)
            prod = t * (1.0 + t)
            u_ref[nb] = jnp.where(prod > 0.0, t * lax.rsqrt(prod), 0.0) * (gi * xc_r2)

    def steps(j, carry):
        base_f = pl.multiple_of(j * (SCAN_UNROLL * v), SCAN_UNROLL * v)
        base_b = pl.multiple_of((SCAN_L - SCAN_UNROLL) * v - j * (SCAN_UNROLL * v), SCAN_UNROLL * v)
        carry = list(carry)
        for k in range(SCAN_UNROLL):
            sf = pl.ds(base_f + k * v, v)
            sb = pl.ds(base_b + (SCAN_UNROLL - 1 - k) * v, v)
            for nb in range(nbt):
                hf, pf, hb, pb = carry[4 * nb:4 * nb + 4]
                a = af_ref[nb, sf, :]
                hf = a * hf + uf_ref[nb, sf, :]
                pf = a * pf
                hf_ref[nb, sf, :] = hf
                pf_ref[nb, sf, :] = pf
                a = ab_ref[nb, sb, :]
                hb = a * hb + ub_ref[nb, sb, :]
                pb = a * pb
                hb_ref[nb, sb, :] = hb
                pb_ref[nb, sb, :] = pb
                carry[4 * nb:4 * nb + 4] = [hf, pf, hb, pb]
        return tuple(carry)

    zero = jnp.zeros((v, RG_BLOCK), F32)
    one = jnp.ones((v, RG_BLOCK), F32)
    lax.fori_loop(0, SCAN_L // SCAN_UNROLL, steps, (zero, one, zero, one) * nbt)

    for nb in range(nbt):
        ls = slice(nb * RG_BLOCK, (nb + 1) * RG_BLOCK)
        h_end, p_end = hf_ref[nb, rows - v:, :], pf_ref[nb, rows - v:, :]
        entry = [chain * h0_ref[0:1, ls]]
        for s in range(v - 1):
            entry.append(chain * (h_end[s:s + 1] + p_end[s:s + 1] * entry[s]))
        ef = jnp.concatenate(entry, axis=0)
        h_start, p_start = hb_ref[nb, :v, :], pb_ref[nb, :v, :]
        entry = [chain * h0_ref[1:2, ls]]
        for s in range(v - 1, 0, -1):
            entry.append(chain * (h_start[s:s + 1] + p_start[s:s + 1] * entry[-1]))
        eb = jnp.concatenate(entry[::-1], axis=0)
        shape3 = (SCAN_L, v, RG_BLOCK)
        hf = hf_ref[nb].reshape(shape3) + pf_ref[nb].reshape(shape3) * ef[None]
        hb = hb_ref[nb].reshape(shape3) + pb_ref[nb].reshape(shape3) * eb[None]
        stf_ref[:, ls] = hf[SCAN_L - 1]
        stb_ref[:, ls] = hb[0]
        y_ref[:, ls] = (hf + hb).reshape(rows, RG_BLOCK) * gate_ref[:, ls]


def _rg_scan(xbr, gate, conv_w, conv_b, w_a, b_a, w_i, b_i, lam, h0):
    n_blk = N_TOK // GROUP_ROWS
    nbt = RG_CT // RG_BLOCK
    tile = pl.BlockSpec((GROUP_ROWS, RG_CT), lambda b, c: (b, c))
    vec2 = pl.BlockSpec((2, RG_CT), lambda b, c: (0, c))
    wspec = pl.BlockSpec((2, nbt, RG_BLOCK, RG_BLOCK), lambda b, c: (0, c, 0, 0))
    return pl.pallas_call(
        _rg_scan_kernel,
        grid=(n_blk, D_RNN // RG_CT),
        in_specs=[
            tile, tile,
            pl.BlockSpec((CONV_W, RG_CT), lambda b, c: (0, c)),
            pl.BlockSpec((1, RG_CT), lambda b, c: (0, c)),
            wspec, vec2, wspec, vec2, vec2,
            pl.BlockSpec((None, 2, RG_CT), lambda b, c: (jnp.maximum(b - PROMPT_BLOCKS, 0), 0, c)),
        ],
        out_specs=[tile] + [pl.BlockSpec((SCAN_V, RG_CT), lambda b, c: (b, c))] * 2,
        out_shape=[jax.ShapeDtypeStruct((N_TOK, D_RNN), F32)]
        + [jax.ShapeDtypeStruct((n_blk * SCAN_V, D_RNN), F32)] * 2,
        scratch_shapes=[pltpu.VMEM((nbt, GROUP_ROWS, RG_BLOCK), F32)] * 8,
        compiler_params=_cparams("arbitrary", "arbitrary"),
        name="rg_scan",
    )(xbr, gate, conv_w, conv_b, w_a, b_a, w_i, b_i, lam, h0)


def _chunk_tri():
    ri = lax.broadcasted_iota(jnp.int32, (GLA_C, GLA_C), 0)
    ci = lax.broadcasted_iota(jnp.int32, (GLA_C, GLA_C), 1)
    return (ri >= ci).astype(BF16), (ri <= ci).astype(BF16)


def _split3(x):
    hi = x.astype(BF16)
    r1 = x - hi.astype(F32)
    mid = r1.astype(BF16)
    lo = (r1 - mid.astype(F32)).astype(BF16)
    return hi, mid, lo


def _gla_proj_kernel(x_ref, mod_ref, gain_ref, w_ref, wz_ref, w2_ref, bg_ref,
                     q_ref, k_ref, v_ref, r_ref, cum_ref):
    hb = _modulated(x_ref[...], mod_ref, gain_ref, 1).astype(BF16)
    z = jnp.dot(hb, wz_ref[...].astype(BF16), preferred_element_type=F32).astype(BF16)
    zz = jnp.dot(z, w2_ref[...], preferred_element_type=F32) + bg_ref[...]
    log_gate = -_softplus(-zz) * (LOG2_E / GLA_TAU)
    parts = _split3(log_gate)
    r = jnp.dot(hb, w_ref[:, 2 * GLA_QK + GLA_V:], preferred_element_type=F32)
    r_ref[...] = r * _sigmoid(r)
    q_ref[...] = jnp.dot(hb, w_ref[:, :GLA_QK], preferred_element_type=F32) * (GLA_DK ** -0.5)
    k_ref[...] = jnp.dot(hb, w_ref[:, GLA_QK:2 * GLA_QK], preferred_element_type=F32)
    v_ref[...] = jnp.dot(hb, w_ref[:, 2 * GLA_QK:2 * GLA_QK + GLA_V], preferred_element_type=F32).astype(BF16)
    tri_f, tri_b = _chunk_tri()
    for ch in range(TM_PROJ // GLA_C):
        rows = slice(ch * GLA_C, (ch + 1) * GLA_C)
        cum_ref[rows, :GLA_QK] = sum(jnp.dot(tri_f, p[rows, :GLA_QK], preferred_element_type=F32) for p in parts)
        cum_ref[rows, GLA_QK:] = sum(jnp.dot(tri_b, p[rows, GLA_QK:], preferred_element_type=F32) for p in parts)


def _gla_proj(x, mods_l, gains_l, w_main, w_z, w2cat, bg):
    def rows(w):
        return pl.BlockSpec((TM_PROJ, w), lambda i: (i, 0))

    def whole(a):
        return pl.BlockSpec(a.shape, lambda i: (0,) * a.ndim, pipeline_mode=pl.Buffered(1))

    widths = (GLA_QK, GLA_QK, GLA_V, GLA_V, 2 * GLA_QK)
    dtypes = (F32, F32, BF16, F32, F32)
    return pl.pallas_call(
        _gla_proj_kernel,
        grid=(N_TOK // TM_PROJ,),
        in_specs=[
            rows(D_MODEL),
            pl.BlockSpec((None, N_MOD, D_MODEL), lambda i: (_group_of_tile(i, TM_PROJ), 0, 0)),
            pl.BlockSpec((3, D_MODEL), lambda i: (0, 0)),
            whole(w_main), whole(w_z), whole(w2cat), whole(bg),
        ],
        out_specs=[rows(w) for w in widths],
        out_shape=[jax.ShapeDtypeStruct((N_TOK, w), t) for w, t in zip(widths, dtypes)],
        compiler_params=_cparams("arbitrary"),
        name="gla_in_proj",
    )(x, mods_l, gains_l, w_main, w_z, w2cat, bg)


def _dot_tn(a, b):
    return lax.dot_general(a, b, (((0,), (0,)), ((), ())), preferred_element_type=F32)


def _dot_nt(a, b):
    return lax.dot_general(a, b, (((1,), (1,)), ((), ())), preferred_element_type=F32)


def _level_ref(cum_ref, row0, cum, m, backward):
    c = GLA_C
    pick = m - 1 if backward else m

    def bcast(r, n):
        return jnp.broadcast_to(cum_ref[pl.ds(row0 + r, 1), :], (n, GLA_DK))

    if m >= 4:
        return jnp.concatenate([bcast(g * 2 * m + pick, 2 * m) for g in range(c // (2 * m))], axis=0)
    pos = lax.broadcasted_iota(jnp.int32, (c, GLA_DK), 0)
    if m == 2:
        lo = jnp.concatenate([bcast(g * 8 + pick, 8) for g in range(c // 8)], axis=0)
        hi = jnp.concatenate([bcast(g * 8 + 4 + pick, 8) for g in range(c // 8)], axis=0)
        return jnp.where((pos & 4) == 0, lo, hi)
    if backward:
        return jnp.where((pos & 1) == 1, pltpu.roll(cum, 1, 0), cum)
    return jnp.where((pos & 1) == 0, pltpu.roll(cum, c - 1, 0), cum)


def _gla_intra(q_ref, k_ref, v_ref, cf_ref, cb_ref, row0, diag, masks, uppers):
    rows = pl.ds(row0, GLA_C)
    cf, cb = cf_ref[rows, :], cb_ref[rows, :]
    qb, kb = q_ref[rows, :].astype(BF16), k_ref[rows, :].astype(BF16)
    scores = jnp.where(diag, 2.0 * _dot_nt(qb, kb), 0.0)
    for lvl, m in enumerate(GLA_LEVELS):
        df = cf - _level_ref(cf_ref, row0, cf, m, False)
        db = cb - _level_ref(cb_ref, row0, cb, m, True)
        gq = jnp.exp2(jnp.where(uppers[lvl], df, db)).astype(BF16)
        gk = jnp.exp2(-jnp.where(uppers[lvl], db, df)).astype(BF16)
        scores = jnp.where(masks[lvl], _dot_nt(qb * gq, kb * gk), scores)
    return jnp.dot(scores.astype(BF16), v_ref[rows, :], preferred_element_type=F32)


def _gla_inter(q_ref, k_ref, v_ref, cum_ref, row0, st, backward):
    rows = pl.ds(row0, GLA_C)
    cum = cum_ref[rows, :]
    tot = cum_ref[pl.ds(row0 + (0 if backward else GLA_C - 1), 1), :]
    o = _dot_nt((q_ref[rows, :] * jnp.exp2(cum)).astype(BF16), st.astype(BF16))
    kd = (k_ref[rows, :] * jnp.exp2(tot - cum)).astype(BF16)
    return o, st * jnp.exp2(tot) + _dot_tn(v_ref[rows, :], kd)


def _gla_kernel(*refs, t_len, seqs, has_s0):
    if has_s0:
        q_ref, k_ref, v_ref, cf_ref, cb_ref, r_ref, gn_ref, s0_ref, y_ref, sf_ref, sb_ref = refs
    else:
        q_ref, k_ref, v_ref, cf_ref, cb_ref, r_ref, gn_ref, y_ref, sout_ref, sf_ref, sb_ref = refs
    c = GLA_C
    n_chunks = t_len // c
    ri = lax.broadcasted_iota(jnp.int32, (c, c), 0)
    ci = lax.broadcasted_iota(jnp.int32, (c, c), 1)
    x = ri ^ ci
    masks = [(x >= m) & (x < 2 * m) for m in GLA_LEVELS]
    pos = lax.broadcasted_iota(jnp.int32, (c, GLA_DK), 0)
    uppers = [(pos & m) != 0 for m in GLA_LEVELS]

    def intra(i, carry):
        row0 = pl.multiple_of(i * c, c)
        y_ref[pl.ds(row0, c), :] = _gla_intra(q_ref, k_ref, v_ref, cf_ref, cb_ref, row0, ri == ci, masks, uppers)
        return carry

    lax.fori_loop(0, seqs * n_chunks, intra, 0, unroll=min(seqs * n_chunks, GLA_INTRA_UNROLL))

    for s in range(seqs):
        if has_s0:
            sf_ref[s] = s0_ref[s, 0].T
            sb_ref[s] = s0_ref[s, 1].T
        else:
            sf_ref[s] = jnp.zeros((GLA_DV, GLA_DK), F32)
            sb_ref[s] = jnp.zeros((GLA_DV, GLA_DK), F32)

        def inter(i, carry, s=s):
            rf = pl.multiple_of(s * t_len + i * c, c)
            rb = pl.multiple_of(s * t_len + (n_chunks - 1 - i) * c, c)
            o, st = _gla_inter(q_ref, k_ref, v_ref, cf_ref, rf, sf_ref[s], False)
            y_ref[pl.ds(rf, c), :] += o
            sf_ref[s] = st
            o, st = _gla_inter(q_ref, k_ref, v_ref, cb_ref, rb, sb_ref[s], True)
            y_ref[pl.ds(rb, c), :] += o
            sb_ref[s] = st
            return carry

        lax.fori_loop(0, n_chunks, inter, 0, unroll=min(n_chunks, GLA_INTER_UNROLL))
        if not has_s0:
            sout_ref[s, 0] = sf_ref[s].T
            sout_ref[s, 1] = sb_ref[s].T
    o = y_ref[...]
    y_ref[...] = _rms(o) * gn_ref[...] * r_ref[...]


def _gla_mix(q, k, v, cum, r, gnorm, s0, *, t_len, seqs, row0, n_seq):
    has_s0 = s0 is not None
    blk = seqs * t_len
    blk0 = row0 // blk
    assert blk0 * blk == row0 and n_seq % seqs == 0

    def rows(w, off=0):
        return pl.BlockSpec((blk, w), lambda s, h: (blk0 + s, off + h))

    state_spec = pl.BlockSpec((seqs, 2, None, GLA_DK, GLA_DV), lambda s, h: (s, 0, h, 0, 0))
    in_specs = [rows(GLA_DK), rows(GLA_DK), rows(GLA_DV), rows(GLA_DK), rows(GLA_DK, GLA_HEADS),
                rows(GLA_DV), pl.BlockSpec((None, 1, GLA_DV), lambda s, h: (h, 0, 0))]
    args = [q, k, v, cum, cum, r, gnorm.reshape(GLA_HEADS, 1, GLA_DV)]
    y_spec = pl.BlockSpec((blk, GLA_DV), lambda s, h: (s, h))
    y_shape = jax.ShapeDtypeStruct((n_seq * t_len, GLA_V), F32)
    if has_s0:
        in_specs.append(state_spec)
        args.append(s0)
        out_specs, out_shape = y_spec, y_shape
    else:
        out_specs = [y_spec, state_spec]
        out_shape = [y_shape, jax.ShapeDtypeStruct((n_seq, 2, GLA_HEADS, GLA_DK, GLA_DV), F32)]
    return pl.pallas_call(
        functools.partial(_gla_kernel, t_len=t_len, seqs=seqs, has_s0=has_s0),
        grid=(n_seq // seqs, GLA_HEADS),
        in_specs=in_specs,
        out_specs=out_specs,
        out_shape=out_shape,
        scratch_shapes=[pltpu.VMEM((seqs, GLA_DV, GLA_DK), F32)] * 2,
        compiler_params=_cparams("arbitrary", "arbitrary"),
        name="gla_mix_sample" if has_s0 else "gla_mix_prompt",
    )(*args)


def _natural_to_scan(x_half):
    x4 = x_half.reshape(-1, SCAN_V, SCAN_L, D_MODEL)
    return x4.swapaxes(1, 2).reshape(-1, D_MODEL)


def _scan_to_natural(x_half):
    x4 = x_half.reshape(-1, SCAN_L, SCAN_V, D_MODEL)
    return x4.swapaxes(1, 2).reshape(-1, D_MODEL)


def _natural_to_col_major(x_half):
    x4 = x_half.reshape(DEC_BATCH, DEC_SEQ // GRID_W, GRID_W, D_MODEL)
    return x4.swapaxes(1, 2).reshape(-1, D_MODEL)


def _col_major_to_natural(x_half):
    x4 = x_half.reshape(DEC_BATCH, GRID_W, DEC_SEQ // GRID_W, D_MODEL)
    return x4.swapaxes(1, 2).reshape(-1, D_MODEL)


def kernel(x_prompt, x_sample, c, state_rglru, state_gla, c_ctx, ada_w, ada_b, norm_g, ffn_w_in, ffn_w_out,
           rg_w_in, rg_conv_w, rg_conv_b, rg_w_a, rg_b_a, rg_w_i, rg_b_i, rg_lambda, rg_w_out,
           gla_w_in, gla_w_g2, gla_b_g, gla_norm, gla_w_out, final_norm):
    depth = ada_w.shape[0]
    cvecs = jnp.concatenate([c_ctx[None, :], c, jnp.zeros((8 - N_GROUPS, D_MODEL), F32)], axis=0)
    mods = _ada_mods(cvecs, ada_w, ada_b)
    fn = final_norm.reshape(1, D_MODEL)
    rg_states, gla_states = [], []

    w_in, w_out = ffn_w_in[0, 0].astype(BF16), ffn_w_out[0, 0].astype(BF16)
    n_main = 2 * GLA_QK + 2 * GLA_V

    def ffn_casts(layer, half):
        return ((ffn_w_in, (layer, half), 2 * D_FF), (ffn_w_out, (layer, half), D_MODEL))

    xp = x_prompt.reshape(N_PROMPT_TOK, D_MODEL)
    xs = x_sample.reshape(DEC_BATCH * DEC_SEQ, D_MODEL)
    for i in range(depth):
        mods_l, gains_l = mods[i], norm_g[i]
        j = i // 2
        last = i == depth - 1
        if i % 2 == 0:
            halves = (_natural_to_scan(xp), _natural_to_scan(xs))
            mixer_casts = ((rg_w_in, (j,), 2 * D_RNN), (rg_w_out, (j,), D_MODEL))
        else:
            halves = (xp, _natural_to_col_major(xs))
            mixer_casts = ((gla_w_in, (j,), n_main), (gla_w_out, (j,), D_MODEL))
        x, w_in, w_out, wm_in, wm_out = _half_ffn(halves, mods_l, gains_l, w_in, w_out, 0, fn, False, False,
                                                  casts=ffn_casts(i, 1) + mixer_casts)
        if i % 2 == 0:
            gate, xbr = _rg_proj(x, mods_l, gains_l, wm_in)
            y, st_f, st_b = _rg_scan(xbr, gate, rg_conv_w[j], rg_conv_b[j].reshape(1, D_RNN),
                                     rg_w_a[j].astype(BF16), rg_b_a[j], rg_w_i[j].astype(BF16), rg_b_i[j],
                                     rg_lambda[j], state_rglru[:, j])
            rg_states.append(jnp.stack([st_f[:BATCH], st_b[:BATCH]], axis=1))
            ys = (y,)
        else:
            w_z = jnp.pad(gla_w_in[j, :, n_main:], ((0, 0), (0, 128 - 2 * GLA_RANK)))
            w2cat = jnp.zeros((128, 2 * GLA_QK), F32)
            w2cat = w2cat.at[:GLA_RANK, :GLA_QK].set(gla_w_g2[j, 0])
            w2cat = w2cat.at[GLA_RANK:2 * GLA_RANK, GLA_QK:].set(gla_w_g2[j, 1]).astype(BF16)
            q, k, v, r, cum = _gla_proj(x, mods_l, gains_l, wm_in, w_z, w2cat, gla_b_g[j].reshape(1, 2 * GLA_QK))
            y_p, st = _gla_mix(q, k, v, cum, r, gla_norm[j], None, t_len=SEQ, seqs=GLA_PROMPT_SEQS, row0=0,
                               n_seq=BATCH)
            y_s = _gla_mix(q, k, v, cum, r, gla_norm[j], state_gla[:, j], t_len=DEC_SEQ, seqs=1,
                           row0=N_PROMPT_TOK, n_seq=DEC_BATCH)
            gla_states.append(st)
            ys = (y_p, y_s)
        xp, xs, *next_pair = _half_ffn((x,), mods_l, gains_l, w_in, w_out, 1, fn, last, True, mix=(ys, wm_out),
                                       casts=() if last else ffn_casts(i + 1, 0))
        if not last:
            w_in, w_out = next_pair
        if i % 2 == 0:
            xp, xs = _scan_to_natural(xp), _scan_to_natural(xs)
        else:
            xs = _col_major_to_natural(xs)

    y_prompt = xp.reshape(BATCH, SEQ, D_MODEL)
    y_sample = xs.reshape(DEC_BATCH, DEC_SEQ, D_MODEL)
    return (y_prompt, y_sample, jnp.stack(rg_states, axis=1), jnp.stack(gla_states, axis=1))
```

```python
import functools

import jax
import jax.numpy as jnp
from jax import lax
from jax.experimental import pallas as pl
from jax.experimental.pallas import tpu as pltpu

F32 = jnp.float32
BF16 = jnp.bfloat16

D_MODEL = 1024
BATCH = 16
SEQ = 256
DEC_BATCH = 2
DEC_SEQ = 2048
GRID_W = 64
EPS = 1e-6
N_MOD = 9
D_FF = 2816
FFN_RES = 0.5
D_RNN = 1280
RG_BLOCK = 128
CONV_W = 4
RG_C = 8.0
GLA_HEADS = 4
GLA_QK = 512
GLA_V = 1024
GLA_DK = 128
GLA_DV = 256
GLA_RANK = 16
GLA_TAU = 16.0
LOG2_E = 1.4426950408889634

N_PROMPT_TOK = BATCH * SEQ
N_TOK = N_PROMPT_TOK + DEC_BATCH * DEC_SEQ
N_GROUPS = 1 + DEC_BATCH
GROUP_ROWS = DEC_SEQ
PROMPT_BLOCKS = N_PROMPT_TOK // GROUP_ROWS

VMEM_LIMIT_BYTES = 56 * 1024 * 1024

TM = 512
HALF_TILES = N_PROMPT_TOK // TM
TM_PROJ = 1024
ADA_SLAB = 256
MXU_DIM = 256
BF16_SUBLANES = 16
FF_CHUNKS = (0, 6 * MXU_DIM, D_FF)
RG_CT = 256
SCAN_L = 256
SCAN_V = GROUP_ROWS // SCAN_L
SCAN_UNROLL = 8
GLA_C = 128
GLA_PROMPT_SEQS = 8
GLA_INTRA_UNROLL = 8
GLA_INTER_UNROLL = 16
GLA_LEVELS = tuple(GLA_C >> (l + 1) for l in range(GLA_C.bit_length() - 1))


def _cparams(*sem):
    return pltpu.CompilerParams(dimension_semantics=sem, vmem_limit_bytes=VMEM_LIMIT_BYTES)


def _group_of_tile(i, rows):
    return jnp.maximum((i * rows) // GROUP_ROWS - (PROMPT_BLOCKS - 1), 0)


def _sigmoid(x):
    return 1.0 / (1.0 + jnp.exp2(x * (-LOG2_E)))


def _softplus(x):
    return jnp.maximum(x, 0.0) + jnp.log1p(jnp.exp(-jnp.abs(x)))


def _rms(x):
    return x * lax.rsqrt(jnp.mean(x * x, axis=-1, keepdims=True) + EPS)


def _modulated(x, mod_ref, gain_ref, j):
    y = _rms(x) * gain_ref[j:j + 1, :]
    return y * (1.0 + mod_ref[3 * j + 1:3 * j + 2, :]) + mod_ref[3 * j:3 * j + 1, :]


def _ada_kernel(c_ref, w_ref, b_ref, o_ref):
    @pl.when(pl.program_id(1) == 0)
    def _():
        o_ref[...] = jnp.broadcast_to(b_ref[...], o_ref.shape)

    c = c_ref[...]
    s = (c * _sigmoid(c)).astype(BF16)
    o_ref[...] += jnp.dot(s, w_ref[...].astype(BF16), preferred_element_type=F32)


def _ada_mods(cvecs, ada_w, ada_b):
    depth = ada_w.shape[0]
    n_slab = D_MODEL // ADA_SLAB
    c_slabs = cvecs.reshape(8, n_slab, ADA_SLAB).swapaxes(0, 1)
    out = pl.pallas_call(
        _ada_kernel,
        grid=(depth, n_slab),
        in_specs=[
            pl.BlockSpec((None, 8, ADA_SLAB), lambda l, k: (k, 0, 0)),
            pl.BlockSpec((None, ADA_SLAB, N_MOD * D_MODEL), lambda l, k: (l, k, 0)),
            pl.BlockSpec((None, 1, N_MOD * D_MODEL), lambda l, k: (l, 0, 0)),
        ],
        out_specs=pl.BlockSpec((None, 8, N_MOD * D_MODEL), lambda l, k: (l, 0, 0)),
        out_shape=jax.ShapeDtypeStruct((depth, 8, N_MOD * D_MODEL), F32),
        compiler_params=_cparams("arbitrary", "arbitrary"),
        name="ada_mod",
    )(c_slabs, ada_w, ada_b.reshape(depth, 1, N_MOD * D_MODEL))
    return out[:, :N_GROUPS, :].reshape(depth, N_GROUPS, N_MOD, D_MODEL)


def _pick_half(refs):
    if len(refs) == 1:
        return refs[0][...]
    return jnp.where(pl.program_id(0) < HALF_TILES, refs[0][...], refs[1][...])


def _ffn_kernel(*refs, j, final, n_x, n_y, split_out, n_cast):
    x_refs, y_refs, refs = refs[:n_x], refs[n_x:n_x + n_y], refs[n_x + n_y:]
    if n_y:
        wmix_ref, refs = refs[0], refs[1:]
    mod_ref, gain_ref, win_ref, wout_ref, fn_ref = refs[:5]
    refs = refs[5:]
    if n_cast:
        f32_refs, refs = refs[:n_cast], refs[n_cast:]
        for src_ref, dst_ref in zip(f32_refs, refs[len(refs) - n_cast:]):
            dst_ref[...] = src_ref[...].astype(BF16)
    o_refs = refs
    x = _pick_half(x_refs)
    if n_y:
        y = _pick_half(y_refs).astype(BF16)
        x = x + mod_ref[5:6, :] * jnp.dot(y, wmix_ref[...], preferred_element_type=F32)
    hb = _modulated(x, mod_ref, gain_ref, j).astype(BF16)
    acc = jnp.zeros((TM, D_MODEL), F32)
    for lo, hi in zip(FF_CHUNKS[:-1], FF_CHUNKS[1:]):
        g = jnp.dot(hb, win_ref[:, lo:hi], preferred_element_type=F32)
        u = jnp.dot(hb, win_ref[:, D_FF + lo:D_FF + hi], preferred_element_type=F32)
        a = (g * _sigmoid(g) * u).astype(BF16)
        acc = acc + jnp.dot(a, wout_ref[lo:hi, :], preferred_element_type=F32)
    out = x + FFN_RES * mod_ref[3 * j + 2:3 * j + 3, :] * acc
    if final:
        out = _rms(out) * fn_ref[...]
    if split_out:
        first_half = pl.program_id(0) < HALF_TILES

        @pl.when(first_half)
        def _():
            o_refs[0][...] = out

        @pl.when(jnp.logical_not(first_half))
        def _():
            o_refs[1][...] = out
    else:
        o_refs[0][...] = out


def _row_specs(width, split):
    if not split:
        return [pl.BlockSpec((TM, width), lambda i: (i, 0))]
    return [pl.BlockSpec((TM, width), lambda i: (jnp.minimum(i, HALF_TILES - 1), 0)),
            pl.BlockSpec((TM, width), lambda i: (jnp.maximum(i - HALF_TILES, 0), 0))]


def _half_ffn(xs, mods_l, gains_l, w_in, w_out, half, final_norm, final, split_out, mix=None, casts=()):
    n_steps = N_TOK // TM
    ys, w_mix = mix if mix is not None else ((), None)
    in_specs = _row_specs(D_MODEL, len(xs) == 2)
    args = list(xs)
    if ys:
        in_specs += _row_specs(ys[0].shape[1], len(ys) == 2)
        in_specs.append(pl.BlockSpec(w_mix.shape, lambda i: (0, 0), pipeline_mode=pl.Buffered(1)))
        args += list(ys) + [w_mix]
    in_specs += [
        pl.BlockSpec((None, N_MOD, D_MODEL), lambda i: (_group_of_tile(i, TM), 0, 0)),
        pl.BlockSpec((3, D_MODEL), lambda i: (0, 0)),
        pl.BlockSpec((D_MODEL, 2 * D_FF), lambda i: (0, 0), pipeline_mode=pl.Buffered(1)),
        pl.BlockSpec((D_FF, D_MODEL), lambda i: (0, 0), pipeline_mode=pl.Buffered(1)),
        pl.BlockSpec((1, D_MODEL), lambda i: (0, 0)),
    ]
    args += [mods_l, gains_l, w_in, w_out, final_norm]
    n_rows = N_PROMPT_TOK if split_out else N_TOK
    out_specs = _row_specs(D_MODEL, split_out)
    out_shape = [jax.ShapeDtypeStruct((n_rows, D_MODEL), F32)] * (2 if split_out else 1)
    for arr, lead, cols in casts:
        rows = arr.shape[len(lead)]
        slab = rows // n_steps
        assert slab * n_steps == rows and slab % BF16_SUBLANES == 0, (rows, n_steps)
        in_specs.append(pl.BlockSpec((None,) * len(lead) + (slab, cols), lambda i, lead=lead: (*lead, i, 0)))
        args.append(arr)
        out_specs.append(pl.BlockSpec((slab, cols), lambda i: (i, 0)))
        out_shape.append(jax.ShapeDtypeStruct((rows, cols), BF16))
    return pl.pallas_call(
        functools.partial(_ffn_kernel, j=2 * half, final=final, n_x=len(xs), n_y=len(ys), split_out=split_out,
                          n_cast=len(casts)),
        grid=(n_steps,),
        in_specs=in_specs,
        out_specs=out_specs,
        out_shape=out_shape,
        compiler_params=_cparams("arbitrary"),
        name="half_ffn",
    )(*args)


def _gelu_tanh(x):
    return 0.5 * x * (1.0 + jnp.tanh(0.7978845608028654 * (x + 0.044715 * (x * x * x))))


def _rg_proj_kernel(x_ref, mod_ref, gain_ref, w_ref, gate_ref, xbr_ref):
    hb = _modulated(x_ref[...], mod_ref, gain_ref, 1).astype(BF16)
    gate_ref[...] = _gelu_tanh(jnp.dot(hb, w_ref[:, :D_RNN], preferred_element_type=F32))
    xbr_ref[...] = jnp.dot(hb, w_ref[:, D_RNN:], preferred_element_type=F32)


def _rg_proj(x, mods_l, gains_l, w_in):
    return pl.pallas_call(
        _rg_proj_kernel,
        grid=(N_TOK // TM_PROJ,),
        in_specs=[
            pl.BlockSpec((TM_PROJ, D_MODEL), lambda i: (i, 0)),
            pl.BlockSpec((None, N_MOD, D_MODEL), lambda i: (_group_of_tile(i, TM_PROJ), 0, 0)),
            pl.BlockSpec((3, D_MODEL), lambda i: (0, 0)),
            pl.BlockSpec((D_MODEL, 2 * D_RNN), lambda i: (0, 0), pipeline_mode=pl.Buffered(1)),
        ],
        out_specs=[pl.BlockSpec((TM_PROJ, D_RNN), lambda i: (i, 0))] * 2,
        out_shape=[jax.ShapeDtypeStruct((N_TOK, D_RNN), F32)] * 2,
        compiler_params=_cparams("arbitrary"),
        name="rg_in_proj",
    )(x, mods_l, gains_l, w_in)


def _seg_shift(row, down):
    v = lax.broadcasted_iota(jnp.int32, row.shape, 0)
    if down:
        return jnp.where(v >= 1, pltpu.roll(row, 1, 0), 0.0)
    return jnp.where(v < SCAN_V - 1, pltpu.roll(row, SCAN_V - 1, 0), 0.0)


def _rg_scan_kernel(xbr_ref, gate_ref, cw_ref, cb_ref, wa_ref, ba_ref, wi_ref, bi_ref, lam_ref, h0_ref,
                    y_ref, stf_ref, stb_ref, af_ref, uf_ref, ab_ref, ub_ref, pf_ref, hf_ref, pb_ref, hb_ref):
    chain = jnp.where(pl.program_id(0) >= PROMPT_BLOCKS, 1.0, 0.0).astype(F32)
    nbt = RG_CT // RG_BLOCK
    rows = SCAN_L * SCAN_V
    v = SCAN_V

    for nb in range(nbt):
        ls = slice(nb * RG_BLOCK, (nb + 1) * RG_BLOCK)
        x = xbr_ref[:, ls]
        before = chain * _seg_shift(x[rows - v:], True)
        after0 = chain * _seg_shift(x[:v], False)
        after1 = chain * _seg_shift(x[v:2 * v], False)
        xm1 = jnp.concatenate([before, x[:rows - v]], axis=0)
        xp1 = jnp.concatenate([x[v:], after0], axis=0)
        xp2 = jnp.concatenate([x[2 * v:], after0, after1], axis=0)
        xc = (cw_ref[0:1, ls] * xm1 + cw_ref[1:2, ls] * x + cw_ref[2:3, ls] * xp1 + cw_ref[3:4, ls] * xp2
              + cb_ref[0:1, ls])
        xcb = xc.astype(BF16)
        xc_r2 = xc * (2.0 ** 0.5)
        for d, (a_ref, u_ref) in enumerate(((af_ref, uf_ref), (ab_ref, ub_ref))):
            r = _sigmoid(jnp.dot(xcb, wa_ref[d, nb], preferred_element_type=F32) + ba_ref[d:d + 1, ls])
            gi = _sigmoid(jnp.dot(xcb, wi_ref[d, nb], preferred_element_type=F32) + bi_ref[d:d + 1, ls])
            decay = RG_C * _softplus(-lam_ref[d:d + 1, ls])
            a_ref[nb] = jnp.exp2(r * (decay * (-LOG2_E)))
            t = jnp.tanh(r * decay)
            prod = t * (1.0 + t)
            u_ref[nb] = jnp.where(prod > 0.0, t * lax.rsqrt(prod), 0.0) * (gi * xc_r2)

    def steps(j, carry):
        base_f = pl.multiple_of(j * (SCAN_UNROLL * v), SCAN_UNROLL * v)
        base_b = pl.multiple_of((SCAN_L - SCAN_UNROLL) * v - j * (SCAN_UNROLL * v), SCAN_UNROLL * v)
        carry = list(carry)
        for k in range(SCAN_UNROLL):
            sf = pl.ds(base_f + k * v, v)
            sb = pl.ds(base_b + (SCAN_UNROLL - 1 - k) * v, v)
            for nb in range(nbt):
                hf, pf, hb, pb = carry[4 * nb:4 * nb + 4]
                a = af_ref[nb, sf, :]
                hf = a * hf + uf_ref[nb, sf, :]
                pf = a * pf
                hf_ref[nb, sf, :] = hf
                pf_ref[nb, sf, :] = pf
                a = ab_ref[nb, sb, :]
                hb = a * hb + ub_ref[nb, sb, :]
                pb = a * pb
                hb_ref[nb, sb, :] = hb
                pb_ref[nb, sb, :] = pb
                carry[4 * nb:4 * nb + 4] = [hf, pf, hb, pb]
        return tuple(carry)

    zero = jnp.zeros((v, RG_BLOCK), F32)
    one = jnp.ones((v, RG_BLOCK), F32)
    lax.fori_loop(0, SCAN_L // SCAN_UNROLL, steps, (zero, one, zero, one) * nbt)

    for nb in range(nbt):
        ls = slice(nb * RG_BLOCK, (nb + 1) * RG_BLOCK)
        h_end, p_end = hf_ref[nb, rows - v:, :], pf_ref[nb, rows - v:, :]
        entry = [chain * h0_ref[0:1, ls]]
        for s in range(v - 1):
            entry.append(chain * (h_end[s:s + 1] + p_end[s:s + 1] * entry[s]))
        ef = jnp.concatenate(entry, axis=0)
        h_start, p_start = hb_ref[nb, :v, :], pb_ref[nb, :v, :]
        entry = [chain * h0_ref[1:2, ls]]
        for s in range(v - 1, 0, -1):
            entry.append(chain * (h_start[s:s + 1] + p_start[s:s + 1] * entry[-1]))
        eb = jnp.concatenate(entry[::-1], axis=0)
        shape3 = (SCAN_L, v, RG_BLOCK)
        hf = hf_ref[nb].reshape(shape3) + pf_ref[nb].reshape(shape3) * ef[None]
        hb = hb_ref[nb].reshape(shape3) + pb_ref[nb].reshape(shape3) * eb[None]
        stf_ref[:, ls] = hf[SCAN_L - 1]
        stb_ref[:, ls] = hb[0]
        y_ref[:, ls] = (hf + hb).reshape(rows, RG_BLOCK) * gate_ref[:, ls]


def _rg_scan(xbr, gate, conv_w, conv_b, w_a, b_a, w_i, b_i, lam, h0):
    n_blk = N_TOK // GROUP_ROWS
    nbt = RG_CT // RG_BLOCK
    tile = pl.BlockSpec((GROUP_ROWS, RG_CT), lambda b, c: (b, c))
    vec2 = pl.BlockSpec((2, RG_CT), lambda b, c: (0, c))
    wspec = pl.BlockSpec((2, nbt, RG_BLOCK, RG_BLOCK), lambda b, c: (0, c, 0, 0))
    return pl.pallas_call(
        _rg_scan_kernel,
        grid=(n_blk, D_RNN // RG_CT),
        in_specs=[
            tile, tile,
            pl.BlockSpec((CONV_W, RG_CT), lambda b, c: (0, c)),
            pl.BlockSpec((1, RG_CT), lambda b, c: (0, c)),
            wspec, vec2, wspec, vec2, vec2,
            pl.BlockSpec((None, 2, RG_CT), lambda b, c: (jnp.maximum(b - PROMPT_BLOCKS, 0), 0, c)),
        ],
        out_specs=[tile] + [pl.BlockSpec((SCAN_V, RG_CT), lambda b, c: (b, c))] * 2,
        out_shape=[jax.ShapeDtypeStruct((N_TOK, D_RNN), F32)]
        + [jax.ShapeDtypeStruct((n_blk * SCAN_V, D_RNN), F32)] * 2,
        scratch_shapes=[pltpu.VMEM((nbt, GROUP_ROWS, RG_BLOCK), F32)] * 8,
        compiler_params=_cparams("arbitrary", "arbitrary"),
        name="rg_scan",
    )(xbr, gate, conv_w, conv_b, w_a, b_a, w_i, b_i, lam, h0)


def _chunk_tri():
    ri = lax.broadcasted_iota(jnp.int32, (GLA_C, GLA_C), 0)
    ci = lax.broadcasted_iota(jnp.int32, (GLA_C, GLA_C), 1)
    return (ri >= ci).astype(BF16), (ri <= ci).astype(BF16)


def _split3(x):
    hi = x.astype(BF16)
    r1 = x - hi.astype(F32)
    mid = r1.astype(BF16)
    lo = (r1 - mid.astype(F32)).astype(BF16)
    return hi, mid, lo


def _gla_proj_kernel(x_ref, mod_ref, gain_ref, w_ref, wz_ref, w2_ref, bg_ref,
                     q_ref, k_ref, v_ref, r_ref, cum_ref):
    hb = _modulated(x_ref[...], mod_ref, gain_ref, 1).astype(BF16)
    z = jnp.dot(hb, wz_ref[...].astype(BF16), preferred_element_type=F32).astype(BF16)
    zz = jnp.dot(z, w2_ref[...], preferred_element_type=F32) + bg_ref[...]
    log_gate = -_softplus(-zz) * (LOG2_E / GLA_TAU)
    parts = _split3(log_gate)
    r = jnp.dot(hb, w_ref[:, 2 * GLA_QK + GLA_V:], preferred_element_type=F32)
    r_ref[...] = r * _sigmoid(r)
    q_ref[...] = jnp.dot(hb, w_ref[:, :GLA_QK], preferred_element_type=F32) * (GLA_DK ** -0.5)
    k_ref[...] = jnp.dot(hb, w_ref[:, GLA_QK:2 * GLA_QK], preferred_element_type=F32)
    v_ref[...] = jnp.dot(hb, w_ref[:, 2 * GLA_QK:2 * GLA_QK + GLA_V], preferred_element_type=F32).astype(BF16)
    tri_f, tri_b = _chunk_tri()
    for ch in range(TM_PROJ // GLA_C):
        rows = slice(ch * GLA_C, (ch + 1) * GLA_C)
        cum_ref[rows, :GLA_QK] = sum(jnp.dot(tri_f, p[rows, :GLA_QK], preferred_element_type=F32) for p in parts)
        cum_ref[rows, GLA_QK:] = sum(jnp.dot(tri_b, p[rows, GLA_QK:], preferred_element_type=F32) for p in parts)


def _gla_proj(x, mods_l, gains_l, w_main, w_z, w2cat, bg):
    def rows(w):
        return pl.BlockSpec((TM_PROJ, w), lambda i: (i, 0))

    def whole(a):
        return pl.BlockSpec(a.shape, lambda i: (0,) * a.ndim, pipeline_mode=pl.Buffered(1))

    widths = (GLA_QK, GLA_QK, GLA_V, GLA_V, 2 * GLA_QK)
    dtypes = (F32, F32, BF16, F32, F32)
    return pl.pallas_call(
        _gla_proj_kernel,
        grid=(N_TOK // TM_PROJ,),
        in_specs=[
            rows(D_MODEL),
            pl.BlockSpec((None, N_MOD, D_MODEL), lambda i: (_group_of_tile(i, TM_PROJ), 0, 0)),
            pl.BlockSpec((3, D_MODEL), lambda i: (0, 0)),
            whole(w_main), whole(w_z), whole(w2cat), whole(bg),
        ],
        out_specs=[rows(w) for w in widths],
        out_shape=[jax.ShapeDtypeStruct((N_TOK, w), t) for w, t in zip(widths, dtypes)],
        compiler_params=_cparams("arbitrary"),
        name="gla_in_proj",
    )(x, mods_l, gains_l, w_main, w_z, w2cat, bg)


def _dot_tn(a, b):
    return lax.dot_general(a, b, (((0,), (0,)), ((), ())), preferred_element_type=F32)


def _dot_nt(a, b):
    return lax.dot_general(a, b, (((1,), (1,)), ((), ())), preferred_element_type=F32)


def _level_ref(cum_ref, row0, cum, m, backward):
    c = GLA_C
    pick = m - 1 if backward else m

    def bcast(r, n):
        return jnp.broadcast_to(cum_ref[pl.ds(row0 + r, 1), :], (n, GLA_DK))

    if m >= 4:
        return jnp.concatenate([bcast(g * 2 * m + pick, 2 * m) for g in range(c // (2 * m))], axis=0)
    pos = lax.broadcasted_iota(jnp.int32, (c, GLA_DK), 0)
    if m == 2:
        lo = jnp.concatenate([bcast(g * 8 + pick, 8) for g in range(c // 8)], axis=0)
        hi = jnp.concatenate([bcast(g * 8 + 4 + pick, 8) for g in range(c // 8)], axis=0)
        return jnp.where((pos & 4) == 0, lo, hi)
    if backward:
        return jnp.where((pos & 1) == 1, pltpu.roll(cum, 1, 0), cum)
    return jnp.where((pos & 1) == 0, pltpu.roll(cum, c - 1, 0), cum)


def _gla_intra(q_ref, k_ref, v_ref, cf_ref, cb_ref, row0, diag, masks, uppers):
    rows = pl.ds(row0, GLA_C)
    cf, cb = cf_ref[rows, :], cb_ref[rows, :]
    qb, kb = q_ref[rows, :].astype(BF16), k_ref[rows, :].astype(BF16)
    scores = jnp.where(diag, 2.0 * _dot_nt(qb, kb), 0.0)
    for lvl, m in enumerate(GLA_LEVELS):
        df = cf - _level_ref(cf_ref, row0, cf, m, False)
        db = cb - _level_ref(cb_ref, row0, cb, m, True)
        gq = jnp.exp2(jnp.where(uppers[lvl], df, db)).astype(BF16)
        gk = jnp.exp2(-jnp.where(uppers[lvl], db, df)).astype(BF16)
        scores = jnp.where(masks[lvl], _dot_nt(qb * gq, kb * gk), scores)
    return jnp.dot(scores.astype(BF16), v_ref[rows, :], preferred_element_type=F32)


def _gla_inter(q_ref, k_ref, v_ref, cum_ref, row0, st, backward):
    rows = pl.ds(row0, GLA_C)
    cum = cum_ref[rows, :]
    tot = cum_ref[pl.ds(row0 + (0 if backward else GLA_C - 1), 1), :]
    o = _dot_nt((q_ref[rows, :] * jnp.exp2(cum)).astype(BF16), st.astype(BF16))
    kd = (k_ref[rows, :] * jnp.exp2(tot - cum)).astype(BF16)
    return o, st * jnp.exp2(tot) + _dot_tn(v_ref[rows, :], kd)


def _gla_kernel(*refs, t_len, seqs, has_s0):
    if has_s0:
        q_ref, k_ref, v_ref, cf_ref, cb_ref, r_ref, gn_ref, s0_ref, y_ref, sf_ref, sb_ref = refs
    else:
        q_ref, k_ref, v_ref, cf_ref, cb_ref, r_ref, gn_ref, y_ref, sout_ref, sf_ref, sb_ref = refs
    c = GLA_C
    n_chunks = t_len // c
    ri = lax.broadcasted_iota(jnp.int32, (c, c), 0)
    ci = lax.broadcasted_iota(jnp.int32, (c, c), 1)
    x = ri ^ ci
    masks = [(x >= m) & (x < 2 * m) for m in GLA_LEVELS]
    pos = lax.broadcasted_iota(jnp.int32, (c, GLA_DK), 0)
    uppers = [(pos & m) != 0 for m in GLA_LEVELS]

    def intra(i, carry):
        row0 = pl.multiple_of(i * c, c)
        y_ref[pl.ds(row0, c), :] = _gla_intra(q_ref, k_ref, v_ref, cf_ref, cb_ref, row0, ri == ci, masks, uppers)
        return carry

    lax.fori_loop(0, seqs * n_chunks, intra, 0, unroll=min(seqs * n_chunks, GLA_INTRA_UNROLL))

    for s in range(seqs):
        if has_s0:
            sf_ref[s] = s0_ref[s, 0].T
            sb_ref[s] = s0_ref[s, 1].T
        else:
            sf_ref[s] = jnp.zeros((GLA_DV, GLA_DK), F32)
            sb_ref[s] = jnp.zeros((GLA_DV, GLA_DK), F32)

        def inter(i, carry, s=s):
            rf = pl.multiple_of(s * t_len + i * c, c)
            rb = pl.multiple_of(s * t_len + (n_chunks - 1 - i) * c, c)
            o, st = _gla_inter(q_ref, k_ref, v_ref, cf_ref, rf, sf_ref[s], False)
            y_ref[pl.ds(rf, c), :] += o
            sf_ref[s] = st
            o, st = _gla_inter(q_ref, k_ref, v_ref, cb_ref, rb, sb_ref[s], True)
            y_ref[pl.ds(rb, c), :] += o
            sb_ref[s] = st
            return carry

        lax.fori_loop(0, n_chunks, inter, 0, unroll=min(n_chunks, GLA_INTER_UNROLL))
        if not has_s0:
            sout_ref[s, 0] = sf_ref[s].T
            sout_ref[s, 1] = sb_ref[s].T
    o = y_ref[...]
    y_ref[...] = _rms(o) * gn_ref[...] * r_ref[...]


def _gla_mix(q, k, v, cum, r, gnorm, s0, *, t_len, seqs, row0, n_seq):
    has_s0 = s0 is not None
    blk = seqs * t_len
    blk0 = row0 // blk
    assert blk0 * blk == row0 and n_seq % seqs == 0

    def rows(w, off=0):
        return pl.BlockSpec((blk, w), lambda s, h: (blk0 + s, off + h))

    state_spec = pl.BlockSpec((seqs, 2, None, GLA_DK, GLA_DV), lambda s, h: (s, 0, h, 0, 0))
    in_specs = [rows(GLA_DK), rows(GLA_DK), rows(GLA_DV), rows(GLA_DK), rows(GLA_DK, GLA_HEADS),
                rows(GLA_DV), pl.BlockSpec((None, 1, GLA_DV), lambda s, h: (h, 0, 0))]
    args = [q, k, v, cum, cum, r, gnorm.reshape(GLA_HEADS, 1, GLA_DV)]
    y_spec = pl.BlockSpec((blk, GLA_DV), lambda s, h: (s, h))
    y_shape = jax.ShapeDtypeStruct((n_seq * t_len, GLA_V), F32)
    if has_s0:
        in_specs.append(state_spec)
        args.append(s0)
        out_specs, out_shape = y_spec, y_shape
    else:
        out_specs = [y_spec, state_spec]
        out_shape = [y_shape, jax.ShapeDtypeStruct((n_seq, 2, GLA_HEADS, GLA_DK, GLA_DV), F32)]
    return pl.pallas_call(
        functools.partial(_gla_kernel, t_len=t_len, seqs=seqs, has_s0=has_s0),
        grid=(n_seq // seqs, GLA_HEADS),
        in_specs=in_specs,
        out_specs=out_specs,
        out_shape=out_shape,
        scratch_shapes=[pltpu.VMEM((seqs, GLA_DV, GLA_DK), F32)] * 2,
        compiler_params=_cparams("arbitrary", "arbitrary"),
        name="gla_mix_sample" if has_s0 else "gla_mix_prompt",
    )(*args)


def _natural_to_scan(x_half):
    x4 = x_half.reshape(-1, SCAN_V, SCAN_L, D_MODEL)
    return x4.swapaxes(1, 2).reshape(-1, D_MODEL)


def _scan_to_natural(x_half):
    x4 = x_half.reshape(-1, SCAN_L, SCAN_V, D_MODEL)
    return x4.swapaxes(1, 2).reshape(-1, D_MODEL)


def _natural_to_col_major(x_half):
    x4 = x_half.reshape(DEC_BATCH, DEC_SEQ // GRID_W, GRID_W, D_MODEL)
    return x4.swapaxes(1, 2).reshape(-1, D_MODEL)


def _col_major_to_natural(x_half):
    x4 = x_half.reshape(DEC_BATCH, GRID_W, DEC_SEQ // GRID_W, D_MODEL)
    return x4.swapaxes(1, 2).reshape(-1, D_MODEL)


def kernel(x_prompt, x_sample, c, state_rglru, state_gla, c_ctx, ada_w, ada_b, norm_g, ffn_w_in, ffn_w_out,
           rg_w_in, rg_conv_w, rg_conv_b, rg_w_a, rg_b_a, rg_w_i, rg_b_i, rg_lambda, rg_w_out,
           gla_w_in, gla_w_g2, gla_b_g, gla_norm, gla_w_out, final_norm):
    depth = ada_w.shape[0]
    cvecs = jnp.concatenate([c_ctx[None, :], c, jnp.zeros((8 - N_GROUPS, D_MODEL), F32)], axis=0)
    mods = _ada_mods(cvecs, ada_w, ada_b)
    fn = final_norm.reshape(1, D_MODEL)
    rg_states, gla_states = [], []

    w_in, w_out = ffn_w_in[0, 0].astype(BF16), ffn_w_out[0, 0].astype(BF16)
    n_main = 2 * GLA_QK + 2 * GLA_V

    def ffn_casts(layer, half):
        return ((ffn_w_in, (layer, half), 2 * D_FF), (ffn_w_out, (layer, half), D_MODEL))

    xp = x_prompt.reshape(N_PROMPT_TOK, D_MODEL)
    xs = x_sample.reshape(DEC_BATCH * DEC_SEQ, D_MODEL)
    for i in range(depth):
        mods_l, gains_l = mods[i], norm_g[i]
        j = i // 2
        last = i == depth - 1
        if i % 2 == 0:
            halves = (_natural_to_scan(xp), _natural_to_scan(xs))
            mixer_casts = ((rg_w_in, (j,), 2 * D_RNN), (rg_w_out, (j,), D_MODEL))
        else:
            halves = (xp, _natural_to_col_major(xs))
            mixer_casts = ((gla_w_in, (j,), n_main), (gla_w_out, (j,), D_MODEL))
        x, w_in, w_out, wm_in, wm_out = _half_ffn(halves, mods_l, gains_l, w_in, w_out, 0, fn, False, False,
                                                  casts=ffn_casts(i, 1) + mixer_casts)
        if i % 2 == 0:
            gate, xbr = _rg_proj(x, mods_l, gains_l, wm_in)
            y, st_f, st_b = _rg_scan(xbr, gate, rg_conv_w[j], rg_conv_b[j].reshape(1, D_RNN),
                                     rg_w_a[j].astype(BF16), rg_b_a[j], rg_w_i[j].astype(BF16), rg_b_i[j],
                                     rg_lambda[j], state_rglru[:, j])
            rg_states.append(jnp.stack([st_f[:BATCH], st_b[:BATCH]], axis=1))
            ys = (y,)
        else:
            w_z = jnp.pad(gla_w_in[j, :, n_main:], ((0, 0), (0, 128 - 2 * GLA_RANK)))
            w2cat = jnp.zeros((128, 2 * GLA_QK), F32)
            w2cat = w2cat.at[:GLA_RANK, :GLA_QK].set(gla_w_g2[j, 0])
            w2cat = w2cat.at[GLA_RANK:2 * GLA_RANK, GLA_QK:].set(gla_w_g2[j, 1]).astype(BF16)
            q, k, v, r, cum = _gla_proj(x, mods_l, gains_l, wm_in, w_z, w2cat, gla_b_g[j].reshape(1, 2 * GLA_QK))
            y_p, st = _gla_mix(q, k, v, cum, r, gla_norm[j], None, t_len=SEQ, seqs=GLA_PROMPT_SEQS, row0=0,
                               n_seq=BATCH)
            y_s = _gla_mix(q, k, v, cum, r, gla_norm[j], state_gla[:, j], t_len=DEC_SEQ, seqs=1,
                           row0=N_PROMPT_TOK, n_seq=DEC_BATCH)
            gla_states.append(st)
            ys = (y_p, y_s)
        xp, xs, *next_pair = _half_ffn((x,), mods_l, gains_l, w_in, w_out, 1, fn, last, True, mix=(ys, wm_out),
                                       casts=() if last else ffn_casts(i + 1, 0))
        if not last:
            w_in, w_out = next_pair
        if i % 2 == 0:
            xp, xs = _scan_to_natural(xp), _scan_to_natural(xs)
        else:
            xs = _col_major_to_natural(xs)

    y_prompt = xp.reshape(BATCH, SEQ, D_MODEL)
    y_sample = xs.reshape(DEC_BATCH, DEC_SEQ, D_MODEL)
    return (y_prompt, y_sample, jnp.stack(rg_states, axis=1), jnp.stack(gla_states, axis=1))
```

```python
import functools

import jax
import jax.numpy as jnp
from jax import lax
from jax.experimental import pallas as pl
from jax.experimental.pallas import tpu as pltpu

F32 = jnp.float32
BF16 = jnp.bfloat16

D_MODEL = 1024
BATCH = 16
SEQ = 256
DEC_BATCH = 2
DEC_SEQ = 2048
GRID_W = 64
EPS = 1e-6
N_MOD = 9
D_FF = 2816
FFN_RES = 0.5
D_RNN = 1280
RG_BLOCK = 128
CONV_W = 4
RG_C = 8.0
GLA_HEADS = 4
GLA_QK = 512
GLA_V = 1024
GLA_DK = 128
GLA_DV = 256
GLA_RANK = 16
GLA_TAU = 16.0
LOG2_E = 1.4426950408889634

N_PROMPT_TOK = BATCH * SEQ
N_TOK = N_PROMPT_TOK + DEC_BATCH * DEC_SEQ
N_GROUPS = 1 + DEC_BATCH
GROUP_ROWS = DEC_SEQ
PROMPT_BLOCKS = N_PROMPT_TOK // GROUP_ROWS

VMEM_LIMIT_BYTES = 56 * 1024 * 1024

TM = 512
HALF_TILES = N_PROMPT_TOK // TM
TM_PROJ = 1024
ADA_SLAB = 256
MXU_DIM = 256
BF16_SUBLANES = 16
FF_CHUNKS = (0, 6 * MXU_DIM, D_FF)
RG_CT = 256
SCAN_L = 256
SCAN_V = GROUP_ROWS // SCAN_L
SCAN_UNROLL = 8
GLA_C = 128
GLA_PROMPT_SEQS = 8
GLA_INTRA_UNROLL = 8
GLA_INTER_UNROLL = 16
GLA_LEVELS = tuple(GLA_C >> (l + 1) for l in range(GLA_C.bit_length() - 1))


def _cparams(*sem):
    return pltpu.CompilerParams(dimension_semantics=sem, vmem_limit_bytes=VMEM_LIMIT_BYTES)


def _group_of_tile(i, rows):
    return jnp.maximum((i * rows) // GROUP_ROWS - (PROMPT_BLOCKS - 1), 0)


def _sigmoid(x):
    return 1.0 / (1.0 + jnp.exp2(x * (-LOG2_E)))


def _softplus(x):
    return jnp.maximum(x, 0.0) + jnp.log1p(jnp.exp(-jnp.abs(x)))


def _rms(x):
    return x * lax.rsqrt(jnp.mean(x * x, axis=-1, keepdims=True) + EPS)


def _modulated(x, mod_ref, gain_ref, j):
    y = _rms(x) * gain_ref[j:j + 1, :]
    return y * (1.0 + mod_ref[3 * j + 1:3 * j + 2, :]) + mod_ref[3 * j:3 * j + 1, :]


def _ada_kernel(c_ref, w_ref, b_ref, o_ref):
    @pl.when(pl.program_id(1) == 0)
    def _():
        o_ref[...] = jnp.broadcast_to(b_ref[...], o_ref.shape)

    c = c_ref[...]
    s = (c * _sigmoid(c)).astype(BF16)
    o_ref[...] += jnp.dot(s, w_ref[...].astype(BF16), preferred_element_type=F32)


def _ada_mods(cvecs, ada_w, ada_b):
    depth = ada_w.shape[0]
    n_slab = D_MODEL // ADA_SLAB
    c_slabs = cvecs.reshape(8, n_slab, ADA_SLAB).swapaxes(0, 1)
    out = pl.pallas_call(
        _ada_kernel,
        grid=(depth, n_slab),
        in_specs=[
            pl.BlockSpec((None, 8, ADA_SLAB), lambda l, k: (k, 0, 0)),
            pl.BlockSpec((None, ADA_SLAB, N_MOD * D_MODEL), lambda l, k: (l, k, 0)),
            pl.BlockSpec((None, 1, N_MOD * D_MODEL), lambda l, k: (l, 0, 0)),
        ],
        out_specs=pl.BlockSpec((None, 8, N_MOD * D_MODEL), lambda l, k: (l, 0, 0)),
        out_shape=jax.ShapeDtypeStruct((depth, 8, N_MOD * D_MODEL), F32),
        compiler_params=_cparams("arbitrary", "arbitrary"),
        name="ada_mod",
    )(c_slabs, ada_w, ada_b.reshape(depth, 1, N_MOD * D_MODEL))
    return out[:, :N_GROUPS, :].reshape(depth, N_GROUPS, N_MOD, D_MODEL)


def _pick_half(refs):
    if len(refs) == 1:
        return refs[0][...]
    return jnp.where(pl.program_id(0) < HALF_TILES, refs[0][...], refs[1][...])


def _ffn_kernel(*refs, j, final, n_x, n_y, split_out, n_cast):
    *refs, win_ref, wout_ref, sem = refs
    x_refs, y_refs, refs = refs[:n_x], refs[n_x:n_x + n_y], refs[n_x + n_y:]
    if n_y:
        wmix_ref, refs = refs[0], refs[1:]
    mod_ref, gain_ref, win_hbm, wout_hbm, fn_ref = refs[:5]
    refs = refs[5:]
    chunks = list(zip(FF_CHUNKS[:-1], FF_CHUNKS[1:]))

    def chunk_copies(c):
        lo, hi = chunks[c]
        return (pltpu.make_async_copy(win_hbm.at[:, lo:hi], win_ref.at[:, lo:hi], sem.at[c, 0]),
                pltpu.make_async_copy(win_hbm.at[:, D_FF + lo:D_FF + hi], win_ref.at[:, D_FF + lo:D_FF + hi],
                                      sem.at[c, 1]),
                pltpu.make_async_copy(wout_hbm.at[lo:hi, :], wout_ref.at[lo:hi, :], sem.at[c, 2]))

    first_step = pl.program_id(0) == 0

    @pl.when(first_step)
    def _():
        for c in range(len(chunks)):
            for cp in chunk_copies(c):
                cp.start()

    if n_cast:
        f32_refs, refs = refs[:n_cast], refs[n_cast:]
        for src_ref, dst_ref in zip(f32_refs, refs[len(refs) - n_cast:]):
            dst_ref[...] = src_ref[...].astype(BF16)
    o_refs = refs
    x = _pick_half(x_refs)
    if n_y:
        y = _pick_half(y_refs).astype(BF16)
        x = x + mod_ref[5:6, :] * jnp.dot(y, wmix_ref[...], preferred_element_type=F32)
    hb = _modulated(x, mod_ref, gain_ref, j).astype(BF16)

    def ffn(wait_for_weights):
        acc = jnp.zeros((TM, D_MODEL), F32)
        for c, (lo, hi) in enumerate(chunks):
            if wait_for_weights:
                for cp in chunk_copies(c):
                    cp.wait()
            g = jnp.dot(hb, win_ref[:, lo:hi], preferred_element_type=F32)
            u = jnp.dot(hb, win_ref[:, D_FF + lo:D_FF + hi], preferred_element_type=F32)
            a = (g * _sigmoid(g) * u).astype(BF16)
            acc = acc + jnp.dot(a, wout_ref[lo:hi, :], preferred_element_type=F32)
        out = x + FFN_RES * mod_ref[3 * j + 2:3 * j + 3, :] * acc
        if final:
            out = _rms(out) * fn_ref[...]
        if split_out:
            first_half = pl.program_id(0) < HALF_TILES

            @pl.when(first_half)
            def _():
                o_refs[0][...] = out

            @pl.when(jnp.logical_not(first_half))
            def _():
                o_refs[1][...] = out
        else:
            o_refs[0][...] = out

    @pl.when(first_step)
    def _():
        ffn(True)

    @pl.when(jnp.logical_not(first_step))
    def _():
        ffn(False)


def _row_specs(width, split):
    if not split:
        return [pl.BlockSpec((TM, width), lambda i: (i, 0))]
    return [pl.BlockSpec((TM, width), lambda i: (jnp.minimum(i, HALF_TILES - 1), 0)),
            pl.BlockSpec((TM, width), lambda i: (jnp.maximum(i - HALF_TILES, 0), 0))]


def _half_ffn(xs, mods_l, gains_l, w_in, w_out, half, final_norm, final, split_out, mix=None, casts=()):
    n_steps = N_TOK // TM
    ys, w_mix = mix if mix is not None else ((), None)
    in_specs = _row_specs(D_MODEL, len(xs) == 2)
    args = list(xs)
    if ys:
        in_specs += _row_specs(ys[0].shape[1], len(ys) == 2)
        in_specs.append(pl.BlockSpec(w_mix.shape, lambda i: (0, 0), pipeline_mode=pl.Buffered(1)))
        args += list(ys) + [w_mix]
    in_specs += [
        pl.BlockSpec((None, N_MOD, D_MODEL), lambda i: (_group_of_tile(i, TM), 0, 0)),
        pl.BlockSpec((3, D_MODEL), lambda i: (0, 0)),
        pl.BlockSpec(memory_space=pl.ANY),
        pl.BlockSpec(memory_space=pl.ANY),
        pl.BlockSpec((1, D_MODEL), lambda i: (0, 0)),
    ]
    args += [mods_l, gains_l, w_in, w_out, final_norm]
    n_rows = N_PROMPT_TOK if split_out else N_TOK
    out_specs = _row_specs(D_MODEL, split_out)
    out_shape = [jax.ShapeDtypeStruct((n_rows, D_MODEL), F32)] * (2 if split_out else 1)
    for arr, lead, cols in casts:
        rows = arr.shape[len(lead)]
        slab = rows // n_steps
        assert slab * n_steps == rows and slab % BF16_SUBLANES == 0, (rows, n_steps)
        in_specs.append(pl.BlockSpec((None,) * len(lead) + (slab, cols), lambda i, lead=lead: (*lead, i, 0)))
        args.append(arr)
        out_specs.append(pl.BlockSpec((slab, cols), lambda i: (i, 0)))
        out_shape.append(jax.ShapeDtypeStruct((rows, cols), BF16))
    return pl.pallas_call(
        functools.partial(_ffn_kernel, j=2 * half, final=final, n_x=len(xs), n_y=len(ys), split_out=split_out,
                          n_cast=len(casts)),
        grid=(n_steps,),
        in_specs=in_specs,
        out_specs=out_specs,
        out_shape=out_shape,
        scratch_shapes=[pltpu.VMEM((D_MODEL, 2 * D_FF), BF16), pltpu.VMEM((D_FF, D_MODEL), BF16),
                        pltpu.SemaphoreType.DMA((len(FF_CHUNKS) - 1, 3))],
        compiler_params=_cparams("arbitrary"),
        name="half_ffn",
    )(*args)


def _gelu_tanh(x):
    return 0.5 * x * (1.0 + jnp.tanh(0.7978845608028654 * (x + 0.044715 * (x * x * x))))


def _rg_proj_kernel(x_ref, mod_ref, gain_ref, w_ref, gate_ref, xbr_ref):
    hb = _modulated(x_ref[...], mod_ref, gain_ref, 1).astype(BF16)
    gate_ref[...] = _gelu_tanh(jnp.dot(hb, w_ref[:, :D_RNN], preferred_element_type=F32))
    xbr_ref[...] = jnp.dot(hb, w_ref[:, D_RNN:], preferred_element_type=F32)


def _rg_proj(x, mods_l, gains_l, w_in):
    return pl.pallas_call(
        _rg_proj_kernel,
        grid=(N_TOK // TM_PROJ,),
        in_specs=[
            pl.BlockSpec((TM_PROJ, D_MODEL), lambda i: (i, 0)),
            pl.BlockSpec((None, N_MOD, D_MODEL), lambda i: (_group_of_tile(i, TM_PROJ), 0, 0)),
            pl.BlockSpec((3, D_MODEL), lambda i: (0, 0)),
            pl.BlockSpec((D_MODEL, 2 * D_RNN), lambda i: (0, 0), pipeline_mode=pl.Buffered(1)),
        ],
        out_specs=[pl.BlockSpec((TM_PROJ, D_RNN), lambda i: (i, 0))] * 2,
        out_shape=[jax.ShapeDtypeStruct((N_TOK, D_RNN), F32)] * 2,
        compiler_params=_cparams("arbitrary"),
        name="rg_in_proj",
    )(x, mods_l, gains_l, w_in)


def _seg_shift(row, down):
    v = lax.broadcasted_iota(jnp.int32, row.shape, 0)
    if down:
        return jnp.where(v >= 1, pltpu.roll(row, 1, 0), 0.0)
    return jnp.where(v < SCAN_V - 1, pltpu.roll(row, SCAN_V - 1, 0), 0.0)


def _rg_scan_kernel(xbr_ref, gate_ref, cw_ref, cb_ref, wa_ref, ba_ref, wi_ref, bi_ref, lam_ref, h0_ref,
                    y_ref, stf_ref, stb_ref, af_ref, uf_ref, ab_ref, ub_ref, pf_ref, hf_ref, pb_ref, hb_ref):
    chain = jnp.where(pl.program_id(0) >= PROMPT_BLOCKS, 1.0, 0.0).astype(F32)
    nbt = RG_CT // RG_BLOCK
    rows = SCAN_L * SCAN_V
    v = SCAN_V

    for nb in range(nbt):
        ls = slice(nb * RG_BLOCK, (nb + 1) * RG_BLOCK)
        x = xbr_ref[:, ls]
        before = chain * _seg_shift(x[rows - v:], True)
        after0 = chain * _seg_shift(x[:v], False)
        after1 = chain * _seg_shift(x[v:2 * v], False)
        xm1 = jnp.concatenate([before, x[:rows - v]], axis=0)
        xp1 = jnp.concatenate([x[v:], after0], axis=0)
        xp2 = jnp.concatenate([x[2 * v:], after0, after1], axis=0)
        xc = (cw_ref[0:1, ls] * xm1 + cw_ref[1:2, ls] * x + cw_ref[2:3, ls] * xp1 + cw_ref[3:4, ls] * xp2
              + cb_ref[0:1, ls])
        xcb = xc.astype(BF16)
        xc_r2 = xc * (2.0 ** 0.5)
        for d, (a_ref, u_ref) in enumerate(((af_ref, uf_ref), (ab_ref, ub_ref))):
            r = _sigmoid(jnp.dot(xcb, wa_ref[d, nb], preferred_element_type=F32) + ba_ref[d:d + 1, ls])
            gi = _sigmoid(jnp.dot(xcb, wi_ref[d, nb], preferred_element_type=F32) + bi_ref[d:d + 1, ls])
            decay = RG_C * _softplus(-lam_ref[d:d + 1, ls])
            a_ref[nb] = jnp.exp2(r * (decay * (-LOG2_E)))
            t = jnp.tanh(r * decay)
            prod = t * (1.0 + t)
            u_ref[nb] = jnp.where(prod > 0.0, t * lax.rsqrt(prod), 0.0) * (gi * xc_r2)

    def steps(j, carry):
        base_f = pl.multiple_of(j * (SCAN_UNROLL * v), SCAN_UNROLL * v)
        base_b = pl.multiple_of((SCAN_L - SCAN_UNROLL) * v - j * (SCAN_UNROLL * v), SCAN_UNROLL * v)
        carry = list(carry)
        for k in range(SCAN_UNROLL):
            sf = pl.ds(base_f + k * v, v)
            sb = pl.ds(base_b + (SCAN_UNROLL - 1 - k) * v, v)
            for nb in range(nbt):
                hf, pf, hb, pb = carry[4 * nb:4 * nb + 4]
                a = af_ref[nb, sf, :]
                hf = a * hf + uf_ref[nb, sf, :]
                pf = a * pf
                hf_ref[nb, sf, :] = hf
                pf_ref[nb, sf, :] = pf
                a = ab_ref[nb, sb, :]
                hb = a * hb + ub_ref[nb, sb, :]
                pb = a * pb
                hb_ref[nb, sb, :] = hb
                pb_ref[nb, sb, :] = pb
                carry[4 * nb:4 * nb + 4] = [hf, pf, hb, pb]
        return tuple(carry)

    zero = jnp.zeros((v, RG_BLOCK), F32)
    one = jnp.ones((v, RG_BLOCK), F32)
    lax.fori_loop(0, SCAN_L // SCAN_UNROLL, steps, (zero, one, zero, one) * nbt)

    for nb in range(nbt):
        ls = slice(nb * RG_BLOCK, (nb + 1) * RG_BLOCK)
        h_end, p_end = hf_ref[nb, rows - v:, :], pf_ref[nb, rows - v:, :]
        entry = [chain * h0_ref[0:1, ls]]
        for s in range(v - 1):
            entry.append(chain * (h_end[s:s + 1] + p_end[s:s + 1] * entry[s]))
        ef = jnp.concatenate(entry, axis=0)
        h_start, p_start = hb_ref[nb, :v, :], pb_ref[nb, :v, :]
        entry = [chain * h0_ref[1:2, ls]]
        for s in range(v - 1, 0, -1):
            entry.append(chain * (h_start[s:s + 1] + p_start[s:s + 1] * entry[-1]))
        eb = jnp.concatenate(entry[::-1], axis=0)
        shape3 = (SCAN_L, v, RG_BLOCK)
        hf = hf_ref[nb].reshape(shape3) + pf_ref[nb].reshape(shape3) * ef[None]
        hb = hb_ref[nb].reshape(shape3) + pb_ref[nb].reshape(shape3) * eb[None]
        stf_ref[:, ls] = hf[SCAN_L - 1]
        stb_ref[:, ls] = hb[0]
        y_ref[:, ls] = (hf + hb).reshape(rows, RG_BLOCK) * gate_ref[:, ls]


def _rg_scan(xbr, gate, conv_w, conv_b, w_a, b_a, w_i, b_i, lam, h0):
    n_blk = N_TOK // GROUP_ROWS
    nbt = RG_CT // RG_BLOCK
    tile = pl.BlockSpec((GROUP_ROWS, RG_CT), lambda b, c: (b, c))
    vec2 = pl.BlockSpec((2, RG_CT), lambda b, c: (0, c))
    wspec = pl.BlockSpec((2, nbt, RG_BLOCK, RG_BLOCK), lambda b, c: (0, c, 0, 0))
    return pl.pallas_call(
        _rg_scan_kernel,
        grid=(n_blk, D_RNN // RG_CT),
        in_specs=[
            tile, tile,
            pl.BlockSpec((CONV_W, RG_CT), lambda b, c: (0, c)),
            pl.BlockSpec((1, RG_CT), lambda b, c: (0, c)),
            wspec, vec2, wspec, vec2, vec2,
            pl.BlockSpec((None, 2, RG_CT), lambda b, c: (jnp.maximum(b - PROMPT_BLOCKS, 0), 0, c)),
        ],
        out_specs=[tile] + [pl.BlockSpec((SCAN_V, RG_CT), lambda b, c: (b, c))] * 2,
        out_shape=[jax.ShapeDtypeStruct((N_TOK, D_RNN), F32)]
        + [jax.ShapeDtypeStruct((n_blk * SCAN_V, D_RNN), F32)] * 2,
        scratch_shapes=[pltpu.VMEM((nbt, GROUP_ROWS, RG_BLOCK), F32)] * 8,
        compiler_params=_cparams("arbitrary", "arbitrary"),
        name="rg_scan",
    )(xbr, gate, conv_w, conv_b, w_a, b_a, w_i, b_i, lam, h0)


def _chunk_tri():
    ri = lax.broadcasted_iota(jnp.int32, (GLA_C, GLA_C), 0)
    ci = lax.broadcasted_iota(jnp.int32, (GLA_C, GLA_C), 1)
    return (ri >= ci).astype(BF16), (ri <= ci).astype(BF16)


def _split3(x):
    hi = x.astype(BF16)
    r1 = x - hi.astype(F32)
    mid = r1.astype(BF16)
    lo = (r1 - mid.astype(F32)).astype(BF16)
    return hi, mid, lo


def _gla_proj_kernel(x_ref, mod_ref, gain_ref, w_ref, wz_ref, w2_ref, bg_ref,
                     q_ref, k_ref, v_ref, r_ref, cum_ref):
    hb = _modulated(x_ref[...], mod_ref, gain_ref, 1).astype(BF16)
    z = jnp.dot(hb, wz_ref[...].astype(BF16), preferred_element_type=F32).astype(BF16)
    zz = jnp.dot(z, w2_ref[...], preferred_element_type=F32) + bg_ref[...]
    log_gate = -_softplus(-zz) * (LOG2_E / GLA_TAU)
    parts = _split3(log_gate)
    r = jnp.dot(hb, w_ref[:, 2 * GLA_QK + GLA_V:], preferred_element_type=F32)
    r_ref[...] = r * _sigmoid(r)
    q_ref[...] = jnp.dot(hb, w_ref[:, :GLA_QK], preferred_element_type=F32) * (GLA_DK ** -0.5)
    k_ref[...] = jnp.dot(hb, w_ref[:, GLA_QK:2 * GLA_QK], preferred_element_type=F32)
    v_ref[...] = jnp.dot(hb, w_ref[:, 2 * GLA_QK:2 * GLA_QK + GLA_V], preferred_element_type=F32).astype(BF16)
    tri_f, tri_b = _chunk_tri()
    for ch in range(TM_PROJ // GLA_C):
        rows = slice(ch * GLA_C, (ch + 1) * GLA_C)
        cum_ref[rows, :GLA_QK] = sum(jnp.dot(tri_f, p[rows, :GLA_QK], preferred_element_type=F32) for p in parts)
        cum_ref[rows, GLA_QK:] = sum(jnp.dot(tri_b, p[rows, GLA_QK:], preferred_element_type=F32) for p in parts)


def _gla_proj(x, mods_l, gains_l, w_main, w_z, w2cat, bg):
    def rows(w):
        return pl.BlockSpec((TM_PROJ, w), lambda i: (i, 0))

    def whole(a):
        return pl.BlockSpec(a.shape, lambda i: (0,) * a.ndim, pipeline_mode=pl.Buffered(1))

    widths = (GLA_QK, GLA_QK, GLA_V, GLA_V, 2 * GLA_QK)
    dtypes = (F32, F32, BF16, F32, F32)
    return pl.pallas_call(
        _gla_proj_kernel,
        grid=(N_TOK // TM_PROJ,),
        in_specs=[
            rows(D_MODEL),
            pl.BlockSpec((None, N_MOD, D_MODEL), lambda i: (_group_of_tile(i, TM_PROJ), 0, 0)),
            pl.BlockSpec((3, D_MODEL), lambda i: (0, 0)),
            whole(w_main), whole(w_z), whole(w2cat), whole(bg),
        ],
        out_specs=[rows(w) for w in widths],
        out_shape=[jax.ShapeDtypeStruct((N_TOK, w), t) for w, t in zip(widths, dtypes)],
        compiler_params=_cparams("arbitrary"),
        name="gla_in_proj",
    )(x, mods_l, gains_l, w_main, w_z, w2cat, bg)


def _dot_tn(a, b):
    return lax.dot_general(a, b, (((0,), (0,)), ((), ())), preferred_element_type=F32)


def _dot_nt(a, b):
    return lax.dot_general(a, b, (((1,), (1,)), ((), ())), preferred_element_type=F32)


def _level_ref(cum_ref, row0, cum, m, backward):
    c = GLA_C
    pick = m - 1 if backward else m

    def bcast(r, n):
        return jnp.broadcast_to(cum_ref[pl.ds(row0 + r, 1), :], (n, GLA_DK))

    if m >= 4:
        return jnp.concatenate([bcast(g * 2 * m + pick, 2 * m) for g in range(c // (2 * m))], axis=0)
    pos = lax.broadcasted_iota(jnp.int32, (c, GLA_DK), 0)
    if m == 2:
        lo = jnp.concatenate([bcast(g * 8 + pick, 8) for g in range(c // 8)], axis=0)
        hi = jnp.concatenate([bcast(g * 8 + 4 + pick, 8) for g in range(c // 8)], axis=0)
        return jnp.where((pos & 4) == 0, lo, hi)
    if backward:
        return jnp.where((pos & 1) == 1, pltpu.roll(cum, 1, 0), cum)
    return jnp.where((pos & 1) == 0, pltpu.roll(cum, c - 1, 0), cum)


def _gla_intra(q_ref, k_ref, v_ref, cf_ref, cb_ref, row0, diag, masks, uppers):
    rows = pl.ds(row0, GLA_C)
    cf, cb = cf_ref[rows, :], cb_ref[rows, :]
    qb, kb = q_ref[rows, :].astype(BF16), k_ref[rows, :].astype(BF16)
    scores = jnp.where(diag, 2.0 * _dot_nt(qb, kb), 0.0)
    for lvl, m in enumerate(GLA_LEVELS):
        df = cf - _level_ref(cf_ref, row0, cf, m, False)
        db = cb - _level_ref(cb_ref, row0, cb, m, True)
        gq = jnp.exp2(jnp.where(uppers[lvl], df, db)).astype(BF16)
        gk = jnp.exp2(-jnp.where(uppers[lvl], db, df)).astype(BF16)
        scores = jnp.where(masks[lvl], _dot_nt(qb * gq, kb * gk), scores)
    return jnp.dot(scores.astype(BF16), v_ref[rows, :], preferred_element_type=F32)


def _gla_inter(q_ref, k_ref, v_ref, cum_ref, row0, st, backward):
    rows = pl.ds(row0, GLA_C)
    cum = cum_ref[rows, :]
    tot = cum_ref[pl.ds(row0 + (0 if backward else GLA_C - 1), 1), :]
    o = _dot_nt((q_ref[rows, :] * jnp.exp2(cum)).astype(BF16), st.astype(BF16))
    kd = (k_ref[rows, :] * jnp.exp2(tot - cum)).astype(BF16)
    return o, st * jnp.exp2(tot) + _dot_tn(v_ref[rows, :], kd)


def _gla_kernel(*refs, t_len, seqs, has_s0):
    if has_s0:
        q_ref, k_ref, v_ref, cf_ref, cb_ref, r_ref, gn_ref, s0_ref, y_ref, sf_ref, sb_ref = refs
    else:
        q_ref, k_ref, v_ref, cf_ref, cb_ref, r_ref, gn_ref, y_ref, sout_ref, sf_ref, sb_ref = refs
    c = GLA_C
    n_chunks = t_len // c
    ri = lax.broadcasted_iota(jnp.int32, (c, c), 0)
    ci = lax.broadcasted_iota(jnp.int32, (c, c), 1)
    x = ri ^ ci
    masks = [(x >= m) & (x < 2 * m) for m in GLA_LEVELS]
    pos = lax.broadcasted_iota(jnp.int32, (c, GLA_DK), 0)
    uppers = [(pos & m) != 0 for m in GLA_LEVELS]

    def intra(i, carry):
        row0 = pl.multiple_of(i * c, c)
        y_ref[pl.ds(row0, c), :] = _gla_intra(q_ref, k_ref, v_ref, cf_ref, cb_ref, row0, ri == ci, masks, uppers)
        return carry

    lax.fori_loop(0, seqs * n_chunks, intra, 0, unroll=min(seqs * n_chunks, GLA_INTRA_UNROLL))

    for s in range(seqs):
        if has_s0:
            sf_ref[s] = s0_ref[s, 0].T
            sb_ref[s] = s0_ref[s, 1].T
        else:
            sf_ref[s] = jnp.zeros((GLA_DV, GLA_DK), F32)
            sb_ref[s] = jnp.zeros((GLA_DV, GLA_DK), F32)

        def inter(i, carry, s=s):
            rf = pl.multiple_of(s * t_len + i * c, c)
            rb = pl.multiple_of(s * t_len + (n_chunks - 1 - i) * c, c)
            o, st = _gla_inter(q_ref, k_ref, v_ref, cf_ref, rf, sf_ref[s], False)
            y_ref[pl.ds(rf, c), :] += o
            sf_ref[s] = st
            o, st = _gla_inter(q_ref, k_ref, v_ref, cb_ref, rb, sb_ref[s], True)
            y_ref[pl.ds(rb, c), :] += o
            sb_ref[s] = st
            return carry

        lax.fori_loop(0, n_chunks, inter, 0, unroll=min(n_chunks, GLA_INTER_UNROLL))
        if not has_s0:
            sout_ref[s, 0] = sf_ref[s].T
            sout_ref[s, 1] = sb_ref[s].T
    o = y_ref[...]
    y_ref[...] = _rms(o) * gn_ref[...] * r_ref[...]


def _gla_mix(q, k, v, cum, r, gnorm, s0, *, t_len, seqs, row0, n_seq):
    has_s0 = s0 is not None
    blk = seqs * t_len
    blk0 = row0 // blk
    assert blk0 * blk == row0 and n_seq % seqs == 0

    def rows(w, off=0):
        return pl.BlockSpec((blk, w), lambda s, h: (blk0 + s, off + h))

    state_spec = pl.BlockSpec((seqs, 2, None, GLA_DK, GLA_DV), lambda s, h: (s, 0, h, 0, 0))
    in_specs = [rows(GLA_DK), rows(GLA_DK), rows(GLA_DV), rows(GLA_DK), rows(GLA_DK, GLA_HEADS),
                rows(GLA_DV), pl.BlockSpec((None, 1, GLA_DV), lambda s, h: (h, 0, 0))]
    args = [q, k, v, cum, cum, r, gnorm.reshape(GLA_HEADS, 1, GLA_DV)]
    y_spec = pl.BlockSpec((blk, GLA_DV), lambda s, h: (s, h))
    y_shape = jax.ShapeDtypeStruct((n_seq * t_len, GLA_V), F32)
    if has_s0:
        in_specs.append(state_spec)
        args.append(s0)
        out_specs, out_shape = y_spec, y_shape
    else:
        out_specs = [y_spec, state_spec]
        out_shape = [y_shape, jax.ShapeDtypeStruct((n_seq, 2, GLA_HEADS, GLA_DK, GLA_DV), F32)]
    return pl.pallas_call(
        functools.partial(_gla_kernel, t_len=t_len, seqs=seqs, has_s0=has_s0),
        grid=(n_seq // seqs, GLA_HEADS),
        in_specs=in_specs,
        out_specs=out_specs,
        out_shape=out_shape,
        scratch_shapes=[pltpu.VMEM((seqs, GLA_DV, GLA_DK), F32)] * 2,
        compiler_params=_cparams("arbitrary", "arbitrary"),
        name="gla_mix_sample" if has_s0 else "gla_mix_prompt",
    )(*args)


def _natural_to_scan(x_half):
    x4 = x_half.reshape(-1, SCAN_V, SCAN_L, D_MODEL)
    return x4.swapaxes(1, 2).reshape(-1, D_MODEL)


def _scan_to_natural(x_half):
    x4 = x_half.reshape(-1, SCAN_L, SCAN_V, D_MODEL)
    return x4.swapaxes(1, 2).reshape(-1, D_MODEL)


def _natural_to_col_major(x_half):
    x4 = x_half.reshape(DEC_BATCH, DEC_SEQ // GRID_W, GRID_W, D_MODEL)
    return x4.swapaxes(1, 2).reshape(-1, D_MODEL)


def _col_major_to_natural(x_half):
    x4 = x_half.reshape(DEC_BATCH, GRID_W, DEC_SEQ // GRID_W, D_MODEL)
    return x4.swapaxes(1, 2).reshape(-1, D_MODEL)


def kernel(x_prompt, x_sample, c, state_rglru, state_gla, c_ctx, ada_w, ada_b, norm_g, ffn_w_in, ffn_w_out,
           rg_w_in, rg_conv_w, rg_conv_b, rg_w_a, rg_b_a, rg_w_i, rg_b_i, rg_lambda, rg_w_out,
           gla_w_in, gla_w_g2, gla_b_g, gla_norm, gla_w_out, final_norm):
    depth = ada_w.shape[0]
    cvecs = jnp.concatenate([c_ctx[None, :], c, jnp.zeros((8 - N_GROUPS, D_MODEL), F32)], axis=0)
    mods = _ada_mods(cvecs, ada_w, ada_b)
    fn = final_norm.reshape(1, D_MODEL)
    rg_states, gla_states = [], []

    w_in, w_out = ffn_w_in[0, 0].astype(BF16), ffn_w_out[0, 0].astype(BF16)
    n_main = 2 * GLA_QK + 2 * GLA_V

    def ffn_casts(layer, half):
        return ((ffn_w_in, (layer, half), 2 * D_FF), (ffn_w_out, (layer, half), D_MODEL))

    xp = x_prompt.reshape(N_PROMPT_TOK, D_MODEL)
    xs = x_sample.reshape(DEC_BATCH * DEC_SEQ, D_MODEL)
    for i in range(depth):
        mods_l, gains_l = mods[i], norm_g[i]
        j = i // 2
        last = i == depth - 1
        if i % 2 == 0:
            halves = (_natural_to_scan(xp), _natural_to_scan(xs))
            mixer_casts = ((rg_w_in, (j,), 2 * D_RNN), (rg_w_out, (j,), D_MODEL))
        else:
            halves = (xp, _natural_to_col_major(xs))
            mixer_casts = ((gla_w_in, (j,), n_main), (gla_w_out, (j,), D_MODEL))
        x, w_in, w_out, wm_in, wm_out = _half_ffn(halves, mods_l, gains_l, w_in, w_out, 0, fn, False, False,
                                                  casts=ffn_casts(i, 1) + mixer_casts)
        if i % 2 == 0:
            gate, xbr = _rg_proj(x, mods_l, gains_l, wm_in)
            y, st_f, st_b = _rg_scan(xbr, gate, rg_conv_w[j], rg_conv_b[j].reshape(1, D_RNN),
                                     rg_w_a[j].astype(BF16), rg_b_a[j], rg_w_i[j].astype(BF16), rg_b_i[j],
                                     rg_lambda[j], state_rglru[:, j])
            rg_states.append(jnp.stack([st_f[:BATCH], st_b[:BATCH]], axis=1))
            ys = (y,)
        else:
            w_z = jnp.pad(gla_w_in[j, :, n_main:], ((0, 0), (0, 128 - 2 * GLA_RANK)))
            w2cat = jnp.zeros((128, 2 * GLA_QK), F32)
            w2cat = w2cat.at[:GLA_RANK, :GLA_QK].set(gla_w_g2[j, 0])
            w2cat = w2cat.at[GLA_RANK:2 * GLA_RANK, GLA_QK:].set(gla_w_g2[j, 1]).astype(BF16)
            q, k, v, r, cum = _gla_proj(x, mods_l, gains_l, wm_in, w_z, w2cat, gla_b_g[j].reshape(1, 2 * GLA_QK))
            y_p, st = _gla_mix(q, k, v, cum, r, gla_norm[j], None, t_len=SEQ, seqs=GLA_PROMPT_SEQS, row0=0,
                               n_seq=BATCH)
            y_s = _gla_mix(q, k, v, cum, r, gla_norm[j], state_gla[:, j], t_len=DEC_SEQ, seqs=1,
                           row0=N_PROMPT_TOK, n_seq=DEC_BATCH)
            gla_states.append(st)
            ys = (y_p, y_s)
        xp, xs, *next_pair = _half_ffn((x,), mods_l, gains_l, w_in, w_out, 1, fn, last, True, mix=(ys, wm_out),
                                       casts=() if last else ffn_casts(i + 1, 0))
        if not last:
            w_in, w_out = next_pair
        if i % 2 == 0:
            xp, xs = _scan_to_natural(xp), _scan_to_natural(xs)
        else:
            xs = _col_major_to_natural(xs)

    y_prompt = xp.reshape(BATCH, SEQ, D_MODEL)
    y_sample = xs.reshape(DEC_BATCH, DEC_SEQ, D_MODEL)
    return (y_prompt, y_sample, jnp.stack(rg_states, axis=1), jnp.stack(gla_states, axis=1))
```

```python
import functools

import jax
import jax.numpy as jnp
from jax import lax
from jax.experimental import pallas as pl
from jax.experimental.pallas import tpu as pltpu

F32 = jnp.float32
BF16 = jnp.bfloat16

D_MODEL = 1024
BATCH = 16
SEQ = 256
DEC_BATCH = 2
DEC_SEQ = 2048
GRID_W = 64
EPS = 1e-6
N_MOD = 9
D_FF = 2816
FFN_RES = 0.5
D_RNN = 1280
RG_BLOCK = 128
CONV_W = 4
RG_C = 8.0
GLA_HEADS = 4
GLA_QK = 512
GLA_V = 1024
GLA_DK = 128
GLA_DV = 256
GLA_RANK = 16
GLA_TAU = 16.0
LOG2_E = 1.4426950408889634

N_PROMPT_TOK = BATCH * SEQ
N_TOK = N_PROMPT_TOK + DEC_BATCH * DEC_SEQ
N_GROUPS = 1 + DEC_BATCH
GROUP_ROWS = DEC_SEQ
PROMPT_BLOCKS = N_PROMPT_TOK // GROUP_ROWS

VMEM_LIMIT_BYTES = 56 * 1024 * 1024

TM = 512
HALF_TILES = N_PROMPT_TOK // TM
TM_PROJ = 1024
ADA_SLAB = 256
MXU_DIM = 256
BF16_SUBLANES = 16
FF_CHUNKS = (0, 6 * MXU_DIM, D_FF)
RG_CT = 256
SCAN_L = 256
SCAN_V = GROUP_ROWS // SCAN_L
SCAN_UNROLL = 8
GLA_C = 128
GLA_PROMPT_SEQS = 8
GLA_INTRA_UNROLL = 8
GLA_INTRA_UNROLL_SHORT = 16
GLA_INTER_UNROLL = 16
GLA_LEVELS = tuple(GLA_C >> (l + 1) for l in range(GLA_C.bit_length() - 1))


def _cparams(*sem):
    return pltpu.CompilerParams(dimension_semantics=sem, vmem_limit_bytes=VMEM_LIMIT_BYTES)


def _group_of_tile(i, rows):
    return jnp.maximum((i * rows) // GROUP_ROWS - (PROMPT_BLOCKS - 1), 0)


def _sigmoid(x):
    return 1.0 / (1.0 + jnp.exp2(x * (-LOG2_E)))


def _softplus(x):
    return jnp.maximum(x, 0.0) + jnp.log1p(jnp.exp(-jnp.abs(x)))


def _rms(x):
    return x * lax.rsqrt(jnp.mean(x * x, axis=-1, keepdims=True) + EPS)


def _modulated(x, mod_ref, gain_ref, j):
    y = _rms(x) * gain_ref[j:j + 1, :]
    return y * (1.0 + mod_ref[3 * j + 1:3 * j + 2, :]) + mod_ref[3 * j:3 * j + 1, :]


def _ada_kernel(c_ref, w_ref, b_ref, o_ref):
    @pl.when(pl.program_id(1) == 0)
    def _():
        o_ref[...] = jnp.broadcast_to(b_ref[...], o_ref.shape)

    c = c_ref[...]
    s = (c * _sigmoid(c)).astype(BF16)
    o_ref[...] += jnp.dot(s, w_ref[...].astype(BF16), preferred_element_type=F32)


def _ada_mods(cvecs, ada_w, ada_b):
    depth = ada_w.shape[0]
    n_slab = D_MODEL // ADA_SLAB
    c_slabs = cvecs.reshape(8, n_slab, ADA_SLAB).swapaxes(0, 1)
    out = pl.pallas_call(
        _ada_kernel,
        grid=(depth, n_slab),
        in_specs=[
            pl.BlockSpec((None, 8, ADA_SLAB), lambda l, k: (k, 0, 0)),
            pl.BlockSpec((None, ADA_SLAB, N_MOD * D_MODEL), lambda l, k: (l, k, 0)),
            pl.BlockSpec((None, 1, N_MOD * D_MODEL), lambda l, k: (l, 0, 0)),
        ],
        out_specs=pl.BlockSpec((None, 8, N_MOD * D_MODEL), lambda l, k: (l, 0, 0)),
        out_shape=jax.ShapeDtypeStruct((depth, 8, N_MOD * D_MODEL), F32),
        compiler_params=_cparams("arbitrary", "arbitrary"),
        name="ada_mod",
    )(c_slabs, ada_w, ada_b.reshape(depth, 1, N_MOD * D_MODEL))
    return out[:, :N_GROUPS, :].reshape(depth, N_GROUPS, N_MOD, D_MODEL)


def _pick_half(refs):
    if len(refs) == 1:
        return refs[0][...]
    return jnp.where(pl.program_id(0) < HALF_TILES, refs[0][...], refs[1][...])


def _ffn_kernel(*refs, j, final, n_x, n_y, split_out, n_cast):
    x_refs, y_refs, refs = refs[:n_x], refs[n_x:n_x + n_y], refs[n_x + n_y:]
    if n_y:
        wmix_ref, refs = refs[0], refs[1:]
    mod_ref, gain_ref, win_ref, wout_ref, fn_ref = refs[:5]
    refs = refs[5:]
    if n_cast:
        f32_refs, refs = refs[:n_cast], refs[n_cast:]
        for src_ref, dst_ref in zip(f32_refs, refs[len(refs) - n_cast:]):
            dst_ref[...] = src_ref[...].astype(BF16)
    o_refs = refs
    x = _pick_half(x_refs)
    if n_y:
        y = _pick_half(y_refs).astype(BF16)
        x = x + mod_ref[5:6, :] * jnp.dot(y, wmix_ref[...], preferred_element_type=F32)
    hb = _modulated(x, mod_ref, gain_ref, j).astype(BF16)
    acc = jnp.zeros((TM, D_MODEL), F32)
    for lo, hi in zip(FF_CHUNKS[:-1], FF_CHUNKS[1:]):
        g = jnp.dot(hb, win_ref[:, lo:hi], preferred_element_type=F32)
        u = jnp.dot(hb, win_ref[:, D_FF + lo:D_FF + hi], preferred_element_type=F32)
        a = (g * _sigmoid(g) * u).astype(BF16)
        acc = acc + jnp.dot(a, wout_ref[lo:hi, :], preferred_element_type=F32)
    out = x + FFN_RES * mod_ref[3 * j + 2:3 * j + 3, :] * acc
    if final:
        out = _rms(out) * fn_ref[...]
    if split_out:
        first_half = pl.program_id(0) < HALF_TILES

        @pl.when(first_half)
        def _():
            o_refs[0][...] = out

        @pl.when(jnp.logical_not(first_half))
        def _():
            o_refs[1][...] = out
    else:
        o_refs[0][...] = out


def _row_specs(width, split):
    if not split:
        return [pl.BlockSpec((TM, width), lambda i: (i, 0))]
    return [pl.BlockSpec((TM, width), lambda i: (jnp.minimum(i, HALF_TILES - 1), 0)),
            pl.BlockSpec((TM, width), lambda i: (jnp.maximum(i - HALF_TILES, 0), 0))]


def _half_ffn(xs, mods_l, gains_l, w_in, w_out, half, final_norm, final, split_out, mix=None, casts=()):
    n_steps = N_TOK // TM
    ys, w_mix = mix if mix is not None else ((), None)
    in_specs = _row_specs(D_MODEL, len(xs) == 2)
    args = list(xs)
    if ys:
        in_specs += _row_specs(ys[0].shape[1], len(ys) == 2)
        in_specs.append(pl.BlockSpec(w_mix.shape, lambda i: (0, 0), pipeline_mode=pl.Buffered(1)))
        args += list(ys) + [w_mix]
    in_specs += [
        pl.BlockSpec((None, N_MOD, D_MODEL), lambda i: (_group_of_tile(i, TM), 0, 0)),
        pl.BlockSpec((3, D_MODEL), lambda i: (0, 0)),
        pl.BlockSpec((D_MODEL, 2 * D_FF), lambda i: (0, 0), pipeline_mode=pl.Buffered(1)),
        pl.BlockSpec((D_FF, D_MODEL), lambda i: (0, 0), pipeline_mode=pl.Buffered(1)),
        pl.BlockSpec((1, D_MODEL), lambda i: (0, 0)),
    ]
    args += [mods_l, gains_l, w_in, w_out, final_norm]
    n_rows = N_PROMPT_TOK if split_out else N_TOK
    out_specs = _row_specs(D_MODEL, split_out)
    out_shape = [jax.ShapeDtypeStruct((n_rows, D_MODEL), F32)] * (2 if split_out else 1)
    for arr, lead, cols in casts:
        rows = arr.shape[len(lead)]
        slab = rows // n_steps
        assert slab * n_steps == rows and slab % BF16_SUBLANES == 0, (rows, n_steps)
        in_specs.append(pl.BlockSpec((None,) * len(lead) + (slab, cols), lambda i, lead=lead: (*lead, i, 0)))
        args.append(arr)
        out_specs.append(pl.BlockSpec((slab, cols), lambda i: (i, 0)))
        out_shape.append(jax.ShapeDtypeStruct((rows, cols), BF16))
    return pl.pallas_call(
        functools.partial(_ffn_kernel, j=2 * half, final=final, n_x=len(xs), n_y=len(ys), split_out=split_out,
                          n_cast=len(casts)),
        grid=(n_steps,),
        in_specs=in_specs,
        out_specs=out_specs,
        out_shape=out_shape,
        compiler_params=_cparams("arbitrary"),
        name="half_ffn",
    )(*args)


def _gelu_tanh(x):
    return 0.5 * x * (1.0 + jnp.tanh(0.7978845608028654 * (x + 0.044715 * (x * x * x))))


def _rg_proj_kernel(x_ref, mod_ref, gain_ref, w_ref, gate_ref, xbr_ref):
    hb = _modulated(x_ref[...], mod_ref, gain_ref, 1).astype(BF16)
    gate_ref[...] = _gelu_tanh(jnp.dot(hb, w_ref[:, :D_RNN], preferred_element_type=F32))
    xbr_ref[...] = jnp.dot(hb, w_ref[:, D_RNN:], preferred_element_type=F32)


def _rg_proj(x, mods_l, gains_l, w_in):
    return pl.pallas_call(
        _rg_proj_kernel,
        grid=(N_TOK // TM_PROJ,),
        in_specs=[
            pl.BlockSpec((TM_PROJ, D_MODEL), lambda i: (i, 0)),
            pl.BlockSpec((None, N_MOD, D_MODEL), lambda i: (_group_of_tile(i, TM_PROJ), 0, 0)),
            pl.BlockSpec((3, D_MODEL), lambda i: (0, 0)),
            pl.BlockSpec((D_MODEL, 2 * D_RNN), lambda i: (0, 0), pipeline_mode=pl.Buffered(1)),
        ],
        out_specs=[pl.BlockSpec((TM_PROJ, D_RNN), lambda i: (i, 0))] * 2,
        out_shape=[jax.ShapeDtypeStruct((N_TOK, D_RNN), F32)] * 2,
        compiler_params=_cparams("arbitrary"),
        name="rg_in_proj",
    )(x, mods_l, gains_l, w_in)


def _seg_shift(row, down):
    v = lax.broadcasted_iota(jnp.int32, row.shape, 0)
    if down:
        return jnp.where(v >= 1, pltpu.roll(row, 1, 0), 0.0)
    return jnp.where(v < SCAN_V - 1, pltpu.roll(row, SCAN_V - 1, 0), 0.0)


def _rg_scan_kernel(xbr_ref, gate_ref, cw_ref, cb_ref, wa_ref, ba_ref, wi_ref, bi_ref, lam_ref, h0_ref,
                    y_ref, stf_ref, stb_ref, af_ref, uf_ref, ab_ref, ub_ref, pf_ref, hf_ref, pb_ref, hb_ref):
    chain = jnp.where(pl.program_id(0) >= PROMPT_BLOCKS, 1.0, 0.0).astype(F32)
    nbt = RG_CT // RG_BLOCK
    rows = SCAN_L * SCAN_V
    v = SCAN_V

    for nb in range(nbt):
        ls = slice(nb * RG_BLOCK, (nb + 1) * RG_BLOCK)
        x = xbr_ref[:, ls]
        before = chain * _seg_shift(x[rows - v:], True)
        after0 = chain * _seg_shift(x[:v], False)
        after1 = chain * _seg_shift(x[v:2 * v], False)
        xm1 = jnp.concatenate([before, x[:rows - v]], axis=0)
        xp1 = jnp.concatenate([x[v:], after0], axis=0)
        xp2 = jnp.concatenate([x[2 * v:], after0, after1], axis=0)
        xc = (cw_ref[0:1, ls] * xm1 + cw_ref[1:2, ls] * x + cw_ref[2:3, ls] * xp1 + cw_ref[3:4, ls] * xp2
              + cb_ref[0:1, ls])
        xcb = xc.astype(BF16)
        xc_r2 = xc * (2.0 ** 0.5)
        for d, (a_ref, u_ref) in enumerate(((af_ref, uf_ref), (ab_ref, ub_ref))):
            r = _sigmoid(jnp.dot(xcb, wa_ref[d, nb], preferred_element_type=F32) + ba_ref[d:d + 1, ls])
            gi = _sigmoid(jnp.dot(xcb, wi_ref[d, nb], preferred_element_type=F32) + bi_ref[d:d + 1, ls])
            decay = RG_C * _softplus(-lam_ref[d:d + 1, ls])
            a_ref[nb] = jnp.exp2(r * (decay * (-LOG2_E)))
            t = jnp.tanh(r * decay)
            prod = t * (1.0 + t)
            u_ref[nb] = jnp.where(prod > 0.0, t * lax.rsqrt(prod), 0.0) * (gi * xc_r2)

    def steps(j, carry):
        base_f = pl.multiple_of(j * (SCAN_UNROLL * v), SCAN_UNROLL * v)
        base_b = pl.multiple_of((SCAN_L - SCAN_UNROLL) * v - j * (SCAN_UNROLL * v), SCAN_UNROLL * v)
        carry = list(carry)
        for k in range(SCAN_UNROLL):
            sf = pl.ds(base_f + k * v, v)
            sb = pl.ds(base_b + (SCAN_UNROLL - 1 - k) * v, v)
            for nb in range(nbt):
                hf, pf, hb, pb = carry[4 * nb:4 * nb + 4]
                a = af_ref[nb, sf, :]
                hf = a * hf + uf_ref[nb, sf, :]
                pf = a * pf
                hf_ref[nb, sf, :] = hf
                pf_ref[nb, sf, :] = pf
                a = ab_ref[nb, sb, :]
                hb = a * hb + ub_ref[nb, sb, :]
                pb = a * pb
                hb_ref[nb, sb, :] = hb
                pb_ref[nb, sb, :] = pb
                carry[4 * nb:4 * nb + 4] = [hf, pf, hb, pb]
        return tuple(carry)

    zero = jnp.zeros((v, RG_BLOCK), F32)
    one = jnp.ones((v, RG_BLOCK), F32)
    lax.fori_loop(0, SCAN_L // SCAN_UNROLL, steps, (zero, one, zero, one) * nbt)

    for nb in range(nbt):
        ls = slice(nb * RG_BLOCK, (nb + 1) * RG_BLOCK)
        h_end, p_end = hf_ref[nb, rows - v:, :], pf_ref[nb, rows - v:, :]
        entry = [chain * h0_ref[0:1, ls]]
        for s in range(v - 1):
            entry.append(chain * (h_end[s:s + 1] + p_end[s:s + 1] * entry[s]))
        ef = jnp.concatenate(entry, axis=0)
        h_start, p_start = hb_ref[nb, :v, :], pb_ref[nb, :v, :]
        entry = [chain * h0_ref[1:2, ls]]
        for s in range(v - 1, 0, -1):
            entry.append(chain * (h_start[s:s + 1] + p_start[s:s + 1] * entry[-1]))
        eb = jnp.concatenate(entry[::-1], axis=0)
        shape3 = (SCAN_L, v, RG_BLOCK)
        hf = hf_ref[nb].reshape(shape3) + pf_ref[nb].reshape(shape3) * ef[None]
        hb = hb_ref[nb].reshape(shape3) + pb_ref[nb].reshape(shape3) * eb[None]
        stf_ref[:, ls] = hf[SCAN_L - 1]
        stb_ref[:, ls] = hb[0]
        y_ref[:, ls] = (hf + hb).reshape(rows, RG_BLOCK) * gate_ref[:, ls]


def _rg_scan(xbr, gate, conv_w, conv_b, w_a, b_a, w_i, b_i, lam, h0):
    n_blk = N_TOK // GROUP_ROWS
    nbt = RG_CT // RG_BLOCK
    tile = pl.BlockSpec((GROUP_ROWS, RG_CT), lambda b, c: (b, c))
    vec2 = pl.BlockSpec((2, RG_CT), lambda b, c: (0, c))
    wspec = pl.BlockSpec((2, nbt, RG_BLOCK, RG_BLOCK), lambda b, c: (0, c, 0, 0))
    return pl.pallas_call(
        _rg_scan_kernel,
        grid=(n_blk, D_RNN // RG_CT),
        in_specs=[
            tile, tile,
            pl.BlockSpec((CONV_W, RG_CT), lambda b, c: (0, c)),
            pl.BlockSpec((1, RG_CT), lambda b, c: (0, c)),
            wspec, vec2, wspec, vec2, vec2,
            pl.BlockSpec((None, 2, RG_CT), lambda b, c: (jnp.maximum(b - PROMPT_BLOCKS, 0), 0, c)),
        ],
        out_specs=[tile] + [pl.BlockSpec((SCAN_V, RG_CT), lambda b, c: (b, c))] * 2,
        out_shape=[jax.ShapeDtypeStruct((N_TOK, D_RNN), F32)]
        + [jax.ShapeDtypeStruct((n_blk * SCAN_V, D_RNN), F32)] * 2,
        scratch_shapes=[pltpu.VMEM((nbt, GROUP_ROWS, RG_BLOCK), F32)] * 8,
        compiler_params=_cparams("arbitrary", "arbitrary"),
        name="rg_scan",
    )(xbr, gate, conv_w, conv_b, w_a, b_a, w_i, b_i, lam, h0)


def _chunk_tri():
    ri = lax.broadcasted_iota(jnp.int32, (GLA_C, GLA_C), 0)
    ci = lax.broadcasted_iota(jnp.int32, (GLA_C, GLA_C), 1)
    return (ri >= ci).astype(BF16), (ri <= ci).astype(BF16)


def _split3(x):
    hi = x.astype(BF16)
    r1 = x - hi.astype(F32)
    mid = r1.astype(BF16)
    lo = (r1 - mid.astype(F32)).astype(BF16)
    return hi, mid, lo


def _gla_proj_kernel(x_ref, mod_ref, gain_ref, w_ref, wz_ref, w2_ref, bg_ref,
                     q_ref, k_ref, v_ref, r_ref, cum_ref):
    hb = _modulated(x_ref[...], mod_ref, gain_ref, 1).astype(BF16)
    z = jnp.dot(hb, wz_ref[...].astype(BF16), preferred_element_type=F32).astype(BF16)
    zz = jnp.dot(z, w2_ref[...], preferred_element_type=F32) + bg_ref[...]
    log_gate = -_softplus(-zz) * (LOG2_E / GLA_TAU)
    parts = _split3(log_gate)
    r = jnp.dot(hb, w_ref[:, 2 * GLA_QK + GLA_V:], preferred_element_type=F32)
    r_ref[...] = r * _sigmoid(r)
    q_ref[...] = jnp.dot(hb, w_ref[:, :GLA_QK], preferred_element_type=F32) * (GLA_DK ** -0.5)
    k_ref[...] = jnp.dot(hb, w_ref[:, GLA_QK:2 * GLA_QK], preferred_element_type=F32)
    v_ref[...] = jnp.dot(hb, w_ref[:, 2 * GLA_QK:2 * GLA_QK + GLA_V], preferred_element_type=F32).astype(BF16)
    tri_f, tri_b = _chunk_tri()
    for ch in range(TM_PROJ // GLA_C):
        rows = slice(ch * GLA_C, (ch + 1) * GLA_C)
        cum_ref[rows, :GLA_QK] = sum(jnp.dot(tri_f, p[rows, :GLA_QK], preferred_element_type=F32) for p in parts)
        cum_ref[rows, GLA_QK:] = sum(jnp.dot(tri_b, p[rows, GLA_QK:], preferred_element_type=F32) for p in parts)


def _gla_proj(x, mods_l, gains_l, w_main, w_z, w2cat, bg):
    def rows(w):
        return pl.BlockSpec((TM_PROJ, w), lambda i: (i, 0))

    def whole(a):
        return pl.BlockSpec(a.shape, lambda i: (0,) * a.ndim, pipeline_mode=pl.Buffered(1))

    widths = (GLA_QK, GLA_QK, GLA_V, GLA_V, 2 * GLA_QK)
    dtypes = (F32, F32, BF16, F32, F32)
    return pl.pallas_call(
        _gla_proj_kernel,
        grid=(N_TOK // TM_PROJ,),
        in_specs=[
            rows(D_MODEL),
            pl.BlockSpec((None, N_MOD, D_MODEL), lambda i: (_group_of_tile(i, TM_PROJ), 0, 0)),
            pl.BlockSpec((3, D_MODEL), lambda i: (0, 0)),
            whole(w_main), whole(w_z), whole(w2cat), whole(bg),
        ],
        out_specs=[rows(w) for w in widths],
        out_shape=[jax.ShapeDtypeStruct((N_TOK, w), t) for w, t in zip(widths, dtypes)],
        compiler_params=_cparams("arbitrary"),
        name="gla_in_proj",
    )(x, mods_l, gains_l, w_main, w_z, w2cat, bg)


def _dot_tn(a, b):
    return lax.dot_general(a, b, (((0,), (0,)), ((), ())), preferred_element_type=F32)


def _dot_nt(a, b):
    return lax.dot_general(a, b, (((1,), (1,)), ((), ())), preferred_element_type=F32)


def _level_ref(cum_ref, row0, cum, m, backward):
    c = GLA_C
    pick = m - 1 if backward else m

    def bcast(r, n):
        return jnp.broadcast_to(cum_ref[pl.ds(row0 + r, 1), :], (n, GLA_DK))

    if m >= 4:
        return jnp.concatenate([bcast(g * 2 * m + pick, 2 * m) for g in range(c // (2 * m))], axis=0)
    pos = lax.broadcasted_iota(jnp.int32, (c, GLA_DK), 0)
    if m == 2:
        lo = jnp.concatenate([bcast(g * 8 + pick, 8) for g in range(c // 8)], axis=0)
        hi = jnp.concatenate([bcast(g * 8 + 4 + pick, 8) for g in range(c // 8)], axis=0)
        return jnp.where((pos & 4) == 0, lo, hi)
    if backward:
        return jnp.where((pos & 1) == 1, pltpu.roll(cum, 1, 0), cum)
    return jnp.where((pos & 1) == 0, pltpu.roll(cum, c - 1, 0), cum)


def _gla_intra(q_ref, k_ref, v_ref, cf_ref, cb_ref, row0, diag, masks, uppers):
    rows = pl.ds(row0, GLA_C)
    cf, cb = cf_ref[rows, :], cb_ref[rows, :]
    qb, kb = q_ref[rows, :].astype(BF16), k_ref[rows, :].astype(BF16)
    scores = jnp.where(diag, 2.0 * _dot_nt(qb, kb), 0.0)
    for lvl, m in enumerate(GLA_LEVELS):
        df = cf - _level_ref(cf_ref, row0, cf, m, False)
        db = cb - _level_ref(cb_ref, row0, cb, m, True)
        gq = jnp.exp2(jnp.where(uppers[lvl], df, db)).astype(BF16)
        gk = jnp.exp2(-jnp.where(uppers[lvl], db, df)).astype(BF16)
        scores = jnp.where(masks[lvl], _dot_nt(qb * gq, kb * gk), scores)
    return jnp.dot(scores.astype(BF16), v_ref[rows, :], preferred_element_type=F32)


def _gla_inter(q_ref, k_ref, v_ref, cum_ref, row0, st, backward):
    rows = pl.ds(row0, GLA_C)
    cum = cum_ref[rows, :]
    tot = cum_ref[pl.ds(row0 + (0 if backward else GLA_C - 1), 1), :]
    o = _dot_nt((q_ref[rows, :] * jnp.exp2(cum)).astype(BF16), st.astype(BF16))
    kd = (k_ref[rows, :] * jnp.exp2(tot - cum)).astype(BF16)
    return o, st * jnp.exp2(tot) + _dot_tn(v_ref[rows, :], kd)


def _gla_kernel(*refs, t_len, seqs, has_s0, intra_unroll):
    if has_s0:
        q_ref, k_ref, v_ref, cf_ref, cb_ref, r_ref, gn_ref, s0_ref, y_ref, sf_ref, sb_ref = refs
    else:
        q_ref, k_ref, v_ref, cf_ref, cb_ref, r_ref, gn_ref, y_ref, sout_ref, sf_ref, sb_ref = refs
    c = GLA_C
    n_chunks = t_len // c
    ri = lax.broadcasted_iota(jnp.int32, (c, c), 0)
    ci = lax.broadcasted_iota(jnp.int32, (c, c), 1)
    x = ri ^ ci
    masks = [(x >= m) & (x < 2 * m) for m in GLA_LEVELS]
    pos = lax.broadcasted_iota(jnp.int32, (c, GLA_DK), 0)
    uppers = [(pos & m) != 0 for m in GLA_LEVELS]

    def intra(i, carry):
        row0 = pl.multiple_of(i * c, c)
        y_ref[pl.ds(row0, c), :] = _gla_intra(q_ref, k_ref, v_ref, cf_ref, cb_ref, row0, ri == ci, masks, uppers)
        return carry

    lax.fori_loop(0, seqs * n_chunks, intra, 0, unroll=min(seqs * n_chunks, intra_unroll))

    for s in range(seqs):
        if has_s0:
            sf_ref[s] = s0_ref[s, 0].T
            sb_ref[s] = s0_ref[s, 1].T
        else:
            sf_ref[s] = jnp.zeros((GLA_DV, GLA_DK), F32)
            sb_ref[s] = jnp.zeros((GLA_DV, GLA_DK), F32)

        def inter(i, carry, s=s):
            rf = pl.multiple_of(s * t_len + i * c, c)
            rb = pl.multiple_of(s * t_len + (n_chunks - 1 - i) * c, c)
            o, st = _gla_inter(q_ref, k_ref, v_ref, cf_ref, rf, sf_ref[s], False)
            y_ref[pl.ds(rf, c), :] += o
            sf_ref[s] = st
            o, st = _gla_inter(q_ref, k_ref, v_ref, cb_ref, rb, sb_ref[s], True)
            y_ref[pl.ds(rb, c), :] += o
            sb_ref[s] = st
            return carry

        lax.fori_loop(0, n_chunks, inter, 0, unroll=min(n_chunks, GLA_INTER_UNROLL))
        if not has_s0:
            sout_ref[s, 0] = sf_ref[s].T
            sout_ref[s, 1] = sb_ref[s].T
    o = y_ref[...]
    y_ref[...] = _rms(o) * gn_ref[...] * r_ref[...]


def _gla_mix(q, k, v, cum, r, gnorm, s0, *, t_len, seqs, row0, n_seq):
    has_s0 = s0 is not None
    blk = seqs * t_len
    blk0 = row0 // blk
    assert blk0 * blk == row0 and n_seq % seqs == 0

    def rows(w, off=0):
        return pl.BlockSpec((blk, w), lambda s, h: (blk0 + s, off + h))

    state_spec = pl.BlockSpec((seqs, 2, None, GLA_DK, GLA_DV), lambda s, h: (s, 0, h, 0, 0))
    in_specs = [rows(GLA_DK), rows(GLA_DK), rows(GLA_DV), rows(GLA_DK), rows(GLA_DK, GLA_HEADS),
                rows(GLA_DV), pl.BlockSpec((None, 1, GLA_DV), lambda s, h: (h, 0, 0))]
    args = [q, k, v, cum, cum, r, gnorm.reshape(GLA_HEADS, 1, GLA_DV)]
    y_spec = pl.BlockSpec((blk, GLA_DV), lambda s, h: (s, h))
    y_shape = jax.ShapeDtypeStruct((n_seq * t_len, GLA_V), F32)
    if has_s0:
        in_specs.append(state_spec)
        args.append(s0)
        out_specs, out_shape = y_spec, y_shape
    else:
        out_specs = [y_spec, state_spec]
        out_shape = [y_shape, jax.ShapeDtypeStruct((n_seq, 2, GLA_HEADS, GLA_DK, GLA_DV), F32)]
    return pl.pallas_call(
        functools.partial(_gla_kernel, t_len=t_len, seqs=seqs, has_s0=has_s0,
                          intra_unroll=GLA_INTRA_UNROLL_SHORT if t_len == SEQ else GLA_INTRA_UNROLL),
        grid=(n_seq // seqs, GLA_HEADS),
        in_specs=in_specs,
        out_specs=out_specs,
        out_shape=out_shape,
        scratch_shapes=[pltpu.VMEM((seqs, GLA_DV, GLA_DK), F32)] * 2,
        compiler_params=_cparams("arbitrary", "arbitrary"),
        name="gla_mix_sample" if has_s0 else "gla_mix_prompt",
    )(*args)


def _natural_to_scan(x_half):
    x4 = x_half.reshape(-1, SCAN_V, SCAN_L, D_MODEL)
    return x4.swapaxes(1, 2).reshape(-1, D_MODEL)


def _scan_to_natural(x_half):
    x4 = x_half.reshape(-1, SCAN_L, SCAN_V, D_MODEL)
    return x4.swapaxes(1, 2).reshape(-1, D_MODEL)


def _natural_to_col_major(x_half):
    x4 = x_half.reshape(DEC_BATCH, DEC_SEQ // GRID_W, GRID_W, D_MODEL)
    return x4.swapaxes(1, 2).reshape(-1, D_MODEL)


def _col_major_to_natural(x_half):
    x4 = x_half.reshape(DEC_BATCH, GRID_W, DEC_SEQ // GRID_W, D_MODEL)
    return x4.swapaxes(1, 2).reshape(-1, D_MODEL)


def kernel(x_prompt, x_sample, c, state_rglru, state_gla, c_ctx, ada_w, ada_b, norm_g, ffn_w_in, ffn_w_out,
           rg_w_in, rg_conv_w, rg_conv_b, rg_w_a, rg_b_a, rg_w_i, rg_b_i, rg_lambda, rg_w_out,
           gla_w_in, gla_w_g2, gla_b_g, gla_norm, gla_w_out, final_norm):
    depth = ada_w.shape[0]
    cvecs = jnp.concatenate([c_ctx[None, :], c, jnp.zeros((8 - N_GROUPS, D_MODEL), F32)], axis=0)
    mods = _ada_mods(cvecs, ada_w, ada_b)
    fn = final_norm.reshape(1, D_MODEL)
    rg_states, gla_states = [], []

    w_in, w_out = ffn_w_in[0, 0].astype(BF16), ffn_w_out[0, 0].astype(BF16)
    n_main = 2 * GLA_QK + 2 * GLA_V

    def ffn_casts(layer, half):
        return ((ffn_w_in, (layer, half), 2 * D_FF), (ffn_w_out, (layer, half), D_MODEL))

    xp = x_prompt.reshape(N_PROMPT_TOK, D_MODEL)
    xs = x_sample.reshape(DEC_BATCH * DEC_SEQ, D_MODEL)
    for i in range(depth):
        mods_l, gains_l = mods[i], norm_g[i]
        j = i // 2
        last = i == depth - 1
        if i % 2 == 0:
            halves = (_natural_to_scan(xp), _natural_to_scan(xs))
            mixer_casts = ((rg_w_in, (j,), 2 * D_RNN), (rg_w_out, (j,), D_MODEL))
        else:
            halves = (xp, _natural_to_col_major(xs))
            mixer_casts = ((gla_w_in, (j,), n_main), (gla_w_out, (j,), D_MODEL))
        x, w_in, w_out, wm_in, wm_out = _half_ffn(halves, mods_l, gains_l, w_in, w_out, 0, fn, False, False,
                                                  casts=ffn_casts(i, 1) + mixer_casts)
        if i % 2 == 0:
            gate, xbr = _rg_proj(x, mods_l, gains_l, wm_in)
            y, st_f, st_b = _rg_scan(xbr, gate, rg_conv_w[j], rg_conv_b[j].reshape(1, D_RNN),
                                     rg_w_a[j].astype(BF16), rg_b_a[j], rg_w_i[j].astype(BF16), rg_b_i[j],
                                     rg_lambda[j], state_rglru[:, j])
            rg_states.append(jnp.stack([st_f[:BATCH], st_b[:BATCH]], axis=1))
            ys = (y,)
        else:
            w_z = jnp.pad(gla_w_in[j, :, n_main:], ((0, 0), (0, 128 - 2 * GLA_RANK)))
            w2cat = jnp.zeros((128, 2 * GLA_QK), F32)
            w2cat = w2cat.at[:GLA_RANK, :GLA_QK].set(gla_w_g2[j, 0])
            w2cat = w2cat.at[GLA_RANK:2 * GLA_RANK, GLA_QK:].set(gla_w_g2[j, 1]).astype(BF16)
            q, k, v, r, cum = _gla_proj(x, mods_l, gains_l, wm_in, w_z, w2cat, gla_b_g[j].reshape(1, 2 * GLA_QK))
            y_p, st = _gla_mix(q, k, v, cum, r, gla_norm[j], None, t_len=SEQ, seqs=GLA_PROMPT_SEQS, row0=0,
                               n_seq=BATCH)
            y_s = _gla_mix(q, k, v, cum, r, gla_norm[j], state_gla[:, j], t_len=DEC_SEQ, seqs=1,
                           row0=N_PROMPT_TOK, n_seq=DEC_BATCH)
            gla_states.append(st)
            ys = (y_p, y_s)
        xp, xs, *next_pair = _half_ffn((x,), mods_l, gains_l, w_in, w_out, 1, fn, last, True, mix=(ys, wm_out),
                                       casts=() if last else ffn_casts(i + 1, 0))
        if not last:
            w_in, w_out = next_pair
        if i % 2 == 0:
            xp, xs = _scan_to_natural(xp), _scan_to_natural(xs)
        else:
            xs = _col_major_to_natural(xs)

    y_prompt = xp.reshape(BATCH, SEQ, D_MODEL)
    y_sample = xs.reshape(DEC_BATCH, DEC_SEQ, D_MODEL)
    return (y_prompt, y_sample, jnp.stack(rg_states, axis=1), jnp.stack(gla_states, axis=1))
```

```python
import functools

import jax
import jax.numpy as jnp
from jax import lax
from jax.experimental import pallas as pl
from jax.experimental.pallas import tpu as pltpu

F32 = jnp.float32
BF16 = jnp.bfloat16

D_MODEL = 1024
BATCH = 16
SEQ = 256
DEC_BATCH = 2
DEC_SEQ = 2048
GRID_W = 64
EPS = 1e-6
N_MOD = 9
D_FF = 2816
FFN_RES = 0.5
D_RNN = 1280
RG_BLOCK = 128
CONV_W = 4
RG_C = 8.0
GLA_HEADS = 4
GLA_QK = 512
GLA_V = 1024
GLA_DK = 128
GLA_DV = 256
GLA_RANK = 16
GLA_TAU = 16.0
LOG2_E = 1.4426950408889634

N_PROMPT_TOK = BATCH * SEQ
N_TOK = N_PROMPT_TOK + DEC_BATCH * DEC_SEQ
N_GROUPS = 1 + DEC_BATCH
GROUP_ROWS = DEC_SEQ
PROMPT_BLOCKS = N_PROMPT_TOK // GROUP_ROWS

VMEM_LIMIT_BYTES = 56 * 1024 * 1024

TM = 512
HALF_TILES = N_PROMPT_TOK // TM
TM_PROJ = 1024
ADA_SLAB = 256
MXU_DIM = 256
BF16_SUBLANES = 16
FF_CHUNKS = (0, 6 * MXU_DIM, D_FF)
RG_CT = 256
SCAN_L = 256
SCAN_V = GROUP_ROWS // SCAN_L
SCAN_UNROLL = 8
GLA_C = 128
GLA_PROMPT_SEQS = 16
GLA_INTRA_UNROLL = 8
GLA_INTRA_UNROLL_SHORT = 16
GLA_INTER_UNROLL = 16
GLA_LEVELS = tuple(GLA_C >> (l + 1) for l in range(GLA_C.bit_length() - 1))


def _cparams(*sem):
    return pltpu.CompilerParams(dimension_semantics=sem, vmem_limit_bytes=VMEM_LIMIT_BYTES)


def _group_of_tile(i, rows):
    return jnp.maximum((i * rows) // GROUP_ROWS - (PROMPT_BLOCKS - 1), 0)


def _sigmoid(x):
    return 1.0 / (1.0 + jnp.exp2(x * (-LOG2_E)))


def _softplus(x):
    return jnp.maximum(x, 0.0) + jnp.log1p(jnp.exp(-jnp.abs(x)))


def _rms(x):
    return x * lax.rsqrt(jnp.mean(x * x, axis=-1, keepdims=True) + EPS)


def _modulated(x, mod_ref, gain_ref, j):
    y = _rms(x) * gain_ref[j:j + 1, :]
    return y * (1.0 + mod_ref[3 * j + 1:3 * j + 2, :]) + mod_ref[3 * j:3 * j + 1, :]


def _ada_kernel(c_ref, w_ref, b_ref, o_ref):
    @pl.when(pl.program_id(1) == 0)
    def _():
        o_ref[...] = jnp.broadcast_to(b_ref[...], o_ref.shape)

    c = c_ref[...]
    s = (c * _sigmoid(c)).astype(BF16)
    o_ref[...] += jnp.dot(s, w_ref[...].astype(BF16), preferred_element_type=F32)


def _ada_mods(cvecs, ada_w, ada_b):
    depth = ada_w.shape[0]
    n_slab = D_MODEL // ADA_SLAB
    c_slabs = cvecs.reshape(8, n_slab, ADA_SLAB).swapaxes(0, 1)
    out = pl.pallas_call(
        _ada_kernel,
        grid=(depth, n_slab),
        in_specs=[
            pl.BlockSpec((None, 8, ADA_SLAB), lambda l, k: (k, 0, 0)),
            pl.BlockSpec((None, ADA_SLAB, N_MOD * D_MODEL), lambda l, k: (l, k, 0)),
            pl.BlockSpec((None, 1, N_MOD * D_MODEL), lambda l, k: (l, 0, 0)),
        ],
        out_specs=pl.BlockSpec((None, 8, N_MOD * D_MODEL), lambda l, k: (l, 0, 0)),
        out_shape=jax.ShapeDtypeStruct((depth, 8, N_MOD * D_MODEL), F32),
        compiler_params=_cparams("arbitrary", "arbitrary"),
        name="ada_mod",
    )(c_slabs, ada_w, ada_b.reshape(depth, 1, N_MOD * D_MODEL))
    return out[:, :N_GROUPS, :].reshape(depth, N_GROUPS, N_MOD, D_MODEL)


def _pick_half(refs):
    if len(refs) == 1:
        return refs[0][...]
    return jnp.where(pl.program_id(0) < HALF_TILES, refs[0][...], refs[1][...])


def _ffn_kernel(*refs, j, final, n_x, n_y, split_out, n_cast):
    x_refs, y_refs, refs = refs[:n_x], refs[n_x:n_x + n_y], refs[n_x + n_y:]
    if n_y:
        wmix_ref, refs = refs[0], refs[1:]
    mod_ref, gain_ref, win_ref, wout_ref, fn_ref = refs[:5]
    refs = refs[5:]
    if n_cast:
        f32_refs, refs = refs[:n_cast], refs[n_cast:]
        for src_ref, dst_ref in zip(f32_refs, refs[len(refs) - n_cast:]):
            dst_ref[...] = src_ref[...].astype(BF16)
    o_refs = refs
    x = _pick_half(x_refs)
    if n_y:
        y = _pick_half(y_refs).astype(BF16)
        x = x + mod_ref[5:6, :] * jnp.dot(y, wmix_ref[...], preferred_element_type=F32)
    hb = _modulated(x, mod_ref, gain_ref, j).astype(BF16)
    acc = jnp.zeros((TM, D_MODEL), F32)
    for lo, hi in zip(FF_CHUNKS[:-1], FF_CHUNKS[1:]):
        g = jnp.dot(hb, win_ref[:, lo:hi], preferred_element_type=F32)
        u = jnp.dot(hb, win_ref[:, D_FF + lo:D_FF + hi], preferred_element_type=F32)
        a = (g * _sigmoid(g) * u).astype(BF16)
        acc = acc + jnp.dot(a, wout_ref[lo:hi, :], preferred_element_type=F32)
    out = x + FFN_RES * mod_ref[3 * j + 2:3 * j + 3, :] * acc
    if final:
        out = _rms(out) * fn_ref[...]
    if split_out:
        first_half = pl.program_id(0) < HALF_TILES

        @pl.when(first_half)
        def _():
            o_refs[0][...] = out

        @pl.when(jnp.logical_not(first_half))
        def _():
            o_refs[1][...] = out
    else:
        o_refs[0][...] = out


def _row_specs(width, split):
    if not split:
        return [pl.BlockSpec((TM, width), lambda i: (i, 0))]
    return [pl.BlockSpec((TM, width), lambda i: (jnp.minimum(i, HALF_TILES - 1), 0)),
            pl.BlockSpec((TM, width), lambda i: (jnp.maximum(i - HALF_TILES, 0), 0))]


def _half_ffn(xs, mods_l, gains_l, w_in, w_out, half, final_norm, final, split_out, mix=None, casts=()):
    n_steps = N_TOK // TM
    ys, w_mix = mix if mix is not None else ((), None)
    in_specs = _row_specs(D_MODEL, len(xs) == 2)
    args = list(xs)
    if ys:
        in_specs += _row_specs(ys[0].shape[1], len(ys) == 2)
        in_specs.append(pl.BlockSpec(w_mix.shape, lambda i: (0, 0), pipeline_mode=pl.Buffered(1)))
        args += list(ys) + [w_mix]
    in_specs += [
        pl.BlockSpec((None, N_MOD, D_MODEL), lambda i: (_group_of_tile(i, TM), 0, 0)),
        pl.BlockSpec((3, D_MODEL), lambda i: (0, 0)),
        pl.BlockSpec((D_MODEL, 2 * D_FF), lambda i: (0, 0), pipeline_mode=pl.Buffered(1)),
        pl.BlockSpec((D_FF, D_MODEL), lambda i: (0, 0), pipeline_mode=pl.Buffered(1)),
        pl.BlockSpec((1, D_MODEL), lambda i: (0, 0)),
    ]
    args += [mods_l, gains_l, w_in, w_out, final_norm]
    n_rows = N_PROMPT_TOK if split_out else N_TOK
    out_specs = _row_specs(D_MODEL, split_out)
    out_shape = [jax.ShapeDtypeStruct((n_rows, D_MODEL), F32)] * (2 if split_out else 1)
    for arr, lead, cols in casts:
        rows = arr.shape[len(lead)]
        slab = rows // n_steps
        assert slab * n_steps == rows and slab % BF16_SUBLANES == 0, (rows, n_steps)
        in_specs.append(pl.BlockSpec((None,) * len(lead) + (slab, cols), lambda i, lead=lead: (*lead, i, 0)))
        args.append(arr)
        out_specs.append(pl.BlockSpec((slab, cols), lambda i: (i, 0)))
        out_shape.append(jax.ShapeDtypeStruct((rows, cols), BF16))
    return pl.pallas_call(
        functools.partial(_ffn_kernel, j=2 * half, final=final, n_x=len(xs), n_y=len(ys), split_out=split_out,
                          n_cast=len(casts)),
        grid=(n_steps,),
        in_specs=in_specs,
        out_specs=out_specs,
        out_shape=out_shape,
        compiler_params=_cparams("arbitrary"),
        name="half_ffn",
    )(*args)


def _gelu_tanh(x):
    return 0.5 * x * (1.0 + jnp.tanh(0.7978845608028654 * (x + 0.044715 * (x * x * x))))


def _rg_proj_kernel(x_ref, mod_ref, gain_ref, w_ref, gate_ref, xbr_ref):
    hb = _modulated(x_ref[...], mod_ref, gain_ref, 1).astype(BF16)
    gate_ref[...] = _gelu_tanh(jnp.dot(hb, w_ref[:, :D_RNN], preferred_element_type=F32))
    xbr_ref[...] = jnp.dot(hb, w_ref[:, D_RNN:], preferred_element_type=F32)


def _rg_proj(x, mods_l, gains_l, w_in):
    return pl.pallas_call(
        _rg_proj_kernel,
        grid=(N_TOK // TM_PROJ,),
        in_specs=[
            pl.BlockSpec((TM_PROJ, D_MODEL), lambda i: (i, 0)),
            pl.BlockSpec((None, N_MOD, D_MODEL), lambda i: (_group_of_tile(i, TM_PROJ), 0, 0)),
            pl.BlockSpec((3, D_MODEL), lambda i: (0, 0)),
            pl.BlockSpec((D_MODEL, 2 * D_RNN), lambda i: (0, 0), pipeline_mode=pl.Buffered(1)),
        ],
        out_specs=[pl.BlockSpec((TM_PROJ, D_RNN), lambda i: (i, 0))] * 2,
        out_shape=[jax.ShapeDtypeStruct((N_TOK, D_RNN), F32)] * 2,
        compiler_params=_cparams("arbitrary"),
        name="rg_in_proj",
    )(x, mods_l, gains_l, w_in)


def _seg_shift(row, down):
    v = lax.broadcasted_iota(jnp.int32, row.shape, 0)
    if down:
        return jnp.where(v >= 1, pltpu.roll(row, 1, 0), 0.0)
    return jnp.where(v < SCAN_V - 1, pltpu.roll(row, SCAN_V - 1, 0), 0.0)


def _rg_scan_kernel(xbr_ref, gate_ref, cw_ref, cb_ref, wa_ref, ba_ref, wi_ref, bi_ref, lam_ref, h0_ref,
                    y_ref, stf_ref, stb_ref, af_ref, uf_ref, ab_ref, ub_ref, pf_ref, hf_ref, pb_ref, hb_ref):
    chain = jnp.where(pl.program_id(0) >= PROMPT_BLOCKS, 1.0, 0.0).astype(F32)
    nbt = RG_CT // RG_BLOCK
    rows = SCAN_L * SCAN_V
    v = SCAN_V

    for nb in range(nbt):
        ls = slice(nb * RG_BLOCK, (nb + 1) * RG_BLOCK)
        x = xbr_ref[:, ls]
        before = chain * _seg_shift(x[rows - v:], True)
        after0 = chain * _seg_shift(x[:v], False)
        after1 = chain * _seg_shift(x[v:2 * v], False)
        xm1 = jnp.concatenate([before, x[:rows - v]], axis=0)
        xp1 = jnp.concatenate([x[v:], after0], axis=0)
        xp2 = jnp.concatenate([x[2 * v:], after0, after1], axis=0)
        xc = (cw_ref[0:1, ls] * xm1 + cw_ref[1:2, ls] * x + cw_ref[2:3, ls] * xp1 + cw_ref[3:4, ls] * xp2
              + cb_ref[0:1, ls])
        xcb = xc.astype(BF16)
        xc_r2 = xc * (2.0 ** 0.5)
        for d, (a_ref, u_ref) in enumerate(((af_ref, uf_ref), (ab_ref, ub_ref))):
            r = _sigmoid(jnp.dot(xcb, wa_ref[d, nb], preferred_element_type=F32) + ba_ref[d:d + 1, ls])
            gi = _sigmoid(jnp.dot(xcb, wi_ref[d, nb], preferred_element_type=F32) + bi_ref[d:d + 1, ls])
            decay = RG_C * _softplus(-lam_ref[d:d + 1, ls])
            a_ref[nb] = jnp.exp2(r * (decay * (-LOG2_E)))
            t = jnp.tanh(r * decay)
            prod = t * (1.0 + t)
            u_ref[nb] = jnp.where(prod > 0.0, t * lax.rsqrt(prod), 0.0) * (gi * xc_r2)

    def steps(j, carry):
        base_f = pl.multiple_of(j * (SCAN_UNROLL * v), SCAN_UNROLL * v)
        base_b = pl.multiple_of((SCAN_L - SCAN_UNROLL) * v - j * (SCAN_UNROLL * v), SCAN_UNROLL * v)
        carry = list(carry)
        for k in range(SCAN_UNROLL):
            sf = pl.ds(base_f + k * v, v)
            sb = pl.ds(base_b + (SCAN_UNROLL - 1 - k) * v, v)
            for nb in range(nbt):
                hf, pf, hb, pb = carry[4 * nb:4 * nb + 4]
                a = af_ref[nb, sf, :]
                hf = a * hf + uf_ref[nb, sf, :]
                pf = a * pf
                hf_ref[nb, sf, :] = hf
                pf_ref[nb, sf, :] = pf
                a = ab_ref[nb, sb, :]
                hb = a * hb + ub_ref[nb, sb, :]
                pb = a * pb
                hb_ref[nb, sb, :] = hb
                pb_ref[nb, sb, :] = pb
                carry[4 * nb:4 * nb + 4] = [hf, pf, hb, pb]
        return tuple(carry)

    zero = jnp.zeros((v, RG_BLOCK), F32)
    one = jnp.ones((v, RG_BLOCK), F32)
    lax.fori_loop(0, SCAN_L // SCAN_UNROLL, steps, (zero, one, zero, one) * nbt)

    for nb in range(nbt):
        ls = slice(nb * RG_BLOCK, (nb + 1) * RG_BLOCK)
        h_end, p_end = hf_ref[nb, rows - v:, :], pf_ref[nb, rows - v:, :]
        entry = [chain * h0_ref[0:1, ls]]
        for s in range(v - 1):
            entry.append(chain * (h_end[s:s + 1] + p_end[s:s + 1] * entry[s]))
        ef = jnp.concatenate(entry, axis=0)
        h_start, p_start = hb_ref[nb, :v, :], pb_ref[nb, :v, :]
        entry = [chain * h0_ref[1:2, ls]]
        for s in range(v - 1, 0, -1):
            entry.append(chain * (h_start[s:s + 1] + p_start[s:s + 1] * entry[-1]))
        eb = jnp.concatenate(entry[::-1], axis=0)
        shape3 = (SCAN_L, v, RG_BLOCK)
        hf = hf_ref[nb].reshape(shape3) + pf_ref[nb].reshape(shape3) * ef[None]
        hb = hb_ref[nb].reshape(shape3) + pb_ref[nb].reshape(shape3) * eb[None]
        stf_ref[:, ls] = hf[SCAN_L - 1]
        stb_ref[:, ls] = hb[0]
        y_ref[:, ls] = (hf + hb).reshape(rows, RG_BLOCK) * gate_ref[:, ls]


def _rg_scan(xbr, gate, conv_w, conv_b, w_a, b_a, w_i, b_i, lam, h0):
    n_blk = N_TOK // GROUP_ROWS
    nbt = RG_CT // RG_BLOCK
    tile = pl.BlockSpec((GROUP_ROWS, RG_CT), lambda b, c: (b, c))
    vec2 = pl.BlockSpec((2, RG_CT), lambda b, c: (0, c))
    wspec = pl.BlockSpec((2, nbt, RG_BLOCK, RG_BLOCK), lambda b, c: (0, c, 0, 0))
    return pl.pallas_call(
        _rg_scan_kernel,
        grid=(n_blk, D_RNN // RG_CT),
        in_specs=[
            tile, tile,
            pl.BlockSpec((CONV_W, RG_CT), lambda b, c: (0, c)),
            pl.BlockSpec((1, RG_CT), lambda b, c: (0, c)),
            wspec, vec2, wspec, vec2, vec2,
            pl.BlockSpec((None, 2, RG_CT), lambda b, c: (jnp.maximum(b - PROMPT_BLOCKS, 0), 0, c)),
        ],
        out_specs=[tile] + [pl.BlockSpec((SCAN_V, RG_CT), lambda b, c: (b, c))] * 2,
        out_shape=[jax.ShapeDtypeStruct((N_TOK, D_RNN), F32)]
        + [jax.ShapeDtypeStruct((n_blk * SCAN_V, D_RNN), F32)] * 2,
        scratch_shapes=[pltpu.VMEM((nbt, GROUP_ROWS, RG_BLOCK), F32)] * 8,
        compiler_params=_cparams("arbitrary", "arbitrary"),
        name="rg_scan",
    )(xbr, gate, conv_w, conv_b, w_a, b_a, w_i, b_i, lam, h0)


def _chunk_tri():
    ri = lax.broadcasted_iota(jnp.int32, (GLA_C, GLA_C), 0)
    ci = lax.broadcasted_iota(jnp.int32, (GLA_C, GLA_C), 1)
    return (ri >= ci).astype(BF16), (ri <= ci).astype(BF16)


def _split3(x):
    hi = x.astype(BF16)
    r1 = x - hi.astype(F32)
    mid = r1.astype(BF16)
    lo = (r1 - mid.astype(F32)).astype(BF16)
    return hi, mid, lo


def _gla_proj_kernel(x_ref, mod_ref, gain_ref, w_ref, wz_ref, w2_ref, bg_ref,
                     q_ref, k_ref, v_ref, r_ref, cum_ref):
    hb = _modulated(x_ref[...], mod_ref, gain_ref, 1).astype(BF16)
    z = jnp.dot(hb, wz_ref[...].astype(BF16), preferred_element_type=F32).astype(BF16)
    zz = jnp.dot(z, w2_ref[...], preferred_element_type=F32) + bg_ref[...]
    log_gate = -_softplus(-zz) * (LOG2_E / GLA_TAU)
    parts = _split3(log_gate)
    r = jnp.dot(hb, w_ref[:, 2 * GLA_QK + GLA_V:], preferred_element_type=F32)
    r_ref[...] = r * _sigmoid(r)
    q_ref[...] = jnp.dot(hb, w_ref[:, :GLA_QK], preferred_element_type=F32) * (GLA_DK ** -0.5)
    k_ref[...] = jnp.dot(hb, w_ref[:, GLA_QK:2 * GLA_QK], preferred_element_type=F32)
    v_ref[...] = jnp.dot(hb, w_ref[:, 2 * GLA_QK:2 * GLA_QK + GLA_V], preferred_element_type=F32).astype(BF16)
    tri_f, tri_b = _chunk_tri()
    for ch in range(TM_PROJ // GLA_C):
        rows = slice(ch * GLA_C, (ch + 1) * GLA_C)
        cum_ref[rows, :GLA_QK] = sum(jnp.dot(tri_f, p[rows, :GLA_QK], preferred_element_type=F32) for p in parts)
        cum_ref[rows, GLA_QK:] = sum(jnp.dot(tri_b, p[rows, GLA_QK:], preferred_element_type=F32) for p in parts)


def _gla_proj(x, mods_l, gains_l, w_main, w_z, w2cat, bg):
    def rows(w):
        return pl.BlockSpec((TM_PROJ, w), lambda i: (i, 0))

    def whole(a):
        return pl.BlockSpec(a.shape, lambda i: (0,) * a.ndim, pipeline_mode=pl.Buffered(1))

    widths = (GLA_QK, GLA_QK, GLA_V, GLA_V, 2 * GLA_QK)
    dtypes = (F32, F32, BF16, F32, F32)
    return pl.pallas_call(
        _gla_proj_kernel,
        grid=(N_TOK // TM_PROJ,),
        in_specs=[
            rows(D_MODEL),
            pl.BlockSpec((None, N_MOD, D_MODEL), lambda i: (_group_of_tile(i, TM_PROJ), 0, 0)),
            pl.BlockSpec((3, D_MODEL), lambda i: (0, 0)),
            whole(w_main), whole(w_z), whole(w2cat), whole(bg),
        ],
        out_specs=[rows(w) for w in widths],
        out_shape=[jax.ShapeDtypeStruct((N_TOK, w), t) for w, t in zip(widths, dtypes)],
        compiler_params=_cparams("arbitrary"),
        name="gla_in_proj",
    )(x, mods_l, gains_l, w_main, w_z, w2cat, bg)


def _dot_tn(a, b):
    return lax.dot_general(a, b, (((0,), (0,)), ((), ())), preferred_element_type=F32)


def _dot_nt(a, b):
    return lax.dot_general(a, b, (((1,), (1,)), ((), ())), preferred_element_type=F32)


def _level_ref(cum_ref, row0, cum, m, backward):
    c = GLA_C
    pick = m - 1 if backward else m

    def bcast(r, n):
        return jnp.broadcast_to(cum_ref[pl.ds(row0 + r, 1), :], (n, GLA_DK))

    if m >= 4:
        return jnp.concatenate([bcast(g * 2 * m + pick, 2 * m) for g in range(c // (2 * m))], axis=0)
    pos = lax.broadcasted_iota(jnp.int32, (c, GLA_DK), 0)
    if m == 2:
        lo = jnp.concatenate([bcast(g * 8 + pick, 8) for g in range(c // 8)], axis=0)
        hi = jnp.concatenate([bcast(g * 8 + 4 + pick, 8) for g in range(c // 8)], axis=0)
        return jnp.where((pos & 4) == 0, lo, hi)
    if backward:
        return jnp.where((pos & 1) == 1, pltpu.roll(cum, 1, 0), cum)
    return jnp.where((pos & 1) == 0, pltpu.roll(cum, c - 1, 0), cum)


def _gla_intra(q_ref, k_ref, v_ref, cf_ref, cb_ref, row0, diag, masks, uppers):
    rows = pl.ds(row0, GLA_C)
    cf, cb = cf_ref[rows, :], cb_ref[rows, :]
    qb, kb = q_ref[rows, :].astype(BF16), k_ref[rows, :].astype(BF16)
    scores = jnp.where(diag, 2.0 * _dot_nt(qb, kb), 0.0)
    for lvl, m in enumerate(GLA_LEVELS):
        df = cf - _level_ref(cf_ref, row0, cf, m, False)
        db = cb - _level_ref(cb_ref, row0, cb, m, True)
        gq = jnp.exp2(jnp.where(uppers[lvl], df, db)).astype(BF16)
        gk = jnp.exp2(-jnp.where(uppers[lvl], db, df)).astype(BF16)
        scores = jnp.where(masks[lvl], _dot_nt(qb * gq, kb * gk), scores)
    return jnp.dot(scores.astype(BF16), v_ref[rows, :], preferred_element_type=F32)


def _gla_inter(q_ref, k_ref, v_ref, cum_ref, row0, st, backward):
    rows = pl.ds(row0, GLA_C)
    cum = cum_ref[rows, :]
    tot = cum_ref[pl.ds(row0 + (0 if backward else GLA_C - 1), 1), :]
    o = _dot_nt((q_ref[rows, :] * jnp.exp2(cum)).astype(BF16), st.astype(BF16))
    kd = (k_ref[rows, :] * jnp.exp2(tot - cum)).astype(BF16)
    return o, st * jnp.exp2(tot) + _dot_tn(v_ref[rows, :], kd)


def _gla_kernel(*refs, t_len, seqs, has_s0, intra_unroll):
    if has_s0:
        q_ref, k_ref, v_ref, cf_ref, cb_ref, r_ref, gn_ref, s0_ref, y_ref, sf_ref, sb_ref = refs
    else:
        q_ref, k_ref, v_ref, cf_ref, cb_ref, r_ref, gn_ref, y_ref, sout_ref, sf_ref, sb_ref = refs
    c = GLA_C
    n_chunks = t_len // c
    ri = lax.broadcasted_iota(jnp.int32, (c, c), 0)
    ci = lax.broadcasted_iota(jnp.int32, (c, c), 1)
    x = ri ^ ci
    masks = [(x >= m) & (x < 2 * m) for m in GLA_LEVELS]
    pos = lax.broadcasted_iota(jnp.int32, (c, GLA_DK), 0)
    uppers = [(pos & m) != 0 for m in GLA_LEVELS]

    def intra(i, carry):
        row0 = pl.multiple_of(i * c, c)
        y_ref[pl.ds(row0, c), :] = _gla_intra(q_ref, k_ref, v_ref, cf_ref, cb_ref, row0, ri == ci, masks, uppers)
        return carry

    lax.fori_loop(0, seqs * n_chunks, intra, 0, unroll=min(seqs * n_chunks, intra_unroll))

    for s in range(seqs):
        if has_s0:
            sf_ref[s] = s0_ref[s, 0].T
            sb_ref[s] = s0_ref[s, 1].T
        else:
            sf_ref[s] = jnp.zeros((GLA_DV, GLA_DK), F32)
            sb_ref[s] = jnp.zeros((GLA_DV, GLA_DK), F32)

        def inter(i, carry, s=s):
            rf = pl.multiple_of(s * t_len + i * c, c)
            rb = pl.multiple_of(s * t_len + (n_chunks - 1 - i) * c, c)
            o, st = _gla_inter(q_ref, k_ref, v_ref, cf_ref, rf, sf_ref[s], False)
            y_ref[pl.ds(rf, c), :] += o
            sf_ref[s] = st
            o, st = _gla_inter(q_ref, k_ref, v_ref, cb_ref, rb, sb_ref[s], True)
            y_ref[pl.ds(rb, c), :] += o
            sb_ref[s] = st
            return carry

        lax.fori_loop(0, n_chunks, inter, 0, unroll=min(n_chunks, GLA_INTER_UNROLL))
        if not has_s0:
            sout_ref[s, 0] = sf_ref[s].T
            sout_ref[s, 1] = sb_ref[s].T
    o = y_ref[...]
    y_ref[...] = _rms(o) * gn_ref[...] * r_ref[...]


def _gla_mix(q, k, v, cum, r, gnorm, s0, *, t_len, seqs, row0, n_seq):
    has_s0 = s0 is not None
    blk = seqs * t_len
    blk0 = row0 // blk
    assert blk0 * blk == row0 and n_seq % seqs == 0

    def rows(w, off=0):
        return pl.BlockSpec((blk, w), lambda s, h: (blk0 + s, off + h))

    state_spec = pl.BlockSpec((seqs, 2, None, GLA_DK, GLA_DV), lambda s, h: (s, 0, h, 0, 0))
    in_specs = [rows(GLA_DK), rows(GLA_DK), rows(GLA_DV), rows(GLA_DK), rows(GLA_DK, GLA_HEADS),
                rows(GLA_DV), pl.BlockSpec((None, 1, GLA_DV), lambda s, h: (h, 0, 0))]
    args = [q, k, v, cum, cum, r, gnorm.reshape(GLA_HEADS, 1, GLA_DV)]
    y_spec = pl.BlockSpec((blk, GLA_DV), lambda s, h: (s, h))
    y_shape = jax.ShapeDtypeStruct((n_seq * t_len, GLA_V), F32)
    if has_s0:
        in_specs.append(state_spec)
        args.append(s0)
        out_specs, out_shape = y_spec, y_shape
    else:
        out_specs = [y_spec, state_spec]
        out_shape = [y_shape, jax.ShapeDtypeStruct((n_seq, 2, GLA_HEADS, GLA_DK, GLA_DV), F32)]
    return pl.pallas_call(
        functools.partial(_gla_kernel, t_len=t_len, seqs=seqs, has_s0=has_s0,
                          intra_unroll=GLA_INTRA_UNROLL_SHORT if t_len == SEQ else GLA_INTRA_UNROLL),
        grid=(n_seq // seqs, GLA_HEADS),
        in_specs=in_specs,
        out_specs=out_specs,
        out_shape=out_shape,
        scratch_shapes=[pltpu.VMEM((seqs, GLA_DV, GLA_DK), F32)] * 2,
        compiler_params=_cparams("arbitrary", "arbitrary"),
        name="gla_mix_sample" if has_s0 else "gla_mix_prompt",
    )(*args)


def _natural_to_scan(x_half):
    x4 = x_half.reshape(-1, SCAN_V, SCAN_L, D_MODEL)
    return x4.swapaxes(1, 2).reshape(-1, D_MODEL)


def _scan_to_natural(x_half):
    x4 = x_half.reshape(-1, SCAN_L, SCAN_V, D_MODEL)
    return x4.swapaxes(1, 2).reshape(-1, D_MODEL)


def _natural_to_col_major(x_half):
    x4 = x_half.reshape(DEC_BATCH, DEC_SEQ // GRID_W, GRID_W, D_MODEL)
    return x4.swapaxes(1, 2).reshape(-1, D_MODEL)


def _col_major_to_natural(x_half):
    x4 = x_half.reshape(DEC_BATCH, GRID_W, DEC_SEQ // GRID_W, D_MODEL)
    return x4.swapaxes(1, 2).reshape(-1, D_MODEL)


def kernel(x_prompt, x_sample, c, state_rglru, state_gla, c_ctx, ada_w, ada_b, norm_g, ffn_w_in, ffn_w_out,
           rg_w_in, rg_conv_w, rg_conv_b, rg_w_a, rg_b_a, rg_w_i, rg_b_i, rg_lambda, rg_w_out,
           gla_w_in, gla_w_g2, gla_b_g, gla_norm, gla_w_out, final_norm):
    depth = ada_w.shape[0]
    cvecs = jnp.concatenate([c_ctx[None, :], c, jnp.zeros((8 - N_GROUPS, D_MODEL), F32)], axis=0)
    mods = _ada_mods(cvecs, ada_w, ada_b)
    fn = final_norm.reshape(1, D_MODEL)
    rg_states, gla_states = [], []

    w_in, w_out = ffn_w_in[0, 0].astype(BF16), ffn_w_out[0, 0].astype(BF16)
    n_main = 2 * GLA_QK + 2 * GLA_V

    def ffn_casts(layer, half):
        return ((ffn_w_in, (layer, half), 2 * D_FF), (ffn_w_out, (layer, half), D_MODEL))

    xp = x_prompt.reshape(N_PROMPT_TOK, D_MODEL)
    xs = x_sample.reshape(DEC_BATCH * DEC_SEQ, D_MODEL)
    for i in range(depth):
        mods_l, gains_l = mods[i], norm_g[i]
        j = i // 2
        last = i == depth - 1
        if i % 2 == 0:
            halves = (_natural_to_scan(xp), _natural_to_scan(xs))
            mixer_casts = ((rg_w_in, (j,), 2 * D_RNN), (rg_w_out, (j,), D_MODEL))
        else:
            halves = (xp, _natural_to_col_major(xs))
            mixer_casts = ((gla_w_in, (j,), n_main), (gla_w_out, (j,), D_MODEL))
        x, w_in, w_out, wm_in, wm_out = _half_ffn(halves, mods_l, gains_l, w_in, w_out, 0, fn, False, False,
                                                  casts=ffn_casts(i, 1) + mixer_casts)
        if i % 2 == 0:
            gate, xbr = _rg_proj(x, mods_l, gains_l, wm_in)
            y, st_f, st_b = _rg_scan(xbr, gate, rg_conv_w[j], rg_conv_b[j].reshape(1, D_RNN),
                                     rg_w_a[j].astype(BF16), rg_b_a[j], rg_w_i[j].astype(BF16), rg_b_i[j],
                                     rg_lambda[j], state_rglru[:, j])
            rg_states.append(jnp.stack([st_f[:BATCH], st_b[:BATCH]], axis=1))
            ys = (y,)
        else:
            w_z = jnp.pad(gla_w_in[j, :, n_main:], ((0, 0), (0, 128 - 2 * GLA_RANK)))
            w2cat = jnp.zeros((128, 2 * GLA_QK), F32)
            w2cat = w2cat.at[:GLA_RANK, :GLA_QK].set(gla_w_g2[j, 0])
            w2cat = w2cat.at[GLA_RANK:2 * GLA_RANK, GLA_QK:].set(gla_w_g2[j, 1]).astype(BF16)
            q, k, v, r, cum = _gla_proj(x, mods_l, gains_l, wm_in, w_z, w2cat, gla_b_g[j].reshape(1, 2 * GLA_QK))
            y_p, st = _gla_mix(q, k, v, cum, r, gla_norm[j], None, t_len=SEQ, seqs=GLA_PROMPT_SEQS, row0=0,
                               n_seq=BATCH)
            y_s = _gla_mix(q, k, v, cum, r, gla_norm[j], state_gla[:, j], t_len=DEC_SEQ, seqs=1,
                           row0=N_PROMPT_TOK, n_seq=DEC_BATCH)
            gla_states.append(st)
            ys = (y_p, y_s)
        xp, xs, *next_pair = _half_ffn((x,), mods_l, gains_l, w_in, w_out, 1, fn, last, True, mix=(ys, wm_out),
                                       casts=() if last else ffn_casts(i + 1, 0))
        if not last:
            w_in, w_out = next_pair
        if i % 2 == 0:
            xp, xs = _scan_to_natural(xp), _scan_to_natural(xs)
        else:
            xs = _col_major_to_natural(xs)

    y_prompt = xp.reshape(BATCH, SEQ, D_MODEL)
    y_sample = xs.reshape(DEC_BATCH, DEC_SEQ, D_MODEL)
    return (y_prompt, y_sample, jnp.stack(rg_states, axis=1), jnp.stack(gla_states, axis=1))
```

```python
import functools

import jax
import jax.numpy as jnp
from jax import lax
from jax.experimental import pallas as pl
from jax.experimental.pallas import tpu as pltpu

F32 = jnp.float32
BF16 = jnp.bfloat16

D_MODEL = 1024
BATCH = 16
SEQ = 256
DEC_BATCH = 2
DEC_SEQ = 2048
GRID_W = 64
EPS = 1e-6
N_MOD = 9
D_FF = 2816
FFN_RES = 0.5
D_RNN = 1280
RG_BLOCK = 128
CONV_W = 4
RG_C = 8.0
GLA_HEADS = 4
GLA_QK = 512
GLA_V = 1024
GLA_DK = 128
GLA_DV = 256
GLA_RANK = 16
GLA_TAU = 16.0
LOG2_E = 1.4426950408889634

N_PROMPT_TOK = BATCH * SEQ
N_TOK = N_PROMPT_TOK + DEC_BATCH * DEC_SEQ
N_GROUPS = 1 + DEC_BATCH
GROUP_ROWS = DEC_SEQ
PROMPT_BLOCKS = N_PROMPT_TOK // GROUP_ROWS

VMEM_LIMIT_BYTES = 56 * 1024 * 1024

TM = 512
HALF_TILES = N_PROMPT_TOK // TM
TM_PROJ = 1024
ADA_SLAB = 512
MXU_DIM = 256
BF16_SUBLANES = 16
FF_CHUNKS = (0, 6 * MXU_DIM, D_FF)
RG_CT = 256
SCAN_L = 256
SCAN_V = GROUP_ROWS // SCAN_L
SCAN_UNROLL = 8
GLA_C = 128
GLA_PROMPT_SEQS = 8
GLA_INTRA_UNROLL = 8
GLA_INTRA_UNROLL_SHORT = 16
GLA_INTER_UNROLL = 16
GLA_LEVELS = tuple(GLA_C >> (l + 1) for l in range(GLA_C.bit_length() - 1))


def _cparams(*sem):
    return pltpu.CompilerParams(dimension_semantics=sem, vmem_limit_bytes=VMEM_LIMIT_BYTES)


def _group_of_tile(i, rows):
    return jnp.maximum((i * rows) // GROUP_ROWS - (PROMPT_BLOCKS - 1), 0)


def _sigmoid(x):
    return 1.0 / (1.0 + jnp.exp2(x * (-LOG2_E)))


def _softplus(x):
    return jnp.maximum(x, 0.0) + jnp.log1p(jnp.exp(-jnp.abs(x)))


def _rms(x):
    return x * lax.rsqrt(jnp.mean(x * x, axis=-1, keepdims=True) + EPS)


def _modulated(x, mod_ref, gain_ref, j):
    y = _rms(x) * gain_ref[j:j + 1, :]
    return y * (1.0 + mod_ref[3 * j + 1:3 * j + 2, :]) + mod_ref[3 * j:3 * j + 1, :]


def _ada_kernel(c_ref, w_ref, b_ref, o_ref):
    @pl.when(pl.program_id(1) == 0)
    def _():
        o_ref[...] = jnp.broadcast_to(b_ref[...], o_ref.shape)

    c = c_ref[...]
    s = (c * _sigmoid(c)).astype(BF16)
    o_ref[...] += jnp.dot(s, w_ref[...].astype(BF16), preferred_element_type=F32)


def _ada_mods(cvecs, ada_w, ada_b):
    depth = ada_w.shape[0]
    n_slab = D_MODEL // ADA_SLAB
    c_slabs = cvecs.reshape(8, n_slab, ADA_SLAB).swapaxes(0, 1)
    out = pl.pallas_call(
        _ada_kernel,
        grid=(depth, n_slab),
        in_specs=[
            pl.BlockSpec((None, 8, ADA_SLAB), lambda l, k: (k, 0, 0)),
            pl.BlockSpec((None, ADA_SLAB, N_MOD * D_MODEL), lambda l, k: (l, k, 0)),
            pl.BlockSpec((None, 1, N_MOD * D_MODEL), lambda l, k: (l, 0, 0)),
        ],
        out_specs=pl.BlockSpec((None, 8, N_MOD * D_MODEL), lambda l, k: (l, 0, 0)),
        out_shape=jax.ShapeDtypeStruct((depth, 8, N_MOD * D_MODEL), F32),
        compiler_params=_cparams("arbitrary", "arbitrary"),
        name="ada_mod",
    )(c_slabs, ada_w, ada_b.reshape(depth, 1, N_MOD * D_MODEL))
    return out[:, :N_GROUPS, :].reshape(depth, N_GROUPS, N_MOD, D_MODEL)


def _pick_half(refs):
    if len(refs) == 1:
        return refs[0][...]
    return jnp.where(pl.program_id(0) < HALF_TILES, refs[0][...], refs[1][...])


def _ffn_kernel(*refs, j, final, n_x, n_y, split_out, n_cast):
    x_refs, y_refs, refs = refs[:n_x], refs[n_x:n_x + n_y], refs[n_x + n_y:]
    if n_y:
        wmix_ref, refs = refs[0], refs[1:]
    mod_ref, gain_ref, win_ref, wout_ref, fn_ref = refs[:5]
    refs = refs[5:]
    if n_cast:
        f32_refs, refs = refs[:n_cast], refs[n_cast:]
        for src_ref, dst_ref in zip(f32_refs, refs[len(refs) - n_cast:]):
            dst_ref[...] = src_ref[...].astype(BF16)
    o_refs = refs
    x = _pick_half(x_refs)
    if n_y:
        y = _pick_half(y_refs).astype(BF16)
        x = x + mod_ref[5:6, :] * jnp.dot(y, wmix_ref[...], preferred_element_type=F32)
    hb = _modulated(x, mod_ref, gain_ref, j).astype(BF16)
    acc = jnp.zeros((TM, D_MODEL), F32)
    for lo, hi in zip(FF_CHUNKS[:-1], FF_CHUNKS[1:]):
        g = jnp.dot(hb, win_ref[:, lo:hi], preferred_element_type=F32)
        u = jnp.dot(hb, win_ref[:, D_FF + lo:D_FF + hi], preferred_element_type=F32)
        a = (g * _sigmoid(g) * u).astype(BF16)
        acc = acc + jnp.dot(a, wout_ref[lo:hi, :], preferred_element_type=F32)
    out = x + FFN_RES * mod_ref[3 * j + 2:3 * j + 3, :] * acc
    if final:
        out = _rms(out) * fn_ref[...]
    if split_out:
        first_half = pl.program_id(0) < HALF_TILES

        @pl.when(first_half)
        def _():
            o_refs[0][...] = out

        @pl.when(jnp.logical_not(first_half))
        def _():
            o_refs[1][...] = out
    else:
        o_refs[0][...] = out


def _row_specs(width, split):
    if not split:
        return [pl.BlockSpec((TM, width), lambda i: (i, 0))]
    return [pl.BlockSpec((TM, width), lambda i: (jnp.minimum(i, HALF_TILES - 1), 0)),
            pl.BlockSpec((TM, width), lambda i: (jnp.maximum(i - HALF_TILES, 0), 0))]


def _half_ffn(xs, mods_l, gains_l, w_in, w_out, half, final_norm, final, split_out, mix=None, casts=()):
    n_steps = N_TOK // TM
    ys, w_mix = mix if mix is not None else ((), None)
    in_specs = _row_specs(D_MODEL, len(xs) == 2)
    args = list(xs)
    if ys:
        in_specs += _row_specs(ys[0].shape[1], len(ys) == 2)
        in_specs.append(pl.BlockSpec(w_mix.shape, lambda i: (0, 0), pipeline_mode=pl.Buffered(1)))
        args += list(ys) + [w_mix]
    in_specs += [
        pl.BlockSpec((None, N_MOD, D_MODEL), lambda i: (_group_of_tile(i, TM), 0, 0)),
        pl.BlockSpec((3, D_MODEL), lambda i: (0, 0)),
        pl.BlockSpec((D_MODEL, 2 * D_FF), lambda i: (0, 0), pipeline_mode=pl.Buffered(1)),
        pl.BlockSpec((D_FF, D_MODEL), lambda i: (0, 0), pipeline_mode=pl.Buffered(1)),
        pl.BlockSpec((1, D_MODEL), lambda i: (0, 0)),
    ]
    args += [mods_l, gains_l, w_in, w_out, final_norm]
    n_rows = N_PROMPT_TOK if split_out else N_TOK
    out_specs = _row_specs(D_MODEL, split_out)
    out_shape = [jax.ShapeDtypeStruct((n_rows, D_MODEL), F32)] * (2 if split_out else 1)
    for arr, lead, cols in casts:
        rows = arr.shape[len(lead)]
        slab = rows // n_steps
        assert slab * n_steps == rows and slab % BF16_SUBLANES == 0, (rows, n_steps)
        in_specs.append(pl.BlockSpec((None,) * len(lead) + (slab, cols), lambda i, lead=lead: (*lead, i, 0)))
        args.append(arr)
        out_specs.append(pl.BlockSpec((slab, cols), lambda i: (i, 0)))
        out_shape.append(jax.ShapeDtypeStruct((rows, cols), BF16))
    return pl.pallas_call(
        functools.partial(_ffn_kernel, j=2 * half, final=final, n_x=len(xs), n_y=len(ys), split_out=split_out,
                          n_cast=len(casts)),
        grid=(n_steps,),
        in_specs=in_specs,
        out_specs=out_specs,
        out_shape=out_shape,
        compiler_params=_cparams("arbitrary"),
        name="half_ffn",
    )(*args)


def _gelu_tanh(x):
    return 0.5 * x * (1.0 + jnp.tanh(0.7978845608028654 * (x + 0.044715 * (x * x * x))))


def _rg_proj_kernel(x_ref, mod_ref, gain_ref, w_ref, gate_ref, xbr_ref):
    hb = _modulated(x_ref[...], mod_ref, gain_ref, 1).astype(BF16)
    gate_ref[...] = _gelu_tanh(jnp.dot(hb, w_ref[:, :D_RNN], preferred_element_type=F32))
    xbr_ref[...] = jnp.dot(hb, w_ref[:, D_RNN:], preferred_element_type=F32)


def _rg_proj(x, mods_l, gains_l, w_in):
    return pl.pallas_call(
        _rg_proj_kernel,
        grid=(N_TOK // TM_PROJ,),
        in_specs=[
            pl.BlockSpec((TM_PROJ, D_MODEL), lambda i: (i, 0)),
            pl.BlockSpec((None, N_MOD, D_MODEL), lambda i: (_group_of_tile(i, TM_PROJ), 0, 0)),
            pl.BlockSpec((3, D_MODEL), lambda i: (0, 0)),
            pl.BlockSpec((D_MODEL, 2 * D_RNN), lambda i: (0, 0), pipeline_mode=pl.Buffered(1)),
        ],
        out_specs=[pl.BlockSpec((TM_PROJ, D_RNN), lambda i: (i, 0))] * 2,
        out_shape=[jax.ShapeDtypeStruct((N_TOK, D_RNN), F32)] * 2,
        compiler_params=_cparams("arbitrary"),
        name="rg_in_proj",
    )(x, mods_l, gains_l, w_in)


def _seg_shift(row, down):
    v = lax.broadcasted_iota(jnp.int32, row.shape, 0)
    if down:
        return jnp.where(v >= 1, pltpu.roll(row, 1, 0), 0.0)
    return jnp.where(v < SCAN_V - 1, pltpu.roll(row, SCAN_V - 1, 0), 0.0)


def _rg_scan_kernel(xbr_ref, gate_ref, cw_ref, cb_ref, wa_ref, ba_ref, wi_ref, bi_ref, lam_ref, h0_ref,
                    y_ref, stf_ref, stb_ref, af_ref, uf_ref, ab_ref, ub_ref, pf_ref, hf_ref, pb_ref, hb_ref):
    chain = jnp.where(pl.program_id(0) >= PROMPT_BLOCKS, 1.0, 0.0).astype(F32)
    nbt = RG_CT // RG_BLOCK
    rows = SCAN_L * SCAN_V
    v = SCAN_V

    for nb in range(nbt):
        ls = slice(nb * RG_BLOCK, (nb + 1) * RG_BLOCK)
        x = xbr_ref[:, ls]
        before = chain * _seg_shift(x[rows - v:], True)
        after0 = chain * _seg_shift(x[:v], False)
        after1 = chain * _seg_shift(x[v:2 * v], False)
        xm1 = jnp.concatenate([before, x[:rows - v]], axis=0)
        xp1 = jnp.concatenate([x[v:], after0], axis=0)
        xp2 = jnp.concatenate([x[2 * v:], after0, after1], axis=0)
        xc = (cw_ref[0:1, ls] * xm1 + cw_ref[1:2, ls] * x + cw_ref[2:3, ls] * xp1 + cw_ref[3:4, ls] * xp2
              + cb_ref[0:1, ls])
        xcb = xc.astype(BF16)
        xc_r2 = xc * (2.0 ** 0.5)
        for d, (a_ref, u_ref) in enumerate(((af_ref, uf_ref), (ab_ref, ub_ref))):
            r = _sigmoid(jnp.dot(xcb, wa_ref[d, nb], preferred_element_type=F32) + ba_ref[d:d + 1, ls])
            gi = _sigmoid(jnp.dot(xcb, wi_ref[d, nb], preferred_element_type=F32) + bi_ref[d:d + 1, ls])
            decay = RG_C * _softplus(-lam_ref[d:d + 1, ls])
            a_ref[nb] = jnp.exp2(r * (decay * (-LOG2_E)))
            t = jnp.tanh(r * decay)
            prod = t * (1.0 + t)
            u_ref[nb] = jnp.where(prod > 0.0, t * lax.rsqrt(prod), 0.0) * (gi * xc_r2)

    def steps(j, carry):
        base_f = pl.multiple_of(j * (SCAN_UNROLL * v), SCAN_UNROLL * v)
        base_b = pl.multiple_of((SCAN_L - SCAN_UNROLL) * v - j * (SCAN_UNROLL * v), SCAN_UNROLL * v)
        carry = list(carry)
        for k in range(SCAN_UNROLL):
            sf = pl.ds(base_f + k * v, v)
            sb = pl.ds(base_b + (SCAN_UNROLL - 1 - k) * v, v)
            for nb in range(nbt):
                hf, pf, hb, pb = carry[4 * nb:4 * nb + 4]
                a = af_ref[nb, sf, :]
                hf = a * hf + uf_ref[nb, sf, :]
                pf = a * pf
                hf_ref[nb, sf, :] = hf
                pf_ref[nb, sf, :] = pf
                a = ab_ref[nb, sb, :]
                hb = a * hb + ub_ref[nb, sb, :]
                pb = a * pb
                hb_ref[nb, sb, :] = hb
                pb_ref[nb, sb, :] = pb
                carry[4 * nb:4 * nb + 4] = [hf, pf, hb, pb]
        return tuple(carry)

    zero = jnp.zeros((v, RG_BLOCK), F32)
    one = jnp.ones((v, RG_BLOCK), F32)
    lax.fori_loop(0, SCAN_L // SCAN_UNROLL, steps, (zero, one, zero, one) * nbt)

    for nb in range(nbt):
        ls = slice(nb * RG_BLOCK, (nb + 1) * RG_BLOCK)
        h_end, p_end = hf_ref[nb, rows - v:, :], pf_ref[nb, rows - v:, :]
        entry = [chain * h0_ref[0:1, ls]]
        for s in range(v - 1):
            entry.append(chain * (h_end[s:s + 1] + p_end[s:s + 1] * entry[s]))
        ef = jnp.concatenate(entry, axis=0)
        h_start, p_start = hb_ref[nb, :v, :], pb_ref[nb, :v, :]
        entry = [chain * h0_ref[1:2, ls]]
        for s in range(v - 1, 0, -1):
            entry.append(chain * (h_start[s:s + 1] + p_start[s:s + 1] * entry[-1]))
        eb = jnp.concatenate(entry[::-1], axis=0)
        shape3 = (SCAN_L, v, RG_BLOCK)
        hf = hf_ref[nb].reshape(shape3) + pf_ref[nb].reshape(shape3) * ef[None]
        hb = hb_ref[nb].reshape(shape3) + pb_ref[nb].reshape(shape3) * eb[None]
        stf_ref[:, ls] = hf[SCAN_L - 1]
        stb_ref[:, ls] = hb[0]
        y_ref[:, ls] = (hf + hb).reshape(rows, RG_BLOCK) * gate_ref[:, ls]


def _rg_scan(xbr, gate, conv_w, conv_b, w_a, b_a, w_i, b_i, lam, h0):
    n_blk = N_TOK // GROUP_ROWS
    nbt = RG_CT // RG_BLOCK
    tile = pl.BlockSpec((GROUP_ROWS, RG_CT), lambda b, c: (b, c))
    vec2 = pl.BlockSpec((2, RG_CT), lambda b, c: (0, c))
    wspec = pl.BlockSpec((2, nbt, RG_BLOCK, RG_BLOCK), lambda b, c: (0, c, 0, 0))
    return pl.pallas_call(
        _rg_scan_kernel,
        grid=(n_blk, D_RNN // RG_CT),
        in_specs=[
            tile, tile,
            pl.BlockSpec((CONV_W, RG_CT), lambda b, c: (0, c)),
            pl.BlockSpec((1, RG_CT), lambda b, c: (0, c)),
            wspec, vec2, wspec, vec2, vec2,
            pl.BlockSpec((None, 2, RG_CT), lambda b, c: (jnp.maximum(b - PROMPT_BLOCKS, 0), 0, c)),
        ],
        out_specs=[tile] + [pl.BlockSpec((SCAN_V, RG_CT), lambda b, c: (b, c))] * 2,
        out_shape=[jax.ShapeDtypeStruct((N_TOK, D_RNN), F32)]
        + [jax.ShapeDtypeStruct((n_blk * SCAN_V, D_RNN), F32)] * 2,
        scratch_shapes=[pltpu.VMEM((nbt, GROUP_ROWS, RG_BLOCK), F32)] * 8,
        compiler_params=_cparams("arbitrary", "arbitrary"),
        name="rg_scan",
    )(xbr, gate, conv_w, conv_b, w_a, b_a, w_i, b_i, lam, h0)


def _chunk_tri():
    ri = lax.broadcasted_iota(jnp.int32, (GLA_C, GLA_C), 0)
    ci = lax.broadcasted_iota(jnp.int32, (GLA_C, GLA_C), 1)
    return (ri >= ci).astype(BF16), (ri <= ci).astype(BF16)


def _split3(x):
    hi = x.astype(BF16)
    r1 = x - hi.astype(F32)
    mid = r1.astype(BF16)
    lo = (r1 - mid.astype(F32)).astype(BF16)
    return hi, mid, lo


def _gla_proj_kernel(x_ref, mod_ref, gain_ref, w_ref, wz_ref, w2_ref, bg_ref,
                     q_ref, k_ref, v_ref, r_ref, cum_ref):
    hb = _modulated(x_ref[...], mod_ref, gain_ref, 1).astype(BF16)
    z = jnp.dot(hb, wz_ref[...].astype(BF16), preferred_element_type=F32).astype(BF16)
    zz = jnp.dot(z, w2_ref[...], preferred_element_type=F32) + bg_ref[...]
    log_gate = -_softplus(-zz) * (LOG2_E / GLA_TAU)
    parts = _split3(log_gate)
    r = jnp.dot(hb, w_ref[:, 2 * GLA_QK + GLA_V:], preferred_element_type=F32)
    r_ref[...] = r * _sigmoid(r)
    q_ref[...] = jnp.dot(hb, w_ref[:, :GLA_QK], preferred_element_type=F32) * (GLA_DK ** -0.5)
    k_ref[...] = jnp.dot(hb, w_ref[:, GLA_QK:2 * GLA_QK], preferred_element_type=F32)
    v_ref[...] = jnp.dot(hb, w_ref[:, 2 * GLA_QK:2 * GLA_QK + GLA_V], preferred_element_type=F32).astype(BF16)
    tri_f, tri_b = _chunk_tri()
    for ch in range(TM_PROJ // GLA_C):
        rows = slice(ch * GLA_C, (ch + 1) * GLA_C)
        cum_ref[rows, :GLA_QK] = sum(jnp.dot(tri_f, p[rows, :GLA_QK], preferred_element_type=F32) for p in parts)
        cum_ref[rows, GLA_QK:] = sum(jnp.dot(tri_b, p[rows, GLA_QK:], preferred_element_type=F32) for p in parts)


def _gla_proj(x, mods_l, gains_l, w_main, w_z, w2cat, bg):
    def rows(w):
        return pl.BlockSpec((TM_PROJ, w), lambda i: (i, 0))

    def whole(a):
        return pl.BlockSpec(a.shape, lambda i: (0,) * a.ndim, pipeline_mode=pl.Buffered(1))

    widths = (GLA_QK, GLA_QK, GLA_V, GLA_V, 2 * GLA_QK)
    dtypes = (F32, F32, BF16, F32, F32)
    return pl.pallas_call(
        _gla_proj_kernel,
        grid=(N_TOK // TM_PROJ,),
        in_specs=[
            rows(D_MODEL),
            pl.BlockSpec((None, N_MOD, D_MODEL), lambda i: (_group_of_tile(i, TM_PROJ), 0, 0)),
            pl.BlockSpec((3, D_MODEL), lambda i: (0, 0)),
            whole(w_main), whole(w_z), whole(w2cat), whole(bg),
        ],
        out_specs=[rows(w) for w in widths],
        out_shape=[jax.ShapeDtypeStruct((N_TOK, w), t) for w, t in zip(widths, dtypes)],
        compiler_params=_cparams("arbitrary"),
        name="gla_in_proj",
    )(x, mods_l, gains_l, w_main, w_z, w2cat, bg)


def _dot_tn(a, b):
    return lax.dot_general(a, b, (((0,), (0,)), ((), ())), preferred_element_type=F32)


def _dot_nt(a, b):
    return lax.dot_general(a, b, (((1,), (1,)), ((), ())), preferred_element_type=F32)


def _level_ref(cum_ref, row0, cum, m, backward):
    c = GLA_C
    pick = m - 1 if backward else m

    def bcast(r, n):
        return jnp.broadcast_to(cum_ref[pl.ds(row0 + r, 1), :], (n, GLA_DK))

    if m >= 4:
        return jnp.concatenate([bcast(g * 2 * m + pick, 2 * m) for g in range(c // (2 * m))], axis=0)
    pos = lax.broadcasted_iota(jnp.int32, (c, GLA_DK), 0)
    if m == 2:
        lo = jnp.concatenate([bcast(g * 8 + pick, 8) for g in range(c // 8)], axis=0)
        hi = jnp.concatenate([bcast(g * 8 + 4 + pick, 8) for g in range(c // 8)], axis=0)
        return jnp.where((pos & 4) == 0, lo, hi)
    if backward:
        return jnp.where((pos & 1) == 1, pltpu.roll(cum, 1, 0), cum)
    return jnp.where((pos & 1) == 0, pltpu.roll(cum, c - 1, 0), cum)


def _gla_intra(q_ref, k_ref, v_ref, cf_ref, cb_ref, row0, diag, masks, uppers):
    rows = pl.ds(row0, GLA_C)
    cf, cb = cf_ref[rows, :], cb_ref[rows, :]
    qb, kb = q_ref[rows, :].astype(BF16), k_ref[rows, :].astype(BF16)
    scores = jnp.where(diag, 2.0 * _dot_nt(qb, kb), 0.0)
    for lvl, m in enumerate(GLA_LEVELS):
        df = cf - _level_ref(cf_ref, row0, cf, m, False)
        db = cb - _level_ref(cb_ref, row0, cb, m, True)
        gq = jnp.exp2(jnp.where(uppers[lvl], df, db)).astype(BF16)
        gk = jnp.exp2(-jnp.where(uppers[lvl], db, df)).astype(BF16)
        scores = jnp.where(masks[lvl], _dot_nt(qb * gq, kb * gk), scores)
    return jnp.dot(scores.astype(BF16), v_ref[rows, :], preferred_element_type=F32)


def _gla_inter(q_ref, k_ref, v_ref, cum_ref, row0, st, backward):
    rows = pl.ds(row0, GLA_C)
    cum = cum_ref[rows, :]
    tot = cum_ref[pl.ds(row0 + (0 if backward else GLA_C - 1), 1), :]
    o = _dot_nt((q_ref[rows, :] * jnp.exp2(cum)).astype(BF16), st.astype(BF16))
    kd = (k_ref[rows, :] * jnp.exp2(tot - cum)).astype(BF16)
    return o, st * jnp.exp2(tot) + _dot_tn(v_ref[rows, :], kd)


def _gla_kernel(*refs, t_len, seqs, has_s0, intra_unroll):
    if has_s0:
        q_ref, k_ref, v_ref, cf_ref, cb_ref, r_ref, gn_ref, s0_ref, y_ref, sf_ref, sb_ref = refs
    else:
        q_ref, k_ref, v_ref, cf_ref, cb_ref, r_ref, gn_ref, y_ref, sout_ref, sf_ref, sb_ref = refs
    c = GLA_C
    n_chunks = t_len // c
    ri = lax.broadcasted_iota(jnp.int32, (c, c), 0)
    ci = lax.broadcasted_iota(jnp.int32, (c, c), 1)
    x = ri ^ ci
    masks = [(x >= m) & (x < 2 * m) for m in GLA_LEVELS]
    pos = lax.broadcasted_iota(jnp.int32, (c, GLA_DK), 0)
    uppers = [(pos & m) != 0 for m in GLA_LEVELS]

    def intra(i, carry):
        row0 = pl.multiple_of(i * c, c)
        y_ref[pl.ds(row0, c), :] = _gla_intra(q_ref, k_ref, v_ref, cf_ref, cb_ref, row0, ri == ci, masks, uppers)
        return carry

    lax.fori_loop(0, seqs * n_chunks, intra, 0, unroll=min(seqs * n_chunks, intra_unroll))

    for s in range(seqs):
        if has_s0:
            sf_ref[s] = s0_ref[s, 0].T
            sb_ref[s] = s0_ref[s, 1].T
        else:
            sf_ref[s] = jnp.zeros((GLA_DV, GLA_DK), F32)
            sb_ref[s] = jnp.zeros((GLA_DV, GLA_DK), F32)

        def inter(i, carry, s=s):
            rf = pl.multiple_of(s * t_len + i * c, c)
            rb = pl.multiple_of(s * t_len + (n_chunks - 1 - i) * c, c)
            o, st = _gla_inter(q_ref, k_ref, v_ref, cf_ref, rf, sf_ref[s], False)
            y_ref[pl.ds(rf, c), :] += o
            sf_ref[s] = st
            o, st = _gla_inter(q_ref, k_ref, v_ref, cb_ref, rb, sb_ref[s], True)
            y_ref[pl.ds(rb, c), :] += o
            sb_ref[s] = st
            return carry

        lax.fori_loop(0, n_chunks, inter, 0, unroll=min(n_chunks, GLA_INTER_UNROLL))
        if not has_s0:
            sout_ref[s, 0] = sf_ref[s].T
            sout_ref[s, 1] = sb_ref[s].T
    o = y_ref[...]
    y_ref[...] = _rms(o) * gn_ref[...] * r_ref[...]


def _gla_mix(q, k, v, cum, r, gnorm, s0, *, t_len, seqs, row0, n_seq):
    has_s0 = s0 is not None
    blk = seqs * t_len
    blk0 = row0 // blk
    assert blk0 * blk == row0 and n_seq % seqs == 0

    def rows(w, off=0):
        return pl.BlockSpec((blk, w), lambda s, h: (blk0 + s, off + h))

    state_spec = pl.BlockSpec((seqs, 2, None, GLA_DK, GLA_DV), lambda s, h: (s, 0, h, 0, 0))
    in_specs = [rows(GLA_DK), rows(GLA_DK), rows(GLA_DV), rows(GLA_DK), rows(GLA_DK, GLA_HEADS),
                rows(GLA_DV), pl.BlockSpec((None, 1, GLA_DV), lambda s, h: (h, 0, 0))]
    args = [q, k, v, cum, cum, r, gnorm.reshape(GLA_HEADS, 1, GLA_DV)]
    y_spec = pl.BlockSpec((blk, GLA_DV), lambda s, h: (s, h))
    y_shape = jax.ShapeDtypeStruct((n_seq * t_len, GLA_V), F32)
    if has_s0:
        in_specs.append(state_spec)
        args.append(s0)
        out_specs, out_shape = y_spec, y_shape
    else:
        out_specs = [y_spec, state_spec]
        out_shape = [y_shape, jax.ShapeDtypeStruct((n_seq, 2, GLA_HEADS, GLA_DK, GLA_DV), F32)]
    return pl.pallas_call(
        functools.partial(_gla_kernel, t_len=t_len, seqs=seqs, has_s0=has_s0,
                          intra_unroll=GLA_INTRA_UNROLL_SHORT if t_len == SEQ else GLA_INTRA_UNROLL),
        grid=(n_seq // seqs, GLA_HEADS),
        in_specs=in_specs,
        out_specs=out_specs,
        out_shape=out_shape,
        scratch_shapes=[pltpu.VMEM((seqs, GLA_DV, GLA_DK), F32)] * 2,
        compiler_params=_cparams("arbitrary", "arbitrary"),
        name="gla_mix_sample" if has_s0 else "gla_mix_prompt",
    )(*args)


def _natural_to_scan(x_half):
    x4 = x_half.reshape(-1, SCAN_V, SCAN_L, D_MODEL)
    return x4.swapaxes(1, 2).reshape(-1, D_MODEL)


def _scan_to_natural(x_half):
    x4 = x_half.reshape(-1, SCAN_L, SCAN_V, D_MODEL)
    return x4.swapaxes(1, 2).reshape(-1, D_MODEL)


def _natural_to_col_major(x_half):
    x4 = x_half.reshape(DEC_BATCH, DEC_SEQ // GRID_W, GRID_W, D_MODEL)
    return x4.swapaxes(1, 2).reshape(-1, D_MODEL)


def _col_major_to_natural(x_half):
    x4 = x_half.reshape(DEC_BATCH, GRID_W, DEC_SEQ // GRID_W, D_MODEL)
    return x4.swapaxes(1, 2).reshape(-1, D_MODEL)


def kernel(x_prompt, x_sample, c, state_rglru, state_gla, c_ctx, ada_w, ada_b, norm_g, ffn_w_in, ffn_w_out,
           rg_w_in, rg_conv_w, rg_conv_b, rg_w_a, rg_b_a, rg_w_i, rg_b_i, rg_lambda, rg_w_out,
           gla_w_in, gla_w_g2, gla_b_g, gla_norm, gla_w_out, final_norm):
    depth = ada_w.shape[0]
    cvecs = jnp.concatenate([c_ctx[None, :], c, jnp.zeros((8 - N_GROUPS, D_MODEL), F32)], axis=0)
    mods = _ada_mods(cvecs, ada_w, ada_b)
    fn = final_norm.reshape(1, D_MODEL)
    rg_states, gla_states = [], []

    w_in, w_out = ffn_w_in[0, 0].astype(BF16), ffn_w_out[0, 0].astype(BF16)
    n_main = 2 * GLA_QK + 2 * GLA_V

    def ffn_casts(layer, half):
        return ((ffn_w_in, (layer, half), 2 * D_FF), (ffn_w_out, (layer, half), D_MODEL))

    xp = x_prompt.reshape(N_PROMPT_TOK, D_MODEL)
    xs = x_sample.reshape(DEC_BATCH * DEC_SEQ, D_MODEL)
    for i in range(depth):
        mods_l, gains_l = mods[i], norm_g[i]
        j = i // 2
        last = i == depth - 1
        if i % 2 == 0:
            halves = (_natural_to_scan(xp), _natural_to_scan(xs))
            mixer_casts = ((rg_w_in, (j,), 2 * D_RNN), (rg_w_out, (j,), D_MODEL))
        else:
            halves = (xp, _natural_to_col_major(xs))
            mixer_casts = ((gla_w_in, (j,), n_main), (gla_w_out, (j,), D_MODEL))
        x, w_in, w_out, wm_in, wm_out = _half_ffn(halves, mods_l, gains_l, w_in, w_out, 0, fn, False, False,
                                                  casts=ffn_casts(i, 1) + mixer_casts)
        if i % 2 == 0:
            gate, xbr = _rg_proj(x, mods_l, gains_l, wm_in)
            y, st_f, st_b = _rg_scan(xbr, gate, rg_conv_w[j], rg_conv_b[j].reshape(1, D_RNN),
                                     rg_w_a[j].astype(BF16), rg_b_a[j], rg_w_i[j].astype(BF16), rg_b_i[j],
                                     rg_lambda[j], state_rglru[:, j])
            rg_states.append(jnp.stack([st_f[:BATCH], st_b[:BATCH]], axis=1))
            ys = (y,)
        else:
            w_z = jnp.pad(gla_w_in[j, :, n_main:], ((0, 0), (0, 128 - 2 * GLA_RANK)))
            w2cat = jnp.zeros((128, 2 * GLA_QK), F32)
            w2cat = w2cat.at[:GLA_RANK, :GLA_QK].set(gla_w_g2[j, 0])
            w2cat = w2cat.at[GLA_RANK:2 * GLA_RANK, GLA_QK:].set(gla_w_g2[j, 1]).astype(BF16)
            q, k, v, r, cum = _gla_proj(x, mods_l, gains_l, wm_in, w_z, w2cat, gla_b_g[j].reshape(1, 2 * GLA_QK))
            y_p, st = _gla_mix(q, k, v, cum, r, gla_norm[j], None, t_len=SEQ, seqs=GLA_PROMPT_SEQS, row0=0,
                               n_seq=BATCH)
            y_s = _gla_mix(q, k, v, cum, r, gla_norm[j], state_gla[:, j], t_len=DEC_SEQ, seqs=1,
                           row0=N_PROMPT_TOK, n_seq=DEC_BATCH)
            gla_states.append(st)
            ys = (y_p, y_s)
        xp, xs, *next_pair = _half_ffn((x,), mods_l, gains_l, w_in, w_out, 1, fn, last, True, mix=(ys, wm_out),
                                       casts=() if last else ffn_casts(i + 1, 0))
        if not last:
            w_in, w_out = next_pair
        if i % 2 == 0:
            xp, xs = _scan_to_natural(xp), _scan_to_natural(xs)
        else:
            xs = _col_major_to_natural(xs)

    y_prompt = xp.reshape(BATCH, SEQ, D_MODEL)
    y_sample = xs.reshape(DEC_BATCH, DEC_SEQ, D_MODEL)
    return (y_prompt, y_sample, jnp.stack(rg_states, axis=1), jnp.stack(gla_states, axis=1))
```
